```python
import jax, jax.numpy as jnp
from jax import lax
import numpy as np

D_MODEL = 1024
BATCH = 16
SEQ = 2048
DEPTH = 2
DEC_BATCH = 2
DEC_SEQ = 16384
PAST_LEN = 128

GRID_W = 64
HEAD_DIM = 64
EPS = 1e-6
NEG = -1e30
NA_HEADS = 8
NA_WIN_H = 8
NA_WIN_W = 16
WB_HEADS = 8
WB_KV_HEADS = 2
WB_WINDOW = 128
WB_BLOCK = 128
T5_BUCKETS = 32
T5_MAX_DIST = 128
MLA_HEADS = 16
MLA_Q_RANK = 256
MLA_KV_RANK = 128
MLA_NOPE = 64
MLA_ROPE = 32
MLA_V = 64
ROPE_THETA = 10000.0
MLA_QBLOCK = 128

NA_W = NA_HEADS * HEAD_DIM
WB_QW = WB_HEADS * HEAD_DIM
WB_KVW = WB_KV_HEADS * HEAD_DIM
EVEN_MIX = NA_W + WB_QW
EVEN_IN = 3 * NA_W + WB_QW + 2 * WB_KVW + EVEN_MIX
ODD_MIX = MLA_HEADS * MLA_V
ODD_IN = MLA_Q_RANK + MLA_KV_RANK + MLA_ROPE + ODD_MIX
N_EVEN = (DEPTH + 1) // 2
N_ODD = DEPTH // 2

kernel_name = "hybrid_natten_swa_mla_encoder"


def rms_norm(x, g):
    xf = x.astype(jnp.float32)
    y = xf * lax.rsqrt(jnp.mean(xf * xf, axis=-1, keepdims=True) + EPS)
    return (y * g.astype(jnp.float32)).astype(x.dtype)


def t5_bucket(rel):
    nb = T5_BUCKETS // 2
    ret = (rel > 0).astype(np.int32) * nb
    n = np.abs(rel)
    max_exact = nb // 2
    large = max_exact + (np.log(np.maximum(n, 1) / max_exact)
                         / np.log(T5_MAX_DIST / max_exact) * (nb - max_exact)).astype(np.int32)
    large = np.minimum(large, nb - 1)
    return ret + np.where(n < max_exact, n, large)


def neighbourhood_attention(q, k, v, rpb):
    b, l = q.shape[:2]
    rows = l // GRID_W
    kh = min(NA_WIN_H, rows)
    r = np.arange(rows)
    row_start = np.clip(r - kh // 2, 0, rows - kh)
    row_idx = row_start[:, None] + np.arange(kh)[None, :]
    c = np.arange(GRID_W)
    col_start = np.clip(c - NA_WIN_W // 2, 0, GRID_W - NA_WIN_W)
    col_ok = (c[None, :] >= col_start[:, None]) & (c[None, :] < col_start[:, None] + NA_WIN_W)
    d_row = row_idx - r[:, None]
    d_col = np.clip(c[None, :] - c[:, None], -(NA_WIN_W - 1), NA_WIN_W - 1)
    idx_r = (d_row + NA_WIN_H - 1)[:, None, :, None]
    idx_c = (d_col + NA_WIN_W - 1)[None, :, None, :]
    bias = rpb[:, idx_r, idx_c].astype(jnp.float32)
    bias = jnp.where(jnp.asarray(col_ok)[None, None, :, None, :], bias, NEG)

    qg = (q * (HEAD_DIM ** -0.5)).reshape(b, rows, GRID_W, NA_HEADS, HEAD_DIM)
    kg = k.reshape(b, rows, GRID_W, NA_HEADS, HEAD_DIM)[:, row_idx]
    vg = v.reshape(b, rows, GRID_W, NA_HEADS, HEAD_DIM)[:, row_idx]
    s = jnp.einsum('brqhd,brkwhd->bhrqkw', qg, kg).astype(jnp.float32) + bias[None]
    p = jax.nn.softmax(s.reshape(b, NA_HEADS, rows, GRID_W, kh * GRID_W), axis=-1)
    p = p.reshape(s.shape).astype(v.dtype)
    o = jnp.einsum('bhrqkw,brkwhd->brqhd', p, vg)
    return o.reshape(b, l, NA_W)


def window_gqa(q, k, v, sink, t5_bias):
    b, l = q.shape[:2]
    nblk = l // WB_BLOCK
    g = WB_HEADS // WB_KV_HEADS
    pad = ((0, 0), (WB_BLOCK, WB_BLOCK), (0, 0), (0, 0))

    def band(t):
        t = jnp.pad(t, pad).reshape(b, nblk + 2, WB_BLOCK, WB_KV_HEADS, HEAD_DIM)
        return jnp.concatenate([t[:, :-2], t[:, 1:-1], t[:, 2:]], axis=2)

    kb, vb = band(k), band(v)
    qb = (q * (HEAD_DIM ** -0.5)).reshape(b, nblk, WB_BLOCK, WB_KV_HEADS, g, HEAD_DIM)
    s = jnp.einsum('bnqhgd,bnkhd->bnhgqk', qb, kb).astype(jnp.float32)
    rel = (np.arange(3 * WB_BLOCK) - WB_BLOCK)[None, :] - np.arange(WB_BLOCK)[:, None]
    bias = t5_bias[t5_bucket(rel)].astype(jnp.float32)
    bias = bias.transpose(2, 0, 1).reshape(WB_KV_HEADS, g, WB_BLOCK, 3 * WB_BLOCK)
    kpos = np.arange(nblk)[:, None] * WB_BLOCK - WB_BLOCK + np.arange(3 * WB_BLOCK)[None, :]
    valid = (np.abs(rel) <= WB_WINDOW)[None] & ((kpos >= 0) & (kpos < l))[:, None, :]
    s = jnp.where(jnp.asarray(valid)[None, :, None, None], s + bias[None, None], NEG)
    sink_b = jnp.broadcast_to(sink.reshape(WB_KV_HEADS, g, 1, 1).astype(jnp.float32), s.shape[:-1] + (1,))
    p = jax.nn.softmax(jnp.concatenate([s, sink_b], axis=-1), axis=-1)[..., :-1].astype(v.dtype)
    o = jnp.einsum('bnhgqk,bnkhd->bnqhgd', p, vb)
    return o.reshape(b, l, WB_QW)


def rope_tables(l, dtype):
    inv_freq = 1.0 / (ROPE_THETA ** (jnp.arange(0, MLA_ROPE, 2, dtype=jnp.float32) / MLA_ROPE))
    ang = jnp.arange(l, dtype=jnp.float32)[:, None] * inv_freq[None, :]
    return jnp.cos(ang).astype(dtype), jnp.sin(ang).astype(dtype)


def apply_rope(x, cos, sin):
    x1, x2 = jnp.split(x, 2, axis=-1)
    return jnp.concatenate([x1 * cos - x2 * sin, x2 * cos + x1 * sin], axis=-1)


def mla(q_lat, kv_lat, k_rope, q_norm_g, w_qb, kv_norm_g, w_kvb):
    b, l = q_lat.shape[:2]
    q = (rms_norm(q_lat, q_norm_g) @ w_qb).reshape(b, l, MLA_HEADS, MLA_NOPE + MLA_ROPE)
    kv = (rms_norm(kv_lat, kv_norm_g) @ w_kvb).reshape(b, l, MLA_HEADS, MLA_NOPE + MLA_V)
    q_nope, q_pe = q[..., :MLA_NOPE], q[..., MLA_NOPE:]
    k_nope, v = kv[..., :MLA_NOPE], kv[..., MLA_NOPE:]
    cos, sin = rope_tables(l, q.dtype)
    q_pe = apply_rope(q_pe, cos[:, None, :], sin[:, None, :])
    k_pe = apply_rope(k_rope, cos, sin)
    scale = (MLA_NOPE + MLA_ROPE) ** -0.5
    nq = l // MLA_QBLOCK
    qn = (q_nope * scale).reshape(b, nq, MLA_QBLOCK, MLA_HEADS, MLA_NOPE).transpose(1, 0, 2, 3, 4)
    qp = (q_pe * scale).reshape(b, nq, MLA_QBLOCK, MLA_HEADS, MLA_ROPE).transpose(1, 0, 2, 3, 4)

    def block(args):
        qn_b, qp_b = args
        s = (jnp.einsum('bqhd,bkhd->bhqk', qn_b, k_nope)
             + jnp.einsum('bqhd,bkd->bhqk', qp_b, k_pe)).astype(jnp.float32)
        p = jax.nn.softmax(s, axis=-1).astype(v.dtype)
        return jnp.einsum('bhqk,bkhd->bqhd', p, v)

    o = lax.map(block, (qn, qp))
    return o.transpose(1, 0, 2, 3, 4).reshape(b, l, ODD_MIX)


def even_layer(h, w_in, w_out, rpb, sink, t5_bias):
    b, l, _ = h.shape
    u = h @ w_in
    qa, ka, va, qb, kb, vb, gate = jnp.split(
        u, [NA_W, 2 * NA_W, 3 * NA_W, 3 * NA_W + WB_QW, 3 * NA_W + WB_QW + WB_KVW,
            3 * NA_W + WB_QW + 2 * WB_KVW], axis=-1)
    oa = neighbourhood_attention(qa.reshape(b, l, NA_HEADS, HEAD_DIM), ka.reshape(b, l, NA_HEADS, HEAD_DIM),
                                 va.reshape(b, l, NA_HEADS, HEAD_DIM), rpb)
    ob = window_gqa(qb.reshape(b, l, WB_HEADS, HEAD_DIM), kb.reshape(b, l, WB_KV_HEADS, HEAD_DIM),
                    vb.reshape(b, l, WB_KV_HEADS, HEAD_DIM), sink, t5_bias)
    o = jnp.concatenate([oa, ob], axis=-1) * jax.nn.silu(gate)
    return o @ w_out


def odd_layer(h, w_in, q_norm_g, w_qb, kv_norm_g, w_kvb, w_out):
    u = h @ w_in
    q_lat, kv_lat, k_rope, gate = jnp.split(
        u, [MLA_Q_RANK, MLA_Q_RANK + MLA_KV_RANK, MLA_Q_RANK + MLA_KV_RANK + MLA_ROPE], axis=-1)
    o = mla(q_lat, kv_lat, k_rope, q_norm_g, w_qb, kv_norm_g, w_kvb) * jax.nn.silu(gate)
    return o @ w_out


def trunk(x, c, ada_w, ada_b, norm_g, t5_bias, ev_w_in, na_rpb, wb_sink, ev_w_out,
          mla_w_in, mla_q_norm, mla_w_qb, mla_kv_norm, mla_w_kvb, mla_w_out, final_g):
    cs = jax.nn.silu(c)
    for i in range(DEPTH):
        mod = (cs @ ada_w[i] + ada_b[i])[:, None, :]
        shift, scale, gate = jnp.split(mod, 3, axis=-1)
        h = rms_norm(x, norm_g[i]) * (1 + scale) + shift
        j = i // 2
        if i % 2 == 0:
            out = even_layer(h, ev_w_in[j], ev_w_out[j], na_rpb[j], wb_sink[j], t5_bias)
        else:
            out = odd_layer(h, mla_w_in[j], mla_q_norm[j], mla_w_qb[j], mla_kv_norm[j], mla_w_kvb[j], mla_w_out[j])
        x = x + gate * out
    return rms_norm(x, final_g)


def setup_inputs(seed: int = 0) -> dict:
    key = jax.random.key(seed)
    ks = jax.random.split(key, 21)
    D = D_MODEL

    def nrm(k, shape, s):
        return jax.random.normal(k, shape, jnp.float32) * s

    return {
        "x_prompt": nrm(ks[0], (BATCH, SEQ, D), 1.0),
        "x_sample": nrm(ks[1], (DEC_BATCH, DEC_SEQ, D), 1.0),
        "c_prompt": nrm(ks[2], (BATCH, D), 1.0),
        "c_sample": nrm(ks[3], (DEC_BATCH, D), 1.0),
        "ada_w": nrm(ks[4], (DEPTH, D, 3 * D), 0.5 * D ** -0.5),
        "ada_b": nrm(ks[5], (DEPTH, 3 * D), 0.02),
        "norm_g": 1.0 + nrm(ks[6], (DEPTH, D), 0.02),
        "t5_bias": nrm(ks[7], (T5_BUCKETS, WB_HEADS), 0.1),
        "ev_w_in": nrm(ks[8], (N_EVEN, D, EVEN_IN), D ** -0.5),
        "na_rpb": nrm(ks[9], (N_EVEN, NA_HEADS, 2 * NA_WIN_H - 1, 2 * NA_WIN_W - 1), 0.1),
        "wb_sink": nrm(ks[10], (N_EVEN, WB_HEADS), 0.5),
        "ev_w_out": nrm(ks[11], (N_EVEN, EVEN_MIX, D), EVEN_MIX ** -0.5),
        "mla_w_in": nrm(ks[12], (N_ODD, D, ODD_IN), D ** -0.5),
        "mla_q_norm": 1.0 + nrm(ks[13], (N_ODD, MLA_Q_RANK), 0.02),
        "mla_w_qb": nrm(ks[14], (N_ODD, MLA_Q_RANK, MLA_HEADS * (MLA_NOPE + MLA_ROPE)), MLA_Q_RANK ** -0.5),
        "mla_kv_norm": 1.0 + nrm(ks[15], (N_ODD, MLA_KV_RANK), 0.02),
        "mla_w_kvb": nrm(ks[16], (N_ODD, MLA_KV_RANK, MLA_HEADS * (MLA_NOPE + MLA_V)), MLA_KV_RANK ** -0.5),
        "mla_w_out": nrm(ks[17], (N_ODD, ODD_MIX, D), ODD_MIX ** -0.5),
        "final_g": 1.0 + nrm(ks[18], (D,), 0.02),
    }


def reference(x_prompt, x_sample, c_prompt, c_sample, ada_w, ada_b, norm_g, t5_bias, ev_w_in, na_rpb,
              wb_sink, ev_w_out, mla_w_in, mla_q_norm, mla_w_qb, mla_kv_norm, mla_w_kvb, mla_w_out, final_g):
    y_prompt = trunk(x_prompt, c_prompt, ada_w, ada_b, norm_g, t5_bias, ev_w_in, na_rpb, wb_sink, ev_w_out,
                     mla_w_in, mla_q_norm, mla_w_qb, mla_kv_norm, mla_w_kvb, mla_w_out, final_g)
    y_sample = trunk(x_sample, c_sample, ada_w, ada_b, norm_g, t5_bias, ev_w_in, na_rpb, wb_sink, ev_w_out,
                     mla_w_in, mla_q_norm, mla_w_qb, mla_kv_norm, mla_w_kvb, mla_w_out, final_g)
    return (y_prompt, y_sample)
```

```python
import functools
import math

import numpy as np
import jax
import jax.numpy as jnp
from jax import lax
from jax.experimental import pallas as pl
from jax.experimental.pallas import tpu as pltpu

D_MODEL = 1024
GRID_W = 64
HEAD_DIM = 64
EPS = 1e-6
NEG = -1e30
NA_HEADS = 8
NA_WIN_H = 8
NA_WIN_W = 16
WB_HEADS = 8
WB_KV_HEADS = 2
WB_WINDOW = 128
WB_BLOCK = 128
T5_BUCKETS = 32
T5_MAX_DIST = 128
MLA_HEADS = 16
MLA_Q_RANK = 256
MLA_KV_RANK = 128
MLA_NOPE = 64
MLA_ROPE = 32
MLA_V = 64
ROPE_THETA = 10000.0
MLA_QK = MLA_NOPE + MLA_ROPE

NA_W = NA_HEADS * HEAD_DIM
WB_QW = WB_HEADS * HEAD_DIM
WB_KVW = WB_KV_HEADS * HEAD_DIM
EVEN_MIX = NA_W + WB_QW
EVEN_IN = 3 * NA_W + WB_QW + 2 * WB_KVW + EVEN_MIX
ODD_MIX = MLA_HEADS * MLA_V

U_GATE = 0
U_QA = EVEN_MIX
U_KA = U_QA + NA_W
U_VA = U_KA + NA_W
U_QB = U_VA + NA_W
U_KB = U_QB + WB_QW
U_VB = U_KB + WB_KVW

LANES = 128
TM = 512
NA_ROWS_PER_STEP = 8
NA_CHUNK = NA_ROWS_PER_STEP * GRID_W
WB_CHUNK = 512
MLA_TQ = 256
MLA_TK = 512
ONES_ROWS = 16
VMEM_LIMIT = 56 * 1024 * 1024

BF16 = jnp.bfloat16
F32 = jnp.float32
LOG2E = math.log2(math.e)

WB_HEAD_ORDER = (0, 4, 1, 5, 2, 6, 3, 7)


def _params(sem):
    return pltpu.CompilerParams(dimension_semantics=sem, vmem_limit_bytes=VMEM_LIMIT)


def _dot(a, b):
    return jnp.dot(a, b, preferred_element_type=F32)


def _dot_nt(a, b):
    return lax.dot_general(a, b, (((1,), (1,)), ((), ())), preferred_element_type=F32)


def _silu(x):
    return x * (1.0 / (1.0 + jnp.exp(-x)))


def _modulated_norm(x, g, scale, shift):
    ms = jnp.mean(x * x, axis=-1, keepdims=True)
    y = x * lax.rsqrt(ms + EPS)
    return (y * g) * (1.0 + scale) + shift


def _ada_kernel(c_ref, w_ref, b_ref, o_ref):
    c = c_ref[...]
    cs = _silu(c).astype(BF16)
    o_ref[0] = _dot(cs, w_ref[0]) + b_ref[0]


def _ada_mod(c_pad, ada_w, ada_b):
    depth = ada_w.shape[0]
    rows = c_pad.shape[0]
    tn = 768
    return pl.pallas_call(
        _ada_kernel,
        grid=(depth, 3 * D_MODEL // tn),
        in_specs=[
            pl.BlockSpec((rows, D_MODEL), lambda i, n: (0, 0)),
            pl.BlockSpec((1, D_MODEL, tn), lambda i, n: (i, 0, n)),
            pl.BlockSpec((1, 1, tn), lambda i, n: (i, 0, n)),
        ],
        out_specs=pl.BlockSpec((1, rows, tn), lambda i, n: (i, 0, n)),
        out_shape=jax.ShapeDtypeStruct((depth, rows, 3 * D_MODEL), F32),
        compiler_params=_params(("arbitrary", "arbitrary")),
        name="ada_mod",
    )(c_pad, ada_w, ada_b)


def _even_front_kernel(x_ref, g_ref, sc_ref, sh_ref, w_ref, u_ref):
    h = _modulated_norm(x_ref[0], g_ref[...], sc_ref[0], sh_ref[0]).astype(BF16)
    u_ref[0] = _dot(h, w_ref[...]).astype(BF16)


def _even_front(x, g, scale, shift, w):
    b, l, _ = x.shape
    return pl.pallas_call(
        _even_front_kernel,
        grid=(b, l // TM),
        in_specs=[
            pl.BlockSpec((1, TM, D_MODEL), lambda bi, i: (bi, i, 0)),
            pl.BlockSpec((1, D_MODEL), lambda bi, i: (0, 0)),
            pl.BlockSpec((1, 1, D_MODEL), lambda bi, i: (bi, 0, 0)),
            pl.BlockSpec((1, 1, D_MODEL), lambda bi, i: (bi, 0, 0)),
            pl.BlockSpec((D_MODEL, EVEN_IN), lambda bi, i: (0, 0)),
        ],
        out_specs=pl.BlockSpec((1, TM, EVEN_IN), lambda bi, i: (bi, i, 0)),
        out_shape=jax.ShapeDtypeStruct((b, l, EVEN_IN), BF16),
        compiler_params=_params(("arbitrary", "arbitrary")),
        name="even_front",
    )(x, g, scale, shift, w)


def _low_lane_mask(shape):
    return lax.broadcasted_iota(jnp.int32, shape, len(shape) - 1) < HEAD_DIM


def _na_kernel(q_ref, kp_ref, kc_ref, kn_ref, vp_ref, vc_ref, vn_ref, bias_ref, o_ref,
               kwin, vwin, *, rows):
    ci = pl.program_id(1)
    kwin[0:NA_CHUNK] = kp_ref[0]
    kwin[NA_CHUNK:2 * NA_CHUNK] = kc_ref[0]
    kwin[2 * NA_CHUNK:3 * NA_CHUNK] = kn_ref[0]
    vwin[0:NA_CHUNK] = vp_ref[0]
    vwin[NA_CHUNK:2 * NA_CHUNK] = vc_ref[0]
    vwin[2 * NA_CHUNK:3 * NA_CHUNK] = vn_ref[0]

    low = _low_lane_mask((GRID_W, LANES))
    kh = NA_WIN_H
    r0 = ci * NA_ROWS_PER_STEP

    def row_body(i, carry):
        r = r0 + i
        row_start = jnp.clip(r - kh // 2, 0, rows - kh)
        variant = r - row_start
        start = pl.multiple_of((row_start - r0 + NA_ROWS_PER_STEP) * GRID_W, GRID_W)
        qoff = pl.multiple_of(i * GRID_W, GRID_W)
        for j in range(NA_HEADS // 2):
            lanes = slice(j * LANES, (j + 1) * LANES)
            q2 = q_ref[0, pl.ds(qoff, GRID_W), lanes]
            zero = jnp.zeros_like(q2)
            lhs = jnp.concatenate([jnp.where(low, q2, zero), jnp.where(low, zero, q2)], axis=0)
            kw = kwin[pl.ds(start, kh * GRID_W), lanes]
            vw = vwin[pl.ds(start, kh * GRID_W), lanes]
            s = _dot_nt(lhs, kw)
            s = s + bias_ref[variant, j]
            m = jnp.max(s, axis=-1, keepdims=True)
            p = jnp.exp(s - m)
            den = jnp.sum(p, axis=-1, keepdims=True)
            o = _dot(p.astype(BF16), vw) / den
            o_ref[0, pl.ds(qoff, GRID_W), lanes] = jnp.where(low, o[:GRID_W], o[GRID_W:]).astype(BF16)
        return carry

    lax.fori_loop(0, NA_ROWS_PER_STEP, row_body, 0)


def _na_attention(u, bias_tbl):
    b, l, _ = u.shape
    rows = l // GRID_W
    nchunk = l // NA_CHUNK
    qblk, kblk, vblk = U_QA // NA_W, U_KA // NA_W, U_VA // NA_W

    def halo(col, d):
        return pl.BlockSpec((1, NA_CHUNK, NA_W),
                            lambda bi, i: (bi, jnp.clip(i + d, 0, nchunk - 1), col))

    return pl.pallas_call(
        functools.partial(_na_kernel, rows=rows),
        grid=(b, nchunk),
        in_specs=[
            pl.BlockSpec((1, NA_CHUNK, NA_W), lambda bi, i: (bi, i, qblk)),
            halo(kblk, -1), halo(kblk, 0), halo(kblk, 1),
            halo(vblk, -1), halo(vblk, 0), halo(vblk, 1),
            pl.BlockSpec(bias_tbl.shape, lambda bi, i: (0, 0, 0, 0)),
        ],
        out_specs=pl.BlockSpec((1, NA_CHUNK, NA_W), lambda bi, i: (bi, i, 0)),
        out_shape=jax.ShapeDtypeStruct((b, l, NA_W), BF16),
        scratch_shapes=[pltpu.VMEM((3 * NA_CHUNK, NA_W), BF16),
                        pltpu.VMEM((3 * NA_CHUNK, NA_W), BF16)],
        compiler_params=_params(("arbitrary", "arbitrary")),
        name="na_attention",
    )(u, u, u, u, u, u, u, bias_tbl)


def _wb_kernel(sink_ref, q_ref, kp_ref, kc_ref, kn_ref, vp_ref, vc_ref, vn_ref, bias_ref, o_ref,
               kwin, vwin, *, nblk):
    ci = pl.program_id(1)
    kwin[0:WB_BLOCK] = kp_ref[0]
    kwin[WB_BLOCK:WB_BLOCK + WB_CHUNK] = kc_ref[0]
    kwin[WB_BLOCK + WB_CHUNK:2 * WB_BLOCK + WB_CHUNK] = kn_ref[0]
    vwin[0:WB_BLOCK] = vp_ref[0]
    vwin[WB_BLOCK:WB_BLOCK + WB_CHUNK] = vc_ref[0]
    vwin[WB_BLOCK + WB_CHUNK:2 * WB_BLOCK + WB_CHUNK] = vn_ref[0]

    low = _low_lane_mask((WB_BLOCK, LANES))
    col = lax.broadcasted_iota(jnp.int32, (2 * WB_BLOCK, 3 * WB_BLOCK), 1)
    row = lax.broadcasted_iota(jnp.int32, (2 * WB_BLOCK, 1), 0)
    per_step = WB_CHUNK // WB_BLOCK
    for n in range(per_step):
        gblk = ci * per_step + n
        lo = jnp.where(gblk > 0, 0, WB_BLOCK)
        hi = jnp.where(gblk < nblk - 1, 3 * WB_BLOCK, 2 * WB_BLOCK)
        in_seq = jnp.logical_and(col >= lo, col < hi)
        kw = kwin[n * WB_BLOCK:(n + 3) * WB_BLOCK]
        vw = vwin[n * WB_BLOCK:(n + 3) * WB_BLOCK]
        for j in range(WB_HEADS // 2):
            lanes = slice(j * LANES, (j + 1) * LANES)
            q2 = q_ref[0, n * WB_BLOCK:(n + 1) * WB_BLOCK, lanes]
            zero = jnp.zeros_like(q2)
            lhs = jnp.concatenate([jnp.where(low, q2, zero), jnp.where(low, zero, q2)], axis=0)
            s = _dot_nt(lhs, kw)
            s = jnp.where(in_seq, s + bias_ref[j], NEG)
            sink = jnp.where(row < WB_BLOCK, sink_ref[WB_HEAD_ORDER[2 * j]],
                             sink_ref[WB_HEAD_ORDER[2 * j + 1]])
            m = jnp.maximum(jnp.max(s, axis=-1, keepdims=True), sink)
            p = jnp.exp(s - m)
            den = jnp.sum(p, axis=-1, keepdims=True) + jnp.exp(sink - m)
            o = _dot(p.astype(BF16), vw) / den
            o_ref[0, n * WB_BLOCK:(n + 1) * WB_BLOCK, lanes] = (
                jnp.where(low, o[:WB_BLOCK], o[WB_BLOCK:]).astype(BF16))


def _wb_attention(u, bias_tbl, sink):
    b, l, _ = u.shape
    nblk = l // WB_BLOCK
    nchunk = l // WB_CHUNK
    per_step = WB_CHUNK // WB_BLOCK
    qblk = U_QB // WB_QW
    kcol, vcol = U_KB // WB_KVW, U_VB // WB_KVW

    def halo(colblk, d):
        if d == 0:
            return pl.BlockSpec((1, WB_CHUNK, WB_KVW), lambda bi, i: (bi, i, colblk))
        return pl.BlockSpec(
            (1, WB_BLOCK, WB_KVW),
            lambda bi, i: (bi, jnp.clip(i * per_step + (per_step if d > 0 else -1), 0, nblk - 1), colblk))

    return pl.pallas_call(
        functools.partial(_wb_kernel, nblk=nblk),
        grid=(b, nchunk),
        in_specs=[
            pl.BlockSpec(memory_space=pltpu.SMEM),
            pl.BlockSpec((1, WB_CHUNK, WB_QW), lambda bi, i: (bi, i, qblk)),
            halo(kcol, -1), halo(kcol, 0), halo(kcol, 1),
            halo(vcol, -1), halo(vcol, 0), halo(vcol, 1),
            pl.BlockSpec(bias_tbl.shape, lambda bi, i: (0, 0, 0)),
        ],
        out_specs=pl.BlockSpec((1, WB_CHUNK, WB_QW), lambda bi, i: (bi, i, 0)),
        out_shape=jax.ShapeDtypeStruct((b, l, WB_QW), BF16),
        scratch_shapes=[pltpu.VMEM((WB_CHUNK + 2 * WB_BLOCK, WB_KVW), BF16),
                        pltpu.VMEM((WB_CHUNK + 2 * WB_BLOCK, WB_KVW), BF16)],
        compiler_params=_params(("arbitrary", "arbitrary")),
        name="wb_attention",
    )(sink, u, u, u, u, u, u, u, bias_tbl)


def _even_out_kernel(x_ref, oa_ref, ob_ref, gate_ref, gm_ref, w_ref, o_ref):
    g = _silu(gate_ref[0].astype(F32))
    o = jnp.concatenate([oa_ref[0], ob_ref[0]], axis=-1).astype(F32)
    z = (o * g).astype(BF16)
    o_ref[0] = x_ref[0] + gm_ref[0] * _dot(z, w_ref[...])


def _even_out(x, oa, ob, u, gate_mod, w_out):
    b, l, _ = x.shape
    return pl.pallas_call(
        _even_out_kernel,
        grid=(b, l // TM),
        in_specs=[
            pl.BlockSpec((1, TM, D_MODEL), lambda bi, i: (bi, i, 0)),
            pl.BlockSpec((1, TM, NA_W), lambda bi, i: (bi, i, 0)),
            pl.BlockSpec((1, TM, WB_QW), lambda bi, i: (bi, i, 0)),
            pl.BlockSpec((1, TM, EVEN_MIX), lambda bi, i: (bi, i, U_GATE // EVEN_MIX)),
            pl.BlockSpec((1, 1, D_MODEL), lambda bi, i: (bi, 0, 0)),
            pl.BlockSpec((EVEN_MIX, D_MODEL), lambda bi, i: (0, 0)),
        ],
        out_specs=pl.BlockSpec((1, TM, D_MODEL), lambda bi, i: (bi, i, 0)),
        out_shape=jax.ShapeDtypeStruct((b, l, D_MODEL), F32),
        compiler_params=_params(("arbitrary", "arbitrary")),
        name="even_out",
    )(x, oa, ob, u, gate_mod, w_out)


def _odd_front_kernel(x_ref, g_ref, sc_ref, sh_ref, w1t_ref, wnat_ref, gq_ref, gkvt_ref, gkv_ref,
                      wqbt_ref, wk_ref, wvt_ref, cost_ref, sint_ref, cosp_ref, sinp_ref,
                      k_ref, qt_ref, vt_ref, sgt_ref):
    h = _modulated_norm(x_ref[0], g_ref[...], sc_ref[0], sh_ref[0]).astype(BF16)
    reps = TM // LANES

    ut = _dot_nt(w1t_ref[...], h)
    sgt_ref[0] = _silu(ut[0:ODD_MIX]).astype(BF16)

    qlt = ut[ODD_MIX:ODD_MIX + MLA_Q_RANK]
    qn = qlt * lax.rsqrt(jnp.mean(qlt * qlt, axis=0, keepdims=True) + EPS)
    qn = (qn * jnp.concatenate([gq_ref[...]] * reps, axis=1)).astype(BF16)
    qt = _dot(wqbt_ref[...], qn)
    qscale = (MLA_QK ** -0.5) * LOG2E
    cos_t = cost_ref[...]
    sin_t = sint_ref[...]
    half = MLA_ROPE // 2
    for hd in range(MLA_HEADS):
        base = hd * MLA_QK
        qt_ref[0, base:base + MLA_NOPE, :] = (qt[base:base + MLA_NOPE] * qscale).astype(BF16)
        x1 = qt[base + MLA_NOPE:base + MLA_NOPE + half]
        x2 = qt[base + MLA_NOPE + half:base + MLA_QK]
        qt_ref[0, base + MLA_NOPE:base + MLA_NOPE + half, :] = (
            (x1 * cos_t - x2 * sin_t) * qscale).astype(BF16)
        qt_ref[0, base + MLA_NOPE + half:base + MLA_QK, :] = (
            (x2 * cos_t + x1 * sin_t) * qscale).astype(BF16)

    kvt = ut[ODD_MIX + MLA_Q_RANK:ODD_MIX + MLA_Q_RANK + MLA_KV_RANK]
    kvnt = kvt * lax.rsqrt(jnp.mean(kvt * kvt, axis=0, keepdims=True) + EPS)
    kvnt = (kvnt * jnp.concatenate([gkvt_ref[...]] * reps, axis=1)).astype(BF16)
    vt = _dot(wvt_ref[...], kvnt)
    vt_ref[0, :, 0] = vt.reshape(MLA_HEADS, MLA_V, TM).astype(BF16)

    nat = _dot(h, wnat_ref[...])
    kvl = nat[:, 0:MLA_KV_RANK]
    kvn = kvl * lax.rsqrt(jnp.mean(kvl * kvl, axis=-1, keepdims=True) + EPS)
    kvn = (kvn * gkv_ref[...]).astype(BF16)
    kn = _dot(kvn, wk_ref[...])
    kpe = nat[:, LANES:2 * LANES] * cosp_ref[...] + nat[:, 2 * LANES:3 * LANES] * sinp_ref[...]
    for hd in range(MLA_HEADS):
        kh = kn[:, hd * LANES:(hd + 1) * LANES] + kpe
        k_ref[0, hd] = kh[:, 0:MLA_QK].astype(BF16)


def _odd_front(x, g, scale, shift, w1t, wnat, gq, gkvt, gkv, wqbt, wk, wvt, cos_t, sin_t, cos_p, sin_p):
    b, l, _ = x.shape
    nt = l // TM
    const2 = lambda bi, i: (0, 0)
    n1 = w1t.shape[0]
    return pl.pallas_call(
        _odd_front_kernel,
        grid=(b, nt),
        in_specs=[
            pl.BlockSpec((1, TM, D_MODEL), lambda bi, i: (bi, i, 0)),
            pl.BlockSpec((1, D_MODEL), const2),
            pl.BlockSpec((1, 1, D_MODEL), lambda bi, i: (bi, 0, 0)),
            pl.BlockSpec((1, 1, D_MODEL), lambda bi, i: (bi, 0, 0)),
            pl.BlockSpec((n1, D_MODEL), const2),
            pl.BlockSpec((D_MODEL, 3 * LANES), const2),
            pl.BlockSpec((MLA_Q_RANK, LANES), const2),
            pl.BlockSpec((MLA_KV_RANK, LANES), const2),
            pl.BlockSpec((1, MLA_KV_RANK), const2),
            pl.BlockSpec((MLA_HEADS * MLA_QK, MLA_Q_RANK), const2),
            pl.BlockSpec((MLA_KV_RANK, MLA_HEADS * LANES), const2),
            pl.BlockSpec((ODD_MIX, MLA_KV_RANK), const2),
            pl.BlockSpec((MLA_ROPE // 2, TM), lambda bi, i: (0, i)),
            pl.BlockSpec((MLA_ROPE // 2, TM), lambda bi, i: (0, i)),
            pl.BlockSpec((TM, LANES), lambda bi, i: (i, 0)),
            pl.BlockSpec((TM, LANES), lambda bi, i: (i, 0)),
        ],
        out_specs=[
            pl.BlockSpec((1, MLA_HEADS, TM, MLA_QK), lambda bi, i: (bi, 0, i, 0)),
            pl.BlockSpec((1, MLA_HEADS * MLA_QK, TM), lambda bi, i: (bi, 0, i)),
            pl.BlockSpec((1, MLA_HEADS, 1, MLA_V, TM), lambda bi, i: (bi, 0, i, 0, 0)),
            pl.BlockSpec((1, ODD_MIX, TM), lambda bi, i: (bi, 0, i)),
        ],
        out_shape=[
            jax.ShapeDtypeStruct((b, MLA_HEADS, l, MLA_QK), BF16),
            jax.ShapeDtypeStruct((b, MLA_HEADS * MLA_QK, l), BF16),
            jax.ShapeDtypeStruct((b, MLA_HEADS, nt, MLA_V, TM), BF16),
            jax.ShapeDtypeStruct((b, ODD_MIX, l), BF16),
        ],
        compiler_params=_params(("arbitrary", "arbitrary")),
        name="odd_front",
    )(x, g, scale, shift, w1t, wnat, gq, gkvt, gkv, wqbt, wk, wvt, cos_t, sin_t, cos_p, sin_p)


def _mla_kernel(qt_ref, k_ref, vt_ref, o_ref, *, nk):
    q = qt_ref[0]
    ones = jnp.ones((ONES_ROWS, MLA_TK), BF16)

    def step(j, carry):
        m, acc = carry
        koff = pl.multiple_of(j * MLA_TK, MLA_TK)
        k = k_ref[0, 0, pl.ds(koff, MLA_TK), :]
        s = _dot(k, q)
        m_new = jnp.maximum(m, jnp.max(s, axis=0, keepdims=True))
        alpha = jnp.exp2(m - m_new)
        p = jnp.exp2(s - m_new).astype(BF16)
        v1 = jnp.concatenate([vt_ref[0, 0, j], ones], axis=0)
        acc = acc * alpha + _dot(v1, p)
        return m_new, acc

    m0 = jnp.full((1, MLA_TQ), NEG, F32)
    acc0 = jnp.zeros((MLA_V + ONES_ROWS, MLA_TQ), F32)
    _, acc = lax.fori_loop(0, nk, step, (m0, acc0))
    o_ref[0] = (acc[0:MLA_V] / acc[MLA_V:MLA_V + 1]).astype(BF16)


def _mla_attention(qt, k, vt):
    b, _, l = qt.shape
    nk = l // MLA_TK
    return pl.pallas_call(
        functools.partial(_mla_kernel, nk=nk),
        grid=(b, MLA_HEADS, l // MLA_TQ),
        in_specs=[
            pl.BlockSpec((1, MLA_QK, MLA_TQ), lambda bi, h, i: (bi, h, i)),
            pl.BlockSpec((1, 1, l, MLA_QK), lambda bi, h, i: (bi, h, 0, 0)),
            pl.BlockSpec((1, 1, nk, MLA_V, MLA_TK), lambda bi, h, i: (bi, h, 0, 0, 0)),
        ],
        out_specs=pl.BlockSpec((1, MLA_V, MLA_TQ), lambda bi, h, i: (bi, h, i)),
        out_shape=jax.ShapeDtypeStruct((b, ODD_MIX, l), BF16),
        compiler_params=_params(("arbitrary", "arbitrary", "arbitrary")),
        name="mla_attention",
    )(qt, k, vt)


def _odd_out_kernel(x_ref, ot_ref, sgt_ref, gm_ref, wt_ref, fg_ref, y_ref):
    z = (ot_ref[0].astype(F32) * sgt_ref[0].astype(F32)).astype(BF16)
    out = _dot(wt_ref[...], z).T
    x2 = x_ref[0] + gm_ref[0] * out
    ms = jnp.mean(x2 * x2, axis=-1, keepdims=True)
    y_ref[0] = (x2 * lax.rsqrt(ms + EPS)) * fg_ref[...]


def _odd_out(x, ot, sgt, gate_mod, w_out_t, final_g):
    b, l, _ = x.shape
    return pl.pallas_call(
        _odd_out_kernel,
        grid=(b, l // TM),
        in_specs=[
            pl.BlockSpec((1, TM, D_MODEL), lambda bi, i: (bi, i, 0)),
            pl.BlockSpec((1, ODD_MIX, TM), lambda bi, i: (bi, 0, i)),
            pl.BlockSpec((1, ODD_MIX, TM), lambda bi, i: (bi, 0, i)),
            pl.BlockSpec((1, 1, D_MODEL), lambda bi, i: (bi, 0, 0)),
            pl.BlockSpec((D_MODEL, ODD_MIX), lambda bi, i: (0, 0)),
            pl.BlockSpec((1, D_MODEL), lambda bi, i: (0, 0)),
        ],
        out_specs=pl.BlockSpec((1, TM, D_MODEL), lambda bi, i: (bi, i, 0)),
        out_shape=jax.ShapeDtypeStruct((b, l, D_MODEL), F32),
        compiler_params=_params(("arbitrary", "arbitrary")),
        name="odd_out",
    )(x, ot, sgt, gate_mod, w_out_t, final_g)


def _t5_bucket(rel):
    nb = T5_BUCKETS // 2
    ret = (rel > 0).astype(np.int32) * nb
    n = np.abs(rel)
    max_exact = nb // 2
    large = max_exact + (np.log(np.maximum(n, 1) / max_exact)
                         / np.log(T5_MAX_DIST / max_exact) * (nb - max_exact)).astype(np.int32)
    large = np.minimum(large, nb - 1)
    return ret + np.where(n < max_exact, n, large)


def _na_bias_table(rpb):
    kh = NA_WIN_H
    c = np.arange(GRID_W)
    col_start = np.clip(c - NA_WIN_W // 2, 0, GRID_W - NA_WIN_W)
    col_ok = (c[None, :] >= col_start[:, None]) & (c[None, :] < col_start[:, None] + NA_WIN_W)
    d_col = np.clip(c[None, :] - c[:, None], -(NA_WIN_W - 1), NA_WIN_W - 1) + NA_WIN_W - 1
    variant = np.arange(kh)
    d_row = np.arange(kh)[None, :] - variant[:, None] + NA_WIN_H - 1
    bias = rpb[:, d_row[:, None, :, None], d_col[None, :, None, :]].astype(F32)
    bias = jnp.where(jnp.asarray(col_ok)[None, None, :, None, :], bias, NEG)
    bias = bias.transpose(1, 0, 2, 3, 4).reshape(kh, NA_HEADS // 2, 2 * GRID_W, kh * GRID_W)
    return bias


def _wb_bias_table(t5_bias):
    rel = (np.arange(3 * WB_BLOCK) - WB_BLOCK)[None, :] - np.arange(WB_BLOCK)[:, None]
    bias = t5_bias[_t5_bucket(rel)].astype(F32).transpose(2, 0, 1)
    bias = jnp.where(jnp.asarray(np.abs(rel) <= WB_WINDOW)[None], bias, NEG)
    bias = bias[np.asarray(WB_HEAD_ORDER)]
    return bias.reshape(WB_HEADS // 2, 2 * WB_BLOCK, 3 * WB_BLOCK)


def _even_weights(w_in, w_out):
    qa, ka, va, qb, kb, vb, gate = jnp.split(
        w_in, [NA_W, 2 * NA_W, 3 * NA_W, 3 * NA_W + WB_QW, 3 * NA_W + WB_QW + WB_KVW,
               3 * NA_W + WB_QW + 2 * WB_KVW], axis=-1)
    order = np.asarray(WB_HEAD_ORDER)
    perm = (order[:, None] * HEAD_DIM + np.arange(HEAD_DIM)[None, :]).reshape(-1)
    qscale = HEAD_DIM ** -0.5
    gate = jnp.concatenate([gate[:, :NA_W], gate[:, NA_W:][:, perm]], axis=-1)
    w = jnp.concatenate([gate, qa * qscale, ka, va, qb[:, perm] * qscale, kb, vb], axis=-1)
    w_out_p = jnp.concatenate([w_out[:NA_W], w_out[NA_W:][perm]], axis=0)
    return w.astype(BF16), w_out_p.astype(BF16)


def _odd_weights(w_in, q_norm, w_qb, kv_norm, w_kvb, w_out):
    q_lat, kv_lat, k_rope, gate = jnp.split(
        w_in, [MLA_Q_RANK, MLA_Q_RANK + MLA_KV_RANK, MLA_Q_RANK + MLA_KV_RANK + MLA_ROPE], axis=-1)
    w1t = jnp.concatenate([gate, q_lat, kv_lat], axis=-1).T.astype(BF16)
    half = MLA_ROPE // 2
    k_rot = jnp.concatenate([-k_rope[:, half:], k_rope[:, :half]], axis=-1)
    z64 = jnp.zeros((D_MODEL, MLA_NOPE), w_in.dtype)
    z32 = jnp.zeros((D_MODEL, LANES - MLA_QK), w_in.dtype)
    wnat = jnp.concatenate([kv_lat, z64, k_rope, z32, z64, k_rot, z32], axis=-1).astype(BF16)
    wkv = w_kvb.reshape(MLA_KV_RANK, MLA_HEADS, MLA_NOPE + MLA_V)
    wk = jnp.concatenate([wkv[:, :, :MLA_NOPE], jnp.zeros_like(wkv[:, :, :MLA_NOPE])], axis=-1)
    wk = wk.reshape(MLA_KV_RANK, MLA_HEADS * LANES).astype(BF16)
    wvt = wkv[:, :, MLA_NOPE:].reshape(MLA_KV_RANK, ODD_MIX).T.astype(BF16)
    gq = jnp.broadcast_to(q_norm.astype(F32)[:, None], (MLA_Q_RANK, LANES))
    gkvt = jnp.broadcast_to(kv_norm.astype(F32)[:, None], (MLA_KV_RANK, LANES))
    gkv = kv_norm.astype(F32)[None, :]
    return w1t, wnat, gq, gkvt, gkv, w_qb.T.astype(BF16), wk, wvt, w_out.T.astype(BF16)


def _rope_tables(l):
    inv_freq = 1.0 / (ROPE_THETA ** (jnp.arange(0, MLA_ROPE, 2, dtype=F32) / MLA_ROPE))
    ang = jnp.arange(l, dtype=F32)[:, None] * inv_freq[None, :]
    cos, sin = jnp.cos(ang), jnp.sin(ang)
    z64 = jnp.zeros((l, MLA_NOPE), F32)
    z32 = jnp.zeros((l, LANES - MLA_QK), F32)
    cos_p = jnp.concatenate([z64, cos, cos, z32], axis=-1)
    sin_p = jnp.concatenate([z64, sin, sin, z32], axis=-1)
    return cos.T, sin.T, cos_p, sin_p


def _trunk(x, mod, norm_g, ev, na_tbl, wb_tbl, sink, od, final_g):
    b, l, _ = x.shape
    assert l % TM == 0 and l % NA_CHUNK == 0 and l % WB_CHUNK == 0 and l % MLA_TK == 0
    assert l // GRID_W >= NA_WIN_H + NA_ROWS_PER_STEP
    shift0, scale0, gate0 = [t[:, None, :] for t in jnp.split(mod[0], 3, axis=-1)]
    shift1, scale1, gate1 = [t[:, None, :] for t in jnp.split(mod[1], 3, axis=-1)]

    w_in0, w_out0 = ev
    u = _even_front(x, norm_g[0][None, :], scale0, shift0, w_in0)
    oa = _na_attention(u, na_tbl)
    ob = _wb_attention(u, wb_tbl, sink)
    x1 = _even_out(x, oa, ob, u, gate0, w_out0)

    w1t, wnat, gq, gkvt, gkv, wqbt, wk, wvt, w_out1t = od
    cos_t, sin_t, cos_p, sin_p = _rope_tables(l)
    k, qt, vt, sgt = _odd_front(x1, norm_g[1][None, :], scale1, shift1, w1t, wnat, gq, gkvt, gkv,
                                wqbt, wk, wvt, cos_t, sin_t, cos_p, sin_p)
    ot = _mla_attention(qt, k, vt)
    return _odd_out(x1, ot, sgt, gate1, w_out1t, final_g[None, :])


def kernel(x_prompt, x_sample, c_prompt, c_sample, ada_w, ada_b, norm_g, t5_bias, ev_w_in, na_rpb,
           wb_sink, ev_w_out, mla_w_in, mla_q_norm, mla_w_qb, mla_kv_norm, mla_w_kvb, mla_w_out, final_g):
    bp, bs = c_prompt.shape[0], c_sample.shape[0]
    rows = -(-(bp + bs) // 16) * 16
    c_pad = jnp.concatenate([c_prompt, c_sample, jnp.zeros((rows - bp - bs, D_MODEL), F32)], axis=0)
    mod = _ada_mod(c_pad, ada_w.astype(BF16), ada_b[:, None, :])

    ev = _even_weights(ev_w_in[0], ev_w_out[0])
    na_tbl = _na_bias_table(na_rpb[0])
    wb_tbl = _wb_bias_table(t5_bias)
    od = _odd_weights(mla_w_in[0], mla_q_norm[0], mla_w_qb[0], mla_kv_norm[0], mla_w_kvb[0], mla_w_out[0])
    sink = wb_sink[0].astype(F32)

    y_prompt = _trunk(x_prompt, mod[:, :bp], norm_g, ev, na_tbl, wb_tbl, sink, od, final_g)
    y_sample = _trunk(x_sample, mod[:, bp:bp + bs], norm_g, ev, na_tbl, wb_tbl, sink, od, final_g)
    return (y_prompt, y_sample)
```

```python
import functools
import math

import numpy as np
import jax
import jax.numpy as jnp
from jax import lax
from jax.experimental import pallas as pl
from jax.experimental.pallas import tpu as pltpu

D_MODEL = 1024
GRID_W = 64
HEAD_DIM = 64
EPS = 1e-6
NEG = -1e30
NA_HEADS = 8
NA_WIN_H = 8
NA_WIN_W = 16
WB_HEADS = 8
WB_KV_HEADS = 2
WB_WINDOW = 128
WB_BLOCK = 128
T5_BUCKETS = 32
T5_MAX_DIST = 128
MLA_HEADS = 16
MLA_Q_RANK = 256
MLA_KV_RANK = 128
MLA_NOPE = 64
MLA_ROPE = 32
MLA_V = 64
ROPE_THETA = 10000.0
MLA_QK = MLA_NOPE + MLA_ROPE

NA_W = NA_HEADS * HEAD_DIM
WB_QW = WB_HEADS * HEAD_DIM
WB_KVW = WB_KV_HEADS * HEAD_DIM
EVEN_MIX = NA_W + WB_QW
EVEN_IN = 3 * NA_W + WB_QW + 2 * WB_KVW + EVEN_MIX
ODD_MIX = MLA_HEADS * MLA_V

U_GATE = 0
U_QA = EVEN_MIX
U_KA = U_QA + NA_W
U_VA = U_KA + NA_W
U_QB = U_VA + NA_W
U_KB = U_QB + WB_QW
U_VB = U_KB + WB_KVW

LANES = 128
TM = 512
NA_ROWS_PER_STEP = 8
NA_CHUNK = NA_ROWS_PER_STEP * GRID_W
WB_CHUNK = 512
MLA_TQ = 1024
MLA_TK = 512
ONES_ROWS = 16
VMEM_LIMIT = 56 * 1024 * 1024

BF16 = jnp.bfloat16
F32 = jnp.float32
LOG2E = math.log2(math.e)

WB_HEAD_ORDER = (0, 4, 1, 5, 2, 6, 3, 7)


def _params(sem):
    return pltpu.CompilerParams(dimension_semantics=sem, vmem_limit_bytes=VMEM_LIMIT)


def _dot(a, b):
    return jnp.dot(a, b, preferred_element_type=F32)


def _dot_nt(a, b):
    return lax.dot_general(a, b, (((1,), (1,)), ((), ())), preferred_element_type=F32)


def _silu(x):
    return x * (1.0 / (1.0 + jnp.exp(-x)))


def _modulated_norm(x, g, scale, shift):
    ms = jnp.mean(x * x, axis=-1, keepdims=True)
    y = x * lax.rsqrt(ms + EPS)
    return (y * g) * (1.0 + scale) + shift


def _ada_kernel(c_ref, w_ref, b_ref, o_ref):
    c = c_ref[...]
    cs = _silu(c).astype(BF16)
    o_ref[0] = _dot(cs, w_ref[0]) + b_ref[0]


def _ada_mod(c_pad, ada_w, ada_b):
    depth = ada_w.shape[0]
    rows = c_pad.shape[0]
    tn = 768
    return pl.pallas_call(
        _ada_kernel,
        grid=(depth, 3 * D_MODEL // tn),
        in_specs=[
            pl.BlockSpec((rows, D_MODEL), lambda i, n: (0, 0)),
            pl.BlockSpec((1, D_MODEL, tn), lambda i, n: (i, 0, n)),
            pl.BlockSpec((1, 1, tn), lambda i, n: (i, 0, n)),
        ],
        out_specs=pl.BlockSpec((1, rows, tn), lambda i, n: (i, 0, n)),
        out_shape=jax.ShapeDtypeStruct((depth, rows, 3 * D_MODEL), F32),
        compiler_params=_params(("arbitrary", "arbitrary")),
        name="ada_mod",
    )(c_pad, ada_w, ada_b)


def _even_front_kernel(x_ref, g_ref, sc_ref, sh_ref, w_ref, u_ref):
    h = _modulated_norm(x_ref[0], g_ref[...], sc_ref[0], sh_ref[0]).astype(BF16)
    u_ref[0] = _dot(h, w_ref[...]).astype(BF16)


def _even_front(x, g, scale, shift, w):
    b, l, _ = x.shape
    return pl.pallas_call(
        _even_front_kernel,
        grid=(b, l // TM),
        in_specs=[
            pl.BlockSpec((1, TM, D_MODEL), lambda bi, i: (bi, i, 0)),
            pl.BlockSpec((1, D_MODEL), lambda bi, i: (0, 0)),
            pl.BlockSpec((1, 1, D_MODEL), lambda bi, i: (bi, 0, 0)),
            pl.BlockSpec((1, 1, D_MODEL), lambda bi, i: (bi, 0, 0)),
            pl.BlockSpec((D_MODEL, EVEN_IN), lambda bi, i: (0, 0)),
        ],
        out_specs=pl.BlockSpec((1, TM, EVEN_IN), lambda bi, i: (bi, i, 0)),
        out_shape=jax.ShapeDtypeStruct((b, l, EVEN_IN), BF16),
        compiler_params=_params(("arbitrary", "arbitrary")),
        name="even_front",
    )(x, g, scale, shift, w)


def _low_lane_mask(shape):
    return lax.broadcasted_iota(jnp.int32, shape, len(shape) - 1) < HEAD_DIM


def _na_kernel(q_ref, kp_ref, kc_ref, kn_ref, vp_ref, vc_ref, vn_ref, bias_ref, o_ref,
               kwin, vwin, *, rows):
    ci = pl.program_id(1)
    kwin[0:NA_CHUNK] = kp_ref[0]
    kwin[NA_CHUNK:2 * NA_CHUNK] = kc_ref[0]
    kwin[2 * NA_CHUNK:3 * NA_CHUNK] = kn_ref[0]
    vwin[0:NA_CHUNK] = vp_ref[0]
    vwin[NA_CHUNK:2 * NA_CHUNK] = vc_ref[0]
    vwin[2 * NA_CHUNK:3 * NA_CHUNK] = vn_ref[0]

    low = _low_lane_mask((GRID_W, LANES))
    kh = NA_WIN_H
    r0 = ci * NA_ROWS_PER_STEP

    def row_body(i, carry):
        r = r0 + i
        row_start = jnp.clip(r - kh // 2, 0, rows - kh)
        variant = r - row_start
        start = pl.multiple_of((row_start - r0 + NA_ROWS_PER_STEP) * GRID_W, GRID_W)
        qoff = pl.multiple_of(i * GRID_W, GRID_W)
        for j in range(NA_HEADS // 2):
            lanes = slice(j * LANES, (j + 1) * LANES)
            q2 = q_ref[0, pl.ds(qoff, GRID_W), lanes]
            zero = jnp.zeros_like(q2)
            lhs = jnp.concatenate([jnp.where(low, q2, zero), jnp.where(low, zero, q2)], axis=0)
            kw = kwin[pl.ds(start, kh * GRID_W), lanes]
            vw = vwin[pl.ds(start, kh * GRID_W), lanes]
            s = _dot_nt(lhs, kw)
            s = s + bias_ref[variant, j]
            m = jnp.max(s, axis=-1, keepdims=True)
            p = jnp.exp(s - m)
            den = jnp.sum(p, axis=-1, keepdims=True)
            o = _dot(p.astype(BF16), vw) / den
            o_ref[0, pl.ds(qoff, GRID_W), lanes] = jnp.where(low, o[:GRID_W], o[GRID_W:]).astype(BF16)
        return carry

    lax.fori_loop(0, NA_ROWS_PER_STEP, row_body, 0)


def _na_attention(u, bias_tbl):
    b, l, _ = u.shape
    rows = l // GRID_W
    nchunk = l // NA_CHUNK
    qblk, kblk, vblk = U_QA // NA_W, U_KA // NA_W, U_VA // NA_W

    def halo(col, d):
        return pl.BlockSpec((1, NA_CHUNK, NA_W),
                            lambda bi, i: (bi, jnp.clip(i + d, 0, nchunk - 1), col))

    return pl.pallas_call(
        functools.partial(_na_kernel, rows=rows),
        grid=(b, nchunk),
        in_specs=[
            pl.BlockSpec((1, NA_CHUNK, NA_W), lambda bi, i: (bi, i, qblk)),
            halo(kblk, -1), halo(kblk, 0), halo(kblk, 1),
            halo(vblk, -1), halo(vblk, 0), halo(vblk, 1),
            pl.BlockSpec(bias_tbl.shape, lambda bi, i: (0, 0, 0, 0)),
        ],
        out_specs=pl.BlockSpec((1, NA_CHUNK, NA_W), lambda bi, i: (bi, i, 0)),
        out_shape=jax.ShapeDtypeStruct((b, l, NA_W), BF16),
        scratch_shapes=[pltpu.VMEM((3 * NA_CHUNK, NA_W), BF16),
                        pltpu.VMEM((3 * NA_CHUNK, NA_W), BF16)],
        compiler_params=_params(("arbitrary", "arbitrary")),
        name="na_attention",
    )(u, u, u, u, u, u, u, bias_tbl)


def _wb_kernel(sink_ref, q_ref, kp_ref, kc_ref, kn_ref, vp_ref, vc_ref, vn_ref, bias_ref, o_ref,
               kwin, vwin, *, nblk):
    ci = pl.program_id(1)
    kwin[0:WB_BLOCK] = kp_ref[0]
    kwin[WB_BLOCK:WB_BLOCK + WB_CHUNK] = kc_ref[0]
    kwin[WB_BLOCK + WB_CHUNK:2 * WB_BLOCK + WB_CHUNK] = kn_ref[0]
    vwin[0:WB_BLOCK] = vp_ref[0]
    vwin[WB_BLOCK:WB_BLOCK + WB_CHUNK] = vc_ref[0]
    vwin[WB_BLOCK + WB_CHUNK:2 * WB_BLOCK + WB_CHUNK] = vn_ref[0]

    low = _low_lane_mask((WB_BLOCK, LANES))
    col = lax.broadcasted_iota(jnp.int32, (2 * WB_BLOCK, 3 * WB_BLOCK), 1)
    row = lax.broadcasted_iota(jnp.int32, (2 * WB_BLOCK, 1), 0)
    per_step = WB_CHUNK // WB_BLOCK
    for n in range(per_step):
        gblk = ci * per_step + n
        lo = jnp.where(gblk > 0, 0, WB_BLOCK)
        hi = jnp.where(gblk < nblk - 1, 3 * WB_BLOCK, 2 * WB_BLOCK)
        in_seq = jnp.logical_and(col >= lo, col < hi)
        kw = kwin[n * WB_BLOCK:(n + 3) * WB_BLOCK]
        vw = vwin[n * WB_BLOCK:(n + 3) * WB_BLOCK]
        for j in range(WB_HEADS // 2):
            lanes = slice(j * LANES, (j + 1) * LANES)
            q2 = q_ref[0, n * WB_BLOCK:(n + 1) * WB_BLOCK, lanes]
            zero = jnp.zeros_like(q2)
            lhs = jnp.concatenate([jnp.where(low, q2, zero), jnp.where(low, zero, q2)], axis=0)
            s = _dot_nt(lhs, kw)
            s = jnp.where(in_seq, s + bias_ref[j], NEG)
            sink = jnp.where(row < WB_BLOCK, sink_ref[WB_HEAD_ORDER[2 * j]],
                             sink_ref[WB_HEAD_ORDER[2 * j + 1]])
            m = jnp.maximum(jnp.max(s, axis=-1, keepdims=True), sink)
            p = jnp.exp(s - m)
            den = jnp.sum(p, axis=-1, keepdims=True) + jnp.exp(sink - m)
            o = _dot(p.astype(BF16), vw) / den
            o_ref[0, n * WB_BLOCK:(n + 1) * WB_BLOCK, lanes] = (
                jnp.where(low, o[:WB_BLOCK], o[WB_BLOCK:]).astype(BF16))


def _wb_attention(u, bias_tbl, sink):
    b, l, _ = u.shape
    nblk = l // WB_BLOCK
    nchunk = l // WB_CHUNK
    per_step = WB_CHUNK // WB_BLOCK
    qblk = U_QB // WB_QW
    kcol, vcol = U_KB // WB_KVW, U_VB // WB_KVW

    def halo(colblk, d):
        if d == 0:
            return pl.BlockSpec((1, WB_CHUNK, WB_KVW), lambda bi, i: (bi, i, colblk))
        return pl.BlockSpec(
            (1, WB_BLOCK, WB_KVW),
            lambda bi, i: (bi, jnp.clip(i * per_step + (per_step if d > 0 else -1), 0, nblk - 1), colblk))

    return pl.pallas_call(
        functools.partial(_wb_kernel, nblk=nblk),
        grid=(b, nchunk),
        in_specs=[
            pl.BlockSpec(memory_space=pltpu.SMEM),
            pl.BlockSpec((1, WB_CHUNK, WB_QW), lambda bi, i: (bi, i, qblk)),
            halo(kcol, -1), halo(kcol, 0), halo(kcol, 1),
            halo(vcol, -1), halo(vcol, 0), halo(vcol, 1),
            pl.BlockSpec(bias_tbl.shape, lambda bi, i: (0, 0, 0)),
        ],
        out_specs=pl.BlockSpec((1, WB_CHUNK, WB_QW), lambda bi, i: (bi, i, 0)),
        out_shape=jax.ShapeDtypeStruct((b, l, WB_QW), BF16),
        scratch_shapes=[pltpu.VMEM((WB_CHUNK + 2 * WB_BLOCK, WB_KVW), BF16),
                        pltpu.VMEM((WB_CHUNK + 2 * WB_BLOCK, WB_KVW), BF16)],
        compiler_params=_params(("arbitrary", "arbitrary")),
        name="wb_attention",
    )(sink, u, u, u, u, u, u, u, bias_tbl)


def _even_out_kernel(x_ref, oa_ref, ob_ref, gate_ref, gm_ref, w_ref, o_ref):
    g = _silu(gate_ref[0].astype(F32))
    o = jnp.concatenate([oa_ref[0], ob_ref[0]], axis=-1).astype(F32)
    z = (o * g).astype(BF16)
    o_ref[0] = x_ref[0] + gm_ref[0] * _dot(z, w_ref[...])


def _even_out(x, oa, ob, u, gate_mod, w_out):
    b, l, _ = x.shape
    return pl.pallas_call(
        _even_out_kernel,
        grid=(b, l // TM),
        in_specs=[
            pl.BlockSpec((1, TM, D_MODEL), lambda bi, i: (bi, i, 0)),
            pl.BlockSpec((1, TM, NA_W), lambda bi, i: (bi, i, 0)),
            pl.BlockSpec((1, TM, WB_QW), lambda bi, i: (bi, i, 0)),
            pl.BlockSpec((1, TM, EVEN_MIX), lambda bi, i: (bi, i, U_GATE // EVEN_MIX)),
            pl.BlockSpec((1, 1, D_MODEL), lambda bi, i: (bi, 0, 0)),
            pl.BlockSpec((EVEN_MIX, D_MODEL), lambda bi, i: (0, 0)),
        ],
        out_specs=pl.BlockSpec((1, TM, D_MODEL), lambda bi, i: (bi, i, 0)),
        out_shape=jax.ShapeDtypeStruct((b, l, D_MODEL), F32),
        compiler_params=_params(("arbitrary", "arbitrary")),
        name="even_out",
    )(x, oa, ob, u, gate_mod, w_out)


def _odd_front_kernel(x_ref, g_ref, sc_ref, sh_ref, w1t_ref, wnat_ref, gq_ref, gkvt_ref, gkv_ref,
                      wqbt_ref, wk_ref, wvt_ref, cost_ref, sint_ref, cosp_ref, sinp_ref,
                      k_ref, qt_ref, vt_ref, sgt_ref):
    h = _modulated_norm(x_ref[0], g_ref[...], sc_ref[0], sh_ref[0]).astype(BF16)
    reps = TM // LANES

    ut = _dot_nt(w1t_ref[...], h)
    sgt_ref[0] = _silu(ut[0:ODD_MIX]).astype(BF16)

    qlt = ut[ODD_MIX:ODD_MIX + MLA_Q_RANK]
    qn = qlt * lax.rsqrt(jnp.mean(qlt * qlt, axis=0, keepdims=True) + EPS)
    qn = (qn * jnp.concatenate([gq_ref[...]] * reps, axis=1)).astype(BF16)
    qt = _dot(wqbt_ref[...], qn)
    qscale = (MLA_QK ** -0.5) * LOG2E
    cos_t = cost_ref[...]
    sin_t = sint_ref[...]
    half = MLA_ROPE // 2
    for hd in range(MLA_HEADS):
        base = hd * MLA_QK
        qt_ref[0, base:base + MLA_NOPE, :] = (qt[base:base + MLA_NOPE] * qscale).astype(BF16)
        x1 = qt[base + MLA_NOPE:base + MLA_NOPE + half]
        x2 = qt[base + MLA_NOPE + half:base + MLA_QK]
        qt_ref[0, base + MLA_NOPE:base + MLA_NOPE + half, :] = (
            (x1 * cos_t - x2 * sin_t) * qscale).astype(BF16)
        qt_ref[0, base + MLA_NOPE + half:base + MLA_QK, :] = (
            (x2 * cos_t + x1 * sin_t) * qscale).astype(BF16)

    kvt = ut[ODD_MIX + MLA_Q_RANK:ODD_MIX + MLA_Q_RANK + MLA_KV_RANK]
    kvnt = kvt * lax.rsqrt(jnp.mean(kvt * kvt, axis=0, keepdims=True) + EPS)
    kvnt = (kvnt * jnp.concatenate([gkvt_ref[...]] * reps, axis=1)).astype(BF16)
    vt = _dot(wvt_ref[...], kvnt)
    vt_ref[0, :, 0] = vt.reshape(MLA_HEADS, MLA_V, TM).astype(BF16)

    nat = _dot(h, wnat_ref[...])
    kvl = nat[:, 0:MLA_KV_RANK]
    kvn = kvl * lax.rsqrt(jnp.mean(kvl * kvl, axis=-1, keepdims=True) + EPS)
    kvn = (kvn * gkv_ref[...]).astype(BF16)
    kn = _dot(kvn, wk_ref[...])
    kpe = nat[:, LANES:2 * LANES] * cosp_ref[...] + nat[:, 2 * LANES:3 * LANES] * sinp_ref[...]
    for hd in range(MLA_HEADS):
        kh = kn[:, hd * LANES:(hd + 1) * LANES] + kpe
        k_ref[0, hd] = kh[:, 0:MLA_QK].astype(BF16)


def _odd_front(x, g, scale, shift, w1t, wnat, gq, gkvt, gkv, wqbt, wk, wvt, cos_t, sin_t, cos_p, sin_p):
    b, l, _ = x.shape
    nt = l // TM
    const2 = lambda bi, i: (0, 0)
    n1 = w1t.shape[0]
    return pl.pallas_call(
        _odd_front_kernel,
        grid=(b, nt),
        in_specs=[
            pl.BlockSpec((1, TM, D_MODEL), lambda bi, i: (bi, i, 0)),
            pl.BlockSpec((1, D_MODEL), const2),
            pl.BlockSpec((1, 1, D_MODEL), lambda bi, i: (bi, 0, 0)),
            pl.BlockSpec((1, 1, D_MODEL), lambda bi, i: (bi, 0, 0)),
            pl.BlockSpec((n1, D_MODEL), const2),
            pl.BlockSpec((D_MODEL, 3 * LANES), const2),
            pl.BlockSpec((MLA_Q_RANK, LANES), const2),
            pl.BlockSpec((MLA_KV_RANK, LANES), const2),
            pl.BlockSpec((1, MLA_KV_RANK), const2),
            pl.BlockSpec((MLA_HEADS * MLA_QK, MLA_Q_RANK), const2),
            pl.BlockSpec((MLA_KV_RANK, MLA_HEADS * LANES), const2),
            pl.BlockSpec((ODD_MIX, MLA_KV_RANK), const2),
            pl.BlockSpec((MLA_ROPE // 2, TM), lambda bi, i: (0, i)),
            pl.BlockSpec((MLA_ROPE // 2, TM), lambda bi, i: (0, i)),
            pl.BlockSpec((TM, LANES), lambda bi, i: (i, 0)),
            pl.BlockSpec((TM, LANES), lambda bi, i: (i, 0)),
        ],
        out_specs=[
            pl.BlockSpec((1, MLA_HEADS, TM, MLA_QK), lambda bi, i: (bi, 0, i, 0)),
            pl.BlockSpec((1, MLA_HEADS * MLA_QK, TM), lambda bi, i: (bi, 0, i)),
            pl.BlockSpec((1, MLA_HEADS, 1, MLA_V, TM), lambda bi, i: (bi, 0, i, 0, 0)),
            pl.BlockSpec((1, ODD_MIX, TM), lambda bi, i: (bi, 0, i)),
        ],
        out_shape=[
            jax.ShapeDtypeStruct((b, MLA_HEADS, l, MLA_QK), BF16),
            jax.ShapeDtypeStruct((b, MLA_HEADS * MLA_QK, l), BF16),
            jax.ShapeDtypeStruct((b, MLA_HEADS, nt, MLA_V, TM), BF16),
            jax.ShapeDtypeStruct((b, ODD_MIX, l), BF16),
        ],
        compiler_params=_params(("arbitrary", "arbitrary")),
        name="odd_front",
    )(x, g, scale, shift, w1t, wnat, gq, gkvt, gkv, wqbt, wk, wvt, cos_t, sin_t, cos_p, sin_p)


def _mla_kernel(qt_ref, k_ref, vt_ref, o_ref, s_scr, p_scr, acc_scr, *, nk):
    q = qt_ref[0]
    ones = jnp.ones((ONES_ROWS, MLA_TK), BF16)

    def scores(t, slot):
        koff = pl.multiple_of(t * MLA_TK, MLA_TK)
        s = _dot(k_ref[0, 0, pl.ds(koff, MLA_TK), :], q)
        s_scr[slot] = s
        return jnp.max(s, axis=0, keepdims=True)

    def probs(slot, m, mt):
        m_new = jnp.maximum(m, mt)
        p_scr[slot] = jnp.exp2(s_scr[slot] - m_new).astype(BF16)
        return m_new, jnp.exp2(m - m_new)

    def accumulate(t, slot, alpha):
        v1 = jnp.concatenate([vt_ref[0, 0, t], ones], axis=0)
        acc_scr[...] = acc_scr[...] * alpha + _dot(v1, p_scr[slot])

    acc_scr[...] = jnp.zeros_like(acc_scr)
    m0 = jnp.full((1, MLA_TQ), NEG, F32)
    mt0 = scores(0, 0)
    mt1 = scores(1, 1)
    m, al0 = probs(0, m0, mt0)

    def body(i, carry):
        m, mt_prev, al_pp = carry
        t = 2 * i + 2
        mt_t = scores(t, 0)
        m, al_p = probs(1, m, mt_prev)
        accumulate(t - 2, 0, al_pp)
        mt_n = scores(t + 1, 1)
        m, al_t = probs(0, m, mt_t)
        accumulate(t - 1, 1, al_p)
        return m, mt_n, al_t

    m, mt_last, al_pp = lax.fori_loop(0, (nk - 2) // 2, body, (m, mt1, al0))
    m, al_last = probs(1, m, mt_last)
    accumulate(nk - 2, 0, al_pp)
    accumulate(nk - 1, 1, al_last)
    acc = acc_scr[...]
    o_ref[0] = (acc[0:MLA_V] / acc[MLA_V:MLA_V + 1]).astype(BF16)


def _mla_attention(qt, k, vt):
    b, _, l = qt.shape
    nk = l // MLA_TK
    assert nk >= 2 and nk % 2 == 0
    return pl.pallas_call(
        functools.partial(_mla_kernel, nk=nk),
        grid=(b, MLA_HEADS, l // MLA_TQ),
        in_specs=[
            pl.BlockSpec((1, MLA_QK, MLA_TQ), lambda bi, h, i: (bi, h, i)),
            pl.BlockSpec((1, 1, l, MLA_QK), lambda bi, h, i: (bi, h, 0, 0)),
            pl.BlockSpec((1, 1, nk, MLA_V, MLA_TK), lambda bi, h, i: (bi, h, 0, 0, 0)),
        ],
        out_specs=pl.BlockSpec((1, MLA_V, MLA_TQ), lambda bi, h, i: (bi, h, i)),
        out_shape=jax.ShapeDtypeStruct((b, ODD_MIX, l), BF16),
        scratch_shapes=[pltpu.VMEM((2, MLA_TK, MLA_TQ), F32),
                        pltpu.VMEM((2, MLA_TK, MLA_TQ), BF16),
                        pltpu.VMEM((MLA_V + ONES_ROWS, MLA_TQ), F32)],
        compiler_params=_params(("arbitrary", "arbitrary", "arbitrary")),
        name="mla_attention",
    )(qt, k, vt)


def _odd_out_kernel(x_ref, ot_ref, sgt_ref, gm_ref, wt_ref, fg_ref, y_ref):
    z = (ot_ref[0].astype(F32) * sgt_ref[0].astype(F32)).astype(BF16)
    out = _dot(wt_ref[...], z).T
    x2 = x_ref[0] + gm_ref[0] * out
    ms = jnp.mean(x2 * x2, axis=-1, keepdims=True)
    y_ref[0] = (x2 * lax.rsqrt(ms + EPS)) * fg_ref[...]


def _odd_out(x, ot, sgt, gate_mod, w_out_t, final_g):
    b, l, _ = x.shape
    return pl.pallas_call(
        _odd_out_kernel,
        grid=(b, l // TM),
        in_specs=[
            pl.BlockSpec((1, TM, D_MODEL), lambda bi, i: (bi, i, 0)),
            pl.BlockSpec((1, ODD_MIX, TM), lambda bi, i: (bi, 0, i)),
            pl.BlockSpec((1, ODD_MIX, TM), lambda bi, i: (bi, 0, i)),
            pl.BlockSpec((1, 1, D_MODEL), lambda bi, i: (bi, 0, 0)),
            pl.BlockSpec((D_MODEL, ODD_MIX), lambda bi, i: (0, 0)),
            pl.BlockSpec((1, D_MODEL), lambda bi, i: (0, 0)),
        ],
        out_specs=pl.BlockSpec((1, TM, D_MODEL), lambda bi, i: (bi, i, 0)),
        out_shape=jax.ShapeDtypeStruct((b, l, D_MODEL), F32),
        compiler_params=_params(("arbitrary", "arbitrary")),
        name="odd_out",
    )(x, ot, sgt, gate_mod, w_out_t, final_g)


def _t5_bucket(rel):
    nb = T5_BUCKETS // 2
    ret = (rel > 0).astype(np.int32) * nb
    n = np.abs(rel)
    max_exact = nb // 2
    large = max_exact + (np.log(np.maximum(n, 1) / max_exact)
                         / np.log(T5_MAX_DIST / max_exact) * (nb - max_exact)).astype(np.int32)
    large = np.minimum(large, nb - 1)
    return ret + np.where(n < max_exact, n, large)


def _na_bias_table(rpb):
    kh = NA_WIN_H
    c = np.arange(GRID_W)
    col_start = np.clip(c - NA_WIN_W // 2, 0, GRID_W - NA_WIN_W)
    col_ok = (c[None, :] >= col_start[:, None]) & (c[None, :] < col_start[:, None] + NA_WIN_W)
    d_col = np.clip(c[None, :] - c[:, None], -(NA_WIN_W - 1), NA_WIN_W - 1) + NA_WIN_W - 1
    nrel = 2 * NA_WIN_W - 1
    rows = jnp.stack([rpb[:, NA_WIN_H - 1 - v:2 * NA_WIN_H - 1 - v] for v in range(kh)], axis=1)
    onehot = (d_col.reshape(-1)[None, :] == np.arange(nrel)[:, None]).astype(np.float32)
    bias = jnp.dot(rows.astype(F32).reshape(-1, nrel), jnp.asarray(onehot),
                   precision=lax.Precision.HIGHEST)
    bias = bias.reshape(NA_HEADS, kh, kh, GRID_W, GRID_W)
    bias = jnp.where(jnp.asarray(col_ok)[None, None, None, :, :], bias, NEG)
    bias = bias.transpose(1, 0, 3, 2, 4).reshape(kh, NA_HEADS // 2, 2 * GRID_W, kh * GRID_W)
    return bias


def _wb_bias_table(t5_bias):
    rel = (np.arange(3 * WB_BLOCK) - WB_BLOCK)[None, :] - np.arange(WB_BLOCK)[:, None]
    offs = np.arange(-(2 * WB_BLOCK - 1), 2 * WB_BLOCK)
    by_off = t5_bias[_t5_bucket(offs)].astype(F32).T
    bias = jnp.stack([by_off[:, WB_BLOCK - 1 - q:4 * WB_BLOCK - 1 - q] for q in range(WB_BLOCK)], axis=1)
    bias = jnp.where(jnp.asarray(np.abs(rel) <= WB_WINDOW)[None], bias, NEG)
    bias = bias[np.asarray(WB_HEAD_ORDER)]
    return bias.reshape(WB_HEADS // 2, 2 * WB_BLOCK, 3 * WB_BLOCK)


def _even_weights(w_in, w_out):
    qa, ka, va, qb, kb, vb, gate = jnp.split(
        w_in, [NA_W, 2 * NA_W, 3 * NA_W, 3 * NA_W + WB_QW, 3 * NA_W + WB_QW + WB_KVW,
               3 * NA_W + WB_QW + 2 * WB_KVW], axis=-1)
    order = np.asarray(WB_HEAD_ORDER)
    perm = (order[:, None] * HEAD_DIM + np.arange(HEAD_DIM)[None, :]).reshape(-1)
    qscale = HEAD_DIM ** -0.5
    gate = jnp.concatenate([gate[:, :NA_W], gate[:, NA_W:][:, perm]], axis=-1)
    w = jnp.concatenate([gate, qa * qscale, ka, va, qb[:, perm] * qscale, kb, vb], axis=-1)
    w_out_p = jnp.concatenate([w_out[:NA_W], w_out[NA_W:][perm]], axis=0)
    return w.astype(BF16), w_out_p.astype(BF16)


def _odd_weights(w_in, q_norm, w_qb, kv_norm, w_kvb, w_out):
    q_lat, kv_lat, k_rope, gate = jnp.split(
        w_in, [MLA_Q_RANK, MLA_Q_RANK + MLA_KV_RANK, MLA_Q_RANK + MLA_KV_RANK + MLA_ROPE], axis=-1)
    w1t = jnp.concatenate([gate, q_lat, kv_lat], axis=-1).T.astype(BF16)
    half = MLA_ROPE // 2
    k_rot = jnp.concatenate([-k_rope[:, half:], k_rope[:, :half]], axis=-1)
    z64 = jnp.zeros((D_MODEL, MLA_NOPE), w_in.dtype)
    z32 = jnp.zeros((D_MODEL, LANES - MLA_QK), w_in.dtype)
    wnat = jnp.concatenate([kv_lat, z64, k_rope, z32, z64, k_rot, z32], axis=-1).astype(BF16)
    wkv = w_kvb.reshape(MLA_KV_RANK, MLA_HEADS, MLA_NOPE + MLA_V)
    wk = jnp.concatenate([wkv[:, :, :MLA_NOPE], jnp.zeros_like(wkv[:, :, :MLA_NOPE])], axis=-1)
    wk = wk.reshape(MLA_KV_RANK, MLA_HEADS * LANES).astype(BF16)
    wvt = wkv[:, :, MLA_NOPE:].reshape(MLA_KV_RANK, ODD_MIX).T.astype(BF16)
    gq = jnp.broadcast_to(q_norm.astype(F32)[:, None], (MLA_Q_RANK, LANES))
    gkvt = jnp.broadcast_to(kv_norm.astype(F32)[:, None], (MLA_KV_RANK, LANES))
    gkv = kv_norm.astype(F32)[None, :]
    return w1t, wnat, gq, gkvt, gkv, w_qb.T.astype(BF16), wk, wvt, w_out.T.astype(BF16)


def _rope_tables(l):
    inv_freq = 1.0 / (ROPE_THETA ** (jnp.arange(0, MLA_ROPE, 2, dtype=F32) / MLA_ROPE))
    ang = jnp.arange(l, dtype=F32)[:, None] * inv_freq[None, :]
    cos, sin = jnp.cos(ang), jnp.sin(ang)
    z64 = jnp.zeros((l, MLA_NOPE), F32)
    z32 = jnp.zeros((l, LANES - MLA_QK), F32)
    cos_p = jnp.concatenate([z64, cos, cos, z32], axis=-1)
    sin_p = jnp.concatenate([z64, sin, sin, z32], axis=-1)
    return cos.T, sin.T, cos_p, sin_p


def _trunk(x, mod, norm_g, ev, na_tbl, wb_tbl, sink, od, final_g):
    b, l, _ = x.shape
    assert l % TM == 0 and l % NA_CHUNK == 0 and l % WB_CHUNK == 0 and l % MLA_TK == 0
    assert l // GRID_W >= NA_WIN_H + NA_ROWS_PER_STEP
    shift0, scale0, gate0 = [t[:, None, :] for t in jnp.split(mod[0], 3, axis=-1)]
    shift1, scale1, gate1 = [t[:, None, :] for t in jnp.split(mod[1], 3, axis=-1)]

    w_in0, w_out0 = ev
    u = _even_front(x, norm_g[0][None, :], scale0, shift0, w_in0)
    oa = _na_attention(u, na_tbl)
    ob = _wb_attention(u, wb_tbl, sink)
    x1 = _even_out(x, oa, ob, u, gate0, w_out0)

    w1t, wnat, gq, gkvt, gkv, wqbt, wk, wvt, w_out1t = od
    cos_t, sin_t, cos_p, sin_p = _rope_tables(l)
    k, qt, vt, sgt = _odd_front(x1, norm_g[1][None, :], scale1, shift1, w1t, wnat, gq, gkvt, gkv,
                                wqbt, wk, wvt, cos_t, sin_t, cos_p, sin_p)
    ot = _mla_attention(qt, k, vt)
    return _odd_out(x1, ot, sgt, gate1, w_out1t, final_g[None, :])


def kernel(x_prompt, x_sample, c_prompt, c_sample, ada_w, ada_b, norm_g, t5_bias, ev_w_in, na_rpb,
           wb_sink, ev_w_out, mla_w_in, mla_q_norm, mla_w_qb, mla_kv_norm, mla_w_kvb, mla_w_out, final_g):
    bp, bs = c_prompt.shape[0], c_sample.shape[0]
    rows = -(-(bp + bs) // 16) * 16
    c_pad = jnp.concatenate([c_prompt, c_sample, jnp.zeros((rows - bp - bs, D_MODEL), F32)], axis=0)
    mod = _ada_mod(c_pad, ada_w.astype(BF16), ada_b[:, None, :])

    ev = _even_weights(ev_w_in[0], ev_w_out[0])
    na_tbl = _na_bias_table(na_rpb[0])
    wb_tbl = _wb_bias_table(t5_bias)
    od = _odd_weights(mla_w_in[0], mla_q_norm[0], mla_w_qb[0], mla_kv_norm[0], mla_w_kvb[0], mla_w_out[0])
    sink = wb_sink[0].astype(F32)

    y_prompt = _trunk(x_prompt, mod[:, :bp], norm_g, ev, na_tbl, wb_tbl, sink, od, final_g)
    y_sample = _trunk(x_sample, mod[:, bp:bp + bs], norm_g, ev, na_tbl, wb_tbl, sink, od, final_g)
    return (y_prompt, y_sample)
```

```python
import functools
import math

import numpy as np
import jax
import jax.numpy as jnp
from jax import lax
from jax.experimental import pallas as pl
from jax.experimental.pallas import tpu as pltpu

D_MODEL = 1024
GRID_W = 64
HEAD_DIM = 64
EPS = 1e-6
NEG = -1e30
NA_HEADS = 8
NA_WIN_H = 8
NA_WIN_W = 16
WB_HEADS = 8
WB_KV_HEADS = 2
WB_WINDOW = 128
WB_BLOCK = 128
T5_BUCKETS = 32
T5_MAX_DIST = 128
MLA_HEADS = 16
MLA_Q_RANK = 256
MLA_KV_RANK = 128
MLA_NOPE = 64
MLA_ROPE = 32
MLA_V = 64
ROPE_THETA = 10000.0
MLA_QK = MLA_NOPE + MLA_ROPE

NA_W = NA_HEADS * HEAD_DIM
WB_QW = WB_HEADS * HEAD_DIM
WB_KVW = WB_KV_HEADS * HEAD_DIM
EVEN_MIX = NA_W + WB_QW
EVEN_IN = 3 * NA_W + WB_QW + 2 * WB_KVW + EVEN_MIX
ODD_MIX = MLA_HEADS * MLA_V

U_GATE = 0
U_QA = EVEN_MIX
U_KA = U_QA + NA_W
U_VA = U_KA + NA_W
U_QB = U_VA + NA_W
U_KB = U_QB + WB_QW
U_VB = U_KB + WB_KVW

LANES = 128
TM = 512
NA_ROWS_PER_STEP = 8
NA_CHUNK = NA_ROWS_PER_STEP * GRID_W
WB_CHUNK = 512
MLA_TQ = 1024
MLA_TK = 512
ONES_ROWS = 16
VMEM_LIMIT = 56 * 1024 * 1024

BF16 = jnp.bfloat16
F32 = jnp.float32
LOG2E = math.log2(math.e)

WB_HEAD_ORDER = (0, 4, 1, 5, 2, 6, 3, 7)


def _params(sem):
    return pltpu.CompilerParams(dimension_semantics=sem, vmem_limit_bytes=VMEM_LIMIT)


def _dot(a, b):
    return jnp.dot(a, b, preferred_element_type=F32)


def _dot_nt(a, b):
    return lax.dot_general(a, b, (((1,), (1,)), ((), ())), preferred_element_type=F32)


def _silu(x):
    return x * (1.0 / (1.0 + jnp.exp(-x)))


def _modulated_norm(x, g, scale, shift):
    ms = jnp.mean(x * x, axis=-1, keepdims=True)
    y = x * lax.rsqrt(ms + EPS)
    return (y * g) * (1.0 + scale) + shift


def _ada_kernel(c_ref, w_ref, b_ref, o_ref):
    c = c_ref[...]
    cs = _silu(c).astype(BF16)
    o_ref[0] = _dot(cs, w_ref[0]) + b_ref[0]


def _ada_mod(c_pad, ada_w, ada_b):
    depth = ada_w.shape[0]
    rows = c_pad.shape[0]
    tn = 768
    return pl.pallas_call(
        _ada_kernel,
        grid=(depth, 3 * D_MODEL // tn),
        in_specs=[
            pl.BlockSpec((rows, D_MODEL), lambda i, n: (0, 0)),
            pl.BlockSpec((1, D_MODEL, tn), lambda i, n: (i, 0, n)),
            pl.BlockSpec((1, 1, tn), lambda i, n: (i, 0, n)),
        ],
        out_specs=pl.BlockSpec((1, rows, tn), lambda i, n: (i, 0, n)),
        out_shape=jax.ShapeDtypeStruct((depth, rows, 3 * D_MODEL), F32),
        compiler_params=_params(("arbitrary", "arbitrary")),
        name="ada_mod",
    )(c_pad, ada_w, ada_b)


def _even_front_kernel(x_ref, g_ref, sc_ref, sh_ref, w_ref, u_ref):
    h = _modulated_norm(x_ref[0], g_ref[...], sc_ref[0], sh_ref[0]).astype(BF16)
    u_ref[0] = _dot(h, w_ref[...]).astype(BF16)


def _even_front(x, g, scale, shift, w):
    b, l, _ = x.shape
    return pl.pallas_call(
        _even_front_kernel,
        grid=(b, l // TM),
        in_specs=[
            pl.BlockSpec((1, TM, D_MODEL), lambda bi, i: (bi, i, 0)),
            pl.BlockSpec((1, D_MODEL), lambda bi, i: (0, 0)),
            pl.BlockSpec((1, 1, D_MODEL), lambda bi, i: (bi, 0, 0)),
            pl.BlockSpec((1, 1, D_MODEL), lambda bi, i: (bi, 0, 0)),
            pl.BlockSpec((D_MODEL, EVEN_IN), lambda bi, i: (0, 0)),
        ],
        out_specs=pl.BlockSpec((1, TM, EVEN_IN), lambda bi, i: (bi, i, 0)),
        out_shape=jax.ShapeDtypeStruct((b, l, EVEN_IN), BF16),
        compiler_params=_params(("arbitrary", "arbitrary")),
        name="even_front",
    )(x, g, scale, shift, w)


def _low_lane_mask(shape):
    return lax.broadcasted_iota(jnp.int32, shape, len(shape) - 1) < HEAD_DIM


def _na_kernel(q_ref, kp_ref, kc_ref, kn_ref, vp_ref, vc_ref, vn_ref, bias_ref, o_ref,
               kwin, vwin, *, rows):
    ci = pl.program_id(1)
    kwin[0:NA_CHUNK] = kp_ref[0]
    kwin[NA_CHUNK:2 * NA_CHUNK] = kc_ref[0]
    kwin[2 * NA_CHUNK:3 * NA_CHUNK] = kn_ref[0]
    vwin[0:NA_CHUNK] = vp_ref[0]
    vwin[NA_CHUNK:2 * NA_CHUNK] = vc_ref[0]
    vwin[2 * NA_CHUNK:3 * NA_CHUNK] = vn_ref[0]

    low = _low_lane_mask((GRID_W, LANES))
    kh = NA_WIN_H
    r0 = ci * NA_ROWS_PER_STEP

    def row_body(i, carry):
        r = r0 + i
        row_start = jnp.clip(r - kh // 2, 0, rows - kh)
        variant = r - row_start
        start = pl.multiple_of((row_start - r0 + NA_ROWS_PER_STEP) * GRID_W, GRID_W)
        qoff = pl.multiple_of(i * GRID_W, GRID_W)
        for j in range(NA_HEADS // 2):
            lanes = slice(j * LANES, (j + 1) * LANES)
            q2 = q_ref[0, pl.ds(qoff, GRID_W), lanes]
            zero = jnp.zeros_like(q2)
            lhs = jnp.concatenate([jnp.where(low, q2, zero), jnp.where(low, zero, q2)], axis=0)
            kw = kwin[pl.ds(start, kh * GRID_W), lanes]
            vw = vwin[pl.ds(start, kh * GRID_W), lanes]
            s = _dot_nt(lhs, kw)
            s = s + bias_ref[variant, j]
            m = jnp.max(s, axis=-1, keepdims=True)
            p = jnp.exp(s - m)
            den = jnp.sum(p, axis=-1, keepdims=True)
            o = _dot(p.astype(BF16), vw) / den
            o_ref[0, pl.ds(qoff, GRID_W), lanes] = jnp.where(low, o[:GRID_W], o[GRID_W:]).astype(BF16)
        return carry

    lax.fori_loop(0, NA_ROWS_PER_STEP, row_body, 0)


def _na_attention(u, bias_tbl):
    b, l, _ = u.shape
    rows = l // GRID_W
    nchunk = l // NA_CHUNK
    qblk, kblk, vblk = U_QA // NA_W, U_KA // NA_W, U_VA // NA_W

    def halo(col, d):
        return pl.BlockSpec((1, NA_CHUNK, NA_W),
                            lambda bi, i: (bi, jnp.clip(i + d, 0, nchunk - 1), col))

    return pl.pallas_call(
        functools.partial(_na_kernel, rows=rows),
        grid=(b, nchunk),
        in_specs=[
            pl.BlockSpec((1, NA_CHUNK, NA_W), lambda bi, i: (bi, i, qblk)),
            halo(kblk, -1), halo(kblk, 0), halo(kblk, 1),
            halo(vblk, -1), halo(vblk, 0), halo(vblk, 1),
            pl.BlockSpec(bias_tbl.shape, lambda bi, i: (0, 0, 0, 0)),
        ],
        out_specs=pl.BlockSpec((1, NA_CHUNK, NA_W), lambda bi, i: (bi, i, 0)),
        out_shape=jax.ShapeDtypeStruct((b, l, NA_W), BF16),
        scratch_shapes=[pltpu.VMEM((3 * NA_CHUNK, NA_W), BF16),
                        pltpu.VMEM((3 * NA_CHUNK, NA_W), BF16)],
        compiler_params=_params(("arbitrary", "arbitrary")),
        name="na_attention",
    )(u, u, u, u, u, u, u, bias_tbl)


def _wb_kernel(sink_ref, q_ref, kp_ref, kc_ref, kn_ref, vp_ref, vc_ref, vn_ref, bias_ref, o_ref,
               kwin, vwin, *, nblk):
    ci = pl.program_id(1)
    kwin[0:WB_BLOCK] = kp_ref[0]
    kwin[WB_BLOCK:WB_BLOCK + WB_CHUNK] = kc_ref[0]
    kwin[WB_BLOCK + WB_CHUNK:2 * WB_BLOCK + WB_CHUNK] = kn_ref[0]
    vwin[0:WB_BLOCK] = vp_ref[0]
    vwin[WB_BLOCK:WB_BLOCK + WB_CHUNK] = vc_ref[0]
    vwin[WB_BLOCK + WB_CHUNK:2 * WB_BLOCK + WB_CHUNK] = vn_ref[0]

    low = _low_lane_mask((WB_BLOCK, LANES))
    col = lax.broadcasted_iota(jnp.int32, (2 * WB_BLOCK, 3 * WB_BLOCK), 1)
    row = lax.broadcasted_iota(jnp.int32, (2 * WB_BLOCK, 1), 0)
    per_step = WB_CHUNK // WB_BLOCK
    for n in range(per_step):
        gblk = ci * per_step + n
        lo = jnp.where(gblk > 0, 0, WB_BLOCK)
        hi = jnp.where(gblk < nblk - 1, 3 * WB_BLOCK, 2 * WB_BLOCK)
        in_seq = jnp.logical_and(col >= lo, col < hi)
        kw = kwin[n * WB_BLOCK:(n + 3) * WB_BLOCK]
        vw = vwin[n * WB_BLOCK:(n + 3) * WB_BLOCK]
        for j in range(WB_HEADS // 2):
            lanes = slice(j * LANES, (j + 1) * LANES)
            q2 = q_ref[0, n * WB_BLOCK:(n + 1) * WB_BLOCK, lanes]
            zero = jnp.zeros_like(q2)
            lhs = jnp.concatenate([jnp.where(low, q2, zero), jnp.where(low, zero, q2)], axis=0)
            s = _dot_nt(lhs, kw)
            s = jnp.where(in_seq, s + bias_ref[j], NEG)
            sink = jnp.where(row < WB_BLOCK, sink_ref[WB_HEAD_ORDER[2 * j]],
                             sink_ref[WB_HEAD_ORDER[2 * j + 1]])
            m = jnp.maximum(jnp.max(s, axis=-1, keepdims=True), sink)
            p = jnp.exp(s - m)
            den = jnp.sum(p, axis=-1, keepdims=True) + jnp.exp(sink - m)
            o = _dot(p.astype(BF16), vw) / den
            o_ref[0, n * WB_BLOCK:(n + 1) * WB_BLOCK, lanes] = (
                jnp.where(low, o[:WB_BLOCK], o[WB_BLOCK:]).astype(BF16))


def _wb_attention(u, bias_tbl, sink):
    b, l, _ = u.shape
    nblk = l // WB_BLOCK
    nchunk = l // WB_CHUNK
    per_step = WB_CHUNK // WB_BLOCK
    qblk = U_QB // WB_QW
    kcol, vcol = U_KB // WB_KVW, U_VB // WB_KVW

    def halo(colblk, d):
        if d == 0:
            return pl.BlockSpec((1, WB_CHUNK, WB_KVW), lambda bi, i: (bi, i, colblk))
        return pl.BlockSpec(
            (1, WB_BLOCK, WB_KVW),
            lambda bi, i: (bi, jnp.clip(i * per_step + (per_step if d > 0 else -1), 0, nblk - 1), colblk))

    return pl.pallas_call(
        functools.partial(_wb_kernel, nblk=nblk),
        grid=(b, nchunk),
        in_specs=[
            pl.BlockSpec(memory_space=pltpu.SMEM),
            pl.BlockSpec((1, WB_CHUNK, WB_QW), lambda bi, i: (bi, i, qblk)),
            halo(kcol, -1), halo(kcol, 0), halo(kcol, 1),
            halo(vcol, -1), halo(vcol, 0), halo(vcol, 1),
            pl.BlockSpec(bias_tbl.shape, lambda bi, i: (0, 0, 0)),
        ],
        out_specs=pl.BlockSpec((1, WB_CHUNK, WB_QW), lambda bi, i: (bi, i, 0)),
        out_shape=jax.ShapeDtypeStruct((b, l, WB_QW), BF16),
        scratch_shapes=[pltpu.VMEM((WB_CHUNK + 2 * WB_BLOCK, WB_KVW), BF16),
                        pltpu.VMEM((WB_CHUNK + 2 * WB_BLOCK, WB_KVW), BF16)],
        compiler_params=_params(("arbitrary", "arbitrary")),
        name="wb_attention",
    )(sink, u, u, u, u, u, u, u, bias_tbl)


def _even_out_kernel(x_ref, oa_ref, ob_ref, gate_ref, gm_ref, w_ref, o_ref):
    g = _silu(gate_ref[0].astype(F32))
    o = jnp.concatenate([oa_ref[0], ob_ref[0]], axis=-1).astype(F32)
    z = (o * g).astype(BF16)
    o_ref[0] = x_ref[0] + gm_ref[0] * _dot(z, w_ref[...])


def _even_out(x, oa, ob, u, gate_mod, w_out):
    b, l, _ = x.shape
    return pl.pallas_call(
        _even_out_kernel,
        grid=(b, l // TM),
        in_specs=[
            pl.BlockSpec((1, TM, D_MODEL), lambda bi, i: (bi, i, 0)),
            pl.BlockSpec((1, TM, NA_W), lambda bi, i: (bi, i, 0)),
            pl.BlockSpec((1, TM, WB_QW), lambda bi, i: (bi, i, 0)),
            pl.BlockSpec((1, TM, EVEN_MIX), lambda bi, i: (bi, i, U_GATE // EVEN_MIX)),
            pl.BlockSpec((1, 1, D_MODEL), lambda bi, i: (bi, 0, 0)),
            pl.BlockSpec((EVEN_MIX, D_MODEL), lambda bi, i: (0, 0)),
        ],
        out_specs=pl.BlockSpec((1, TM, D_MODEL), lambda bi, i: (bi, i, 0)),
        out_shape=jax.ShapeDtypeStruct((b, l, D_MODEL), F32),
        compiler_params=_params(("arbitrary", "arbitrary")),
        name="even_out",
    )(x, oa, ob, u, gate_mod, w_out)


def _odd_front_kernel(x_ref, g_ref, sc_ref, sh_ref, w1t_ref, wnat_ref, gq_ref, gkvt_ref, gkv_ref,
                      wqbt_ref, wk_ref, wvt_ref, cost_ref, sint_ref, cosp_ref, sinp_ref,
                      k_ref, qt_ref, vt_ref, sgt_ref):
    h = _modulated_norm(x_ref[0], g_ref[...], sc_ref[0], sh_ref[0]).astype(BF16)
    reps = TM // LANES

    ut = _dot_nt(w1t_ref[...], h)
    sgt_ref[0] = _silu(ut[0:ODD_MIX]).astype(BF16)

    qlt = ut[ODD_MIX:ODD_MIX + MLA_Q_RANK]
    qn = qlt * lax.rsqrt(jnp.mean(qlt * qlt, axis=0, keepdims=True) + EPS)
    qn = (qn * jnp.concatenate([gq_ref[...]] * reps, axis=1)).astype(BF16)
    qt = _dot(wqbt_ref[...], qn)
    qscale = (MLA_QK ** -0.5) * LOG2E
    cos_t = cost_ref[...]
    sin_t = sint_ref[...]
    half = MLA_ROPE // 2
    for hd in range(MLA_HEADS):
        base = hd * MLA_QK
        qt_ref[0, base:base + MLA_NOPE, :] = (qt[base:base + MLA_NOPE] * qscale).astype(BF16)
        x1 = qt[base + MLA_NOPE:base + MLA_NOPE + half]
        x2 = qt[base + MLA_NOPE + half:base + MLA_QK]
        qt_ref[0, base + MLA_NOPE:base + MLA_NOPE + half, :] = (
            (x1 * cos_t - x2 * sin_t) * qscale).astype(BF16)
        qt_ref[0, base + MLA_NOPE + half:base + MLA_QK, :] = (
            (x2 * cos_t + x1 * sin_t) * qscale).astype(BF16)

    kvt = ut[ODD_MIX + MLA_Q_RANK:ODD_MIX + MLA_Q_RANK + MLA_KV_RANK]
    kvnt = kvt * lax.rsqrt(jnp.mean(kvt * kvt, axis=0, keepdims=True) + EPS)
    kvnt = (kvnt * jnp.concatenate([gkvt_ref[...]] * reps, axis=1)).astype(BF16)
    vt = _dot(wvt_ref[...], kvnt)
    vt_ref[0, :, 0] = vt.reshape(MLA_HEADS, MLA_V, TM).astype(BF16)

    nat = _dot(h, wnat_ref[...])
    kvl = nat[:, 0:MLA_KV_RANK]
    kvn = kvl * lax.rsqrt(jnp.mean(kvl * kvl, axis=-1, keepdims=True) + EPS)
    kvn = (kvn * gkv_ref[...]).astype(BF16)
    kn = _dot(kvn, wk_ref[...])
    kpe = nat[:, LANES:2 * LANES] * cosp_ref[...] + nat[:, 2 * LANES:3 * LANES] * sinp_ref[...]
    for hd in range(MLA_HEADS):
        kh = kn[:, hd * LANES:(hd + 1) * LANES] + kpe
        k_ref[0, hd] = kh[:, 0:MLA_QK].astype(BF16)


def _odd_front(x, g, scale, shift, w1t, wnat, gq, gkvt, gkv, wqbt, wk, wvt, cos_t, sin_t, cos_p, sin_p):
    b, l, _ = x.shape
    nt = l // TM
    const2 = lambda bi, i: (0, 0)
    n1 = w1t.shape[0]
    return pl.pallas_call(
        _odd_front_kernel,
        grid=(b, nt),
        in_specs=[
            pl.BlockSpec((1, TM, D_MODEL), lambda bi, i: (bi, i, 0)),
            pl.BlockSpec((1, D_MODEL), const2),
            pl.BlockSpec((1, 1, D_MODEL), lambda bi, i: (bi, 0, 0)),
            pl.BlockSpec((1, 1, D_MODEL), lambda bi, i: (bi, 0, 0)),
            pl.BlockSpec((n1, D_MODEL), const2),
            pl.BlockSpec((D_MODEL, 3 * LANES), const2),
            pl.BlockSpec((MLA_Q_RANK, LANES), const2),
            pl.BlockSpec((MLA_KV_RANK, LANES), const2),
            pl.BlockSpec((1, MLA_KV_RANK), const2),
            pl.BlockSpec((MLA_HEADS * MLA_QK, MLA_Q_RANK), const2),
            pl.BlockSpec((MLA_KV_RANK, MLA_HEADS * LANES), const2),
            pl.BlockSpec((ODD_MIX, MLA_KV_RANK), const2),
            pl.BlockSpec((MLA_ROPE // 2, TM), lambda bi, i: (0, i)),
            pl.BlockSpec((MLA_ROPE // 2, TM), lambda bi, i: (0, i)),
            pl.BlockSpec((TM, LANES), lambda bi, i: (i, 0)),
            pl.BlockSpec((TM, LANES), lambda bi, i: (i, 0)),
        ],
        out_specs=[
            pl.BlockSpec((1, MLA_HEADS, TM, MLA_QK), lambda bi, i: (bi, 0, i, 0)),
            pl.BlockSpec((1, MLA_HEADS * MLA_QK, TM), lambda bi, i: (bi, 0, i)),
            pl.BlockSpec((1, MLA_HEADS, 1, MLA_V, TM), lambda bi, i: (bi, 0, i, 0, 0)),
            pl.BlockSpec((1, ODD_MIX, TM), lambda bi, i: (bi, 0, i)),
        ],
        out_shape=[
            jax.ShapeDtypeStruct((b, MLA_HEADS, l, MLA_QK), BF16),
            jax.ShapeDtypeStruct((b, MLA_HEADS * MLA_QK, l), BF16),
            jax.ShapeDtypeStruct((b, MLA_HEADS, nt, MLA_V, TM), BF16),
            jax.ShapeDtypeStruct((b, ODD_MIX, l), BF16),
        ],
        compiler_params=_params(("arbitrary", "arbitrary")),
        name="odd_front",
    )(x, g, scale, shift, w1t, wnat, gq, gkvt, gkv, wqbt, wk, wvt, cos_t, sin_t, cos_p, sin_p)


def _mla_kernel(qt_ref, k_ref, vt_ref, o_ref, s_scr, p_scr, acc_scr, *, nk):
    q = qt_ref[0]
    ones = jnp.ones((ONES_ROWS, MLA_TK), BF16)

    def scores(t, slot):
        koff = pl.multiple_of(t * MLA_TK, MLA_TK)
        s = _dot(k_ref[0, 0, pl.ds(koff, MLA_TK), :], q)
        s_scr[slot] = s
        return jnp.max(s, axis=0, keepdims=True)

    def probs(slot, m, mt):
        m_new = jnp.maximum(m, mt)
        p_scr[slot] = jnp.exp2(s_scr[slot] - m_new).astype(BF16)
        return m_new, jnp.exp2(m - m_new)

    def accumulate(t, slot, alpha):
        v1 = jnp.concatenate([vt_ref[0, 0, t], ones], axis=0)
        acc_scr[...] = acc_scr[...] * alpha + _dot(v1, p_scr[slot])

    def scores_pair(u):
        return scores(2 * u, 0), scores(2 * u + 1, 1)

    def probs_pair(m, mts):
        m, al0 = probs(0, m, mts[0])
        m, al1 = probs(1, m, mts[1])
        return m, (al0, al1)

    def accumulate_pair(u, als):
        accumulate(2 * u, 0, als[0])
        accumulate(2 * u + 1, 1, als[1])

    npairs = nk // 2
    acc_scr[...] = jnp.zeros_like(acc_scr)
    m = jnp.full((1, MLA_TQ), NEG, F32)
    mts = scores_pair(0)
    m, als = probs_pair(m, mts)
    mts = scores_pair(1)

    def body(u, carry):
        m, mts, als = carry
        accumulate_pair(u - 2, als)
        m, als = probs_pair(m, mts)
        mts = scores_pair(u)
        return m, mts, als

    if npairs <= 4:
        carry = (m, mts, als)
        for u in range(2, npairs):
            carry = body(u, carry)
        m, mts, als = carry
    else:
        m, mts, als = lax.fori_loop(2, npairs, body, (m, mts, als))
    accumulate_pair(npairs - 2, als)
    m, als = probs_pair(m, mts)
    accumulate_pair(npairs - 1, als)
    acc = acc_scr[...]
    o_ref[0] = (acc[0:MLA_V] / acc[MLA_V:MLA_V + 1]).astype(BF16)


def _mla_attention(qt, k, vt):
    b, _, l = qt.shape
    nk = l // MLA_TK
    assert nk >= 4 and nk % 2 == 0
    return pl.pallas_call(
        functools.partial(_mla_kernel, nk=nk),
        grid=(b, MLA_HEADS, l // MLA_TQ),
        in_specs=[
            pl.BlockSpec((1, MLA_QK, MLA_TQ), lambda bi, h, i: (bi, h, i)),
            pl.BlockSpec((1, 1, l, MLA_QK), lambda bi, h, i: (bi, h, 0, 0)),
            pl.BlockSpec((1, 1, nk, MLA_V, MLA_TK), lambda bi, h, i: (bi, h, 0, 0, 0)),
        ],
        out_specs=pl.BlockSpec((1, MLA_V, MLA_TQ), lambda bi, h, i: (bi, h, i)),
        out_shape=jax.ShapeDtypeStruct((b, ODD_MIX, l), BF16),
        scratch_shapes=[pltpu.VMEM((2, MLA_TK, MLA_TQ), F32),
                        pltpu.VMEM((2, MLA_TK, MLA_TQ), BF16),
                        pltpu.VMEM((MLA_V + ONES_ROWS, MLA_TQ), F32)],
        compiler_params=_params(("arbitrary", "arbitrary", "arbitrary")),
        name="mla_attention",
    )(qt, k, vt)


def _odd_out_kernel(x_ref, ot_ref, sgt_ref, gm_ref, wt_ref, fg_ref, y_ref):
    z = (ot_ref[0].astype(F32) * sgt_ref[0].astype(F32)).astype(BF16)
    out = _dot(wt_ref[...], z).T
    x2 = x_ref[0] + gm_ref[0] * out
    ms = jnp.mean(x2 * x2, axis=-1, keepdims=True)
    y_ref[0] = (x2 * lax.rsqrt(ms + EPS)) * fg_ref[...]


def _odd_out(x, ot, sgt, gate_mod, w_out_t, final_g):
    b, l, _ = x.shape
    return pl.pallas_call(
        _odd_out_kernel,
        grid=(b, l // TM),
        in_specs=[
            pl.BlockSpec((1, TM, D_MODEL), lambda bi, i: (bi, i, 0)),
            pl.BlockSpec((1, ODD_MIX, TM), lambda bi, i: (bi, 0, i)),
            pl.BlockSpec((1, ODD_MIX, TM), lambda bi, i: (bi, 0, i)),
            pl.BlockSpec((1, 1, D_MODEL), lambda bi, i: (bi, 0, 0)),
            pl.BlockSpec((D_MODEL, ODD_MIX), lambda bi, i: (0, 0)),
            pl.BlockSpec((1, D_MODEL), lambda bi, i: (0, 0)),
        ],
        out_specs=pl.BlockSpec((1, TM, D_MODEL), lambda bi, i: (bi, i, 0)),
        out_shape=jax.ShapeDtypeStruct((b, l, D_MODEL), F32),
        compiler_params=_params(("arbitrary", "arbitrary")),
        name="odd_out",
    )(x, ot, sgt, gate_mod, w_out_t, final_g)


def _t5_bucket(rel):
    nb = T5_BUCKETS // 2
    ret = (rel > 0).astype(np.int32) * nb
    n = np.abs(rel)
    max_exact = nb // 2
    large = max_exact + (np.log(np.maximum(n, 1) / max_exact)
                         / np.log(T5_MAX_DIST / max_exact) * (nb - max_exact)).astype(np.int32)
    large = np.minimum(large, nb - 1)
    return ret + np.where(n < max_exact, n, large)


def _na_bias_table(rpb):
    kh = NA_WIN_H
    c = np.arange(GRID_W)
    col_start = np.clip(c - NA_WIN_W // 2, 0, GRID_W - NA_WIN_W)
    col_ok = (c[None, :] >= col_start[:, None]) & (c[None, :] < col_start[:, None] + NA_WIN_W)
    d_col = np.clip(c[None, :] - c[:, None], -(NA_WIN_W - 1), NA_WIN_W - 1) + NA_WIN_W - 1
    nrel = 2 * NA_WIN_W - 1
    rows = jnp.stack([rpb[:, NA_WIN_H - 1 - v:2 * NA_WIN_H - 1 - v] for v in range(kh)], axis=1)
    onehot = (d_col.reshape(-1)[None, :] == np.arange(nrel)[:, None]).astype(np.float32)
    bias = jnp.dot(rows.astype(F32).reshape(-1, nrel), jnp.asarray(onehot),
                   precision=lax.Precision.HIGHEST)
    bias = bias.reshape(NA_HEADS, kh, kh, GRID_W, GRID_W)
    bias = jnp.where(jnp.asarray(col_ok)[None, None, None, :, :], bias, NEG)
    bias = bias.transpose(1, 0, 3, 2, 4).reshape(kh, NA_HEADS // 2, 2 * GRID_W, kh * GRID_W)
    return bias


def _wb_bias_table(t5_bias):
    rel = (np.arange(3 * WB_BLOCK) - WB_BLOCK)[None, :] - np.arange(WB_BLOCK)[:, None]
    offs = np.arange(-(2 * WB_BLOCK - 1), 2 * WB_BLOCK)
    by_off = t5_bias[_t5_bucket(offs)].astype(F32).T
    bias = jnp.stack([by_off[:, WB_BLOCK - 1 - q:4 * WB_BLOCK - 1 - q] for q in range(WB_BLOCK)], axis=1)
    bias = jnp.where(jnp.asarray(np.abs(rel) <= WB_WINDOW)[None], bias, NEG)
    bias = bias[np.asarray(WB_HEAD_ORDER)]
    return bias.reshape(WB_HEADS // 2, 2 * WB_BLOCK, 3 * WB_BLOCK)


def _even_weights(w_in, w_out):
    qa, ka, va, qb, kb, vb, gate = jnp.split(
        w_in, [NA_W, 2 * NA_W, 3 * NA_W, 3 * NA_W + WB_QW, 3 * NA_W + WB_QW + WB_KVW,
               3 * NA_W + WB_QW + 2 * WB_KVW], axis=-1)
    order = np.asarray(WB_HEAD_ORDER)
    perm = (order[:, None] * HEAD_DIM + np.arange(HEAD_DIM)[None, :]).reshape(-1)
    qscale = HEAD_DIM ** -0.5
    gate = jnp.concatenate([gate[:, :NA_W], gate[:, NA_W:][:, perm]], axis=-1)
    w = jnp.concatenate([gate, qa * qscale, ka, va, qb[:, perm] * qscale, kb, vb], axis=-1)
    w_out_p = jnp.concatenate([w_out[:NA_W], w_out[NA_W:][perm]], axis=0)
    return w.astype(BF16), w_out_p.astype(BF16)


def _odd_weights(w_in, q_norm, w_qb, kv_norm, w_kvb, w_out):
    q_lat, kv_lat, k_rope, gate = jnp.split(
        w_in, [MLA_Q_RANK, MLA_Q_RANK + MLA_KV_RANK, MLA_Q_RANK + MLA_KV_RANK + MLA_ROPE], axis=-1)
    w1t = jnp.concatenate([gate, q_lat, kv_lat], axis=-1).T.astype(BF16)
    half = MLA_ROPE // 2
    k_rot = jnp.concatenate([-k_rope[:, half:], k_rope[:, :half]], axis=-1)
    z64 = jnp.zeros((D_MODEL, MLA_NOPE), w_in.dtype)
    z32 = jnp.zeros((D_MODEL, LANES - MLA_QK), w_in.dtype)
    wnat = jnp.concatenate([kv_lat, z64, k_rope, z32, z64, k_rot, z32], axis=-1).astype(BF16)
    wkv = w_kvb.reshape(MLA_KV_RANK, MLA_HEADS, MLA_NOPE + MLA_V)
    wk = jnp.concatenate([wkv[:, :, :MLA_NOPE], jnp.zeros_like(wkv[:, :, :MLA_NOPE])], axis=-1)
    wk = wk.reshape(MLA_KV_RANK, MLA_HEADS * LANES).astype(BF16)
    wvt = wkv[:, :, MLA_NOPE:].reshape(MLA_KV_RANK, ODD_MIX).T.astype(BF16)
    gq = jnp.broadcast_to(q_norm.astype(F32)[:, None], (MLA_Q_RANK, LANES))
    gkvt = jnp.broadcast_to(kv_norm.astype(F32)[:, None], (MLA_KV_RANK, LANES))
    gkv = kv_norm.astype(F32)[None, :]
    return w1t, wnat, gq, gkvt, gkv, w_qb.T.astype(BF16), wk, wvt, w_out.T.astype(BF16)


def _rope_tables(l):
    inv_freq = 1.0 / (ROPE_THETA ** (jnp.arange(0, MLA_ROPE, 2, dtype=F32) / MLA_ROPE))
    ang = jnp.arange(l, dtype=F32)[:, None] * inv_freq[None, :]
    cos, sin = jnp.cos(ang), jnp.sin(ang)
    z64 = jnp.zeros((l, MLA_NOPE), F32)
    z32 = jnp.zeros((l, LANES - MLA_QK), F32)
    cos_p = jnp.concatenate([z64, cos, cos, z32], axis=-1)
    sin_p = jnp.concatenate([z64, sin, sin, z32], axis=-1)
    return cos.T, sin.T, cos_p, sin_p


def _trunk(x, mod, norm_g, ev, na_tbl, wb_tbl, sink, od, final_g):
    b, l, _ = x.shape
    assert l % TM == 0 and l % NA_CHUNK == 0 and l % WB_CHUNK == 0 and l % MLA_TK == 0
    assert l // GRID_W >= NA_WIN_H + NA_ROWS_PER_STEP
    shift0, scale0, gate0 = [t[:, None, :] for t in jnp.split(mod[0], 3, axis=-1)]
    shift1, scale1, gate1 = [t[:, None, :] for t in jnp.split(mod[1], 3, axis=-1)]

    w_in0, w_out0 = ev
    u = _even_front(x, norm_g[0][None, :], scale0, shift0, w_in0)
    oa = _na_attention(u, na_tbl)
    ob = _wb_attention(u, wb_tbl, sink)
    x1 = _even_out(x, oa, ob, u, gate0, w_out0)

    w1t, wnat, gq, gkvt, gkv, wqbt, wk, wvt, w_out1t = od
    cos_t, sin_t, cos_p, sin_p = _rope_tables(l)
    k, qt, vt, sgt = _odd_front(x1, norm_g[1][None, :], scale1, shift1, w1t, wnat, gq, gkvt, gkv,
                                wqbt, wk, wvt, cos_t, sin_t, cos_p, sin_p)
    ot = _mla_attention(qt, k, vt)
    return _odd_out(x1, ot, sgt, gate1, w_out1t, final_g[None, :])


def kernel(x_prompt, x_sample, c_prompt, c_sample, ada_w, ada_b, norm_g, t5_bias, ev_w_in, na_rpb,
           wb_sink, ev_w_out, mla_w_in, mla_q_norm, mla_w_qb, mla_kv_norm, mla_w_kvb, mla_w_out, final_g):
    bp, bs = c_prompt.shape[0], c_sample.shape[0]
    rows = -(-(bp + bs) // 16) * 16
    c_pad = jnp.concatenate([c_prompt, c_sample, jnp.zeros((rows - bp - bs, D_MODEL), F32)], axis=0)
    mod = _ada_mod(c_pad, ada_w.astype(BF16), ada_b[:, None, :])

    ev = _even_weights(ev_w_in[0], ev_w_out[0])
    na_tbl = _na_bias_table(na_rpb[0])
    wb_tbl = _wb_bias_table(t5_bias)
    od = _odd_weights(mla_w_in[0], mla_q_norm[0], mla_w_qb[0], mla_kv_norm[0], mla_w_kvb[0], mla_w_out[0])
    sink = wb_sink[0].astype(F32)

    y_prompt = _trunk(x_prompt, mod[:, :bp], norm_g, ev, na_tbl, wb_tbl, sink, od, final_g)
    y_sample = _trunk(x_sample, mod[:, bp:bp + bs], norm_g, ev, na_tbl, wb_tbl, sink, od, final_g)
    return (y_prompt, y_sample)
```

```python
import functools
import math

import numpy as np
import jax
import jax.numpy as jnp
from jax import lax
from jax.experimental import pallas as pl
from jax.experimental.pallas import tpu as pltpu

D_MODEL = 1024
GRID_W = 64
HEAD_DIM = 64
EPS = 1e-6
NEG = -1e30
NA_HEADS = 8
NA_WIN_H = 8
NA_WIN_W = 16
WB_HEADS = 8
WB_KV_HEADS = 2
WB_WINDOW = 128
WB_BLOCK = 128
T5_BUCKETS = 32
T5_MAX_DIST = 128
MLA_HEADS = 16
MLA_Q_RANK = 256
MLA_KV_RANK = 128
MLA_NOPE = 64
MLA_ROPE = 32
MLA_V = 64
ROPE_THETA = 10000.0
MLA_QK = MLA_NOPE + MLA_ROPE

NA_W = NA_HEADS * HEAD_DIM
WB_QW = WB_HEADS * HEAD_DIM
WB_KVW = WB_KV_HEADS * HEAD_DIM
EVEN_MIX = NA_W + WB_QW
EVEN_IN = 3 * NA_W + WB_QW + 2 * WB_KVW + EVEN_MIX
ODD_MIX = MLA_HEADS * MLA_V

U_GATE = 0
U_QA = EVEN_MIX
U_KA = U_QA + NA_W
U_VA = U_KA + NA_W
U_QB = U_VA + NA_W
U_KB = U_QB + WB_QW
U_VB = U_KB + WB_KVW

LANES = 128
TM = 512
NA_ROWS_PER_STEP = 8
NA_CHUNK = NA_ROWS_PER_STEP * GRID_W
WB_CHUNK = 512
MLA_TQ = 1024
MLA_TK = 512
MLA_RESIDENT_TOKENS = 16384
ONES_ROWS = 16
VMEM_LIMIT = 56 * 1024 * 1024

BF16 = jnp.bfloat16
F32 = jnp.float32
LOG2E = math.log2(math.e)

WB_HEAD_ORDER = (0, 4, 1, 5, 2, 6, 3, 7)


def _params(sem):
    return pltpu.CompilerParams(dimension_semantics=sem, vmem_limit_bytes=VMEM_LIMIT)


def _dot(a, b):
    return jnp.dot(a, b, preferred_element_type=F32)


def _dot_nt(a, b):
    return lax.dot_general(a, b, (((1,), (1,)), ((), ())), preferred_element_type=F32)


def _silu(x):
    return x * (1.0 / (1.0 + jnp.exp(-x)))


def _modulated_norm(x, g, scale, shift):
    ms = jnp.mean(x * x, axis=-1, keepdims=True)
    y = x * lax.rsqrt(ms + EPS)
    return (y * g) * (1.0 + scale) + shift


def _ada_kernel(c_ref, w_ref, b_ref, o_ref):
    c = c_ref[...]
    cs = _silu(c).astype(BF16)
    o_ref[0] = _dot(cs, w_ref[0]) + b_ref[0]


def _ada_mod(c_pad, ada_w, ada_b):
    depth = ada_w.shape[0]
    rows = c_pad.shape[0]
    tn = 768
    return pl.pallas_call(
        _ada_kernel,
        grid=(depth, 3 * D_MODEL // tn),
        in_specs=[
            pl.BlockSpec((rows, D_MODEL), lambda i, n: (0, 0)),
            pl.BlockSpec((1, D_MODEL, tn), lambda i, n: (i, 0, n)),
            pl.BlockSpec((1, 1, tn), lambda i, n: (i, 0, n)),
        ],
        out_specs=pl.BlockSpec((1, rows, tn), lambda i, n: (i, 0, n)),
        out_shape=jax.ShapeDtypeStruct((depth, rows, 3 * D_MODEL), F32),
        compiler_params=_params(("arbitrary", "arbitrary")),
        name="ada_mod",
    )(c_pad, ada_w, ada_b)


def _even_front_kernel(x_ref, g_ref, sc_ref, sh_ref, w_ref, u_ref):
    h = _modulated_norm(x_ref[0], g_ref[...], sc_ref[0], sh_ref[0]).astype(BF16)
    u_ref[0] = _dot(h, w_ref[...]).astype(BF16)


def _even_front(x, g, scale, shift, w):
    b, l, _ = x.shape
    return pl.pallas_call(
        _even_front_kernel,
        grid=(b, l // TM),
        in_specs=[
            pl.BlockSpec((1, TM, D_MODEL), lambda bi, i: (bi, i, 0)),
            pl.BlockSpec((1, D_MODEL), lambda bi, i: (0, 0)),
            pl.BlockSpec((1, 1, D_MODEL), lambda bi, i: (bi, 0, 0)),
            pl.BlockSpec((1, 1, D_MODEL), lambda bi, i: (bi, 0, 0)),
            pl.BlockSpec((D_MODEL, EVEN_IN), lambda bi, i: (0, 0)),
        ],
        out_specs=pl.BlockSpec((1, TM, EVEN_IN), lambda bi, i: (bi, i, 0)),
        out_shape=jax.ShapeDtypeStruct((b, l, EVEN_IN), BF16),
        compiler_params=_params(("arbitrary", "arbitrary")),
        name="even_front",
    )(x, g, scale, shift, w)


def _low_lane_mask(shape):
    return lax.broadcasted_iota(jnp.int32, shape, len(shape) - 1) < HEAD_DIM


def _na_kernel(q_ref, kp_ref, kc_ref, kn_ref, vp_ref, vc_ref, vn_ref, bias_ref, o_ref,
               kwin, vwin, *, rows):
    ci = pl.program_id(1)
    kwin[0:NA_CHUNK] = kp_ref[0]
    kwin[NA_CHUNK:2 * NA_CHUNK] = kc_ref[0]
    kwin[2 * NA_CHUNK:3 * NA_CHUNK] = kn_ref[0]
    vwin[0:NA_CHUNK] = vp_ref[0]
    vwin[NA_CHUNK:2 * NA_CHUNK] = vc_ref[0]
    vwin[2 * NA_CHUNK:3 * NA_CHUNK] = vn_ref[0]

    low = _low_lane_mask((GRID_W, LANES))
    kh = NA_WIN_H
    r0 = ci * NA_ROWS_PER_STEP

    def row_body(i, carry):
        r = r0 + i
        row_start = jnp.clip(r - kh // 2, 0, rows - kh)
        variant = r - row_start
        start = pl.multiple_of((row_start - r0 + NA_ROWS_PER_STEP) * GRID_W, GRID_W)
        qoff = pl.multiple_of(i * GRID_W, GRID_W)
        for j in range(NA_HEADS // 2):
            lanes = slice(j * LANES, (j + 1) * LANES)
            q2 = q_ref[0, pl.ds(qoff, GRID_W), lanes]
            zero = jnp.zeros_like(q2)
            lhs = jnp.concatenate([jnp.where(low, q2, zero), jnp.where(low, zero, q2)], axis=0)
            kw = kwin[pl.ds(start, kh * GRID_W), lanes]
            vw = vwin[pl.ds(start, kh * GRID_W), lanes]
            s = _dot_nt(lhs, kw)
            s = s + bias_ref[variant, j]
            m = jnp.max(s, axis=-1, keepdims=True)
            p = jnp.exp(s - m)
            den = jnp.sum(p, axis=-1, keepdims=True)
            o = _dot(p.astype(BF16), vw) / den
            o_ref[0, pl.ds(qoff, GRID_W), lanes] = jnp.where(low, o[:GRID_W], o[GRID_W:]).astype(BF16)
        return carry

    lax.fori_loop(0, NA_ROWS_PER_STEP, row_body, 0)


def _na_attention(u, bias_tbl):
    b, l, _ = u.shape
    rows = l // GRID_W
    nchunk = l // NA_CHUNK
    qblk, kblk, vblk = U_QA // NA_W, U_KA // NA_W, U_VA // NA_W

    def halo(col, d):
        return pl.BlockSpec((1, NA_CHUNK, NA_W),
                            lambda bi, i: (bi, jnp.clip(i + d, 0, nchunk - 1), col))

    return pl.pallas_call(
        functools.partial(_na_kernel, rows=rows),
        grid=(b, nchunk),
        in_specs=[
            pl.BlockSpec((1, NA_CHUNK, NA_W), lambda bi, i: (bi, i, qblk)),
            halo(kblk, -1), halo(kblk, 0), halo(kblk, 1),
            halo(vblk, -1), halo(vblk, 0), halo(vblk, 1),
            pl.BlockSpec(bias_tbl.shape, lambda bi, i: (0, 0, 0, 0)),
        ],
        out_specs=pl.BlockSpec((1, NA_CHUNK, NA_W), lambda bi, i: (bi, i, 0)),
        out_shape=jax.ShapeDtypeStruct((b, l, NA_W), BF16),
        scratch_shapes=[pltpu.VMEM((3 * NA_CHUNK, NA_W), BF16),
                        pltpu.VMEM((3 * NA_CHUNK, NA_W), BF16)],
        compiler_params=_params(("arbitrary", "arbitrary")),
        name="na_attention",
    )(u, u, u, u, u, u, u, bias_tbl)


def _wb_kernel(sink_ref, q_ref, kp_ref, kc_ref, kn_ref, vp_ref, vc_ref, vn_ref, bias_ref, o_ref,
               kwin, vwin, *, nblk):
    ci = pl.program_id(1)
    kwin[0:WB_BLOCK] = kp_ref[0]
    kwin[WB_BLOCK:WB_BLOCK + WB_CHUNK] = kc_ref[0]
    kwin[WB_BLOCK + WB_CHUNK:2 * WB_BLOCK + WB_CHUNK] = kn_ref[0]
    vwin[0:WB_BLOCK] = vp_ref[0]
    vwin[WB_BLOCK:WB_BLOCK + WB_CHUNK] = vc_ref[0]
    vwin[WB_BLOCK + WB_CHUNK:2 * WB_BLOCK + WB_CHUNK] = vn_ref[0]

    low = _low_lane_mask((WB_BLOCK, LANES))
    col = lax.broadcasted_iota(jnp.int32, (2 * WB_BLOCK, 3 * WB_BLOCK), 1)
    row = lax.broadcasted_iota(jnp.int32, (2 * WB_BLOCK, 1), 0)
    per_step = WB_CHUNK // WB_BLOCK
    for n in range(per_step):
        gblk = ci * per_step + n
        lo = jnp.where(gblk > 0, 0, WB_BLOCK)
        hi = jnp.where(gblk < nblk - 1, 3 * WB_BLOCK, 2 * WB_BLOCK)
        in_seq = jnp.logical_and(col >= lo, col < hi)
        kw = kwin[n * WB_BLOCK:(n + 3) * WB_BLOCK]
        vw = vwin[n * WB_BLOCK:(n + 3) * WB_BLOCK]
        for j in range(WB_HEADS // 2):
            lanes = slice(j * LANES, (j + 1) * LANES)
            q2 = q_ref[0, n * WB_BLOCK:(n + 1) * WB_BLOCK, lanes]
            zero = jnp.zeros_like(q2)
            lhs = jnp.concatenate([jnp.where(low, q2, zero), jnp.where(low, zero, q2)], axis=0)
            s = _dot_nt(lhs, kw)
            s = jnp.where(in_seq, s + bias_ref[j], NEG)
            sink = jnp.where(row < WB_BLOCK, sink_ref[WB_HEAD_ORDER[2 * j]],
                             sink_ref[WB_HEAD_ORDER[2 * j + 1]])
            m = jnp.maximum(jnp.max(s, axis=-1, keepdims=True), sink)
            p = jnp.exp(s - m)
            den = jnp.sum(p, axis=-1, keepdims=True) + jnp.exp(sink - m)
            o = _dot(p.astype(BF16), vw) / den
            o_ref[0, n * WB_BLOCK:(n + 1) * WB_BLOCK, lanes] = (
                jnp.where(low, o[:WB_BLOCK], o[WB_BLOCK:]).astype(BF16))


def _wb_attention(u, bias_tbl, sink):
    b, l, _ = u.shape
    nblk = l // WB_BLOCK
    nchunk = l // WB_CHUNK
    per_step = WB_CHUNK // WB_BLOCK
    qblk = U_QB // WB_QW
    kcol, vcol = U_KB // WB_KVW, U_VB // WB_KVW

    def halo(colblk, d):
        if d == 0:
            return pl.BlockSpec((1, WB_CHUNK, WB_KVW), lambda bi, i: (bi, i, colblk))
        return pl.BlockSpec(
            (1, WB_BLOCK, WB_KVW),
            lambda bi, i: (bi, jnp.clip(i * per_step + (per_step if d > 0 else -1), 0, nblk - 1), colblk))

    return pl.pallas_call(
        functools.partial(_wb_kernel, nblk=nblk),
        grid=(b, nchunk),
        in_specs=[
            pl.BlockSpec(memory_space=pltpu.SMEM),
            pl.BlockSpec((1, WB_CHUNK, WB_QW), lambda bi, i: (bi, i, qblk)),
            halo(kcol, -1), halo(kcol, 0), halo(kcol, 1),
            halo(vcol, -1), halo(vcol, 0), halo(vcol, 1),
            pl.BlockSpec(bias_tbl.shape, lambda bi, i: (0, 0, 0)),
        ],
        out_specs=pl.BlockSpec((1, WB_CHUNK, WB_QW), lambda bi, i: (bi, i, 0)),
        out_shape=jax.ShapeDtypeStruct((b, l, WB_QW), BF16),
        scratch_shapes=[pltpu.VMEM((WB_CHUNK + 2 * WB_BLOCK, WB_KVW), BF16),
                        pltpu.VMEM((WB_CHUNK + 2 * WB_BLOCK, WB_KVW), BF16)],
        compiler_params=_params(("arbitrary", "arbitrary")),
        name="wb_attention",
    )(sink, u, u, u, u, u, u, u, bias_tbl)


def _even_out_kernel(x_ref, oa_ref, ob_ref, gate_ref, gm_ref, w_ref, o_ref):
    g = _silu(gate_ref[0].astype(F32))
    o = jnp.concatenate([oa_ref[0], ob_ref[0]], axis=-1).astype(F32)
    z = (o * g).astype(BF16)
    o_ref[0] = x_ref[0] + gm_ref[0] * _dot(z, w_ref[...])


def _even_out(x, oa, ob, u, gate_mod, w_out):
    b, l, _ = x.shape
    return pl.pallas_call(
        _even_out_kernel,
        grid=(b, l // TM),
        in_specs=[
            pl.BlockSpec((1, TM, D_MODEL), lambda bi, i: (bi, i, 0)),
            pl.BlockSpec((1, TM, NA_W), lambda bi, i: (bi, i, 0)),
            pl.BlockSpec((1, TM, WB_QW), lambda bi, i: (bi, i, 0)),
            pl.BlockSpec((1, TM, EVEN_MIX), lambda bi, i: (bi, i, U_GATE // EVEN_MIX)),
            pl.BlockSpec((1, 1, D_MODEL), lambda bi, i: (bi, 0, 0)),
            pl.BlockSpec((EVEN_MIX, D_MODEL), lambda bi, i: (0, 0)),
        ],
        out_specs=pl.BlockSpec((1, TM, D_MODEL), lambda bi, i: (bi, i, 0)),
        out_shape=jax.ShapeDtypeStruct((b, l, D_MODEL), F32),
        compiler_params=_params(("arbitrary", "arbitrary")),
        name="even_out",
    )(x, oa, ob, u, gate_mod, w_out)


def _odd_front_kernel(x_ref, g_ref, sc_ref, sh_ref, w1t_ref, wnat_ref, gq_ref, gkvt_ref, gkv_ref,
                      wqbt_ref, wk_ref, wvt_ref, cost_ref, sint_ref, cosp_ref, sinp_ref,
                      k_ref, qt_ref, vt_ref, sgt_ref):
    h = _modulated_norm(x_ref[0], g_ref[...], sc_ref[0], sh_ref[0]).astype(BF16)
    reps = TM // LANES

    ut = _dot_nt(w1t_ref[...], h)
    sgt_ref[0] = _silu(ut[0:ODD_MIX]).astype(BF16)

    qlt = ut[ODD_MIX:ODD_MIX + MLA_Q_RANK]
    qn = qlt * lax.rsqrt(jnp.mean(qlt * qlt, axis=0, keepdims=True) + EPS)
    qn = (qn * jnp.concatenate([gq_ref[...]] * reps, axis=1)).astype(BF16)
    qt = _dot(wqbt_ref[...], qn)
    qscale = (MLA_QK ** -0.5) * LOG2E
    cos_t = cost_ref[...]
    sin_t = sint_ref[...]
    half = MLA_ROPE // 2
    for hd in range(MLA_HEADS):
        base = hd * MLA_QK
        qt_ref[0, hd, 0, 0:MLA_NOPE, :] = (qt[base:base + MLA_NOPE] * qscale).astype(BF16)
        x1 = qt[base + MLA_NOPE:base + MLA_NOPE + half]
        x2 = qt[base + MLA_NOPE + half:base + MLA_QK]
        qt_ref[0, hd, 0, MLA_NOPE:MLA_NOPE + half, :] = ((x1 * cos_t - x2 * sin_t) * qscale).astype(BF16)
        qt_ref[0, hd, 0, MLA_NOPE + half:MLA_QK, :] = ((x2 * cos_t + x1 * sin_t) * qscale).astype(BF16)

    kvt = ut[ODD_MIX + MLA_Q_RANK:ODD_MIX + MLA_Q_RANK + MLA_KV_RANK]
    kvnt = kvt * lax.rsqrt(jnp.mean(kvt * kvt, axis=0, keepdims=True) + EPS)
    kvnt = (kvnt * jnp.concatenate([gkvt_ref[...]] * reps, axis=1)).astype(BF16)
    vt = _dot(wvt_ref[...], kvnt)
    vt_ref[0, :, 0] = vt.reshape(MLA_HEADS, MLA_V, TM).astype(BF16)

    nat = _dot(h, wnat_ref[...])
    kvl = nat[:, 0:MLA_KV_RANK]
    kvn = kvl * lax.rsqrt(jnp.mean(kvl * kvl, axis=-1, keepdims=True) + EPS)
    kvn = (kvn * gkv_ref[...]).astype(BF16)
    kn = _dot(kvn, wk_ref[...])
    kpe = nat[:, LANES:2 * LANES] * cosp_ref[...] + nat[:, 2 * LANES:3 * LANES] * sinp_ref[...]
    for hd in range(MLA_HEADS):
        kh = kn[:, hd * LANES:(hd + 1) * LANES] + kpe
        k_ref[0, hd] = kh[:, 0:MLA_QK].astype(BF16)


def _odd_front(x, g, scale, shift, w1t, wnat, gq, gkvt, gkv, wqbt, wk, wvt, cos_t, sin_t, cos_p, sin_p):
    b, l, _ = x.shape
    nt = l // TM
    const2 = lambda bi, i: (0, 0)
    n1 = w1t.shape[0]
    return pl.pallas_call(
        _odd_front_kernel,
        grid=(b, nt),
        in_specs=[
            pl.BlockSpec((1, TM, D_MODEL), lambda bi, i: (bi, i, 0)),
            pl.BlockSpec((1, D_MODEL), const2),
            pl.BlockSpec((1, 1, D_MODEL), lambda bi, i: (bi, 0, 0)),
            pl.BlockSpec((1, 1, D_MODEL), lambda bi, i: (bi, 0, 0)),
            pl.BlockSpec((n1, D_MODEL), const2),
            pl.BlockSpec((D_MODEL, 3 * LANES), const2),
            pl.BlockSpec((MLA_Q_RANK, LANES), const2),
            pl.BlockSpec((MLA_KV_RANK, LANES), const2),
            pl.BlockSpec((1, MLA_KV_RANK), const2),
            pl.BlockSpec((MLA_HEADS * MLA_QK, MLA_Q_RANK), const2),
            pl.BlockSpec((MLA_KV_RANK, MLA_HEADS * LANES), const2),
            pl.BlockSpec((ODD_MIX, MLA_KV_RANK), const2),
            pl.BlockSpec((MLA_ROPE // 2, TM), lambda bi, i: (0, i)),
            pl.BlockSpec((MLA_ROPE // 2, TM), lambda bi, i: (0, i)),
            pl.BlockSpec((TM, LANES), lambda bi, i: (i, 0)),
            pl.BlockSpec((TM, LANES), lambda bi, i: (i, 0)),
        ],
        out_specs=[
            pl.BlockSpec((1, MLA_HEADS, TM, MLA_QK), lambda bi, i: (bi, 0, i, 0)),
            pl.BlockSpec((1, MLA_HEADS, 1, MLA_QK, TM), lambda bi, i: (bi, 0, i, 0, 0)),
            pl.BlockSpec((1, MLA_HEADS, 1, MLA_V, TM), lambda bi, i: (bi, 0, i, 0, 0)),
            pl.BlockSpec((1, ODD_MIX, TM), lambda bi, i: (bi, 0, i)),
        ],
        out_shape=[
            jax.ShapeDtypeStruct((b, MLA_HEADS, l, MLA_QK), BF16),
            jax.ShapeDtypeStruct((b, MLA_HEADS, nt, MLA_QK, TM), BF16),
            jax.ShapeDtypeStruct((b, MLA_HEADS, nt, MLA_V, TM), BF16),
            jax.ShapeDtypeStruct((b, ODD_MIX, l), BF16),
        ],
        compiler_params=_params(("arbitrary", "arbitrary")),
        name="odd_front",
    )(x, g, scale, shift, w1t, wnat, gq, gkvt, gkv, wqbt, wk, wvt, cos_t, sin_t, cos_p, sin_p)


def _mla_kernel(qt_ref, k_ref, vt_ref, o_ref, s_scr, p_scr, acc_scr, *, hb, nqb, npairs):
    ones = jnp.ones((ONES_ROWS, MLA_TK), BF16)
    sub = MLA_TQ // TM
    pair_bits = npairs.bit_length() - 1
    qb_bits = nqb.bit_length() - 1
    total = hb * nqb * npairs

    def decode(i):
        return i >> (pair_bits + qb_bits), (i >> pair_bits) & (nqb - 1), i & (npairs - 1)

    def scores(i):
        h, qb, u = decode(i)
        q = jnp.concatenate([qt_ref[0, h, sub * qb + j] for j in range(sub)], axis=1)
        mts = []
        for slot in range(2):
            koff = pl.multiple_of((2 * u + slot) * MLA_TK, MLA_TK)
            s = _dot(k_ref[0, h, pl.ds(koff, MLA_TK), :], q)
            s_scr[slot] = s
            mts.append(jnp.max(s, axis=0, keepdims=True))
        return tuple(mts)

    def probs(i, m, mts):
        _, _, u = decode(i)
        m = jnp.where(u == 0, NEG, m)
        als = []
        for slot in range(2):
            m_new = jnp.maximum(m, mts[slot])
            p_scr[slot] = jnp.exp2(s_scr[slot] - m_new).astype(BF16)
            als.append(jnp.exp2(m - m_new))
            m = m_new
        return m, tuple(als)

    def accumulate(i, als):
        h, qb, u = decode(i)
        for slot in range(2):
            v1 = jnp.concatenate([vt_ref[0, h, 2 * u + slot], ones], axis=0)
            acc_scr[...] = acc_scr[...] * als[slot] + _dot(v1, p_scr[slot])
        acc = acc_scr[...]
        o_ref[0, h, qb] = (acc[0:MLA_V] * (1.0 / acc[MLA_V:MLA_V + 1])).astype(BF16)

    acc_scr[...] = jnp.zeros_like(acc_scr)
    m = jnp.full((1, MLA_TQ), NEG, F32)
    mts = scores(0)
    m, als = probs(0, m, mts)
    mts = scores(1)

    def body(i, carry):
        m, mts, als = carry
        accumulate(i - 2, als)
        m, als = probs(i - 1, m, mts)
        mts = scores(i)
        return m, mts, als

    m, mts, als = lax.fori_loop(2, total, body, (m, mts, als))
    accumulate(total - 2, als)
    m, als = probs(total - 1, m, mts)
    accumulate(total - 1, als)


def _mla_attention(qt, k, vt):
    b, _, nt, _, _ = qt.shape
    l = nt * TM
    nk = l // MLA_TK
    nqb = l // MLA_TQ
    npairs = nk // 2
    assert MLA_TK == TM and nk % 2 == 0
    assert nqb & (nqb - 1) == 0 and npairs & (npairs - 1) == 0
    hb = max(1, min(MLA_HEADS, MLA_RESIDENT_TOKENS // l))
    return pl.pallas_call(
        functools.partial(_mla_kernel, hb=hb, nqb=nqb, npairs=npairs),
        grid=(b, MLA_HEADS // hb),
        in_specs=[
            pl.BlockSpec((1, hb, nt, MLA_QK, TM), lambda bi, h: (bi, h, 0, 0, 0)),
            pl.BlockSpec((1, hb, l, MLA_QK), lambda bi, h: (bi, h, 0, 0)),
            pl.BlockSpec((1, hb, nk, MLA_V, MLA_TK), lambda bi, h: (bi, h, 0, 0, 0)),
        ],
        out_specs=pl.BlockSpec((1, hb, nqb, MLA_V, MLA_TQ), lambda bi, h: (bi, h, 0, 0, 0)),
        out_shape=jax.ShapeDtypeStruct((b, MLA_HEADS, nqb, MLA_V, MLA_TQ), BF16),
        scratch_shapes=[pltpu.VMEM((2, MLA_TK, MLA_TQ), F32),
                        pltpu.VMEM((2, MLA_TK, MLA_TQ), BF16),
                        pltpu.VMEM((MLA_V + ONES_ROWS, MLA_TQ), F32)],
        compiler_params=_params(("arbitrary", "arbitrary")),
        name="mla_attention",
    )(qt, k, vt)


def _odd_out_kernel(x_ref, ot_ref, sgt_ref, gm_ref, wt_ref, fg_ref, y_ref):
    ot = ot_ref[0, :, 0].reshape(ODD_MIX, TM)
    z = (ot.astype(F32) * sgt_ref[0].astype(F32)).astype(BF16)
    out = _dot(wt_ref[...], z).T
    x2 = x_ref[0] + gm_ref[0] * out
    ms = jnp.mean(x2 * x2, axis=-1, keepdims=True)
    y_ref[0] = (x2 * lax.rsqrt(ms + EPS)) * fg_ref[...]


def _odd_out(x, ot, sgt, gate_mod, w_out_t, final_g):
    b, l, _ = x.shape
    sub = MLA_TQ // TM
    return pl.pallas_call(
        _odd_out_kernel,
        grid=(b, l // TM),
        in_specs=[
            pl.BlockSpec((1, TM, D_MODEL), lambda bi, i: (bi, i, 0)),
            pl.BlockSpec((1, MLA_HEADS, 1, MLA_V, TM), lambda bi, i: (bi, 0, i // sub, 0, i % sub)),
            pl.BlockSpec((1, ODD_MIX, TM), lambda bi, i: (bi, 0, i)),
            pl.BlockSpec((1, 1, D_MODEL), lambda bi, i: (bi, 0, 0)),
            pl.BlockSpec((D_MODEL, ODD_MIX), lambda bi, i: (0, 0)),
            pl.BlockSpec((1, D_MODEL), lambda bi, i: (0, 0)),
        ],
        out_specs=pl.BlockSpec((1, TM, D_MODEL), lambda bi, i: (bi, i, 0)),
        out_shape=jax.ShapeDtypeStruct((b, l, D_MODEL), F32),
        compiler_params=_params(("arbitrary", "arbitrary")),
        name="odd_out",
    )(x, ot, sgt, gate_mod, w_out_t, final_g)


def _t5_bucket(rel):
    nb = T5_BUCKETS // 2
    ret = (rel > 0).astype(np.int32) * nb
    n = np.abs(rel)
    max_exact = nb // 2
    large = max_exact + (np.log(np.maximum(n, 1) / max_exact)
                         / np.log(T5_MAX_DIST / max_exact) * (nb - max_exact)).astype(np.int32)
    large = np.minimum(large, nb - 1)
    return ret + np.where(n < max_exact, n, large)


def _na_bias_table(rpb):
    kh = NA_WIN_H
    c = np.arange(GRID_W)
    col_start = np.clip(c - NA_WIN_W // 2, 0, GRID_W - NA_WIN_W)
    col_ok = (c[None, :] >= col_start[:, None]) & (c[None, :] < col_start[:, None] + NA_WIN_W)
    d_col = np.clip(c[None, :] - c[:, None], -(NA_WIN_W - 1), NA_WIN_W - 1) + NA_WIN_W - 1
    nrel = 2 * NA_WIN_W - 1
    rows = jnp.stack([rpb[:, NA_WIN_H - 1 - v:2 * NA_WIN_H - 1 - v] for v in range(kh)], axis=1)
    onehot = (d_col.reshape(-1)[None, :] == np.arange(nrel)[:, None]).astype(np.float32)
    bias = jnp.dot(rows.astype(F32).reshape(-1, nrel), jnp.asarray(onehot),
                   precision=lax.Precision.HIGHEST)
    bias = bias.reshape(NA_HEADS, kh, kh, GRID_W, GRID_W)
    bias = jnp.where(jnp.asarray(col_ok)[None, None, None, :, :], bias, NEG)
    bias = bias.transpose(1, 0, 3, 2, 4).reshape(kh, NA_HEADS // 2, 2 * GRID_W, kh * GRID_W)
    return bias


def _wb_bias_table(t5_bias):
    rel = (np.arange(3 * WB_BLOCK) - WB_BLOCK)[None, :] - np.arange(WB_BLOCK)[:, None]
    offs = np.arange(-(2 * WB_BLOCK - 1), 2 * WB_BLOCK)
    by_off = t5_bias[_t5_bucket(offs)].astype(F32).T
    bias = jnp.stack([by_off[:, WB_BLOCK - 1 - q:4 * WB_BLOCK - 1 - q] for q in range(WB_BLOCK)], axis=1)
    bias = jnp.where(jnp.asarray(np.abs(rel) <= WB_WINDOW)[None], bias, NEG)
    bias = bias[np.asarray(WB_HEAD_ORDER)]
    return bias.reshape(WB_HEADS // 2, 2 * WB_BLOCK, 3 * WB_BLOCK)


def _even_weights(w_in, w_out):
    qa, ka, va, qb, kb, vb, gate = jnp.split(
        w_in, [NA_W, 2 * NA_W, 3 * NA_W, 3 * NA_W + WB_QW, 3 * NA_W + WB_QW + WB_KVW,
               3 * NA_W + WB_QW + 2 * WB_KVW], axis=-1)
    order = np.asarray(WB_HEAD_ORDER)
    perm = (order[:, None] * HEAD_DIM + np.arange(HEAD_DIM)[None, :]).reshape(-1)
    qscale = HEAD_DIM ** -0.5
    gate = jnp.concatenate([gate[:, :NA_W], gate[:, NA_W:][:, perm]], axis=-1)
    w = jnp.concatenate([gate, qa * qscale, ka, va, qb[:, perm] * qscale, kb, vb], axis=-1)
    w_out_p = jnp.concatenate([w_out[:NA_W], w_out[NA_W:][perm]], axis=0)
    return w.astype(BF16), w_out_p.astype(BF16)


def _odd_weights(w_in, q_norm, w_qb, kv_norm, w_kvb, w_out):
    q_lat, kv_lat, k_rope, gate = jnp.split(
        w_in, [MLA_Q_RANK, MLA_Q_RANK + MLA_KV_RANK, MLA_Q_RANK + MLA_KV_RANK + MLA_ROPE], axis=-1)
    w1t = jnp.concatenate([gate, q_lat, kv_lat], axis=-1).T.astype(BF16)
    half = MLA_ROPE // 2
    k_rot = jnp.concatenate([-k_rope[:, half:], k_rope[:, :half]], axis=-1)
    z64 = jnp.zeros((D_MODEL, MLA_NOPE), w_in.dtype)
    z32 = jnp.zeros((D_MODEL, LANES - MLA_QK), w_in.dtype)
    wnat = jnp.concatenate([kv_lat, z64, k_rope, z32, z64, k_rot, z32], axis=-1).astype(BF16)
    wkv = w_kvb.reshape(MLA_KV_RANK, MLA_HEADS, MLA_NOPE + MLA_V)
    wk = jnp.concatenate([wkv[:, :, :MLA_NOPE], jnp.zeros_like(wkv[:, :, :MLA_NOPE])], axis=-1)
    wk = wk.reshape(MLA_KV_RANK, MLA_HEADS * LANES).astype(BF16)
    wvt = wkv[:, :, MLA_NOPE:].reshape(MLA_KV_RANK, ODD_MIX).T.astype(BF16)
    gq = jnp.broadcast_to(q_norm.astype(F32)[:, None], (MLA_Q_RANK, LANES))
    gkvt = jnp.broadcast_to(kv_norm.astype(F32)[:, None], (MLA_KV_RANK, LANES))
    gkv = kv_norm.astype(F32)[None, :]
    return w1t, wnat, gq, gkvt, gkv, w_qb.T.astype(BF16), wk, wvt, w_out.T.astype(BF16)


def _rope_tables(l):
    inv_freq = 1.0 / (ROPE_THETA ** (jnp.arange(0, MLA_ROPE, 2, dtype=F32) / MLA_ROPE))
    ang = jnp.arange(l, dtype=F32)[:, None] * inv_freq[None, :]
    cos, sin = jnp.cos(ang), jnp.sin(ang)
    z64 = jnp.zeros((l, MLA_NOPE), F32)
    z32 = jnp.zeros((l, LANES - MLA_QK), F32)
    cos_p = jnp.concatenate([z64, cos, cos, z32], axis=-1)
    sin_p = jnp.concatenate([z64, sin, sin, z32], axis=-1)
    return cos.T, sin.T, cos_p, sin_p


def _trunk(x, mod, norm_g, ev, na_tbl, wb_tbl, sink, od, final_g):
    b, l, _ = x.shape
    assert l % TM == 0 and l % NA_CHUNK == 0 and l % WB_CHUNK == 0 and l % MLA_TK == 0
    assert l // GRID_W >= NA_WIN_H + NA_ROWS_PER_STEP
    shift0, scale0, gate0 = [t[:, None, :] for t in jnp.split(mod[0], 3, axis=-1)]
    shift1, scale1, gate1 = [t[:, None, :] for t in jnp.split(mod[1], 3, axis=-1)]

    w_in0, w_out0 = ev
    u = _even_front(x, norm_g[0][None, :], scale0, shift0, w_in0)
    oa = _na_attention(u, na_tbl)
    ob = _wb_attention(u, wb_tbl, sink)
    x1 = _even_out(x, oa, ob, u, gate0, w_out0)

    w1t, wnat, gq, gkvt, gkv, wqbt, wk, wvt, w_out1t = od
    cos_t, sin_t, cos_p, sin_p = _rope_tables(l)
    k, qt, vt, sgt = _odd_front(x1, norm_g[1][None, :], scale1, shift1, w1t, wnat, gq, gkvt, gkv,
                                wqbt, wk, wvt, cos_t, sin_t, cos_p, sin_p)
    ot = _mla_attention(qt, k, vt)
    return _odd_out(x1, ot, sgt, gate1, w_out1t, final_g[None, :])


def kernel(x_prompt, x_sample, c_prompt, c_sample, ada_w, ada_b, norm_g, t5_bias, ev_w_in, na_rpb,
           wb_sink, ev_w_out, mla_w_in, mla_q_norm, mla_w_qb, mla_kv_norm, mla_w_kvb, mla_w_out, final_g):
    bp, bs = c_prompt.shape[0], c_sample.shape[0]
    rows = -(-(bp + bs) // 16) * 16
    c_pad = jnp.concatenate([c_prompt, c_sample, jnp.zeros((rows - bp - bs, D_MODEL), F32)], axis=0)
    mod = _ada_mod(c_pad, ada_w.astype(BF16), ada_b[:, None, :])

    ev = _even_weights(ev_w_in[0], ev_w_out[0])
    na_tbl = _na_bias_table(na_rpb[0])
    wb_tbl = _wb_bias_table(t5_bias)
    od = _odd_weights(mla_w_in[0], mla_q_norm[0], mla_w_qb[0], mla_kv_norm[0], mla_w_kvb[0], mla_w_out[0])
    sink = wb_sink[0].astype(F32)

    y_prompt = _trunk(x_prompt, mod[:, :bp], norm_g, ev, na_tbl, wb_tbl, sink, od, final_g)
    y_sample = _trunk(x_sample, mod[:, bp:bp + bs], norm_g, ev, na_tbl, wb_tbl, sink, od, final_g)
    return (y_prompt, y_sample)
```

```python
import functools
import math

import numpy as np
import jax
import jax.numpy as jnp
from jax import lax
from jax.experimental import pallas as pl
from jax.experimental.pallas import tpu as pltpu

D_MODEL = 1024
GRID_W = 64
HEAD_DIM = 64
EPS = 1e-6
NEG = -1e30
NA_HEADS = 8
NA_WIN_H = 8
NA_WIN_W = 16
WB_HEADS = 8
WB_KV_HEADS = 2
WB_WINDOW = 128
WB_BLOCK = 128
T5_BUCKETS = 32
T5_MAX_DIST = 128
MLA_HEADS = 16
MLA_Q_RANK = 256
MLA_KV_RANK = 128
MLA_NOPE = 64
MLA_ROPE = 32
MLA_V = 64
ROPE_THETA = 10000.0
MLA_QK = MLA_NOPE + MLA_ROPE

NA_W = NA_HEADS * HEAD_DIM
WB_QW = WB_HEADS * HEAD_DIM
WB_KVW = WB_KV_HEADS * HEAD_DIM
EVEN_MIX = NA_W + WB_QW
EVEN_IN = 3 * NA_W + WB_QW + 2 * WB_KVW + EVEN_MIX
ODD_MIX = MLA_HEADS * MLA_V

U_GATE = 0
U_QA = EVEN_MIX
U_KA = U_QA + NA_W
U_VA = U_KA + NA_W
U_QB = U_VA + NA_W
U_KB = U_QB + WB_QW
U_VB = U_KB + WB_KVW

LANES = 128
TM = 512
NA_ROWS_PER_STEP = 16
NA_HALO_ROWS = 8
NA_CHUNK = NA_ROWS_PER_STEP * GRID_W
WB_CHUNK = 1024
MLA_TQ = 1024
MLA_TK = 512
MLA_RESIDENT_TOKENS = 16384
ONES_ROWS = 16
VMEM_LIMIT = 56 * 1024 * 1024

BF16 = jnp.bfloat16
F32 = jnp.float32
LOG2E = math.log2(math.e)

WB_HEAD_ORDER = (0, 4, 1, 5, 2, 6, 3, 7)


def _params(sem):
    return pltpu.CompilerParams(dimension_semantics=sem, vmem_limit_bytes=VMEM_LIMIT)


def _dot(a, b):
    return jnp.dot(a, b, preferred_element_type=F32)


def _dot_nt(a, b):
    return lax.dot_general(a, b, (((1,), (1,)), ((), ())), preferred_element_type=F32)


def _silu(x):
    return x * (1.0 / (1.0 + jnp.exp(-x)))


def _modulated_norm(x, g, scale, shift):
    ms = jnp.mean(x * x, axis=-1, keepdims=True)
    y = x * lax.rsqrt(ms + EPS)
    return (y * g) * (1.0 + scale) + shift


def _ada_kernel(c_ref, w_ref, b_ref, o_ref):
    c = c_ref[...]
    cs = _silu(c).astype(BF16)
    o_ref[0] = _dot(cs, w_ref[0]) + b_ref[0]


def _ada_mod(c_pad, ada_w, ada_b):
    depth = ada_w.shape[0]
    rows = c_pad.shape[0]
    tn = 768
    return pl.pallas_call(
        _ada_kernel,
        grid=(depth, 3 * D_MODEL // tn),
        in_specs=[
            pl.BlockSpec((rows, D_MODEL), lambda i, n: (0, 0)),
            pl.BlockSpec((1, D_MODEL, tn), lambda i, n: (i, 0, n)),
            pl.BlockSpec((1, 1, tn), lambda i, n: (i, 0, n)),
        ],
        out_specs=pl.BlockSpec((1, rows, tn), lambda i, n: (i, 0, n)),
        out_shape=jax.ShapeDtypeStruct((depth, rows, 3 * D_MODEL), F32),
        compiler_params=_params(("arbitrary", "arbitrary")),
        name="ada_mod",
    )(c_pad, ada_w, ada_b)


def _even_front_kernel(x_ref, g_ref, sc_ref, sh_ref, w_ref, u_ref):
    h = _modulated_norm(x_ref[0], g_ref[...], sc_ref[0], sh_ref[0]).astype(BF16)
    u_ref[0] = _dot(h, w_ref[...]).astype(BF16)


def _even_front(x, g, scale, shift, w):
    b, l, _ = x.shape
    return pl.pallas_call(
        _even_front_kernel,
        grid=(b, l // TM),
        in_specs=[
            pl.BlockSpec((1, TM, D_MODEL), lambda bi, i: (bi, i, 0)),
            pl.BlockSpec((1, D_MODEL), lambda bi, i: (0, 0)),
            pl.BlockSpec((1, 1, D_MODEL), lambda bi, i: (bi, 0, 0)),
            pl.BlockSpec((1, 1, D_MODEL), lambda bi, i: (bi, 0, 0)),
            pl.BlockSpec((D_MODEL, EVEN_IN), lambda bi, i: (0, 0)),
        ],
        out_specs=pl.BlockSpec((1, TM, EVEN_IN), lambda bi, i: (bi, i, 0)),
        out_shape=jax.ShapeDtypeStruct((b, l, EVEN_IN), BF16),
        compiler_params=_params(("arbitrary", "arbitrary")),
        name="even_front",
    )(x, g, scale, shift, w)


def _low_lane_mask(shape):
    return lax.broadcasted_iota(jnp.int32, shape, len(shape) - 1) < HEAD_DIM


def _na_kernel(q_ref, kp_ref, kc_ref, kn_ref, vp_ref, vc_ref, vn_ref, bias_ref, o_ref,
               kwin, vwin, s_scr, p_scr, *, rows):
    ci = pl.program_id(1)
    halo = NA_HALO_ROWS * GRID_W
    npair = NA_HEADS // 2
    kwin[0:halo] = kp_ref[0]
    kwin[halo:halo + NA_CHUNK] = kc_ref[0]
    kwin[halo + NA_CHUNK:2 * halo + NA_CHUNK] = kn_ref[0]
    for j in range(npair):
        lanes = slice(j * LANES, (j + 1) * LANES)
        dst = slice(2 * j * LANES, (2 * j + 1) * LANES)
        vwin[0:halo, dst] = vp_ref[0, :, lanes]
        vwin[halo:halo + NA_CHUNK, dst] = vc_ref[0, :, lanes]
        vwin[halo + NA_CHUNK:2 * halo + NA_CHUNK, dst] = vn_ref[0, :, lanes]
        vwin[:, (2 * j + 1) * LANES:(2 * j + 2) * LANES] = jnp.ones((2 * halo + NA_CHUNK, LANES), BF16)

    low = _low_lane_mask((GRID_W, LANES))
    kh = NA_WIN_H
    r0 = ci * NA_ROWS_PER_STEP

    def offsets(i):
        r = r0 + i
        row_start = jnp.clip(r - kh // 2, 0, rows - kh)
        variant = r - row_start
        start = pl.multiple_of((row_start - r0 + NA_HALO_ROWS) * GRID_W, GRID_W)
        return variant, start, pl.multiple_of(i * GRID_W, GRID_W)

    def scores(i):
        variant, start, qoff = offsets(i)
        for j in range(npair):
            lanes = slice(j * LANES, (j + 1) * LANES)
            q2 = q_ref[0, pl.ds(qoff, GRID_W), lanes]
            zero = jnp.zeros_like(q2)
            lhs = jnp.concatenate([jnp.where(low, q2, zero), jnp.where(low, zero, q2)], axis=0)
            kw = kwin[pl.ds(start, kh * GRID_W), lanes]
            s_scr[j] = _dot_nt(lhs, kw) + bias_ref[variant, j]

    def probs():
        for j in range(npair):
            s = s_scr[j]
            p_scr[j] = jnp.exp(s - jnp.max(s, axis=-1, keepdims=True)).astype(BF16)

    def output(i):
        _, start, qoff = offsets(i)
        for j in range(npair):
            vw = vwin[pl.ds(start, kh * GRID_W), 2 * j * LANES:(2 * j + 2) * LANES]
            res = _dot(p_scr[j], vw)
            o = res[:, 0:LANES] / res[:, LANES:2 * LANES]
            o_ref[0, pl.ds(qoff, GRID_W), j * LANES:(j + 1) * LANES] = (
                jnp.where(low, o[:GRID_W], o[GRID_W:]).astype(BF16))

    scores(0)
    probs()
    scores(1)

    def body(i, carry):
        output(i - 2)
        probs()
        scores(i)
        return carry

    lax.fori_loop(2, NA_ROWS_PER_STEP, body, 0)
    output(NA_ROWS_PER_STEP - 2)
    probs()
    output(NA_ROWS_PER_STEP - 1)


def _na_attention(u, bias_tbl):
    b, l, _ = u.shape
    rows = l // GRID_W
    nchunk = l // NA_CHUNK
    halo = NA_HALO_ROWS * GRID_W
    per_chunk = NA_CHUNK // halo
    nhalo = l // halo
    qblk, kblk, vblk = U_QA // NA_W, U_KA // NA_W, U_VA // NA_W

    def window(col):
        return [
            pl.BlockSpec((1, halo, NA_W), lambda bi, i: (bi, jnp.maximum(i * per_chunk - 1, 0), col)),
            pl.BlockSpec((1, NA_CHUNK, NA_W), lambda bi, i: (bi, i, col)),
            pl.BlockSpec((1, halo, NA_W), lambda bi, i: (bi, jnp.minimum((i + 1) * per_chunk, nhalo - 1), col)),
        ]

    npair = NA_HEADS // 2
    return pl.pallas_call(
        functools.partial(_na_kernel, rows=rows),
        grid=(b, nchunk),
        in_specs=[pl.BlockSpec((1, NA_CHUNK, NA_W), lambda bi, i: (bi, i, qblk))]
        + window(kblk) + window(vblk)
        + [pl.BlockSpec(bias_tbl.shape, lambda bi, i: (0, 0, 0, 0))],
        out_specs=pl.BlockSpec((1, NA_CHUNK, NA_W), lambda bi, i: (bi, i, 0)),
        out_shape=jax.ShapeDtypeStruct((b, l, NA_W), BF16),
        scratch_shapes=[pltpu.VMEM((NA_CHUNK + 2 * halo, NA_W), BF16),
                        pltpu.VMEM((NA_CHUNK + 2 * halo, 2 * NA_W), BF16),
                        pltpu.VMEM((npair, 2 * GRID_W, NA_WIN_H * GRID_W), F32),
                        pltpu.VMEM((npair, 2 * GRID_W, NA_WIN_H * GRID_W), BF16)],
        compiler_params=_params(("arbitrary", "arbitrary")),
        name="na_attention",
    )(u, u, u, u, u, u, u, bias_tbl)


def _wb_kernel(sink_ref, q_ref, kp_ref, kc_ref, kn_ref, vp_ref, vc_ref, vn_ref, bias_ref, o_ref,
               kwin, vwin, s_scr, p_scr, ps_scr, *, nblk):
    ci = pl.program_id(1)
    npair = WB_HEADS // 2
    per_step = WB_CHUNK // WB_BLOCK
    win = WB_CHUNK + 2 * WB_BLOCK
    kwin[0:WB_BLOCK] = kp_ref[0]
    kwin[WB_BLOCK:WB_BLOCK + WB_CHUNK] = kc_ref[0]
    kwin[WB_BLOCK + WB_CHUNK:win] = kn_ref[0]
    vwin[0:WB_BLOCK, 0:LANES] = vp_ref[0]
    vwin[WB_BLOCK:WB_BLOCK + WB_CHUNK, 0:LANES] = vc_ref[0]
    vwin[WB_BLOCK + WB_CHUNK:win, 0:LANES] = vn_ref[0]
    vwin[:, LANES:2 * LANES] = jnp.ones((win, LANES), BF16)

    low = _low_lane_mask((WB_BLOCK, LANES))
    col = lax.broadcasted_iota(jnp.int32, (2 * WB_BLOCK, 3 * WB_BLOCK), 1)
    row = lax.broadcasted_iota(jnp.int32, (2 * WB_BLOCK, 1), 0)

    def scores(n):
        gblk = ci * per_step + n
        lo = jnp.where(gblk > 0, 0, WB_BLOCK)
        hi = jnp.where(gblk < nblk - 1, 3 * WB_BLOCK, 2 * WB_BLOCK)
        in_seq = jnp.logical_and(col >= lo, col < hi)
        off = pl.multiple_of(n * WB_BLOCK, WB_BLOCK)
        kw = kwin[pl.ds(off, 3 * WB_BLOCK), :]
        for j in range(npair):
            q2 = q_ref[0, pl.ds(off, WB_BLOCK), j * LANES:(j + 1) * LANES]
            zero = jnp.zeros_like(q2)
            lhs = jnp.concatenate([jnp.where(low, q2, zero), jnp.where(low, zero, q2)], axis=0)
            s_scr[j] = jnp.where(in_seq, _dot_nt(lhs, kw) + bias_ref[j], NEG)

    def probs():
        for j in range(npair):
            s = s_scr[j]
            sink = jnp.where(row < WB_BLOCK, sink_ref[WB_HEAD_ORDER[2 * j]],
                             sink_ref[WB_HEAD_ORDER[2 * j + 1]])
            m = jnp.maximum(jnp.max(s, axis=-1, keepdims=True), sink)
            p_scr[j] = jnp.exp(s - m).astype(BF16)
            ps_scr[j] = jnp.broadcast_to(jnp.exp(sink - m), (2 * WB_BLOCK, LANES))

    def output(n):
        off = pl.multiple_of(n * WB_BLOCK, WB_BLOCK)
        vw = vwin[pl.ds(off, 3 * WB_BLOCK), :]
        for j in range(npair):
            res = _dot(p_scr[j], vw)
            o = res[:, 0:LANES] / (res[:, LANES:2 * LANES] + ps_scr[j])
            o_ref[0, pl.ds(off, WB_BLOCK), j * LANES:(j + 1) * LANES] = (
                jnp.where(low, o[:WB_BLOCK], o[WB_BLOCK:]).astype(BF16))

    scores(0)
    probs()
    scores(1)

    def body(n, carry):
        output(n - 2)
        probs()
        scores(n)
        return carry

    lax.fori_loop(2, per_step, body, 0)
    output(per_step - 2)
    probs()
    output(per_step - 1)


def _wb_attention(u, bias_tbl, sink):
    b, l, _ = u.shape
    nblk = l // WB_BLOCK
    nchunk = l // WB_CHUNK
    per_step = WB_CHUNK // WB_BLOCK
    qblk = U_QB // WB_QW
    kcol, vcol = U_KB // WB_KVW, U_VB // WB_KVW

    def halo(colblk, d):
        if d == 0:
            return pl.BlockSpec((1, WB_CHUNK, WB_KVW), lambda bi, i: (bi, i, colblk))
        return pl.BlockSpec(
            (1, WB_BLOCK, WB_KVW),
            lambda bi, i: (bi, jnp.clip(i * per_step + (per_step if d > 0 else -1), 0, nblk - 1), colblk))

    return pl.pallas_call(
        functools.partial(_wb_kernel, nblk=nblk),
        grid=(b, nchunk),
        in_specs=[
            pl.BlockSpec(memory_space=pltpu.SMEM),
            pl.BlockSpec((1, WB_CHUNK, WB_QW), lambda bi, i: (bi, i, qblk)),
            halo(kcol, -1), halo(kcol, 0), halo(kcol, 1),
            halo(vcol, -1), halo(vcol, 0), halo(vcol, 1),
            pl.BlockSpec(bias_tbl.shape, lambda bi, i: (0, 0, 0)),
        ],
        out_specs=pl.BlockSpec((1, WB_CHUNK, WB_QW), lambda bi, i: (bi, i, 0)),
        out_shape=jax.ShapeDtypeStruct((b, l, WB_QW), BF16),
        scratch_shapes=[pltpu.VMEM((WB_CHUNK + 2 * WB_BLOCK, WB_KVW), BF16),
                        pltpu.VMEM((WB_CHUNK + 2 * WB_BLOCK, 2 * WB_KVW), BF16),
                        pltpu.VMEM((WB_HEADS // 2, 2 * WB_BLOCK, 3 * WB_BLOCK), F32),
                        pltpu.VMEM((WB_HEADS // 2, 2 * WB_BLOCK, 3 * WB_BLOCK), BF16),
                        pltpu.VMEM((WB_HEADS // 2, 2 * WB_BLOCK, LANES), F32)],
        compiler_params=_params(("arbitrary", "arbitrary")),
        name="wb_attention",
    )(sink, u, u, u, u, u, u, u, bias_tbl)


def _even_out_kernel(x_ref, oa_ref, ob_ref, gate_ref, gm_ref, w_ref, o_ref):
    g = _silu(gate_ref[0].astype(F32))
    o = jnp.concatenate([oa_ref[0], ob_ref[0]], axis=-1).astype(F32)
    z = (o * g).astype(BF16)
    o_ref[0] = x_ref[0] + gm_ref[0] * _dot(z, w_ref[...])


def _even_out(x, oa, ob, u, gate_mod, w_out):
    b, l, _ = x.shape
    return pl.pallas_call(
        _even_out_kernel,
        grid=(b, l // TM),
        in_specs=[
            pl.BlockSpec((1, TM, D_MODEL), lambda bi, i: (bi, i, 0)),
            pl.BlockSpec((1, TM, NA_W), lambda bi, i: (bi, i, 0)),
            pl.BlockSpec((1, TM, WB_QW), lambda bi, i: (bi, i, 0)),
            pl.BlockSpec((1, TM, EVEN_MIX), lambda bi, i: (bi, i, U_GATE // EVEN_MIX)),
            pl.BlockSpec((1, 1, D_MODEL), lambda bi, i: (bi, 0, 0)),
            pl.BlockSpec((EVEN_MIX, D_MODEL), lambda bi, i: (0, 0)),
        ],
        out_specs=pl.BlockSpec((1, TM, D_MODEL), lambda bi, i: (bi, i, 0)),
        out_shape=jax.ShapeDtypeStruct((b, l, D_MODEL), F32),
        compiler_params=_params(("arbitrary", "arbitrary")),
        name="even_out",
    )(x, oa, ob, u, gate_mod, w_out)


def _odd_front_kernel(x_ref, g_ref, sc_ref, sh_ref, w1t_ref, wnat_ref, gq_ref, gkvt_ref, gkv_ref,
                      wqbt_ref, wk_ref, wvt_ref, cost_ref, sint_ref, cosp_ref, sinp_ref,
                      k_ref, qt_ref, vt_ref, sgt_ref):
    h = _modulated_norm(x_ref[0], g_ref[...], sc_ref[0], sh_ref[0]).astype(BF16)
    reps = TM // LANES

    ut = _dot_nt(w1t_ref[...], h)
    sgt_ref[0] = _silu(ut[0:ODD_MIX]).astype(BF16)

    qlt = ut[ODD_MIX:ODD_MIX + MLA_Q_RANK]
    qn = qlt * lax.rsqrt(jnp.mean(qlt * qlt, axis=0, keepdims=True) + EPS)
    qn = (qn * jnp.concatenate([gq_ref[...]] * reps, axis=1)).astype(BF16)
    qt = _dot(wqbt_ref[...], qn)
    qscale = (MLA_QK ** -0.5) * LOG2E
    cos_t = cost_ref[...]
    sin_t = sint_ref[...]
    half = MLA_ROPE // 2
    for hd in range(MLA_HEADS):
        base = hd * MLA_QK
        qt_ref[0, hd, 0, 0:MLA_NOPE, :] = (qt[base:base + MLA_NOPE] * qscale).astype(BF16)
        x1 = qt[base + MLA_NOPE:base + MLA_NOPE + half]
        x2 = qt[base + MLA_NOPE + half:base + MLA_QK]
        qt_ref[0, hd, 0, MLA_NOPE:MLA_NOPE + half, :] = ((x1 * cos_t - x2 * sin_t) * qscale).astype(BF16)
        qt_ref[0, hd, 0, MLA_NOPE + half:MLA_QK, :] = ((x2 * cos_t + x1 * sin_t) * qscale).astype(BF16)

    kvt = ut[ODD_MIX + MLA_Q_RANK:ODD_MIX + MLA_Q_RANK + MLA_KV_RANK]
    kvnt = kvt * lax.rsqrt(jnp.mean(kvt * kvt, axis=0, keepdims=True) + EPS)
    kvnt = (kvnt * jnp.concatenate([gkvt_ref[...]] * reps, axis=1)).astype(BF16)
    vt = _dot(wvt_ref[...], kvnt)
    vt_ref[0, :, 0] = vt.reshape(MLA_HEADS, MLA_V, TM).astype(BF16)

    nat = _dot(h, wnat_ref[...])
    kvl = nat[:, 0:MLA_KV_RANK]
    kvn = kvl * lax.rsqrt(jnp.mean(kvl * kvl, axis=-1, keepdims=True) + EPS)
    kvn = (kvn * gkv_ref[...]).astype(BF16)
    kn = _dot(kvn, wk_ref[...])
    kpe = nat[:, LANES:2 * LANES] * cosp_ref[...] + nat[:, 2 * LANES:3 * LANES] * sinp_ref[...]
    for hd in range(MLA_HEADS):
        kh = kn[:, hd * LANES:(hd + 1) * LANES] + kpe
        k_ref[0, hd] = kh[:, 0:MLA_QK].astype(BF16)


def _odd_front(x, g, scale, shift, w1t, wnat, gq, gkvt, gkv, wqbt, wk, wvt, cos_t, sin_t, cos_p, sin_p):
    b, l, _ = x.shape
    nt = l // TM
    const2 = lambda bi, i: (0, 0)
    n1 = w1t.shape[0]
    return pl.pallas_call(
        _odd_front_kernel,
        grid=(b, nt),
        in_specs=[
            pl.BlockSpec((1, TM, D_MODEL), lambda bi, i: (bi, i, 0)),
            pl.BlockSpec((1, D_MODEL), const2),
            pl.BlockSpec((1, 1, D_MODEL), lambda bi, i: (bi, 0, 0)),
            pl.BlockSpec((1, 1, D_MODEL), lambda bi, i: (bi, 0, 0)),
            pl.BlockSpec((n1, D_MODEL), const2),
            pl.BlockSpec((D_MODEL, 3 * LANES), const2),
            pl.BlockSpec((MLA_Q_RANK, LANES), const2),
            pl.BlockSpec((MLA_KV_RANK, LANES), const2),
            pl.BlockSpec((1, MLA_KV_RANK), const2),
            pl.BlockSpec((MLA_HEADS * MLA_QK, MLA_Q_RANK), const2),
            pl.BlockSpec((MLA_KV_RANK, MLA_HEADS * LANES), const2),
            pl.BlockSpec((ODD_MIX, MLA_KV_RANK), const2),
            pl.BlockSpec((MLA_ROPE // 2, TM), lambda bi, i: (0, i)),
            pl.BlockSpec((MLA_ROPE // 2, TM), lambda bi, i: (0, i)),
            pl.BlockSpec((TM, LANES), lambda bi, i: (i, 0)),
            pl.BlockSpec((TM, LANES), lambda bi, i: (i, 0)),
        ],
        out_specs=[
            pl.BlockSpec((1, MLA_HEADS, TM, MLA_QK), lambda bi, i: (bi, 0, i, 0)),
            pl.BlockSpec((1, MLA_HEADS, 1, MLA_QK, TM), lambda bi, i: (bi, 0, i, 0, 0)),
            pl.BlockSpec((1, MLA_HEADS, 1, MLA_V, TM), lambda bi, i: (bi, 0, i, 0, 0)),
            pl.BlockSpec((1, ODD_MIX, TM), lambda bi, i: (bi, 0, i)),
        ],
        out_shape=[
            jax.ShapeDtypeStruct((b, MLA_HEADS, l, MLA_QK), BF16),
            jax.ShapeDtypeStruct((b, MLA_HEADS, nt, MLA_QK, TM), BF16),
            jax.ShapeDtypeStruct((b, MLA_HEADS, nt, MLA_V, TM), BF16),
            jax.ShapeDtypeStruct((b, ODD_MIX, l), BF16),
        ],
        compiler_params=_params(("arbitrary", "arbitrary")),
        name="odd_front",
    )(x, g, scale, shift, w1t, wnat, gq, gkvt, gkv, wqbt, wk, wvt, cos_t, sin_t, cos_p, sin_p)


def _mla_kernel(qt_ref, k_ref, vt_ref, o_ref, s_scr, p_scr, acc_scr, *, hb, nqb, npairs):
    ones = jnp.ones((ONES_ROWS, MLA_TK), BF16)
    sub = MLA_TQ // TM
    pair_bits = npairs.bit_length() - 1
    qb_bits = nqb.bit_length() - 1
    total = hb * nqb * npairs

    def decode(i):
        return i >> (pair_bits + qb_bits), (i >> pair_bits) & (nqb - 1), i & (npairs - 1)

    def scores(i):
        h, qb, u = decode(i)
        q = jnp.concatenate([qt_ref[0, h, sub * qb + j] for j in range(sub)], axis=1)
        mts = []
        for slot in range(2):
            koff = pl.multiple_of((2 * u + slot) * MLA_TK, MLA_TK)
            s = _dot(k_ref[0, h, pl.ds(koff, MLA_TK), :], q)
            s_scr[slot] = s
            mts.append(jnp.max(s, axis=0, keepdims=True))
        return tuple(mts)

    def probs(i, m, mts):
        _, _, u = decode(i)
        m = jnp.where(u == 0, NEG, m)
        als = []
        for slot in range(2):
            m_new = jnp.maximum(m, mts[slot])
            p_scr[slot] = jnp.exp2(s_scr[slot] - m_new).astype(BF16)
            als.append(jnp.exp2(m - m_new))
            m = m_new
        return m, tuple(als)

    def accumulate(i, als):
        h, qb, u = decode(i)
        for slot in range(2):
            v1 = jnp.concatenate([vt_ref[0, h, 2 * u + slot], ones], axis=0)
            acc_scr[...] = acc_scr[...] * als[slot] + _dot(v1, p_scr[slot])
        acc = acc_scr[...]
        o_ref[0, h, qb] = (acc[0:MLA_V] * (1.0 / acc[MLA_V:MLA_V + 1])).astype(BF16)

    acc_scr[...] = jnp.zeros_like(acc_scr)
    m = jnp.full((1, MLA_TQ), NEG, F32)
    mts = scores(0)
    m, als = probs(0, m, mts)
    mts = scores(1)

    def body(i, carry):
        m, mts, als = carry
        accumulate(i - 2, als)
        m, als = probs(i - 1, m, mts)
        mts = scores(i)
        return m, mts, als

    m, mts, als = lax.fori_loop(2, total, body, (m, mts, als))
    accumulate(total - 2, als)
    m, als = probs(total - 1, m, mts)
    accumulate(total - 1, als)


def _mla_attention(qt, k, vt):
    b, _, nt, _, _ = qt.shape
    l = nt * TM
    nk = l // MLA_TK
    nqb = l // MLA_TQ
    npairs = nk // 2
    assert MLA_TK == TM and nk % 2 == 0
    assert nqb & (nqb - 1) == 0 and npairs & (npairs - 1) == 0
    hb = max(1, min(MLA_HEADS, MLA_RESIDENT_TOKENS // l))
    return pl.pallas_call(
        functools.partial(_mla_kernel, hb=hb, nqb=nqb, npairs=npairs),
        grid=(b, MLA_HEADS // hb),
        in_specs=[
            pl.BlockSpec((1, hb, nt, MLA_QK, TM), lambda bi, h: (bi, h, 0, 0, 0)),
            pl.BlockSpec((1, hb, l, MLA_QK), lambda bi, h: (bi, h, 0, 0)),
            pl.BlockSpec((1, hb, nk, MLA_V, MLA_TK), lambda bi, h: (bi, h, 0, 0, 0)),
        ],
        out_specs=pl.BlockSpec((1, hb, nqb, MLA_V, MLA_TQ), lambda bi, h: (bi, h, 0, 0, 0)),
        out_shape=jax.ShapeDtypeStruct((b, MLA_HEADS, nqb, MLA_V, MLA_TQ), BF16),
        scratch_shapes=[pltpu.VMEM((2, MLA_TK, MLA_TQ), F32),
                        pltpu.VMEM((2, MLA_TK, MLA_TQ), BF16),
                        pltpu.VMEM((MLA_V + ONES_ROWS, MLA_TQ), F32)],
        compiler_params=_params(("arbitrary", "arbitrary")),
        name="mla_attention",
    )(qt, k, vt)


def _odd_out_kernel(x_ref, ot_ref, sgt_ref, gm_ref, wt_ref, fg_ref, y_ref):
    ot = ot_ref[0, :, 0].reshape(ODD_MIX, TM)
    z = (ot.astype(F32) * sgt_ref[0].astype(F32)).astype(BF16)
    out = _dot(wt_ref[...], z).T
    x2 = x_ref[0] + gm_ref[0] * out
    ms = jnp.mean(x2 * x2, axis=-1, keepdims=True)
    y_ref[0] = (x2 * lax.rsqrt(ms + EPS)) * fg_ref[...]


def _odd_out(x, ot, sgt, gate_mod, w_out_t, final_g):
    b, l, _ = x.shape
    sub = MLA_TQ // TM
    return pl.pallas_call(
        _odd_out_kernel,
        grid=(b, l // TM),
        in_specs=[
            pl.BlockSpec((1, TM, D_MODEL), lambda bi, i: (bi, i, 0)),
            pl.BlockSpec((1, MLA_HEADS, 1, MLA_V, TM), lambda bi, i: (bi, 0, i // sub, 0, i % sub)),
            pl.BlockSpec((1, ODD_MIX, TM), lambda bi, i: (bi, 0, i)),
            pl.BlockSpec((1, 1, D_MODEL), lambda bi, i: (bi, 0, 0)),
            pl.BlockSpec((D_MODEL, ODD_MIX), lambda bi, i: (0, 0)),
            pl.BlockSpec((1, D_MODEL), lambda bi, i: (0, 0)),
        ],
        out_specs=pl.BlockSpec((1, TM, D_MODEL), lambda bi, i: (bi, i, 0)),
        out_shape=jax.ShapeDtypeStruct((b, l, D_MODEL), F32),
        compiler_params=_params(("arbitrary", "arbitrary")),
        name="odd_out",
    )(x, ot, sgt, gate_mod, w_out_t, final_g)


def _t5_bucket(rel):
    nb = T5_BUCKETS // 2
    ret = (rel > 0).astype(np.int32) * nb
    n = np.abs(rel)
    max_exact = nb // 2
    large = max_exact + (np.log(np.maximum(n, 1) / max_exact)
                         / np.log(T5_MAX_DIST / max_exact) * (nb - max_exact)).astype(np.int32)
    large = np.minimum(large, nb - 1)
    return ret + np.where(n < max_exact, n, large)


def _na_bias_table(rpb):
    kh = NA_WIN_H
    c = np.arange(GRID_W)
    col_start = np.clip(c - NA_WIN_W // 2, 0, GRID_W - NA_WIN_W)
    col_ok = (c[None, :] >= col_start[:, None]) & (c[None, :] < col_start[:, None] + NA_WIN_W)
    d_col = np.clip(c[None, :] - c[:, None], -(NA_WIN_W - 1), NA_WIN_W - 1) + NA_WIN_W - 1
    nrel = 2 * NA_WIN_W - 1
    rows = jnp.stack([rpb[:, NA_WIN_H - 1 - v:2 * NA_WIN_H - 1 - v] for v in range(kh)], axis=1)
    onehot = (d_col.reshape(-1)[None, :] == np.arange(nrel)[:, None]).astype(np.float32)
    bias = jnp.dot(rows.astype(F32).reshape(-1, nrel), jnp.asarray(onehot),
                   precision=lax.Precision.HIGHEST)
    bias = bias.reshape(NA_HEADS, kh, kh, GRID_W, GRID_W)
    bias = jnp.where(jnp.asarray(col_ok)[None, None, None, :, :], bias, NEG)
    bias = bias.transpose(1, 0, 3, 2, 4).reshape(kh, NA_HEADS // 2, 2 * GRID_W, kh * GRID_W)
    return bias


def _wb_bias_table(t5_bias):
    rel = (np.arange(3 * WB_BLOCK) - WB_BLOCK)[None, :] - np.arange(WB_BLOCK)[:, None]
    offs = np.arange(-(2 * WB_BLOCK - 1), 2 * WB_BLOCK)
    by_off = t5_bias[_t5_bucket(offs)].astype(F32).T
    bias = jnp.stack([by_off[:, WB_BLOCK - 1 - q:4 * WB_BLOCK - 1 - q] for q in range(WB_BLOCK)], axis=1)
    bias = jnp.where(jnp.asarray(np.abs(rel) <= WB_WINDOW)[None], bias, NEG)
    bias = bias[np.asarray(WB_HEAD_ORDER)]
    return bias.reshape(WB_HEADS // 2, 2 * WB_BLOCK, 3 * WB_BLOCK)


def _even_weights(w_in, w_out):
    qa, ka, va, qb, kb, vb, gate = jnp.split(
        w_in, [NA_W, 2 * NA_W, 3 * NA_W, 3 * NA_W + WB_QW, 3 * NA_W + WB_QW + WB_KVW,
               3 * NA_W + WB_QW + 2 * WB_KVW], axis=-1)
    order = np.asarray(WB_HEAD_ORDER)
    perm = (order[:, None] * HEAD_DIM + np.arange(HEAD_DIM)[None, :]).reshape(-1)
    qscale = HEAD_DIM ** -0.5
    gate = jnp.concatenate([gate[:, :NA_W], gate[:, NA_W:][:, perm]], axis=-1)
    w = jnp.concatenate([gate, qa * qscale, ka, va, qb[:, perm] * qscale, kb, vb], axis=-1)
    w_out_p = jnp.concatenate([w_out[:NA_W], w_out[NA_W:][perm]], axis=0)
    return w.astype(BF16), w_out_p.astype(BF16)


def _odd_weights(w_in, q_norm, w_qb, kv_norm, w_kvb, w_out):
    q_lat, kv_lat, k_rope, gate = jnp.split(
        w_in, [MLA_Q_RANK, MLA_Q_RANK + MLA_KV_RANK, MLA_Q_RANK + MLA_KV_RANK + MLA_ROPE], axis=-1)
    w1t = jnp.concatenate([gate, q_lat, kv_lat], axis=-1).T.astype(BF16)
    half = MLA_ROPE // 2
    k_rot = jnp.concatenate([-k_rope[:, half:], k_rope[:, :half]], axis=-1)
    z64 = jnp.zeros((D_MODEL, MLA_NOPE), w_in.dtype)
    z32 = jnp.zeros((D_MODEL, LANES - MLA_QK), w_in.dtype)
    wnat = jnp.concatenate([kv_lat, z64, k_rope, z32, z64, k_rot, z32], axis=-1).astype(BF16)
    wkv = w_kvb.reshape(MLA_KV_RANK, MLA_HEADS, MLA_NOPE + MLA_V)
    wk = jnp.concatenate([wkv[:, :, :MLA_NOPE], jnp.zeros_like(wkv[:, :, :MLA_NOPE])], axis=-1)
    wk = wk.reshape(MLA_KV_RANK, MLA_HEADS * LANES).astype(BF16)
    wvt = wkv[:, :, MLA_NOPE:].reshape(MLA_KV_RANK, ODD_MIX).T.astype(BF16)
    gq = jnp.broadcast_to(q_norm.astype(F32)[:, None], (MLA_Q_RANK, LANES))
    gkvt = jnp.broadcast_to(kv_norm.astype(F32)[:, None], (MLA_KV_RANK, LANES))
    gkv = kv_norm.astype(F32)[None, :]
    return w1t, wnat, gq, gkvt, gkv, w_qb.T.astype(BF16), wk, wvt, w_out.T.astype(BF16)


def _rope_tables(l):
    inv_freq = 1.0 / (ROPE_THETA ** (jnp.arange(0, MLA_ROPE, 2, dtype=F32) / MLA_ROPE))
    ang = jnp.arange(l, dtype=F32)[:, None] * inv_freq[None, :]
    cos, sin = jnp.cos(ang), jnp.sin(ang)
    z64 = jnp.zeros((l, MLA_NOPE), F32)
    z32 = jnp.zeros((l, LANES - MLA_QK), F32)
    cos_p = jnp.concatenate([z64, cos, cos, z32], axis=-1)
    sin_p = jnp.concatenate([z64, sin, sin, z32], axis=-1)
    return cos.T, sin.T, cos_p, sin_p


def _trunk(x, mod, norm_g, ev, na_tbl, wb_tbl, sink, od, final_g):
    b, l, _ = x.shape
    assert l % TM == 0 and l % NA_CHUNK == 0 and l % WB_CHUNK == 0 and l % MLA_TK == 0
    assert l // GRID_W >= NA_ROWS_PER_STEP >= NA_HALO_ROWS >= NA_WIN_H
    shift0, scale0, gate0 = [t[:, None, :] for t in jnp.split(mod[0], 3, axis=-1)]
    shift1, scale1, gate1 = [t[:, None, :] for t in jnp.split(mod[1], 3, axis=-1)]

    w_in0, w_out0 = ev
    u = _even_front(x, norm_g[0][None, :], scale0, shift0, w_in0)
    oa = _na_attention(u, na_tbl)
    ob = _wb_attention(u, wb_tbl, sink)
    x1 = _even_out(x, oa, ob, u, gate0, w_out0)

    w1t, wnat, gq, gkvt, gkv, wqbt, wk, wvt, w_out1t = od
    cos_t, sin_t, cos_p, sin_p = _rope_tables(l)
    k, qt, vt, sgt = _odd_front(x1, norm_g[1][None, :], scale1, shift1, w1t, wnat, gq, gkvt, gkv,
                                wqbt, wk, wvt, cos_t, sin_t, cos_p, sin_p)
    ot = _mla_attention(qt, k, vt)
    return _odd_out(x1, ot, sgt, gate1, w_out1t, final_g[None, :])


def kernel(x_prompt, x_sample, c_prompt, c_sample, ada_w, ada_b, norm_g, t5_bias, ev_w_in, na_rpb,
           wb_sink, ev_w_out, mla_w_in, mla_q_norm, mla_w_qb, mla_kv_norm, mla_w_kvb, mla_w_out, final_g):
    bp, bs = c_prompt.shape[0], c_sample.shape[0]
    rows = -(-(bp + bs) // 16) * 16
    c_pad = jnp.concatenate([c_prompt, c_sample, jnp.zeros((rows - bp - bs, D_MODEL), F32)], axis=0)
    mod = _ada_mod(c_pad, ada_w.astype(BF16), ada_b[:, None, :])

    ev = _even_weights(ev_w_in[0], ev_w_out[0])
    na_tbl = _na_bias_table(na_rpb[0])
    wb_tbl = _wb_bias_table(t5_bias)
    od = _odd_weights(mla_w_in[0], mla_q_norm[0], mla_w_qb[0], mla_kv_norm[0], mla_w_kvb[0], mla_w_out[0])
    sink = wb_sink[0].astype(F32)

    y_prompt = _trunk(x_prompt, mod[:, :bp], norm_g, ev, na_tbl, wb_tbl, sink, od, final_g)
    y_sample = _trunk(x_sample, mod[:, bp:bp + bs], norm_g, ev, na_tbl, wb_tbl, sink, od, final_g)
    return (y_prompt, y_sample)
```

```python
import functools
import math

import numpy as np
import jax
import jax.numpy as jnp
from jax import lax
from jax.experimental import pallas as pl
from jax.experimental.pallas import tpu as pltpu

D_MODEL = 1024
GRID_W = 64
HEAD_DIM = 64
EPS = 1e-6
NEG = -1e30
NA_HEADS = 8
NA_WIN_H = 8
NA_WIN_W = 16
WB_HEADS = 8
WB_KV_HEADS = 2
WB_WINDOW = 128
WB_BLOCK = 128
T5_BUCKETS = 32
T5_MAX_DIST = 128
MLA_HEADS = 16
MLA_Q_RANK = 256
MLA_KV_RANK = 128
MLA_NOPE = 64
MLA_ROPE = 32
MLA_V = 64
ROPE_THETA = 10000.0
MLA_QK = MLA_NOPE + MLA_ROPE

NA_W = NA_HEADS * HEAD_DIM
WB_QW = WB_HEADS * HEAD_DIM
WB_KVW = WB_KV_HEADS * HEAD_DIM
EVEN_MIX = NA_W + WB_QW
EVEN_IN = 3 * NA_W + WB_QW + 2 * WB_KVW + EVEN_MIX
ODD_MIX = MLA_HEADS * MLA_V

U_GATE = 0
U_QA = EVEN_MIX
U_KA = U_QA + NA_W
U_VA = U_KA + NA_W
U_QB = U_VA + NA_W
U_KB = U_QB + WB_QW
U_VB = U_KB + WB_KVW

LANES = 128
TM = 512
NA_ROWS_PER_STEP = 16
NA_HALO_ROWS = 8
NA_CHUNK = NA_ROWS_PER_STEP * GRID_W
WB_CHUNK = 1024
MLA_TQ = 1024
MLA_TK = 512
MLA_RESIDENT_TOKENS = 16384
ONES_ROWS = 16
VMEM_LIMIT = 56 * 1024 * 1024

BF16 = jnp.bfloat16
F32 = jnp.float32
LOG2E = math.log2(math.e)

WB_HEAD_ORDER = (0, 4, 1, 5, 2, 6, 3, 7)


def _params(sem):
    return pltpu.CompilerParams(dimension_semantics=sem, vmem_limit_bytes=VMEM_LIMIT)


def _dot(a, b):
    return jnp.dot(a, b, preferred_element_type=F32)


def _dot_nt(a, b):
    return lax.dot_general(a, b, (((1,), (1,)), ((), ())), preferred_element_type=F32)


def _silu(x):
    return x * (1.0 / (1.0 + jnp.exp(-x)))


def _modulated_norm(x, g, scale, shift):
    ms = jnp.mean(x * x, axis=-1, keepdims=True)
    y = x * lax.rsqrt(ms + EPS)
    return (y * g) * (1.0 + scale) + shift


def _ada_kernel(c_ref, w_ref, b_ref, o_ref):
    c = c_ref[...]
    cs = _silu(c).astype(BF16)
    o_ref[0] = _dot(cs, w_ref[0]) + b_ref[0]


def _ada_mod(c_pad, ada_w, ada_b):
    depth = ada_w.shape[0]
    rows = c_pad.shape[0]
    tn = 768
    return pl.pallas_call(
        _ada_kernel,
        grid=(depth, 3 * D_MODEL // tn),
        in_specs=[
            pl.BlockSpec((rows, D_MODEL), lambda i, n: (0, 0)),
            pl.BlockSpec((1, D_MODEL, tn), lambda i, n: (i, 0, n)),
            pl.BlockSpec((1, 1, tn), lambda i, n: (i, 0, n)),
        ],
        out_specs=pl.BlockSpec((1, rows, tn), lambda i, n: (i, 0, n)),
        out_shape=jax.ShapeDtypeStruct((depth, rows, 3 * D_MODEL), F32),
        compiler_params=_params(("arbitrary", "arbitrary")),
        name="ada_mod",
    )(c_pad, ada_w, ada_b)


def _even_front_kernel(x_ref, g_ref, sc_ref, sh_ref, w_ref, u_ref):
    h = _modulated_norm(x_ref[0], g_ref[...], sc_ref[0], sh_ref[0]).astype(BF16)
    u_ref[0] = _dot(h, w_ref[...]).astype(BF16)


def _even_front(x, g, scale, shift, w):
    b, l, _ = x.shape
    return pl.pallas_call(
        _even_front_kernel,
        grid=(b, l // TM),
        in_specs=[
            pl.BlockSpec((1, TM, D_MODEL), lambda bi, i: (bi, i, 0)),
            pl.BlockSpec((1, D_MODEL), lambda bi, i: (0, 0)),
            pl.BlockSpec((1, 1, D_MODEL), lambda bi, i: (bi, 0, 0)),
            pl.BlockSpec((1, 1, D_MODEL), lambda bi, i: (bi, 0, 0)),
            pl.BlockSpec((D_MODEL, EVEN_IN), lambda bi, i: (0, 0)),
        ],
        out_specs=pl.BlockSpec((1, TM, EVEN_IN), lambda bi, i: (bi, i, 0)),
        out_shape=jax.ShapeDtypeStruct((b, l, EVEN_IN), BF16),
        compiler_params=_params(("arbitrary", "arbitrary")),
        name="even_front",
    )(x, g, scale, shift, w)


def _low_lane_mask(shape):
    return lax.broadcasted_iota(jnp.int32, shape, len(shape) - 1) < HEAD_DIM


def _na_kernel(q_ref, kp_ref, kc_ref, kn_ref, vp_ref, vc_ref, vn_ref, bias_ref, o_ref,
               kwin, vwin, s_scr, p_scr, *, rows):
    ci = pl.program_id(1)
    halo = NA_HALO_ROWS * GRID_W
    npair = NA_HEADS // 2
    kwin[0:halo] = kp_ref[0]
    kwin[halo:halo + NA_CHUNK] = kc_ref[0]
    kwin[halo + NA_CHUNK:2 * halo + NA_CHUNK] = kn_ref[0]
    for j in range(npair):
        lanes = slice(j * LANES, (j + 1) * LANES)
        dst = slice(2 * j * LANES, (2 * j + 1) * LANES)
        vwin[0:halo, dst] = vp_ref[0, :, lanes]
        vwin[halo:halo + NA_CHUNK, dst] = vc_ref[0, :, lanes]
        vwin[halo + NA_CHUNK:2 * halo + NA_CHUNK, dst] = vn_ref[0, :, lanes]
        vwin[:, (2 * j + 1) * LANES:(2 * j + 2) * LANES] = jnp.ones((2 * halo + NA_CHUNK, LANES), BF16)

    low = _low_lane_mask((GRID_W, LANES))
    kh = NA_WIN_H
    r0 = ci * NA_ROWS_PER_STEP

    def offsets(i):
        r = r0 + i
        row_start = jnp.clip(r - kh // 2, 0, rows - kh)
        variant = r - row_start
        start = pl.multiple_of((row_start - r0 + NA_HALO_ROWS) * GRID_W, GRID_W)
        return variant, start, pl.multiple_of(i * GRID_W, GRID_W)

    def scores(i):
        variant, start, qoff = offsets(i)
        for j in range(npair):
            lanes = slice(j * LANES, (j + 1) * LANES)
            q2 = q_ref[0, pl.ds(qoff, GRID_W), lanes]
            zero = jnp.zeros_like(q2)
            lhs = jnp.concatenate([jnp.where(low, q2, zero), jnp.where(low, zero, q2)], axis=0)
            kw = kwin[pl.ds(start, kh * GRID_W), lanes]
            s_scr[j] = _dot_nt(lhs, kw) + bias_ref[variant, j]

    def probs():
        for j in range(npair):
            s = s_scr[j]
            p_scr[j] = jnp.exp(s - jnp.max(s, axis=-1, keepdims=True)).astype(BF16)

    def output(i):
        _, start, qoff = offsets(i)
        for j in range(npair):
            vw = vwin[pl.ds(start, kh * GRID_W), 2 * j * LANES:(2 * j + 2) * LANES]
            res = _dot(p_scr[j], vw)
            o = res[:, 0:LANES] / res[:, LANES:2 * LANES]
            o_ref[0, pl.ds(qoff, GRID_W), j * LANES:(j + 1) * LANES] = (
                jnp.where(low, o[:GRID_W], o[GRID_W:]).astype(BF16))

    scores(0)
    probs()
    scores(1)

    def body(i, carry):
        output(i - 2)
        probs()
        scores(i)
        return carry

    lax.fori_loop(2, NA_ROWS_PER_STEP, body, 0, unroll=2)
    output(NA_ROWS_PER_STEP - 2)
    probs()
    output(NA_ROWS_PER_STEP - 1)


def _na_attention(u, bias_tbl):
    b, l, _ = u.shape
    rows = l // GRID_W
    nchunk = l // NA_CHUNK
    halo = NA_HALO_ROWS * GRID_W
    per_chunk = NA_CHUNK // halo
    nhalo = l // halo
    qblk, kblk, vblk = U_QA // NA_W, U_KA // NA_W, U_VA // NA_W

    def window(col):
        return [
            pl.BlockSpec((1, halo, NA_W), lambda bi, i: (bi, jnp.maximum(i * per_chunk - 1, 0), col)),
            pl.BlockSpec((1, NA_CHUNK, NA_W), lambda bi, i: (bi, i, col)),
            pl.BlockSpec((1, halo, NA_W), lambda bi, i: (bi, jnp.minimum((i + 1) * per_chunk, nhalo - 1), col)),
        ]

    npair = NA_HEADS // 2
    return pl.pallas_call(
        functools.partial(_na_kernel, rows=rows),
        grid=(b, nchunk),
        in_specs=[pl.BlockSpec((1, NA_CHUNK, NA_W), lambda bi, i: (bi, i, qblk))]
        + window(kblk) + window(vblk)
        + [pl.BlockSpec(bias_tbl.shape, lambda bi, i: (0, 0, 0, 0))],
        out_specs=pl.BlockSpec((1, NA_CHUNK, NA_W), lambda bi, i: (bi, i, 0)),
        out_shape=jax.ShapeDtypeStruct((b, l, NA_W), BF16),
        scratch_shapes=[pltpu.VMEM((NA_CHUNK + 2 * halo, NA_W), BF16),
                        pltpu.VMEM((NA_CHUNK + 2 * halo, 2 * NA_W), BF16),
                        pltpu.VMEM((npair, 2 * GRID_W, NA_WIN_H * GRID_W), F32),
                        pltpu.VMEM((npair, 2 * GRID_W, NA_WIN_H * GRID_W), BF16)],
        compiler_params=_params(("arbitrary", "arbitrary")),
        name="na_attention",
    )(u, u, u, u, u, u, u, bias_tbl)


def _wb_kernel(sink_ref, q_ref, kp_ref, kc_ref, kn_ref, vp_ref, vc_ref, vn_ref, bias_ref, o_ref,
               kwin, vwin, s_scr, p_scr, ps_scr, *, nblk):
    ci = pl.program_id(1)
    npair = WB_HEADS // 2
    per_step = WB_CHUNK // WB_BLOCK
    win = WB_CHUNK + 2 * WB_BLOCK
    kwin[0:WB_BLOCK] = kp_ref[0]
    kwin[WB_BLOCK:WB_BLOCK + WB_CHUNK] = kc_ref[0]
    kwin[WB_BLOCK + WB_CHUNK:win] = kn_ref[0]
    vwin[0:WB_BLOCK, 0:LANES] = vp_ref[0]
    vwin[WB_BLOCK:WB_BLOCK + WB_CHUNK, 0:LANES] = vc_ref[0]
    vwin[WB_BLOCK + WB_CHUNK:win, 0:LANES] = vn_ref[0]
    vwin[:, LANES:2 * LANES] = jnp.ones((win, LANES), BF16)

    low = _low_lane_mask((WB_BLOCK, LANES))
    col = lax.broadcasted_iota(jnp.int32, (2 * WB_BLOCK, 3 * WB_BLOCK), 1)
    row = lax.broadcasted_iota(jnp.int32, (2 * WB_BLOCK, 1), 0)

    def scores(n):
        gblk = ci * per_step + n
        lo = jnp.where(gblk > 0, 0, WB_BLOCK)
        hi = jnp.where(gblk < nblk - 1, 3 * WB_BLOCK, 2 * WB_BLOCK)
        in_seq = jnp.logical_and(col >= lo, col < hi)
        off = pl.multiple_of(n * WB_BLOCK, WB_BLOCK)
        kw = kwin[pl.ds(off, 3 * WB_BLOCK), :]
        for j in range(npair):
            q2 = q_ref[0, pl.ds(off, WB_BLOCK), j * LANES:(j + 1) * LANES]
            zero = jnp.zeros_like(q2)
            lhs = jnp.concatenate([jnp.where(low, q2, zero), jnp.where(low, zero, q2)], axis=0)
            s_scr[j] = jnp.where(in_seq, _dot_nt(lhs, kw) + bias_ref[j], NEG)

    def probs():
        for j in range(npair):
            s = s_scr[j]
            sink = jnp.where(row < WB_BLOCK, sink_ref[WB_HEAD_ORDER[2 * j]],
                             sink_ref[WB_HEAD_ORDER[2 * j + 1]])
            m = jnp.maximum(jnp.max(s, axis=-1, keepdims=True), sink)
            p_scr[j] = jnp.exp(s - m).astype(BF16)
            ps_scr[j] = jnp.broadcast_to(jnp.exp(sink - m), (2 * WB_BLOCK, LANES))

    def output(n):
        off = pl.multiple_of(n * WB_BLOCK, WB_BLOCK)
        vw = vwin[pl.ds(off, 3 * WB_BLOCK), :]
        for j in range(npair):
            res = _dot(p_scr[j], vw)
            o = res[:, 0:LANES] / (res[:, LANES:2 * LANES] + ps_scr[j])
            o_ref[0, pl.ds(off, WB_BLOCK), j * LANES:(j + 1) * LANES] = (
                jnp.where(low, o[:WB_BLOCK], o[WB_BLOCK:]).astype(BF16))

    scores(0)
    probs()
    scores(1)

    def body(n, carry):
        output(n - 2)
        probs()
        scores(n)
        return carry

    lax.fori_loop(2, per_step, body, 0, unroll=2)
    output(per_step - 2)
    probs()
    output(per_step - 1)


def _wb_attention(u, bias_tbl, sink):
    b, l, _ = u.shape
    nblk = l // WB_BLOCK
    nchunk = l // WB_CHUNK
    per_step = WB_CHUNK // WB_BLOCK
    qblk = U_QB // WB_QW
    kcol, vcol = U_KB // WB_KVW, U_VB // WB_KVW

    def halo(colblk, d):
        if d == 0:
            return pl.BlockSpec((1, WB_CHUNK, WB_KVW), lambda bi, i: (bi, i, colblk))
        return pl.BlockSpec(
            (1, WB_BLOCK, WB_KVW),
            lambda bi, i: (bi, jnp.clip(i * per_step + (per_step if d > 0 else -1), 0, nblk - 1), colblk))

    return pl.pallas_call(
        functools.partial(_wb_kernel, nblk=nblk),
        grid=(b, nchunk),
        in_specs=[
            pl.BlockSpec(memory_space=pltpu.SMEM),
            pl.BlockSpec((1, WB_CHUNK, WB_QW), lambda bi, i: (bi, i, qblk)),
            halo(kcol, -1), halo(kcol, 0), halo(kcol, 1),
            halo(vcol, -1), halo(vcol, 0), halo(vcol, 1),
            pl.BlockSpec(bias_tbl.shape, lambda bi, i: (0, 0, 0)),
        ],
        out_specs=pl.BlockSpec((1, WB_CHUNK, WB_QW), lambda bi, i: (bi, i, 0)),
        out_shape=jax.ShapeDtypeStruct((b, l, WB_QW), BF16),
        scratch_shapes=[pltpu.VMEM((WB_CHUNK + 2 * WB_BLOCK, WB_KVW), BF16),
                        pltpu.VMEM((WB_CHUNK + 2 * WB_BLOCK, 2 * WB_KVW), BF16),
                        pltpu.VMEM((WB_HEADS // 2, 2 * WB_BLOCK, 3 * WB_BLOCK), F32),
                        pltpu.VMEM((WB_HEADS // 2, 2 * WB_BLOCK, 3 * WB_BLOCK), BF16),
                        pltpu.VMEM((WB_HEADS // 2, 2 * WB_BLOCK, LANES), F32)],
        compiler_params=_params(("arbitrary", "arbitrary")),
        name="wb_attention",
    )(sink, u, u, u, u, u, u, u, bias_tbl)


def _even_out_kernel(x_ref, oa_ref, ob_ref, gate_ref, gm_ref, w_ref, o_ref):
    g = _silu(gate_ref[0].astype(F32))
    o = jnp.concatenate([oa_ref[0], ob_ref[0]], axis=-1).astype(F32)
    z = (o * g).astype(BF16)
    o_ref[0] = x_ref[0] + gm_ref[0] * _dot(z, w_ref[...])


def _even_out(x, oa, ob, u, gate_mod, w_out):
    b, l, _ = x.shape
    return pl.pallas_call(
        _even_out_kernel,
        grid=(b, l // TM),
        in_specs=[
            pl.BlockSpec((1, TM, D_MODEL), lambda bi, i: (bi, i, 0)),
            pl.BlockSpec((1, TM, NA_W), lambda bi, i: (bi, i, 0)),
            pl.BlockSpec((1, TM, WB_QW), lambda bi, i: (bi, i, 0)),
            pl.BlockSpec((1, TM, EVEN_MIX), lambda bi, i: (bi, i, U_GATE // EVEN_MIX)),
            pl.BlockSpec((1, 1, D_MODEL), lambda bi, i: (bi, 0, 0)),
            pl.BlockSpec((EVEN_MIX, D_MODEL), lambda bi, i: (0, 0)),
        ],
        out_specs=pl.BlockSpec((1, TM, D_MODEL), lambda bi, i: (bi, i, 0)),
        out_shape=jax.ShapeDtypeStruct((b, l, D_MODEL), F32),
        compiler_params=_params(("arbitrary", "arbitrary")),
        name="even_out",
    )(x, oa, ob, u, gate_mod, w_out)


def _odd_front_kernel(x_ref, g_ref, sc_ref, sh_ref, w1t_ref, wnat_ref, gq_ref, gkvt_ref, gkv_ref,
                      wqbt_ref, wk_ref, wvt_ref, cost_ref, sint_ref, cosp_ref, sinp_ref,
                      k_ref, qt_ref, vt_ref, sgt_ref):
    h = _modulated_norm(x_ref[0], g_ref[...], sc_ref[0], sh_ref[0]).astype(BF16)
    reps = TM // LANES

    ut = _dot_nt(w1t_ref[...], h)
    sgt_ref[0] = _silu(ut[0:ODD_MIX]).astype(BF16)

    qlt = ut[ODD_MIX:ODD_MIX + MLA_Q_RANK]
    qn = qlt * lax.rsqrt(jnp.mean(qlt * qlt, axis=0, keepdims=True) + EPS)
    qn = (qn * jnp.concatenate([gq_ref[...]] * reps, axis=1)).astype(BF16)
    qt = _dot(wqbt_ref[...], qn)
    qscale = (MLA_QK ** -0.5) * LOG2E
    cos_t = cost_ref[...]
    sin_t = sint_ref[...]
    half = MLA_ROPE // 2
    for hd in range(MLA_HEADS):
        base = hd * MLA_QK
        qt_ref[0, hd, 0, 0:MLA_NOPE, :] = (qt[base:base + MLA_NOPE] * qscale).astype(BF16)
        x1 = qt[base + MLA_NOPE:base + MLA_NOPE + half]
        x2 = qt[base + MLA_NOPE + half:base + MLA_QK]
        qt_ref[0, hd, 0, MLA_NOPE:MLA_NOPE + half, :] = ((x1 * cos_t - x2 * sin_t) * qscale).astype(BF16)
        qt_ref[0, hd, 0, MLA_NOPE + half:MLA_QK, :] = ((x2 * cos_t + x1 * sin_t) * qscale).astype(BF16)

    kvt = ut[ODD_MIX + MLA_Q_RANK:ODD_MIX + MLA_Q_RANK + MLA_KV_RANK]
    kvnt = kvt * lax.rsqrt(jnp.mean(kvt * kvt, axis=0, keepdims=True) + EPS)
    kvnt = (kvnt * jnp.concatenate([gkvt_ref[...]] * reps, axis=1)).astype(BF16)
    vt = _dot(wvt_ref[...], kvnt)
    vt_ref[0, :, 0] = vt.reshape(MLA_HEADS, MLA_V, TM).astype(BF16)

    nat = _dot(h, wnat_ref[...])
    kvl = nat[:, 0:MLA_KV_RANK]
    kvn = kvl * lax.rsqrt(jnp.mean(kvl * kvl, axis=-1, keepdims=True) + EPS)
    kvn = (kvn * gkv_ref[...]).astype(BF16)
    kn = _dot(kvn, wk_ref[...])
    kpe = nat[:, LANES:2 * LANES] * cosp_ref[...] + nat[:, 2 * LANES:3 * LANES] * sinp_ref[...]
    for hd in range(MLA_HEADS):
        kh = kn[:, hd * LANES:(hd + 1) * LANES] + kpe
        k_ref[0, hd] = kh[:, 0:MLA_QK].astype(BF16)


def _odd_front(x, g, scale, shift, w1t, wnat, gq, gkvt, gkv, wqbt, wk, wvt, cos_t, sin_t, cos_p, sin_p):
    b, l, _ = x.shape
    nt = l // TM
    const2 = lambda bi, i: (0, 0)
    n1 = w1t.shape[0]
    return pl.pallas_call(
        _odd_front_kernel,
        grid=(b, nt),
        in_specs=[
            pl.BlockSpec((1, TM, D_MODEL), lambda bi, i: (bi, i, 0)),
            pl.BlockSpec((1, D_MODEL), const2),
            pl.BlockSpec((1, 1, D_MODEL), lambda bi, i: (bi, 0, 0)),
            pl.BlockSpec((1, 1, D_MODEL), lambda bi, i: (bi, 0, 0)),
            pl.BlockSpec((n1, D_MODEL), const2),
            pl.BlockSpec((D_MODEL, 3 * LANES), const2),
            pl.BlockSpec((MLA_Q_RANK, LANES), const2),
            pl.BlockSpec((MLA_KV_RANK, LANES), const2),
            pl.BlockSpec((1, MLA_KV_RANK), const2),
            pl.BlockSpec((MLA_HEADS * MLA_QK, MLA_Q_RANK), const2),
            pl.BlockSpec((MLA_KV_RANK, MLA_HEADS * LANES), const2),
            pl.BlockSpec((ODD_MIX, MLA_KV_RANK), const2),
            pl.BlockSpec((MLA_ROPE // 2, TM), lambda bi, i: (0, i)),
            pl.BlockSpec((MLA_ROPE // 2, TM), lambda bi, i: (0, i)),
            pl.BlockSpec((TM, LANES), lambda bi, i: (i, 0)),
            pl.BlockSpec((TM, LANES), lambda bi, i: (i, 0)),
        ],
        out_specs=[
            pl.BlockSpec((1, MLA_HEADS, TM, MLA_QK), lambda bi, i: (bi, 0, i, 0)),
            pl.BlockSpec((1, MLA_HEADS, 1, MLA_QK, TM), lambda bi, i: (bi, 0, i, 0, 0)),
            pl.BlockSpec((1, MLA_HEADS, 1, MLA_V, TM), lambda bi, i: (bi, 0, i, 0, 0)),
            pl.BlockSpec((1, ODD_MIX, TM), lambda bi, i: (bi, 0, i)),
        ],
        out_shape=[
            jax.ShapeDtypeStruct((b, MLA_HEADS, l, MLA_QK), BF16),
            jax.ShapeDtypeStruct((b, MLA_HEADS, nt, MLA_QK, TM), BF16),
            jax.ShapeDtypeStruct((b, MLA_HEADS, nt, MLA_V, TM), BF16),
            jax.ShapeDtypeStruct((b, ODD_MIX, l), BF16),
        ],
        compiler_params=_params(("arbitrary", "arbitrary")),
        name="odd_front",
    )(x, g, scale, shift, w1t, wnat, gq, gkvt, gkv, wqbt, wk, wvt, cos_t, sin_t, cos_p, sin_p)


def _mla_kernel(qt_ref, k_ref, vt_ref, o_ref, s_scr, p_scr, acc_scr, *, hb, nqb, npairs):
    ones = jnp.ones((ONES_ROWS, MLA_TK), BF16)
    sub = MLA_TQ // TM
    pair_bits = npairs.bit_length() - 1
    qb_bits = nqb.bit_length() - 1
    total = hb * nqb * npairs

    def decode(i):
        return i >> (pair_bits + qb_bits), (i >> pair_bits) & (nqb - 1), i & (npairs - 1)

    def scores(i):
        h, qb, u = decode(i)
        q = jnp.concatenate([qt_ref[0, h, sub * qb + j] for j in range(sub)], axis=1)
        mts = []
        for slot in range(2):
            koff = pl.multiple_of((2 * u + slot) * MLA_TK, MLA_TK)
            s = _dot(k_ref[0, h, pl.ds(koff, MLA_TK), :], q)
            s_scr[slot] = s
            mts.append(jnp.max(s, axis=0, keepdims=True))
        return tuple(mts)

    def probs(i, m, mts):
        _, _, u = decode(i)
        m = jnp.where(u == 0, NEG, m)
        als = []
        for slot in range(2):
            m_new = jnp.maximum(m, mts[slot])
            p_scr[slot] = jnp.exp2(s_scr[slot] - m_new).astype(BF16)
            als.append(jnp.exp2(m - m_new))
            m = m_new
        return m, tuple(als)

    def accumulate(i, als):
        h, qb, u = decode(i)
        for slot in range(2):
            v1 = jnp.concatenate([vt_ref[0, h, 2 * u + slot], ones], axis=0)
            acc_scr[...] = acc_scr[...] * als[slot] + _dot(v1, p_scr[slot])
        acc = acc_scr[...]
        o_ref[0, h, qb] = (acc[0:MLA_V] * (1.0 / acc[MLA_V:MLA_V + 1])).astype(BF16)

    acc_scr[...] = jnp.zeros_like(acc_scr)
    m = jnp.full((1, MLA_TQ), NEG, F32)
    mts = scores(0)
    m, als = probs(0, m, mts)
    mts = scores(1)

    def body(i, carry):
        m, mts, als = carry
        accumulate(i - 2, als)
        m, als = probs(i - 1, m, mts)
        mts = scores(i)
        return m, mts, als

    m, mts, als = lax.fori_loop(2, total, body, (m, mts, als), unroll=2)
    accumulate(total - 2, als)
    m, als = probs(total - 1, m, mts)
    accumulate(total - 1, als)


def _mla_attention(qt, k, vt):
    b, _, nt, _, _ = qt.shape
    l = nt * TM
    nk = l // MLA_TK
    nqb = l // MLA_TQ
    npairs = nk // 2
    assert MLA_TK == TM and nk % 2 == 0
    assert nqb & (nqb - 1) == 0 and npairs & (npairs - 1) == 0
    hb = max(1, min(MLA_HEADS, MLA_RESIDENT_TOKENS // l))
    return pl.pallas_call(
        functools.partial(_mla_kernel, hb=hb, nqb=nqb, npairs=npairs),
        grid=(b, MLA_HEADS // hb),
        in_specs=[
            pl.BlockSpec((1, hb, nt, MLA_QK, TM), lambda bi, h: (bi, h, 0, 0, 0)),
            pl.BlockSpec((1, hb, l, MLA_QK), lambda bi, h: (bi, h, 0, 0)),
            pl.BlockSpec((1, hb, nk, MLA_V, MLA_TK), lambda bi, h: (bi, h, 0, 0, 0)),
        ],
        out_specs=pl.BlockSpec((1, hb, nqb, MLA_V, MLA_TQ), lambda bi, h: (bi, h, 0, 0, 0)),
        out_shape=jax.ShapeDtypeStruct((b, MLA_HEADS, nqb, MLA_V, MLA_TQ), BF16),
        scratch_shapes=[pltpu.VMEM((2, MLA_TK, MLA_TQ), F32),
                        pltpu.VMEM((2, MLA_TK, MLA_TQ), BF16),
                        pltpu.VMEM((MLA_V + ONES_ROWS, MLA_TQ), F32)],
        compiler_params=_params(("arbitrary", "arbitrary")),
        name="mla_attention",
    )(qt, k, vt)


def _odd_out_kernel(x_ref, ot_ref, sgt_ref, gm_ref, wt_ref, fg_ref, y_ref):
    ot = ot_ref[0, :, 0].reshape(ODD_MIX, TM)
    z = (ot.astype(F32) * sgt_ref[0].astype(F32)).astype(BF16)
    out = _dot(wt_ref[...], z).T
    x2 = x_ref[0] + gm_ref[0] * out
    ms = jnp.mean(x2 * x2, axis=-1, keepdims=True)
    y_ref[0] = (x2 * lax.rsqrt(ms + EPS)) * fg_ref[...]


def _odd_out(x, ot, sgt, gate_mod, w_out_t, final_g):
    b, l, _ = x.shape
    sub = MLA_TQ // TM
    return pl.pallas_call(
        _odd_out_kernel,
        grid=(b, l // TM),
        in_specs=[
            pl.BlockSpec((1, TM, D_MODEL), lambda bi, i: (bi, i, 0)),
            pl.BlockSpec((1, MLA_HEADS, 1, MLA_V, TM), lambda bi, i: (bi, 0, i // sub, 0, i % sub)),
            pl.BlockSpec((1, ODD_MIX, TM), lambda bi, i: (bi, 0, i)),
            pl.BlockSpec((1, 1, D_MODEL), lambda bi, i: (bi, 0, 0)),
            pl.BlockSpec((D_MODEL, ODD_MIX), lambda bi, i: (0, 0)),
            pl.BlockSpec((1, D_MODEL), lambda bi, i: (0, 0)),
        ],
        out_specs=pl.BlockSpec((1, TM, D_MODEL), lambda bi, i: (bi, i, 0)),
        out_shape=jax.ShapeDtypeStruct((b, l, D_MODEL), F32),
        compiler_params=_params(("arbitrary", "arbitrary")),
        name="odd_out",
    )(x, ot, sgt, gate_mod, w_out_t, final_g)


def _t5_bucket(rel):
    nb = T5_BUCKETS // 2
    ret = (rel > 0).astype(np.int32) * nb
    n = np.abs(rel)
    max_exact = nb // 2
    large = max_exact + (np.log(np.maximum(n, 1) / max_exact)
                         / np.log(T5_MAX_DIST / max_exact) * (nb - max_exact)).astype(np.int32)
    large = np.minimum(large, nb - 1)
    return ret + np.where(n < max_exact, n, large)


def _na_bias_table(rpb):
    kh = NA_WIN_H
    c = np.arange(GRID_W)
    col_start = np.clip(c - NA_WIN_W // 2, 0, GRID_W - NA_WIN_W)
    col_ok = (c[None, :] >= col_start[:, None]) & (c[None, :] < col_start[:, None] + NA_WIN_W)
    d_col = np.clip(c[None, :] - c[:, None], -(NA_WIN_W - 1), NA_WIN_W - 1) + NA_WIN_W - 1
    nrel = 2 * NA_WIN_W - 1
    rows = jnp.stack([rpb[:, NA_WIN_H - 1 - v:2 * NA_WIN_H - 1 - v] for v in range(kh)], axis=1)
    onehot = (d_col.reshape(-1)[None, :] == np.arange(nrel)[:, None]).astype(np.float32)
    bias = jnp.dot(rows.astype(F32).reshape(-1, nrel), jnp.asarray(onehot),
                   precision=lax.Precision.HIGHEST)
    bias = bias.reshape(NA_HEADS, kh, kh, GRID_W, GRID_W)
    bias = jnp.where(jnp.asarray(col_ok)[None, None, None, :, :], bias, NEG)
    bias = bias.transpose(1, 0, 3, 2, 4).reshape(kh, NA_HEADS // 2, 2 * GRID_W, kh * GRID_W)
    return bias


def _wb_bias_table(t5_bias):
    rel = (np.arange(3 * WB_BLOCK) - WB_BLOCK)[None, :] - np.arange(WB_BLOCK)[:, None]
    offs = np.arange(-(2 * WB_BLOCK - 1), 2 * WB_BLOCK + 1)
    period = 4 * WB_BLOCK
    by_off = t5_bias[_t5_bucket(offs)].astype(F32).T
    shifted = jnp.tile(by_off, (1, WB_BLOCK))[:, :WB_BLOCK * (period - 1)]
    shifted = shifted.reshape(WB_HEADS, WB_BLOCK, period - 1)
    bias = shifted[:, :, WB_BLOCK - 1:4 * WB_BLOCK - 1]
    bias = jnp.where(jnp.asarray(np.abs(rel) <= WB_WINDOW)[None], bias, NEG)
    bias = bias[np.asarray(WB_HEAD_ORDER)]
    return bias.reshape(WB_HEADS // 2, 2 * WB_BLOCK, 3 * WB_BLOCK)


def _even_weights(w_in, w_out):
    qa, ka, va, qb, kb, vb, gate = jnp.split(
        w_in, [NA_W, 2 * NA_W, 3 * NA_W, 3 * NA_W + WB_QW, 3 * NA_W + WB_QW + WB_KVW,
               3 * NA_W + WB_QW + 2 * WB_KVW], axis=-1)
    order = np.asarray(WB_HEAD_ORDER)
    perm = (order[:, None] * HEAD_DIM + np.arange(HEAD_DIM)[None, :]).reshape(-1)
    qscale = HEAD_DIM ** -0.5
    gate = jnp.concatenate([gate[:, :NA_W], gate[:, NA_W:][:, perm]], axis=-1)
    w = jnp.concatenate([gate, qa * qscale, ka, va, qb[:, perm] * qscale, kb, vb], axis=-1)
    w_out_p = jnp.concatenate([w_out[:NA_W], w_out[NA_W:][perm]], axis=0)
    return w.astype(BF16), w_out_p.astype(BF16)


def _odd_weights(w_in, q_norm, w_qb, kv_norm, w_kvb, w_out):
    q_lat, kv_lat, k_rope, gate = jnp.split(
        w_in, [MLA_Q_RANK, MLA_Q_RANK + MLA_KV_RANK, MLA_Q_RANK + MLA_KV_RANK + MLA_ROPE], axis=-1)
    w1t = jnp.concatenate([gate, q_lat, kv_lat], axis=-1).T.astype(BF16)
    half = MLA_ROPE // 2
    k_rot = jnp.concatenate([-k_rope[:, half:], k_rope[:, :half]], axis=-1)
    z64 = jnp.zeros((D_MODEL, MLA_NOPE), w_in.dtype)
    z32 = jnp.zeros((D_MODEL, LANES - MLA_QK), w_in.dtype)
    wnat = jnp.concatenate([kv_lat, z64, k_rope, z32, z64, k_rot, z32], axis=-1).astype(BF16)
    wkv = w_kvb.reshape(MLA_KV_RANK, MLA_HEADS, MLA_NOPE + MLA_V)
    wk = jnp.concatenate([wkv[:, :, :MLA_NOPE], jnp.zeros_like(wkv[:, :, :MLA_NOPE])], axis=-1)
    wk = wk.reshape(MLA_KV_RANK, MLA_HEADS * LANES).astype(BF16)
    wvt = wkv[:, :, MLA_NOPE:].reshape(MLA_KV_RANK, ODD_MIX).T.astype(BF16)
    gq = jnp.broadcast_to(q_norm.astype(F32)[:, None], (MLA_Q_RANK, LANES))
    gkvt = jnp.broadcast_to(kv_norm.astype(F32)[:, None], (MLA_KV_RANK, LANES))
    gkv = kv_norm.astype(F32)[None, :]
    return w1t, wnat, gq, gkvt, gkv, w_qb.T.astype(BF16), wk, wvt, w_out.T.astype(BF16)


def _rope_tables(l):
    inv_freq = 1.0 / (ROPE_THETA ** (jnp.arange(0, MLA_ROPE, 2, dtype=F32) / MLA_ROPE))
    ang = jnp.arange(l, dtype=F32)[:, None] * inv_freq[None, :]
    cos, sin = jnp.cos(ang), jnp.sin(ang)
    z64 = jnp.zeros((l, MLA_NOPE), F32)
    z32 = jnp.zeros((l, LANES - MLA_QK), F32)
    cos_p = jnp.concatenate([z64, cos, cos, z32], axis=-1)
    sin_p = jnp.concatenate([z64, sin, sin, z32], axis=-1)
    return cos.T, sin.T, cos_p, sin_p


def _trunk(x, mod, norm_g, ev, na_tbl, wb_tbl, sink, od, final_g):
    b, l, _ = x.shape
    assert l % TM == 0 and l % NA_CHUNK == 0 and l % WB_CHUNK == 0 and l % MLA_TK == 0
    assert l // GRID_W >= NA_ROWS_PER_STEP >= NA_HALO_ROWS >= NA_WIN_H
    shift0, scale0, gate0 = [t[:, None, :] for t in jnp.split(mod[0], 3, axis=-1)]
    shift1, scale1, gate1 = [t[:, None, :] for t in jnp.split(mod[1], 3, axis=-1)]

    w_in0, w_out0 = ev
    u = _even_front(x, norm_g[0][None, :], scale0, shift0, w_in0)
    oa = _na_attention(u, na_tbl)
    ob = _wb_attention(u, wb_tbl, sink)
    x1 = _even_out(x, oa, ob, u, gate0, w_out0)

    w1t, wnat, gq, gkvt, gkv, wqbt, wk, wvt, w_out1t = od
    cos_t, sin_t, cos_p, sin_p = _rope_tables(l)
    k, qt, vt, sgt = _odd_front(x1, norm_g[1][None, :], scale1, shift1, w1t, wnat, gq, gkvt, gkv,
                                wqbt, wk, wvt, cos_t, sin_t, cos_p, sin_p)
    ot = _mla_attention(qt, k, vt)
    return _odd_out(x1, ot, sgt, gate1, w_out1t, final_g[None, :])


def kernel(x_prompt, x_sample, c_prompt, c_sample, ada_w, ada_b, norm_g, t5_bias, ev_w_in, na_rpb,
           wb_sink, ev_w_out, mla_w_in, mla_q_norm, mla_w_qb, mla_kv_norm, mla_w_kvb, mla_w_out, final_g):
    bp, bs = c_prompt.shape[0], c_sample.shape[0]
    rows = -(-(bp + bs) // 16) * 16
    c_pad = jnp.concatenate([c_prompt, c_sample, jnp.zeros((rows - bp - bs, D_MODEL), F32)], axis=0)
    mod = _ada_mod(c_pad, ada_w.astype(BF16), ada_b[:, None, :])

    ev = _even_weights(ev_w_in[0], ev_w_out[0])
    na_tbl = _na_bias_table(na_rpb[0])
    wb_tbl = _wb_bias_table(t5_bias)
    od = _odd_weights(mla_w_in[0], mla_q_norm[0], mla_w_qb[0], mla_kv_norm[0], mla_w_kvb[0], mla_w_out[0])
    sink = wb_sink[0].astype(F32)

    y_prompt = _trunk(x_prompt, mod[:, :bp], norm_g, ev, na_tbl, wb_tbl, sink, od, final_g)
    y_sample = _trunk(x_sample, mod[:, bp:bp + bs], norm_g, ev, na_tbl, wb_tbl, sink, od, final_g)
    return (y_prompt, y_sample)
```

```python
import functools
import math

import numpy as np
import jax
import jax.numpy as jnp
from jax import lax
from jax.experimental import pallas as pl
from jax.experimental.pallas import tpu as pltpu

D_MODEL = 1024
GRID_W = 64
HEAD_DIM = 64
EPS = 1e-6
NEG = -1e30
NA_HEADS = 8
NA_WIN_H = 8
NA_WIN_W = 16
WB_HEADS = 8
WB_KV_HEADS = 2
WB_WINDOW = 128
WB_BLOCK = 128
T5_BUCKETS = 32
T5_MAX_DIST = 128
MLA_HEADS = 16
MLA_Q_RANK = 256
MLA_KV_RANK = 128
MLA_NOPE = 64
MLA_ROPE = 32
MLA_V = 64
ROPE_THETA = 10000.0
MLA_QK = MLA_NOPE + MLA_ROPE

NA_W = NA_HEADS * HEAD_DIM
WB_QW = WB_HEADS * HEAD_DIM
WB_KVW = WB_KV_HEADS * HEAD_DIM
EVEN_MIX = NA_W + WB_QW
EVEN_IN = 3 * NA_W + WB_QW + 2 * WB_KVW + EVEN_MIX
ODD_MIX = MLA_HEADS * MLA_V

U_GATE = 0
U_QA = EVEN_MIX
U_KA = U_QA + NA_W
U_VA = U_KA + NA_W
U_QB = U_VA + NA_W
U_KB = U_QB + WB_QW
U_VB = U_KB + WB_KVW

LANES = 128
TM = 512
NA_ROWS_PER_STEP = 16
NA_HALO_ROWS = 8
NA_CHUNK = NA_ROWS_PER_STEP * GRID_W
WB_CHUNK = 1024
MLA_TQ = 1024
MLA_TK = 512
MLA_RESIDENT_TOKENS = 16384
ONES_ROWS = 16
VMEM_LIMIT = 56 * 1024 * 1024

BF16 = jnp.bfloat16
F32 = jnp.float32
LOG2E = math.log2(math.e)

WB_HEAD_ORDER = (0, 4, 1, 5, 2, 6, 3, 7)


def _params(sem):
    return pltpu.CompilerParams(dimension_semantics=sem, vmem_limit_bytes=VMEM_LIMIT)


def _dot(a, b):
    return jnp.dot(a, b, preferred_element_type=F32)


def _dot_nt(a, b):
    return lax.dot_general(a, b, (((1,), (1,)), ((), ())), preferred_element_type=F32)


def _silu(x):
    return x * (1.0 / (1.0 + jnp.exp(-x)))


def _modulated_norm(x, g, scale, shift):
    ms = jnp.mean(x * x, axis=-1, keepdims=True)
    y = x * lax.rsqrt(ms + EPS)
    return (y * g) * (1.0 + scale) + shift


def _ada_kernel(c_ref, w_ref, b_ref, o_ref):
    c = c_ref[...]
    cs = _silu(c).astype(BF16)
    o_ref[0] = _dot(cs, w_ref[0]) + b_ref[0]


def _ada_mod(c_pad, ada_w, ada_b):
    depth = ada_w.shape[0]
    rows = c_pad.shape[0]
    tn = 768
    return pl.pallas_call(
        _ada_kernel,
        grid=(depth, 3 * D_MODEL // tn),
        in_specs=[
            pl.BlockSpec((rows, D_MODEL), lambda i, n: (0, 0)),
            pl.BlockSpec((1, D_MODEL, tn), lambda i, n: (i, 0, n)),
            pl.BlockSpec((1, 1, tn), lambda i, n: (i, 0, n)),
        ],
        out_specs=pl.BlockSpec((1, rows, tn), lambda i, n: (i, 0, n)),
        out_shape=jax.ShapeDtypeStruct((depth, rows, 3 * D_MODEL), F32),
        compiler_params=_params(("arbitrary", "arbitrary")),
        name="ada_mod",
    )(c_pad, ada_w, ada_b)


def _even_front_kernel(x_ref, g_ref, sc_ref, sh_ref, w_ref, u_ref):
    h = _modulated_norm(x_ref[0], g_ref[...], sc_ref[0], sh_ref[0]).astype(BF16)
    u_ref[0] = _dot(h, w_ref[...]).astype(BF16)


def _even_front(x, g, scale, shift, w):
    b, l, _ = x.shape
    return pl.pallas_call(
        _even_front_kernel,
        grid=(b, l // TM),
        in_specs=[
            pl.BlockSpec((1, TM, D_MODEL), lambda bi, i: (bi, i, 0)),
            pl.BlockSpec((1, D_MODEL), lambda bi, i: (0, 0)),
            pl.BlockSpec((1, 1, D_MODEL), lambda bi, i: (bi, 0, 0)),
            pl.BlockSpec((1, 1, D_MODEL), lambda bi, i: (bi, 0, 0)),
            pl.BlockSpec((D_MODEL, EVEN_IN), lambda bi, i: (0, 0)),
        ],
        out_specs=pl.BlockSpec((1, TM, EVEN_IN), lambda bi, i: (bi, i, 0)),
        out_shape=jax.ShapeDtypeStruct((b, l, EVEN_IN), BF16),
        compiler_params=_params(("arbitrary", "arbitrary")),
        name="even_front",
    )(x, g, scale, shift, w)


def _low_lane_mask(shape):
    return lax.broadcasted_iota(jnp.int32, shape, len(shape) - 1) < HEAD_DIM


def _na_kernel(q_ref, kp_ref, kc_ref, kn_ref, vp_ref, vc_ref, vn_ref, bias_ref, o_ref,
               kwin, vwin, s_scr, p_scr, *, rows):
    ci = pl.program_id(1)
    halo = NA_HALO_ROWS * GRID_W
    npair = NA_HEADS // 2
    kwin[0:halo] = kp_ref[0]
    kwin[halo:halo + NA_CHUNK] = kc_ref[0]
    kwin[halo + NA_CHUNK:2 * halo + NA_CHUNK] = kn_ref[0]
    for j in range(npair):
        lanes = slice(j * LANES, (j + 1) * LANES)
        dst = slice(2 * j * LANES, (2 * j + 1) * LANES)
        vwin[0:halo, dst] = vp_ref[0, :, lanes]
        vwin[halo:halo + NA_CHUNK, dst] = vc_ref[0, :, lanes]
        vwin[halo + NA_CHUNK:2 * halo + NA_CHUNK, dst] = vn_ref[0, :, lanes]
        vwin[:, (2 * j + 1) * LANES:(2 * j + 2) * LANES] = jnp.ones((2 * halo + NA_CHUNK, LANES), BF16)

    low = _low_lane_mask((GRID_W, LANES))
    kh = NA_WIN_H
    r0 = ci * NA_ROWS_PER_STEP

    def offsets(i):
        r = r0 + i
        row_start = jnp.clip(r - kh // 2, 0, rows - kh)
        variant = r - row_start
        start = pl.multiple_of((row_start - r0 + NA_HALO_ROWS) * GRID_W, GRID_W)
        return variant, start, pl.multiple_of(i * GRID_W, GRID_W)

    def scores(i):
        variant, start, qoff = offsets(i)
        for j in range(npair):
            lanes = slice(j * LANES, (j + 1) * LANES)
            q2 = q_ref[0, pl.ds(qoff, GRID_W), lanes]
            zero = jnp.zeros_like(q2)
            lhs = jnp.concatenate([jnp.where(low, q2, zero), jnp.where(low, zero, q2)], axis=0)
            kw = kwin[pl.ds(start, kh * GRID_W), lanes]
            s_scr[j] = _dot_nt(lhs, kw) + bias_ref[variant, j]

    def probs():
        for j in range(npair):
            s = s_scr[j]
            p_scr[j] = jnp.exp(s - jnp.max(s, axis=-1, keepdims=True)).astype(BF16)

    def output(i):
        _, start, qoff = offsets(i)
        for j in range(npair):
            vw = vwin[pl.ds(start, kh * GRID_W), 2 * j * LANES:(2 * j + 2) * LANES]
            res = _dot(p_scr[j], vw)
            o = res[:, 0:LANES] / res[:, LANES:2 * LANES]
            o_ref[0, pl.ds(qoff, GRID_W), j * LANES:(j + 1) * LANES] = (
                jnp.where(low, o[:GRID_W], o[GRID_W:]).astype(BF16))

    scores(0)
    probs()
    scores(1)

    def body(i, carry):
        output(i - 2)
        probs()
        scores(i)
        return carry

    lax.fori_loop(2, NA_ROWS_PER_STEP, body, 0, unroll=2)
    output(NA_ROWS_PER_STEP - 2)
    probs()
    output(NA_ROWS_PER_STEP - 1)


def _na_attention(u, bias_tbl):
    b, l, _ = u.shape
    rows = l // GRID_W
    nchunk = l // NA_CHUNK
    halo = NA_HALO_ROWS * GRID_W
    per_chunk = NA_CHUNK // halo
    nhalo = l // halo
    qblk, kblk, vblk = U_QA // NA_W, U_KA // NA_W, U_VA // NA_W

    def window(col):
        return [
            pl.BlockSpec((1, halo, NA_W), lambda bi, i: (bi, jnp.maximum(i * per_chunk - 1, 0), col)),
            pl.BlockSpec((1, NA_CHUNK, NA_W), lambda bi, i: (bi, i, col)),
            pl.BlockSpec((1, halo, NA_W), lambda bi, i: (bi, jnp.minimum((i + 1) * per_chunk, nhalo - 1), col)),
        ]

    npair = NA_HEADS // 2
    return pl.pallas_call(
        functools.partial(_na_kernel, rows=rows),
        grid=(b, nchunk),
        in_specs=[pl.BlockSpec((1, NA_CHUNK, NA_W), lambda bi, i: (bi, i, qblk))]
        + window(kblk) + window(vblk)
        + [pl.BlockSpec(bias_tbl.shape, lambda bi, i: (0, 0, 0, 0))],
        out_specs=pl.BlockSpec((1, NA_CHUNK, NA_W), lambda bi, i: (bi, i, 0)),
        out_shape=jax.ShapeDtypeStruct((b, l, NA_W), BF16),
        scratch_shapes=[pltpu.VMEM((NA_CHUNK + 2 * halo, NA_W), BF16),
                        pltpu.VMEM((NA_CHUNK + 2 * halo, 2 * NA_W), BF16),
                        pltpu.VMEM((npair, 2 * GRID_W, NA_WIN_H * GRID_W), F32),
                        pltpu.VMEM((npair, 2 * GRID_W, NA_WIN_H * GRID_W), BF16)],
        compiler_params=_params(("arbitrary", "arbitrary")),
        name="na_attention",
    )(u, u, u, u, u, u, u, bias_tbl)


def _wb_kernel(sink_ref, q_ref, kp_ref, kc_ref, kn_ref, vp_ref, vc_ref, vn_ref, bias_ref, o_ref,
               kwin, vwin, s_scr, p_scr, ps_scr, *, nblk):
    ci = pl.program_id(1)
    npair = WB_HEADS // 2
    per_step = WB_CHUNK // WB_BLOCK
    win = WB_CHUNK + 2 * WB_BLOCK
    kwin[0:WB_BLOCK] = kp_ref[0]
    kwin[WB_BLOCK:WB_BLOCK + WB_CHUNK] = kc_ref[0]
    kwin[WB_BLOCK + WB_CHUNK:win] = kn_ref[0]
    vwin[0:WB_BLOCK, 0:LANES] = vp_ref[0]
    vwin[WB_BLOCK:WB_BLOCK + WB_CHUNK, 0:LANES] = vc_ref[0]
    vwin[WB_BLOCK + WB_CHUNK:win, 0:LANES] = vn_ref[0]
    vwin[:, LANES:2 * LANES] = jnp.ones((win, LANES), BF16)

    low = _low_lane_mask((WB_BLOCK, LANES))
    col = lax.broadcasted_iota(jnp.int32, (2 * WB_BLOCK, 3 * WB_BLOCK), 1)
    row = lax.broadcasted_iota(jnp.int32, (2 * WB_BLOCK, 1), 0)

    def scores(n):
        gblk = ci * per_step + n
        lo = jnp.where(gblk > 0, 0, WB_BLOCK)
        hi = jnp.where(gblk < nblk - 1, 3 * WB_BLOCK, 2 * WB_BLOCK)
        in_seq = jnp.logical_and(col >= lo, col < hi)
        off = pl.multiple_of(n * WB_BLOCK, WB_BLOCK)
        kw = kwin[pl.ds(off, 3 * WB_BLOCK), :]
        for j in range(npair):
            q2 = q_ref[0, pl.ds(off, WB_BLOCK), j * LANES:(j + 1) * LANES]
            zero = jnp.zeros_like(q2)
            lhs = jnp.concatenate([jnp.where(low, q2, zero), jnp.where(low, zero, q2)], axis=0)
            s_scr[j] = jnp.where(in_seq, _dot_nt(lhs, kw) + bias_ref[j], NEG)

    def probs():
        for j in range(npair):
            s = s_scr[j]
            sink = jnp.where(row < WB_BLOCK, sink_ref[WB_HEAD_ORDER[2 * j]],
                             sink_ref[WB_HEAD_ORDER[2 * j + 1]])
            m = jnp.maximum(jnp.max(s, axis=-1, keepdims=True), sink)
            p_scr[j] = jnp.exp(s - m).astype(BF16)
            ps_scr[j] = jnp.broadcast_to(jnp.exp(sink - m), (2 * WB_BLOCK, LANES))

    def output(n):
        off = pl.multiple_of(n * WB_BLOCK, WB_BLOCK)
        vw = vwin[pl.ds(off, 3 * WB_BLOCK), :]
        for j in range(npair):
            res = _dot(p_scr[j], vw)
            o = res[:, 0:LANES] / (res[:, LANES:2 * LANES] + ps_scr[j])
            o_ref[0, pl.ds(off, WB_BLOCK), j * LANES:(j + 1) * LANES] = (
                jnp.where(low, o[:WB_BLOCK], o[WB_BLOCK:]).astype(BF16))

    scores(0)
    probs()
    scores(1)

    def body(n, carry):
        output(n - 2)
        probs()
        scores(n)
        return carry

    lax.fori_loop(2, per_step, body, 0, unroll=2)
    output(per_step - 2)
    probs()
    output(per_step - 1)


def _wb_attention(u, bias_tbl, sink):
    b, l, _ = u.shape
    nblk = l // WB_BLOCK
    nchunk = l // WB_CHUNK
    per_step = WB_CHUNK // WB_BLOCK
    qblk = U_QB // WB_QW
    kcol, vcol = U_KB // WB_KVW, U_VB // WB_KVW

    def halo(colblk, d):
        if d == 0:
            return pl.BlockSpec((1, WB_CHUNK, WB_KVW), lambda bi, i: (bi, i, colblk))
        return pl.BlockSpec(
            (1, WB_BLOCK, WB_KVW),
            lambda bi, i: (bi, jnp.clip(i * per_step + (per_step if d > 0 else -1), 0, nblk - 1), colblk))

    return pl.pallas_call(
        functools.partial(_wb_kernel, nblk=nblk),
        grid=(b, nchunk),
        in_specs=[
            pl.BlockSpec(memory_space=pltpu.SMEM),
            pl.BlockSpec((1, WB_CHUNK, WB_QW), lambda bi, i: (bi, i, qblk)),
            halo(kcol, -1), halo(kcol, 0), halo(kcol, 1),
            halo(vcol, -1), halo(vcol, 0), halo(vcol, 1),
            pl.BlockSpec(bias_tbl.shape, lambda bi, i: (0, 0, 0)),
        ],
        out_specs=pl.BlockSpec((1, WB_CHUNK, WB_QW), lambda bi, i: (bi, i, 0)),
        out_shape=jax.ShapeDtypeStruct((b, l, WB_QW), BF16),
        scratch_shapes=[pltpu.VMEM((WB_CHUNK + 2 * WB_BLOCK, WB_KVW), BF16),
                        pltpu.VMEM((WB_CHUNK + 2 * WB_BLOCK, 2 * WB_KVW), BF16),
                        pltpu.VMEM((WB_HEADS // 2, 2 * WB_BLOCK, 3 * WB_BLOCK), F32),
                        pltpu.VMEM((WB_HEADS // 2, 2 * WB_BLOCK, 3 * WB_BLOCK), BF16),
                        pltpu.VMEM((WB_HEADS // 2, 2 * WB_BLOCK, LANES), F32)],
        compiler_params=_params(("arbitrary", "arbitrary")),
        name="wb_attention",
    )(sink, u, u, u, u, u, u, u, bias_tbl)


def _even_out_tile(x_ref, oa_ref, ob_ref, gate_ref, gm_ref, w_ref):
    g = _silu(gate_ref[0].astype(F32))
    o = jnp.concatenate([oa_ref[0], ob_ref[0]], axis=-1).astype(F32)
    z = (o * g).astype(BF16)
    return x_ref[0] + gm_ref[0] * _dot(z, w_ref[...])


def _odd_front_kernel(x_ref, oa_ref, ob_ref, gate_ref, gm_ref, wo_ref,
                      g_ref, sc_ref, sh_ref, w1t_ref, wnat_ref, gq_ref, gkvt_ref, gkv_ref,
                      wqbt_ref, wk_ref, wvt_ref, cost_ref, sint_ref, cosp_ref, sinp_ref,
                      x1_ref, k_ref, qt_ref, vt_ref, sgt_ref):
    x1 = _even_out_tile(x_ref, oa_ref, ob_ref, gate_ref, gm_ref, wo_ref)
    x1_ref[0] = x1
    h = _modulated_norm(x1, g_ref[...], sc_ref[0], sh_ref[0]).astype(BF16)
    reps = TM // LANES

    ut = _dot_nt(w1t_ref[...], h)
    sgt_ref[0] = _silu(ut[0:ODD_MIX]).astype(BF16)

    qlt = ut[ODD_MIX:ODD_MIX + MLA_Q_RANK]
    qn = qlt * lax.rsqrt(jnp.mean(qlt * qlt, axis=0, keepdims=True) + EPS)
    qn = (qn * jnp.concatenate([gq_ref[...]] * reps, axis=1)).astype(BF16)
    qt = _dot(wqbt_ref[...], qn)
    qscale = (MLA_QK ** -0.5) * LOG2E
    cos_t = cost_ref[...]
    sin_t = sint_ref[...]
    half = MLA_ROPE // 2
    for hd in range(MLA_HEADS):
        base = hd * MLA_QK
        qt_ref[0, hd, 0, 0:MLA_NOPE, :] = (qt[base:base + MLA_NOPE] * qscale).astype(BF16)
        x1 = qt[base + MLA_NOPE:base + MLA_NOPE + half]
        x2 = qt[base + MLA_NOPE + half:base + MLA_QK]
        qt_ref[0, hd, 0, MLA_NOPE:MLA_NOPE + half, :] = ((x1 * cos_t - x2 * sin_t) * qscale).astype(BF16)
        qt_ref[0, hd, 0, MLA_NOPE + half:MLA_QK, :] = ((x2 * cos_t + x1 * sin_t) * qscale).astype(BF16)

    kvt = ut[ODD_MIX + MLA_Q_RANK:ODD_MIX + MLA_Q_RANK + MLA_KV_RANK]
    kvnt = kvt * lax.rsqrt(jnp.mean(kvt * kvt, axis=0, keepdims=True) + EPS)
    kvnt = (kvnt * jnp.concatenate([gkvt_ref[...]] * reps, axis=1)).astype(BF16)
    vt = _dot(wvt_ref[...], kvnt)
    vt_ref[0, :, 0] = vt.reshape(MLA_HEADS, MLA_V, TM).astype(BF16)

    nat = _dot(h, wnat_ref[...])
    kvl = nat[:, 0:MLA_KV_RANK]
    kvn = kvl * lax.rsqrt(jnp.mean(kvl * kvl, axis=-1, keepdims=True) + EPS)
    kvn = (kvn * gkv_ref[...]).astype(BF16)
    kn = _dot(kvn, wk_ref[...])
    kpe = nat[:, LANES:2 * LANES] * cosp_ref[...] + nat[:, 2 * LANES:3 * LANES] * sinp_ref[...]
    for hd in range(MLA_HEADS):
        kh = kn[:, hd * LANES:(hd + 1) * LANES] + kpe
        k_ref[0, hd] = kh[:, 0:MLA_QK].astype(BF16)


def _odd_front(x, oa, ob, u, gate_mod, w_out, g, scale, shift, w1t, wnat, gq, gkvt, gkv, wqbt, wk, wvt,
               cos_t, sin_t, cos_p, sin_p):
    b, l, _ = x.shape
    nt = l // TM
    const2 = lambda bi, i: (0, 0)
    n1 = w1t.shape[0]
    return pl.pallas_call(
        _odd_front_kernel,
        grid=(b, nt),
        in_specs=[
            pl.BlockSpec((1, TM, D_MODEL), lambda bi, i: (bi, i, 0)),
            pl.BlockSpec((1, TM, NA_W), lambda bi, i: (bi, i, 0)),
            pl.BlockSpec((1, TM, WB_QW), lambda bi, i: (bi, i, 0)),
            pl.BlockSpec((1, TM, EVEN_MIX), lambda bi, i: (bi, i, U_GATE // EVEN_MIX)),
            pl.BlockSpec((1, 1, D_MODEL), lambda bi, i: (bi, 0, 0)),
            pl.BlockSpec((EVEN_MIX, D_MODEL), const2),
            pl.BlockSpec((1, D_MODEL), const2),
            pl.BlockSpec((1, 1, D_MODEL), lambda bi, i: (bi, 0, 0)),
            pl.BlockSpec((1, 1, D_MODEL), lambda bi, i: (bi, 0, 0)),
            pl.BlockSpec((n1, D_MODEL), const2),
            pl.BlockSpec((D_MODEL, 3 * LANES), const2),
            pl.BlockSpec((MLA_Q_RANK, LANES), const2),
            pl.BlockSpec((MLA_KV_RANK, LANES), const2),
            pl.BlockSpec((1, MLA_KV_RANK), const2),
            pl.BlockSpec((MLA_HEADS * MLA_QK, MLA_Q_RANK), const2),
            pl.BlockSpec((MLA_KV_RANK, MLA_HEADS * LANES), const2),
            pl.BlockSpec((ODD_MIX, MLA_KV_RANK), const2),
            pl.BlockSpec((MLA_ROPE // 2, TM), lambda bi, i: (0, i)),
            pl.BlockSpec((MLA_ROPE // 2, TM), lambda bi, i: (0, i)),
            pl.BlockSpec((TM, LANES), lambda bi, i: (i, 0)),
            pl.BlockSpec((TM, LANES), lambda bi, i: (i, 0)),
        ],
        out_specs=[
            pl.BlockSpec((1, TM, D_MODEL), lambda bi, i: (bi, i, 0)),
            pl.BlockSpec((1, MLA_HEADS, TM, MLA_QK), lambda bi, i: (bi, 0, i, 0)),
            pl.BlockSpec((1, MLA_HEADS, 1, MLA_QK, TM), lambda bi, i: (bi, 0, i, 0, 0)),
            pl.BlockSpec((1, MLA_HEADS, 1, MLA_V, TM), lambda bi, i: (bi, 0, i, 0, 0)),
            pl.BlockSpec((1, ODD_MIX, TM), lambda bi, i: (bi, 0, i)),
        ],
        out_shape=[
            jax.ShapeDtypeStruct((b, l, D_MODEL), F32),
            jax.ShapeDtypeStruct((b, MLA_HEADS, l, MLA_QK), BF16),
            jax.ShapeDtypeStruct((b, MLA_HEADS, nt, MLA_QK, TM), BF16),
            jax.ShapeDtypeStruct((b, MLA_HEADS, nt, MLA_V, TM), BF16),
            jax.ShapeDtypeStruct((b, ODD_MIX, l), BF16),
        ],
        compiler_params=_params(("arbitrary", "arbitrary")),
        name="odd_front",
    )(x, oa, ob, u, gate_mod, w_out, g, scale, shift, w1t, wnat, gq, gkvt, gkv, wqbt, wk, wvt,
      cos_t, sin_t, cos_p, sin_p)


def _mla_kernel(qt_ref, k_ref, vt_ref, o_ref, s_scr, p_scr, acc_scr, *, hb, nqb, npairs):
    ones = jnp.ones((ONES_ROWS, MLA_TK), BF16)
    sub = MLA_TQ // TM
    pair_bits = npairs.bit_length() - 1
    qb_bits = nqb.bit_length() - 1
    total = hb * nqb * npairs

    def decode(i):
        return i >> (pair_bits + qb_bits), (i >> pair_bits) & (nqb - 1), i & (npairs - 1)

    def scores(i):
        h, qb, u = decode(i)
        q = jnp.concatenate([qt_ref[0, h, sub * qb + j] for j in range(sub)], axis=1)
        mts = []
        for slot in range(2):
            koff = pl.multiple_of((2 * u + slot) * MLA_TK, MLA_TK)
            s = _dot(k_ref[0, h, pl.ds(koff, MLA_TK), :], q)
            s_scr[slot] = s
            mts.append(jnp.max(s, axis=0, keepdims=True))
        return tuple(mts)

    def probs(i, m, mts):
        _, _, u = decode(i)
        m = jnp.where(u == 0, NEG, m)
        als = []
        for slot in range(2):
            m_new = jnp.maximum(m, mts[slot])
            p_scr[slot] = jnp.exp2(s_scr[slot] - m_new).astype(BF16)
            als.append(jnp.exp2(m - m_new))
            m = m_new
        return m, tuple(als)

    def accumulate(i, als):
        h, qb, u = decode(i)
        for slot in range(2):
            v1 = jnp.concatenate([vt_ref[0, h, 2 * u + slot], ones], axis=0)
            acc_scr[...] = acc_scr[...] * als[slot] + _dot(v1, p_scr[slot])
        acc = acc_scr[...]
        o_ref[0, h, qb] = (acc[0:MLA_V] * (1.0 / acc[MLA_V:MLA_V + 1])).astype(BF16)

    acc_scr[...] = jnp.zeros_like(acc_scr)
    m = jnp.full((1, MLA_TQ), NEG, F32)
    mts = scores(0)
    m, als = probs(0, m, mts)
    mts = scores(1)

    def body(i, carry):
        m, mts, als = carry
        accumulate(i - 2, als)
        m, als = probs(i - 1, m, mts)
        mts = scores(i)
        return m, mts, als

    m, mts, als = lax.fori_loop(2, total, body, (m, mts, als), unroll=2)
    accumulate(total - 2, als)
    m, als = probs(total - 1, m, mts)
    accumulate(total - 1, als)


def _mla_attention(qt, k, vt):
    b, _, nt, _, _ = qt.shape
    l = nt * TM
    nk = l // MLA_TK
    nqb = l // MLA_TQ
    npairs = nk // 2
    assert MLA_TK == TM and nk % 2 == 0
    assert nqb & (nqb - 1) == 0 and npairs & (npairs - 1) == 0
    hb = max(1, min(MLA_HEADS, MLA_RESIDENT_TOKENS // l))
    return pl.pallas_call(
        functools.partial(_mla_kernel, hb=hb, nqb=nqb, npairs=npairs),
        grid=(b, MLA_HEADS // hb),
        in_specs=[
            pl.BlockSpec((1, hb, nt, MLA_QK, TM), lambda bi, h: (bi, h, 0, 0, 0)),
            pl.BlockSpec((1, hb, l, MLA_QK), lambda bi, h: (bi, h, 0, 0)),
            pl.BlockSpec((1, hb, nk, MLA_V, MLA_TK), lambda bi, h: (bi, h, 0, 0, 0)),
        ],
        out_specs=pl.BlockSpec((1, hb, nqb, MLA_V, MLA_TQ), lambda bi, h: (bi, h, 0, 0, 0)),
        out_shape=jax.ShapeDtypeStruct((b, MLA_HEADS, nqb, MLA_V, MLA_TQ), BF16),
        scratch_shapes=[pltpu.VMEM((2, MLA_TK, MLA_TQ), F32),
                        pltpu.VMEM((2, MLA_TK, MLA_TQ), BF16),
                        pltpu.VMEM((MLA_V + ONES_ROWS, MLA_TQ), F32)],
        compiler_params=_params(("arbitrary", "arbitrary")),
        name="mla_attention",
    )(qt, k, vt)


def _odd_out_kernel(x_ref, ot_ref, sgt_ref, gm_ref, wt_ref, fg_ref, y_ref):
    ot = ot_ref[0, :, 0].reshape(ODD_MIX, TM)
    z = (ot.astype(F32) * sgt_ref[0].astype(F32)).astype(BF16)
    out = lax.dot_general(z, wt_ref[...], (((0,), (1,)), ((), ())), preferred_element_type=F32)
    x2 = x_ref[0] + gm_ref[0] * out
    ms = jnp.mean(x2 * x2, axis=-1, keepdims=True)
    y_ref[0] = (x2 * lax.rsqrt(ms + EPS)) * fg_ref[...]


def _odd_out(x, ot, sgt, gate_mod, w_out_t, final_g):
    b, l, _ = x.shape
    sub = MLA_TQ // TM
    return pl.pallas_call(
        _odd_out_kernel,
        grid=(b, l // TM),
        in_specs=[
            pl.BlockSpec((1, TM, D_MODEL), lambda bi, i: (bi, i, 0)),
            pl.BlockSpec((1, MLA_HEADS, 1, MLA_V, TM), lambda bi, i: (bi, 0, i // sub, 0, i % sub)),
            pl.BlockSpec((1, ODD_MIX, TM), lambda bi, i: (bi, 0, i)),
            pl.BlockSpec((1, 1, D_MODEL), lambda bi, i: (bi, 0, 0)),
            pl.BlockSpec((D_MODEL, ODD_MIX), lambda bi, i: (0, 0)),
            pl.BlockSpec((1, D_MODEL), lambda bi, i: (0, 0)),
        ],
        out_specs=pl.BlockSpec((1, TM, D_MODEL), lambda bi, i: (bi, i, 0)),
        out_shape=jax.ShapeDtypeStruct((b, l, D_MODEL), F32),
        compiler_params=_params(("arbitrary", "arbitrary")),
        name="odd_out",
    )(x, ot, sgt, gate_mod, w_out_t, final_g)


def _t5_bucket(rel):
    nb = T5_BUCKETS // 2
    ret = (rel > 0).astype(np.int32) * nb
    n = np.abs(rel)
    max_exact = nb // 2
    large = max_exact + (np.log(np.maximum(n, 1) / max_exact)
                         / np.log(T5_MAX_DIST / max_exact) * (nb - max_exact)).astype(np.int32)
    large = np.minimum(large, nb - 1)
    return ret + np.where(n < max_exact, n, large)


def _na_bias_table(rpb):
    kh = NA_WIN_H
    c = np.arange(GRID_W)
    col_start = np.clip(c - NA_WIN_W // 2, 0, GRID_W - NA_WIN_W)
    col_ok = (c[None, :] >= col_start[:, None]) & (c[None, :] < col_start[:, None] + NA_WIN_W)
    d_col = np.clip(c[None, :] - c[:, None], -(NA_WIN_W - 1), NA_WIN_W - 1) + NA_WIN_W - 1
    nrel = 2 * NA_WIN_W - 1
    rows = jnp.stack([rpb[:, NA_WIN_H - 1 - v:2 * NA_WIN_H - 1 - v] for v in range(kh)], axis=1)
    onehot = (d_col.reshape(-1)[None, :] == np.arange(nrel)[:, None]).astype(np.float32)
    bias = jnp.dot(rows.astype(F32).reshape(-1, nrel), jnp.asarray(onehot),
                   precision=lax.Precision.HIGHEST)
    bias = bias.reshape(NA_HEADS, kh, kh, GRID_W, GRID_W)
    bias = jnp.where(jnp.asarray(col_ok)[None, None, None, :, :], bias, NEG)
    bias = bias.transpose(1, 0, 3, 2, 4).reshape(kh, NA_HEADS // 2, 2 * GRID_W, kh * GRID_W)
    return bias


def _wb_bias_table(t5_bias):
    rel = (np.arange(3 * WB_BLOCK) - WB_BLOCK)[None, :] - np.arange(WB_BLOCK)[:, None]
    offs = np.arange(-(2 * WB_BLOCK - 1), 2 * WB_BLOCK + 1)
    period = 4 * WB_BLOCK
    by_off = t5_bias[_t5_bucket(offs)].astype(F32).T
    shifted = jnp.tile(by_off, (1, WB_BLOCK))[:, :WB_BLOCK * (period - 1)]
    shifted = shifted.reshape(WB_HEADS, WB_BLOCK, period - 1)
    bias = shifted[:, :, WB_BLOCK - 1:4 * WB_BLOCK - 1]
    bias = jnp.where(jnp.asarray(np.abs(rel) <= WB_WINDOW)[None], bias, NEG)
    bias = bias[np.asarray(WB_HEAD_ORDER)]
    return bias.reshape(WB_HEADS // 2, 2 * WB_BLOCK, 3 * WB_BLOCK)


def _even_weights(w_in, w_out):
    qa, ka, va, qb, kb, vb, gate = jnp.split(
        w_in, [NA_W, 2 * NA_W, 3 * NA_W, 3 * NA_W + WB_QW, 3 * NA_W + WB_QW + WB_KVW,
               3 * NA_W + WB_QW + 2 * WB_KVW], axis=-1)
    order = np.asarray(WB_HEAD_ORDER)
    perm = (order[:, None] * HEAD_DIM + np.arange(HEAD_DIM)[None, :]).reshape(-1)
    qscale = HEAD_DIM ** -0.5
    gate = jnp.concatenate([gate[:, :NA_W], gate[:, NA_W:][:, perm]], axis=-1)
    w = jnp.concatenate([gate, qa * qscale, ka, va, qb[:, perm] * qscale, kb, vb], axis=-1)
    w_out_p = jnp.concatenate([w_out[:NA_W], w_out[NA_W:][perm]], axis=0)
    return w.astype(BF16), w_out_p.astype(BF16)


def _odd_weights(w_in, q_norm, w_qb, kv_norm, w_kvb, w_out):
    q_lat, kv_lat, k_rope, gate = jnp.split(
        w_in, [MLA_Q_RANK, MLA_Q_RANK + MLA_KV_RANK, MLA_Q_RANK + MLA_KV_RANK + MLA_ROPE], axis=-1)
    w1t = jnp.concatenate([gate, q_lat, kv_lat], axis=-1).T.astype(BF16)
    half = MLA_ROPE // 2
    k_rot = jnp.concatenate([-k_rope[:, half:], k_rope[:, :half]], axis=-1)
    z64 = jnp.zeros((D_MODEL, MLA_NOPE), w_in.dtype)
    z32 = jnp.zeros((D_MODEL, LANES - MLA_QK), w_in.dtype)
    wnat = jnp.concatenate([kv_lat, z64, k_rope, z32, z64, k_rot, z32], axis=-1).astype(BF16)
    wkv = w_kvb.reshape(MLA_KV_RANK, MLA_HEADS, MLA_NOPE + MLA_V)
    wk = jnp.concatenate([wkv[:, :, :MLA_NOPE], jnp.zeros_like(wkv[:, :, :MLA_NOPE])], axis=-1)
    wk = wk.reshape(MLA_KV_RANK, MLA_HEADS * LANES).astype(BF16)
    wvt = wkv[:, :, MLA_NOPE:].reshape(MLA_KV_RANK, ODD_MIX).T.astype(BF16)
    gq = jnp.broadcast_to(q_norm.astype(F32)[:, None], (MLA_Q_RANK, LANES))
    gkvt = jnp.broadcast_to(kv_norm.astype(F32)[:, None], (MLA_KV_RANK, LANES))
    gkv = kv_norm.astype(F32)[None, :]
    return w1t, wnat, gq, gkvt, gkv, w_qb.T.astype(BF16), wk, wvt, w_out.T.astype(BF16)


def _rope_tables(l):
    inv_freq = 1.0 / (ROPE_THETA ** (jnp.arange(0, MLA_ROPE, 2, dtype=F32) / MLA_ROPE))
    ang = jnp.arange(l, dtype=F32)[:, None] * inv_freq[None, :]
    cos, sin = jnp.cos(ang), jnp.sin(ang)
    z64 = jnp.zeros((l, MLA_NOPE), F32)
    z32 = jnp.zeros((l, LANES - MLA_QK), F32)
    cos_p = jnp.concatenate([z64, cos, cos, z32], axis=-1)
    sin_p = jnp.concatenate([z64, sin, sin, z32], axis=-1)
    return cos.T, sin.T, cos_p, sin_p


def _trunk(x, mod, norm_g, ev, na_tbl, wb_tbl, sink, od, final_g):
    b, l, _ = x.shape
    assert l % TM == 0 and l % NA_CHUNK == 0 and l % WB_CHUNK == 0 and l % MLA_TK == 0
    assert l // GRID_W >= NA_ROWS_PER_STEP >= NA_HALO_ROWS >= NA_WIN_H
    shift0, scale0, gate0 = [t[:, None, :] for t in jnp.split(mod[0], 3, axis=-1)]
    shift1, scale1, gate1 = [t[:, None, :] for t in jnp.split(mod[1], 3, axis=-1)]

    w_in0, w_out0 = ev
    u = _even_front(x, norm_g[0][None, :], scale0, shift0, w_in0)
    oa = _na_attention(u, na_tbl)
    ob = _wb_attention(u, wb_tbl, sink)

    w1t, wnat, gq, gkvt, gkv, wqbt, wk, wvt, w_out1t = od
    cos_t, sin_t, cos_p, sin_p = _rope_tables(l)
    x1, k, qt, vt, sgt = _odd_front(x, oa, ob, u, gate0, w_out0, norm_g[1][None, :], scale1, shift1, w1t, wnat, gq, gkvt, gkv,
                                wqbt, wk, wvt, cos_t, sin_t, cos_p, sin_p)
    ot = _mla_attention(qt, k, vt)
    return _odd_out(x1, ot, sgt, gate1, w_out1t, final_g[None, :])


def kernel(x_prompt, x_sample, c_prompt, c_sample, ada_w, ada_b, norm_g, t5_bias, ev_w_in, na_rpb,
           wb_sink, ev_w_out, mla_w_in, mla_q_norm, mla_w_qb, mla_kv_norm, mla_w_kvb, mla_w_out, final_g):
    bp, bs = c_prompt.shape[0], c_sample.shape[0]
    rows = -(-(bp + bs) // 16) * 16
    c_pad = jnp.concatenate([c_prompt, c_sample, jnp.zeros((rows - bp - bs, D_MODEL), F32)], axis=0)
    mod = _ada_mod(c_pad, ada_w.astype(BF16), ada_b[:, None, :])

    ev = _even_weights(ev_w_in[0], ev_w_out[0])
    na_tbl = _na_bias_table(na_rpb[0])
    wb_tbl = _wb_bias_table(t5_bias)
    od = _odd_weights(mla_w_in[0], mla_q_norm[0], mla_w_qb[0], mla_kv_norm[0], mla_w_kvb[0], mla_w_out[0])
    sink = wb_sink[0].astype(F32)

    y_prompt = _trunk(x_prompt, mod[:, :bp], norm_g, ev, na_tbl, wb_tbl, sink, od, final_g)
    y_sample = _trunk(x_sample, mod[:, bp:bp + bs], norm_g, ev, na_tbl, wb_tbl, sink, od, final_g)
    return (y_prompt, y_sample)
```

```python
import functools
import math

import numpy as np
import jax
import jax.numpy as jnp
from jax import lax
from jax.experimental import pallas as pl
from jax.experimental.pallas import tpu as pltpu

D_MODEL = 1024
GRID_W = 64
HEAD_DIM = 64
EPS = 1e-6
NEG = -1e30
NA_HEADS = 8
NA_WIN_H = 8
NA_WIN_W = 16
WB_HEADS = 8
WB_KV_HEADS = 2
WB_WINDOW = 128
WB_BLOCK = 128
T5_BUCKETS = 32
T5_MAX_DIST = 128
MLA_HEADS = 16
MLA_Q_RANK = 256
MLA_KV_RANK = 128
MLA_NOPE = 64
MLA_ROPE = 32
MLA_V = 64
ROPE_THETA = 10000.0
MLA_QK = MLA_NOPE + MLA_ROPE

NA_W = NA_HEADS * HEAD_DIM
WB_QW = WB_HEADS * HEAD_DIM
WB_KVW = WB_KV_HEADS * HEAD_DIM
EVEN_MIX = NA_W + WB_QW
EVEN_IN = 3 * NA_W + WB_QW + 2 * WB_KVW + EVEN_MIX
ODD_MIX = MLA_HEADS * MLA_V

U_GATE = 0
U_QA = EVEN_MIX
U_KA = U_QA + NA_W
U_VA = U_KA + NA_W
U_QB = U_VA + NA_W
U_KB = U_QB + WB_QW
U_VB = U_KB + WB_KVW

LANES = 128
TM = 512
NA_ROWS_PER_STEP = 16
NA_HALO_ROWS = 8
NA_CHUNK = NA_ROWS_PER_STEP * GRID_W
WB_CHUNK = 1024
MLA_TQ = 1024
MLA_TK = 1024
MLA_RESIDENT_TOKENS = 16384
ONES_ROWS = 16
VMEM_LIMIT = 56 * 1024 * 1024

BF16 = jnp.bfloat16
F32 = jnp.float32
LOG2E = math.log2(math.e)

WB_HEAD_ORDER = (0, 4, 1, 5, 2, 6, 3, 7)


def _params(sem):
    return pltpu.CompilerParams(dimension_semantics=sem, vmem_limit_bytes=VMEM_LIMIT)


def _dot(a, b):
    return jnp.dot(a, b, preferred_element_type=F32)


def _dot_nt(a, b):
    return lax.dot_general(a, b, (((1,), (1,)), ((), ())), preferred_element_type=F32)


def _silu(x):
    return x * (1.0 / (1.0 + jnp.exp(-x)))


def _modulated_norm(x, g, scale, shift):
    ms = jnp.mean(x * x, axis=-1, keepdims=True)
    y = x * lax.rsqrt(ms + EPS)
    return (y * g) * (1.0 + scale) + shift


def _ada_kernel(c_ref, w_ref, b_ref, o_ref):
    c = c_ref[...]
    cs = _silu(c).astype(BF16)
    o_ref[0] = _dot(cs, w_ref[0]) + b_ref[0]


def _ada_mod(c_pad, ada_w, ada_b):
    depth = ada_w.shape[0]
    rows = c_pad.shape[0]
    tn = 768
    return pl.pallas_call(
        _ada_kernel,
        grid=(depth, 3 * D_MODEL // tn),
        in_specs=[
            pl.BlockSpec((rows, D_MODEL), lambda i, n: (0, 0)),
            pl.BlockSpec((1, D_MODEL, tn), lambda i, n: (i, 0, n)),
            pl.BlockSpec((1, 1, tn), lambda i, n: (i, 0, n)),
        ],
        out_specs=pl.BlockSpec((1, rows, tn), lambda i, n: (i, 0, n)),
        out_shape=jax.ShapeDtypeStruct((depth, rows, 3 * D_MODEL), F32),
        compiler_params=_params(("arbitrary", "arbitrary")),
        name="ada_mod",
    )(c_pad, ada_w, ada_b)


def _even_front_kernel(x_ref, g_ref, sc_ref, sh_ref, w_ref, u_ref):
    h = _modulated_norm(x_ref[0], g_ref[...], sc_ref[0], sh_ref[0]).astype(BF16)
    u_ref[0] = _dot(h, w_ref[...]).astype(BF16)


def _even_front(x, g, scale, shift, w):
    b, l, _ = x.shape
    return pl.pallas_call(
        _even_front_kernel,
        grid=(b, l // TM),
        in_specs=[
            pl.BlockSpec((1, TM, D_MODEL), lambda bi, i: (bi, i, 0)),
            pl.BlockSpec((1, D_MODEL), lambda bi, i: (0, 0)),
            pl.BlockSpec((1, 1, D_MODEL), lambda bi, i: (bi, 0, 0)),
            pl.BlockSpec((1, 1, D_MODEL), lambda bi, i: (bi, 0, 0)),
            pl.BlockSpec((D_MODEL, EVEN_IN), lambda bi, i: (0, 0)),
        ],
        out_specs=pl.BlockSpec((1, TM, EVEN_IN), lambda bi, i: (bi, i, 0)),
        out_shape=jax.ShapeDtypeStruct((b, l, EVEN_IN), BF16),
        compiler_params=_params(("arbitrary", "arbitrary")),
        name="even_front",
    )(x, g, scale, shift, w)


def _low_lane_mask(shape):
    return lax.broadcasted_iota(jnp.int32, shape, len(shape) - 1) < HEAD_DIM


def _na_kernel(q_ref, kp_ref, kc_ref, kn_ref, vp_ref, vc_ref, vn_ref, bias_ref, o_ref,
               kwin, vwin, s_scr, p_scr, *, rows):
    ci = pl.program_id(1)
    halo = NA_HALO_ROWS * GRID_W
    npair = NA_HEADS // 2
    kwin[0:halo] = kp_ref[0]
    kwin[halo:halo + NA_CHUNK] = kc_ref[0]
    kwin[halo + NA_CHUNK:2 * halo + NA_CHUNK] = kn_ref[0]
    for j in range(npair):
        lanes = slice(j * LANES, (j + 1) * LANES)
        dst = slice(2 * j * LANES, (2 * j + 1) * LANES)
        vwin[0:halo, dst] = vp_ref[0, :, lanes]
        vwin[halo:halo + NA_CHUNK, dst] = vc_ref[0, :, lanes]
        vwin[halo + NA_CHUNK:2 * halo + NA_CHUNK, dst] = vn_ref[0, :, lanes]
        vwin[:, (2 * j + 1) * LANES:(2 * j + 2) * LANES] = jnp.ones((2 * halo + NA_CHUNK, LANES), BF16)

    low = _low_lane_mask((GRID_W, LANES))
    kh = NA_WIN_H
    r0 = ci * NA_ROWS_PER_STEP

    def offsets(i):
        r = r0 + i
        row_start = jnp.clip(r - kh // 2, 0, rows - kh)
        variant = r - row_start
        start = pl.multiple_of((row_start - r0 + NA_HALO_ROWS) * GRID_W, GRID_W)
        return variant, start, pl.multiple_of(i * GRID_W, GRID_W)

    def scores(i):
        variant, start, qoff = offsets(i)
        for j in range(npair):
            lanes = slice(j * LANES, (j + 1) * LANES)
            q2 = q_ref[0, pl.ds(qoff, GRID_W), lanes]
            zero = jnp.zeros_like(q2)
            lhs = jnp.concatenate([jnp.where(low, q2, zero), jnp.where(low, zero, q2)], axis=0)
            kw = kwin[pl.ds(start, kh * GRID_W), lanes]
            s_scr[j] = _dot_nt(lhs, kw) + bias_ref[variant, j]

    def probs():
        for j in range(npair):
            s = s_scr[j]
            p_scr[j] = jnp.exp(s - jnp.max(s, axis=-1, keepdims=True)).astype(BF16)

    def output(i):
        _, start, qoff = offsets(i)
        for j in range(npair):
            vw = vwin[pl.ds(start, kh * GRID_W), 2 * j * LANES:(2 * j + 2) * LANES]
            res = _dot(p_scr[j], vw)
            o = res[:, 0:LANES] / res[:, LANES:2 * LANES]
            o_ref[0, pl.ds(qoff, GRID_W), j * LANES:(j + 1) * LANES] = (
                jnp.where(low, o[:GRID_W], o[GRID_W:]).astype(BF16))

    scores(0)
    probs()
    scores(1)

    def body(i, carry):
        output(i - 2)
        probs()
        scores(i)
        return carry

    lax.fori_loop(2, NA_ROWS_PER_STEP, body, 0, unroll=2)
    output(NA_ROWS_PER_STEP - 2)
    probs()
    output(NA_ROWS_PER_STEP - 1)


def _na_attention(u, bias_tbl):
    b, l, _ = u.shape
    rows = l // GRID_W
    nchunk = l // NA_CHUNK
    halo = NA_HALO_ROWS * GRID_W
    per_chunk = NA_CHUNK // halo
    nhalo = l // halo
    qblk, kblk, vblk = U_QA // NA_W, U_KA // NA_W, U_VA // NA_W

    def window(col):
        return [
            pl.BlockSpec((1, halo, NA_W), lambda bi, i: (bi, jnp.maximum(i * per_chunk - 1, 0), col)),
            pl.BlockSpec((1, NA_CHUNK, NA_W), lambda bi, i: (bi, i, col)),
            pl.BlockSpec((1, halo, NA_W), lambda bi, i: (bi, jnp.minimum((i + 1) * per_chunk, nhalo - 1), col)),
        ]

    npair = NA_HEADS // 2
    return pl.pallas_call(
        functools.partial(_na_kernel, rows=rows),
        grid=(b, nchunk),
        in_specs=[pl.BlockSpec((1, NA_CHUNK, NA_W), lambda bi, i: (bi, i, qblk))]
        + window(kblk) + window(vblk)
        + [pl.BlockSpec(bias_tbl.shape, lambda bi, i: (0, 0, 0, 0))],
        out_specs=pl.BlockSpec((1, NA_CHUNK, NA_W), lambda bi, i: (bi, i, 0)),
        out_shape=jax.ShapeDtypeStruct((b, l, NA_W), BF16),
        scratch_shapes=[pltpu.VMEM((NA_CHUNK + 2 * halo, NA_W), BF16),
                        pltpu.VMEM((NA_CHUNK + 2 * halo, 2 * NA_W), BF16),
                        pltpu.VMEM((npair, 2 * GRID_W, NA_WIN_H * GRID_W), F32),
                        pltpu.VMEM((npair, 2 * GRID_W, NA_WIN_H * GRID_W), BF16)],
        compiler_params=_params(("arbitrary", "arbitrary")),
        name="na_attention",
    )(u, u, u, u, u, u, u, bias_tbl)


def _wb_kernel(sink_ref, q_ref, kp_ref, kc_ref, kn_ref, vp_ref, vc_ref, vn_ref, bias_ref, o_ref,
               kwin, vwin, s_scr, p_scr, ps_scr, *, nblk):
    ci = pl.program_id(1)
    npair = WB_HEADS // 2
    per_step = WB_CHUNK // WB_BLOCK
    win = WB_CHUNK + 2 * WB_BLOCK
    kwin[0:WB_BLOCK] = kp_ref[0]
    kwin[WB_BLOCK:WB_BLOCK + WB_CHUNK] = kc_ref[0]
    kwin[WB_BLOCK + WB_CHUNK:win] = kn_ref[0]
    vwin[0:WB_BLOCK, 0:LANES] = vp_ref[0]
    vwin[WB_BLOCK:WB_BLOCK + WB_CHUNK, 0:LANES] = vc_ref[0]
    vwin[WB_BLOCK + WB_CHUNK:win, 0:LANES] = vn_ref[0]
    vwin[:, LANES:2 * LANES] = jnp.ones((win, LANES), BF16)

    low = _low_lane_mask((WB_BLOCK, LANES))
    col = lax.broadcasted_iota(jnp.int32, (2 * WB_BLOCK, 3 * WB_BLOCK), 1)
    row = lax.broadcasted_iota(jnp.int32, (2 * WB_BLOCK, 1), 0)

    def scores(n):
        gblk = ci * per_step + n
        lo = jnp.where(gblk > 0, 0, WB_BLOCK)
        hi = jnp.where(gblk < nblk - 1, 3 * WB_BLOCK, 2 * WB_BLOCK)
        in_seq = jnp.logical_and(col >= lo, col < hi)
        off = pl.multiple_of(n * WB_BLOCK, WB_BLOCK)
        kw = kwin[pl.ds(off, 3 * WB_BLOCK), :]
        for j in range(npair):
            q2 = q_ref[0, pl.ds(off, WB_BLOCK), j * LANES:(j + 1) * LANES]
            zero = jnp.zeros_like(q2)
            lhs = jnp.concatenate([jnp.where(low, q2, zero), jnp.where(low, zero, q2)], axis=0)
            s_scr[j] = jnp.where(in_seq, _dot_nt(lhs, kw) + bias_ref[j], NEG)

    def probs():
        for j in range(npair):
            s = s_scr[j]
            sink = jnp.where(row < WB_BLOCK, sink_ref[WB_HEAD_ORDER[2 * j]],
                             sink_ref[WB_HEAD_ORDER[2 * j + 1]])
            m = jnp.maximum(jnp.max(s, axis=-1, keepdims=True), sink)
            p_scr[j] = jnp.exp(s - m).astype(BF16)
            ps_scr[j] = jnp.broadcast_to(jnp.exp(sink - m), (2 * WB_BLOCK, LANES))

    def output(n):
        off = pl.multiple_of(n * WB_BLOCK, WB_BLOCK)
        vw = vwin[pl.ds(off, 3 * WB_BLOCK), :]
        for j in range(npair):
            res = _dot(p_scr[j], vw)
            o = res[:, 0:LANES] / (res[:, LANES:2 * LANES] + ps_scr[j])
            o_ref[0, pl.ds(off, WB_BLOCK), j * LANES:(j + 1) * LANES] = (
                jnp.where(low, o[:WB_BLOCK], o[WB_BLOCK:]).astype(BF16))

    scores(0)
    probs()
    scores(1)

    def body(n, carry):
        output(n - 2)
        probs()
        scores(n)
        return carry

    lax.fori_loop(2, per_step, body, 0, unroll=2)
    output(per_step - 2)
    probs()
    output(per_step - 1)


def _wb_attention(u, bias_tbl, sink):
    b, l, _ = u.shape
    nblk = l // WB_BLOCK
    nchunk = l // WB_CHUNK
    per_step = WB_CHUNK // WB_BLOCK
    qblk = U_QB // WB_QW
    kcol, vcol = U_KB // WB_KVW, U_VB // WB_KVW

    def halo(colblk, d):
        if d == 0:
            return pl.BlockSpec((1, WB_CHUNK, WB_KVW), lambda bi, i: (bi, i, colblk))
        return pl.BlockSpec(
            (1, WB_BLOCK, WB_KVW),
            lambda bi, i: (bi, jnp.clip(i * per_step + (per_step if d > 0 else -1), 0, nblk - 1), colblk))

    return pl.pallas_call(
        functools.partial(_wb_kernel, nblk=nblk),
        grid=(b, nchunk),
        in_specs=[
            pl.BlockSpec(memory_space=pltpu.SMEM),
            pl.BlockSpec((1, WB_CHUNK, WB_QW), lambda bi, i: (bi, i, qblk)),
            halo(kcol, -1), halo(kcol, 0), halo(kcol, 1),
            halo(vcol, -1), halo(vcol, 0), halo(vcol, 1),
            pl.BlockSpec(bias_tbl.shape, lambda bi, i: (0, 0, 0)),
        ],
        out_specs=pl.BlockSpec((1, WB_CHUNK, WB_QW), lambda bi, i: (bi, i, 0)),
        out_shape=jax.ShapeDtypeStruct((b, l, WB_QW), BF16),
        scratch_shapes=[pltpu.VMEM((WB_CHUNK + 2 * WB_BLOCK, WB_KVW), BF16),
                        pltpu.VMEM((WB_CHUNK + 2 * WB_BLOCK, 2 * WB_KVW), BF16),
                        pltpu.VMEM((WB_HEADS // 2, 2 * WB_BLOCK, 3 * WB_BLOCK), F32),
                        pltpu.VMEM((WB_HEADS // 2, 2 * WB_BLOCK, 3 * WB_BLOCK), BF16),
                        pltpu.VMEM((WB_HEADS // 2, 2 * WB_BLOCK, LANES), F32)],
        compiler_params=_params(("arbitrary", "arbitrary")),
        name="wb_attention",
    )(sink, u, u, u, u, u, u, u, bias_tbl)


def _even_out_tile(x_ref, oa_ref, ob_ref, gate_ref, gm_ref, w_ref):
    g = _silu(gate_ref[0].astype(F32))
    o = jnp.concatenate([oa_ref[0], ob_ref[0]], axis=-1).astype(F32)
    z = (o * g).astype(BF16)
    return x_ref[0] + gm_ref[0] * _dot(z, w_ref[...])


def _odd_front_kernel(x_ref, oa_ref, ob_ref, gate_ref, gm_ref, wo_ref,
                      g_ref, sc_ref, sh_ref, w1t_ref, wnat_ref, gq_ref, gkvt_ref, gkv_ref,
                      wqbt_ref, wk_ref, wvt_ref, cost_ref, sint_ref, cosp_ref, sinp_ref,
                      x1_ref, k_ref, qt_ref, vt_ref, sgt_ref):
    x1 = _even_out_tile(x_ref, oa_ref, ob_ref, gate_ref, gm_ref, wo_ref)
    x1_ref[0] = x1
    h = _modulated_norm(x1, g_ref[...], sc_ref[0], sh_ref[0]).astype(BF16)
    reps = TM // LANES

    ut = _dot_nt(w1t_ref[...], h)
    sgt_ref[0] = _silu(ut[0:ODD_MIX]).astype(BF16)

    qlt = ut[ODD_MIX:ODD_MIX + MLA_Q_RANK]
    qn = qlt * lax.rsqrt(jnp.mean(qlt * qlt, axis=0, keepdims=True) + EPS)
    qn = (qn * jnp.concatenate([gq_ref[...]] * reps, axis=1)).astype(BF16)
    qt = _dot(wqbt_ref[...], qn)
    qscale = (MLA_QK ** -0.5) * LOG2E
    cos_t = cost_ref[...]
    sin_t = sint_ref[...]
    half = MLA_ROPE // 2
    for hd in range(MLA_HEADS):
        base = hd * MLA_QK
        qt_ref[0, hd, 0, 0:MLA_NOPE, :] = (qt[base:base + MLA_NOPE] * qscale).astype(BF16)
        x1 = qt[base + MLA_NOPE:base + MLA_NOPE + half]
        x2 = qt[base + MLA_NOPE + half:base + MLA_QK]
        qt_ref[0, hd, 0, MLA_NOPE:MLA_NOPE + half, :] = ((x1 * cos_t - x2 * sin_t) * qscale).astype(BF16)
        qt_ref[0, hd, 0, MLA_NOPE + half:MLA_QK, :] = ((x2 * cos_t + x1 * sin_t) * qscale).astype(BF16)

    kvt = ut[ODD_MIX + MLA_Q_RANK:ODD_MIX + MLA_Q_RANK + MLA_KV_RANK]
    kvnt = kvt * lax.rsqrt(jnp.mean(kvt * kvt, axis=0, keepdims=True) + EPS)
    kvnt = (kvnt * jnp.concatenate([gkvt_ref[...]] * reps, axis=1)).astype(BF16)
    vt = _dot(wvt_ref[...], kvnt)
    vt_ref[0, :, 0] = vt.reshape(MLA_HEADS, MLA_V, TM).astype(BF16)

    nat = _dot(h, wnat_ref[...])
    kvl = nat[:, 0:MLA_KV_RANK]
    kvn = kvl * lax.rsqrt(jnp.mean(kvl * kvl, axis=-1, keepdims=True) + EPS)
    kvn = (kvn * gkv_ref[...]).astype(BF16)
    kn = _dot(kvn, wk_ref[...])
    kpe = nat[:, LANES:2 * LANES] * cosp_ref[...] + nat[:, 2 * LANES:3 * LANES] * sinp_ref[...]
    for hd in range(MLA_HEADS):
        kh = kn[:, hd * LANES:(hd + 1) * LANES] + kpe
        k_ref[0, hd] = kh[:, 0:MLA_QK].astype(BF16)


def _odd_front(x, oa, ob, u, gate_mod, w_out, g, scale, shift, w1t, wnat, gq, gkvt, gkv, wqbt, wk, wvt,
               cos_t, sin_t, cos_p, sin_p):
    b, l, _ = x.shape
    nt = l // TM
    const2 = lambda bi, i: (0, 0)
    n1 = w1t.shape[0]
    return pl.pallas_call(
        _odd_front_kernel,
        grid=(b, nt),
        in_specs=[
            pl.BlockSpec((1, TM, D_MODEL), lambda bi, i: (bi, i, 0)),
            pl.BlockSpec((1, TM, NA_W), lambda bi, i: (bi, i, 0)),
            pl.BlockSpec((1, TM, WB_QW), lambda bi, i: (bi, i, 0)),
            pl.BlockSpec((1, TM, EVEN_MIX), lambda bi, i: (bi, i, U_GATE // EVEN_MIX)),
            pl.BlockSpec((1, 1, D_MODEL), lambda bi, i: (bi, 0, 0)),
            pl.BlockSpec((EVEN_MIX, D_MODEL), const2),
            pl.BlockSpec((1, D_MODEL), const2),
            pl.BlockSpec((1, 1, D_MODEL), lambda bi, i: (bi, 0, 0)),
            pl.BlockSpec((1, 1, D_MODEL), lambda bi, i: (bi, 0, 0)),
            pl.BlockSpec((n1, D_MODEL), const2),
            pl.BlockSpec((D_MODEL, 3 * LANES), const2),
            pl.BlockSpec((MLA_Q_RANK, LANES), const2),
            pl.BlockSpec((MLA_KV_RANK, LANES), const2),
            pl.BlockSpec((1, MLA_KV_RANK), const2),
            pl.BlockSpec((MLA_HEADS * MLA_QK, MLA_Q_RANK), const2),
            pl.BlockSpec((MLA_KV_RANK, MLA_HEADS * LANES), const2),
            pl.BlockSpec((ODD_MIX, MLA_KV_RANK), const2),
            pl.BlockSpec((MLA_ROPE // 2, TM), lambda bi, i: (0, i)),
            pl.BlockSpec((MLA_ROPE // 2, TM), lambda bi, i: (0, i)),
            pl.BlockSpec((TM, LANES), lambda bi, i: (i, 0)),
            pl.BlockSpec((TM, LANES), lambda bi, i: (i, 0)),
        ],
        out_specs=[
            pl.BlockSpec((1, TM, D_MODEL), lambda bi, i: (bi, i, 0)),
            pl.BlockSpec((1, MLA_HEADS, TM, MLA_QK), lambda bi, i: (bi, 0, i, 0)),
            pl.BlockSpec((1, MLA_HEADS, 1, MLA_QK, TM), lambda bi, i: (bi, 0, i, 0, 0)),
            pl.BlockSpec((1, MLA_HEADS, 1, MLA_V, TM), lambda bi, i: (bi, 0, i, 0, 0)),
            pl.BlockSpec((1, ODD_MIX, TM), lambda bi, i: (bi, 0, i)),
        ],
        out_shape=[
            jax.ShapeDtypeStruct((b, l, D_MODEL), F32),
            jax.ShapeDtypeStruct((b, MLA_HEADS, l, MLA_QK), BF16),
            jax.ShapeDtypeStruct((b, MLA_HEADS, nt, MLA_QK, TM), BF16),
            jax.ShapeDtypeStruct((b, MLA_HEADS, nt, MLA_V, TM), BF16),
            jax.ShapeDtypeStruct((b, ODD_MIX, l), BF16),
        ],
        compiler_params=_params(("arbitrary", "arbitrary")),
        name="odd_front",
    )(x, oa, ob, u, gate_mod, w_out, g, scale, shift, w1t, wnat, gq, gkvt, gkv, wqbt, wk, wvt,
      cos_t, sin_t, cos_p, sin_p)


def _mla_kernel(qt_ref, k_ref, vt_ref, o_ref, s_scr, p_scr, acc_scr, *, hb, nqb, nchunks):
    ones = jnp.ones((ONES_ROWS, MLA_TK), BF16)
    qsub = MLA_TQ // TM
    ksub = MLA_TK // TM
    chunk_bits = nchunks.bit_length() - 1
    qb_bits = nqb.bit_length() - 1
    total = hb * nqb * nchunks

    def decode(i):
        return i >> (chunk_bits + qb_bits), (i >> chunk_bits) & (nqb - 1), i & (nchunks - 1)

    def scores(i):
        h, qb, t = decode(i)
        q = jnp.concatenate([qt_ref[0, h, qsub * qb + j] for j in range(qsub)], axis=1)
        koff = pl.multiple_of(t * MLA_TK, MLA_TK)
        s = _dot(k_ref[0, h, pl.ds(koff, MLA_TK), :], q)
        s_scr[...] = s
        return jnp.max(s, axis=0, keepdims=True)

    def probs(i, m, mt):
        _, _, t = decode(i)
        m = jnp.where(t == 0, NEG, m)
        m_new = jnp.maximum(m, mt)
        p_scr[...] = jnp.exp2(s_scr[...] - m_new).astype(BF16)
        return m_new, jnp.exp2(m - m_new)

    def accumulate(i, alpha):
        h, qb, t = decode(i)
        vt = jnp.concatenate([vt_ref[0, h, ksub * t + j] for j in range(ksub)], axis=1)
        acc_scr[...] = acc_scr[...] * alpha + _dot(jnp.concatenate([vt, ones], axis=0), p_scr[...])
        acc = acc_scr[...]
        o_ref[0, h, qb] = (acc[0:MLA_V] * (1.0 / acc[MLA_V:MLA_V + 1])).astype(BF16)

    acc_scr[...] = jnp.zeros_like(acc_scr)
    m = jnp.full((1, MLA_TQ), NEG, F32)
    mt = scores(0)
    m, alpha = probs(0, m, mt)
    mt = scores(1)

    def body(i, carry):
        m, mt, alpha = carry
        accumulate(i - 2, alpha)
        m, alpha = probs(i - 1, m, mt)
        mt = scores(i)
        return m, mt, alpha

    m, mt, alpha = lax.fori_loop(2, total, body, (m, mt, alpha), unroll=2)
    accumulate(total - 2, alpha)
    m, alpha = probs(total - 1, m, mt)
    accumulate(total - 1, alpha)


def _mla_attention(qt, k, vt):
    b, _, nt, _, _ = qt.shape
    l = nt * TM
    nchunks = l // MLA_TK
    nqb = l // MLA_TQ
    assert MLA_TK % TM == 0 and MLA_TQ % TM == 0
    assert nqb & (nqb - 1) == 0 and nchunks & (nchunks - 1) == 0
    hb = max(1, min(MLA_HEADS, MLA_RESIDENT_TOKENS // l))
    assert hb * nqb * nchunks >= 2 and (hb * nqb * nchunks) % 2 == 0
    return pl.pallas_call(
        functools.partial(_mla_kernel, hb=hb, nqb=nqb, nchunks=nchunks),
        grid=(b, MLA_HEADS // hb),
        in_specs=[
            pl.BlockSpec((1, hb, nt, MLA_QK, TM), lambda bi, h: (bi, h, 0, 0, 0)),
            pl.BlockSpec((1, hb, l, MLA_QK), lambda bi, h: (bi, h, 0, 0)),
            pl.BlockSpec((1, hb, nt, MLA_V, TM), lambda bi, h: (bi, h, 0, 0, 0)),
        ],
        out_specs=pl.BlockSpec((1, hb, nqb, MLA_V, MLA_TQ), lambda bi, h: (bi, h, 0, 0, 0)),
        out_shape=jax.ShapeDtypeStruct((b, MLA_HEADS, nqb, MLA_V, MLA_TQ), BF16),
        scratch_shapes=[pltpu.VMEM((MLA_TK, MLA_TQ), F32),
                        pltpu.VMEM((MLA_TK, MLA_TQ), BF16),
                        pltpu.VMEM((MLA_V + ONES_ROWS, MLA_TQ), F32)],
        compiler_params=_params(("arbitrary", "arbitrary")),
        name="mla_attention",
    )(qt, k, vt)


def _odd_out_kernel(x_ref, ot_ref, sgt_ref, gm_ref, wt_ref, fg_ref, y_ref):
    ot = ot_ref[0, :, 0].reshape(ODD_MIX, TM)
    z = (ot.astype(F32) * sgt_ref[0].astype(F32)).astype(BF16)
    out = lax.dot_general(z, wt_ref[...], (((0,), (1,)), ((), ())), preferred_element_type=F32)
    x2 = x_ref[0] + gm_ref[0] * out
    ms = jnp.mean(x2 * x2, axis=-1, keepdims=True)
    y_ref[0] = (x2 * lax.rsqrt(ms + EPS)) * fg_ref[...]


def _odd_out(x, ot, sgt, gate_mod, w_out_t, final_g):
    b, l, _ = x.shape
    sub = MLA_TQ // TM
    return pl.pallas_call(
        _odd_out_kernel,
        grid=(b, l // TM),
        in_specs=[
            pl.BlockSpec((1, TM, D_MODEL), lambda bi, i: (bi, i, 0)),
            pl.BlockSpec((1, MLA_HEADS, 1, MLA_V, TM), lambda bi, i: (bi, 0, i // sub, 0, i % sub)),
            pl.BlockSpec((1, ODD_MIX, TM), lambda bi, i: (bi, 0, i)),
            pl.BlockSpec((1, 1, D_MODEL), lambda bi, i: (bi, 0, 0)),
            pl.BlockSpec((D_MODEL, ODD_MIX), lambda bi, i: (0, 0)),
            pl.BlockSpec((1, D_MODEL), lambda bi, i: (0, 0)),
        ],
        out_specs=pl.BlockSpec((1, TM, D_MODEL), lambda bi, i: (bi, i, 0)),
        out_shape=jax.ShapeDtypeStruct((b, l, D_MODEL), F32),
        compiler_params=_params(("arbitrary", "arbitrary")),
        name="odd_out",
    )(x, ot, sgt, gate_mod, w_out_t, final_g)


def _t5_bucket(rel):
    nb = T5_BUCKETS // 2
    ret = (rel > 0).astype(np.int32) * nb
    n = np.abs(rel)
    max_exact = nb // 2
    large = max_exact + (np.log(np.maximum(n, 1) / max_exact)
                         / np.log(T5_MAX_DIST / max_exact) * (nb - max_exact)).astype(np.int32)
    large = np.minimum(large, nb - 1)
    return ret + np.where(n < max_exact, n, large)


def _na_bias_table(rpb):
    kh = NA_WIN_H
    c = np.arange(GRID_W)
    col_start = np.clip(c - NA_WIN_W // 2, 0, GRID_W - NA_WIN_W)
    col_ok = (c[None, :] >= col_start[:, None]) & (c[None, :] < col_start[:, None] + NA_WIN_W)
    d_col = np.clip(c[None, :] - c[:, None], -(NA_WIN_W - 1), NA_WIN_W - 1) + NA_WIN_W - 1
    nrel = 2 * NA_WIN_W - 1
    rows = jnp.stack([rpb[:, NA_WIN_H - 1 - v:2 * NA_WIN_H - 1 - v] for v in range(kh)], axis=1)
    onehot = (d_col.reshape(-1)[None, :] == np.arange(nrel)[:, None]).astype(np.float32)
    bias = jnp.dot(rows.astype(F32).reshape(-1, nrel), jnp.asarray(onehot),
                   precision=lax.Precision.HIGHEST)
    bias = bias.reshape(NA_HEADS, kh, kh, GRID_W, GRID_W)
    bias = jnp.where(jnp.asarray(col_ok)[None, None, None, :, :], bias, NEG)
    bias = bias.transpose(1, 0, 3, 2, 4).reshape(kh, NA_HEADS // 2, 2 * GRID_W, kh * GRID_W)
    return bias


def _wb_bias_table(t5_bias):
    rel = (np.arange(3 * WB_BLOCK) - WB_BLOCK)[None, :] - np.arange(WB_BLOCK)[:, None]
    offs = np.arange(-(2 * WB_BLOCK - 1), 2 * WB_BLOCK + 1)
    period = 4 * WB_BLOCK
    by_off = t5_bias[_t5_bucket(offs)].astype(F32).T
    shifted = jnp.tile(by_off, (1, WB_BLOCK))[:, :WB_BLOCK * (period - 1)]
    shifted = shifted.reshape(WB_HEADS, WB_BLOCK, period - 1)
    bias = shifted[:, :, WB_BLOCK - 1:4 * WB_BLOCK - 1]
    bias = jnp.where(jnp.asarray(np.abs(rel) <= WB_WINDOW)[None], bias, NEG)
    bias = bias[np.asarray(WB_HEAD_ORDER)]
    return bias.reshape(WB_HEADS // 2, 2 * WB_BLOCK, 3 * WB_BLOCK)


def _even_weights(w_in, w_out):
    qa, ka, va, qb, kb, vb, gate = jnp.split(
        w_in, [NA_W, 2 * NA_W, 3 * NA_W, 3 * NA_W + WB_QW, 3 * NA_W + WB_QW + WB_KVW,
               3 * NA_W + WB_QW + 2 * WB_KVW], axis=-1)
    order = np.asarray(WB_HEAD_ORDER)
    perm = (order[:, None] * HEAD_DIM + np.arange(HEAD_DIM)[None, :]).reshape(-1)
    qscale = HEAD_DIM ** -0.5
    gate = jnp.concatenate([gate[:, :NA_W], gate[:, NA_W:][:, perm]], axis=-1)
    w = jnp.concatenate([gate, qa * qscale, ka, va, qb[:, perm] * qscale, kb, vb], axis=-1)
    w_out_p = jnp.concatenate([w_out[:NA_W], w_out[NA_W:][perm]], axis=0)
    return w.astype(BF16), w_out_p.astype(BF16)


def _odd_weights(w_in, q_norm, w_qb, kv_norm, w_kvb, w_out):
    q_lat, kv_lat, k_rope, gate = jnp.split(
        w_in, [MLA_Q_RANK, MLA_Q_RANK + MLA_KV_RANK, MLA_Q_RANK + MLA_KV_RANK + MLA_ROPE], axis=-1)
    w1t = jnp.concatenate([gate, q_lat, kv_lat], axis=-1).T.astype(BF16)
    half = MLA_ROPE // 2
    k_rot = jnp.concatenate([-k_rope[:, half:], k_rope[:, :half]], axis=-1)
    z64 = jnp.zeros((D_MODEL, MLA_NOPE), w_in.dtype)
    z32 = jnp.zeros((D_MODEL, LANES - MLA_QK), w_in.dtype)
    wnat = jnp.concatenate([kv_lat, z64, k_rope, z32, z64, k_rot, z32], axis=-1).astype(BF16)
    wkv = w_kvb.reshape(MLA_KV_RANK, MLA_HEADS, MLA_NOPE + MLA_V)
    wk = jnp.concatenate([wkv[:, :, :MLA_NOPE], jnp.zeros_like(wkv[:, :, :MLA_NOPE])], axis=-1)
    wk = wk.reshape(MLA_KV_RANK, MLA_HEADS * LANES).astype(BF16)
    wvt = wkv[:, :, MLA_NOPE:].reshape(MLA_KV_RANK, ODD_MIX).T.astype(BF16)
    gq = jnp.broadcast_to(q_norm.astype(F32)[:, None], (MLA_Q_RANK, LANES))
    gkvt = jnp.broadcast_to(kv_norm.astype(F32)[:, None], (MLA_KV_RANK, LANES))
    gkv = kv_norm.astype(F32)[None, :]
    return w1t, wnat, gq, gkvt, gkv, w_qb.T.astype(BF16), wk, wvt, w_out.T.astype(BF16)


def _rope_tables(l):
    inv_freq = 1.0 / (ROPE_THETA ** (jnp.arange(0, MLA_ROPE, 2, dtype=F32) / MLA_ROPE))
    ang = jnp.arange(l, dtype=F32)[:, None] * inv_freq[None, :]
    cos, sin = jnp.cos(ang), jnp.sin(ang)
    z64 = jnp.zeros((l, MLA_NOPE), F32)
    z32 = jnp.zeros((l, LANES - MLA_QK), F32)
    cos_p = jnp.concatenate([z64, cos, cos, z32], axis=-1)
    sin_p = jnp.concatenate([z64, sin, sin, z32], axis=-1)
    return cos.T, sin.T, cos_p, sin_p


def _trunk(x, mod, norm_g, ev, na_tbl, wb_tbl, sink, od, final_g):
    b, l, _ = x.shape
    assert l % TM == 0 and l % NA_CHUNK == 0 and l % WB_CHUNK == 0 and l % MLA_TK == 0
    assert l // GRID_W >= NA_ROWS_PER_STEP >= NA_HALO_ROWS >= NA_WIN_H
    shift0, scale0, gate0 = [t[:, None, :] for t in jnp.split(mod[0], 3, axis=-1)]
    shift1, scale1, gate1 = [t[:, None, :] for t in jnp.split(mod[1], 3, axis=-1)]

    w_in0, w_out0 = ev
    u = _even_front(x, norm_g[0][None, :], scale0, shift0, w_in0)
    oa = _na_attention(u, na_tbl)
    ob = _wb_attention(u, wb_tbl, sink)

    w1t, wnat, gq, gkvt, gkv, wqbt, wk, wvt, w_out1t = od
    cos_t, sin_t, cos_p, sin_p = _rope_tables(l)
    x1, k, qt, vt, sgt = _odd_front(x, oa, ob, u, gate0, w_out0, norm_g[1][None, :], scale1, shift1, w1t, wnat, gq, gkvt, gkv,
                                wqbt, wk, wvt, cos_t, sin_t, cos_p, sin_p)
    ot = _mla_attention(qt, k, vt)
    return _odd_out(x1, ot, sgt, gate1, w_out1t, final_g[None, :])


def kernel(x_prompt, x_sample, c_prompt, c_sample, ada_w, ada_b, norm_g, t5_bias, ev_w_in, na_rpb,
           wb_sink, ev_w_out, mla_w_in, mla_q_norm, mla_w_qb, mla_kv_norm, mla_w_kvb, mla_w_out, final_g):
    bp, bs = c_prompt.shape[0], c_sample.shape[0]
    rows = -(-(bp + bs) // 16) * 16
    c_pad = jnp.concatenate([c_prompt, c_sample, jnp.zeros((rows - bp - bs, D_MODEL), F32)], axis=0)
    mod = _ada_mod(c_pad, ada_w.astype(BF16), ada_b[:, None, :])

    ev = _even_weights(ev_w_in[0], ev_w_out[0])
    na_tbl = _na_bias_table(na_rpb[0])
    wb_tbl = _wb_bias_table(t5_bias)
    od = _odd_weights(mla_w_in[0], mla_q_norm[0], mla_w_qb[0], mla_kv_norm[0], mla_w_kvb[0], mla_w_out[0])
    sink = wb_sink[0].astype(F32)

    y_prompt = _trunk(x_prompt, mod[:, :bp], norm_g, ev, na_tbl, wb_tbl, sink, od, final_g)
    y_sample = _trunk(x_sample, mod[:, bp:bp + bs], norm_g, ev, na_tbl, wb_tbl, sink, od, final_g)
    return (y_prompt, y_sample)
```

```python
import functools
import math

import numpy as np
import jax
import jax.numpy as jnp
from jax import lax
from jax.experimental import pallas as pl
from jax.experimental.pallas import tpu as pltpu

D_MODEL = 1024
GRID_W = 64
HEAD_DIM = 64
EPS = 1e-6
NEG = -1e30
NA_HEADS = 8
NA_WIN_H = 8
NA_WIN_W = 16
WB_HEADS = 8
WB_KV_HEADS = 2
WB_WINDOW = 128
WB_BLOCK = 128
T5_BUCKETS = 32
T5_MAX_DIST = 128
MLA_HEADS = 16
MLA_Q_RANK = 256
MLA_KV_RANK = 128
MLA_NOPE = 64
MLA_ROPE = 32
MLA_V = 64
ROPE_THETA = 10000.0
MLA_QK = MLA_NOPE + MLA_ROPE

NA_W = NA_HEADS * HEAD_DIM
WB_QW = WB_HEADS * HEAD_DIM
WB_KVW = WB_KV_HEADS * HEAD_DIM
EVEN_MIX = NA_W + WB_QW
EVEN_IN = 3 * NA_W + WB_QW + 2 * WB_KVW + EVEN_MIX
ODD_MIX = MLA_HEADS * MLA_V

U_GATE = 0
U_QA = EVEN_MIX
U_KA = U_QA + NA_W
U_VA = U_KA + NA_W
U_QB = U_VA + NA_W
U_KB = U_QB + WB_QW
U_VB = U_KB + WB_KVW

LANES = 128
TM = 512
NA_ROWS_PER_STEP = 16
NA_HALO_ROWS = 8
NA_CHUNK = NA_ROWS_PER_STEP * GRID_W
WB_CHUNK = 1024
MLA_TQ = 1024
MLA_TK = 2048
MLA_RESIDENT_TOKENS = 16384
ONES_ROWS = 16
VMEM_LIMIT = 56 * 1024 * 1024

BF16 = jnp.bfloat16
F32 = jnp.float32
LOG2E = math.log2(math.e)

WB_HEAD_ORDER = (0, 4, 1, 5, 2, 6, 3, 7)


def _params(sem):
    return pltpu.CompilerParams(dimension_semantics=sem, vmem_limit_bytes=VMEM_LIMIT)


def _dot(a, b):
    return jnp.dot(a, b, preferred_element_type=F32)


def _dot_nt(a, b):
    return lax.dot_general(a, b, (((1,), (1,)), ((), ())), preferred_element_type=F32)


def _silu(x):
    return x * (1.0 / (1.0 + jnp.exp(-x)))


def _modulated_norm(x, g, scale, shift):
    ms = jnp.mean(x * x, axis=-1, keepdims=True)
    y = x * lax.rsqrt(ms + EPS)
    return (y * g) * (1.0 + scale) + shift


def _ada_kernel(c_ref, w_ref, b_ref, o_ref):
    c = c_ref[...]
    cs = _silu(c).astype(BF16)
    o_ref[0] = _dot(cs, w_ref[0]) + b_ref[0]


def _ada_mod(c_pad, ada_w, ada_b):
    depth = ada_w.shape[0]
    rows = c_pad.shape[0]
    tn = 768
    return pl.pallas_call(
        _ada_kernel,
        grid=(depth, 3 * D_MODEL // tn),
        in_specs=[
            pl.BlockSpec((rows, D_MODEL), lambda i, n: (0, 0)),
            pl.BlockSpec((1, D_MODEL, tn), lambda i, n: (i, 0, n)),
            pl.BlockSpec((1, 1, tn), lambda i, n: (i, 0, n)),
        ],
        out_specs=pl.BlockSpec((1, rows, tn), lambda i, n: (i, 0, n)),
        out_shape=jax.ShapeDtypeStruct((depth, rows, 3 * D_MODEL), F32),
        compiler_params=_params(("arbitrary", "arbitrary")),
        name="ada_mod",
    )(c_pad, ada_w, ada_b)


def _even_front_kernel(x_ref, g_ref, sc_ref, sh_ref, w_ref, u_ref):
    h = _modulated_norm(x_ref[0], g_ref[...], sc_ref[0], sh_ref[0]).astype(BF16)
    u_ref[0] = _dot(h, w_ref[...]).astype(BF16)


def _even_front(x, g, scale, shift, w):
    b, l, _ = x.shape
    return pl.pallas_call(
        _even_front_kernel,
        grid=(b, l // TM),
        in_specs=[
            pl.BlockSpec((1, TM, D_MODEL), lambda bi, i: (bi, i, 0)),
            pl.BlockSpec((1, D_MODEL), lambda bi, i: (0, 0)),
            pl.BlockSpec((1, 1, D_MODEL), lambda bi, i: (bi, 0, 0)),
            pl.BlockSpec((1, 1, D_MODEL), lambda bi, i: (bi, 0, 0)),
            pl.BlockSpec((D_MODEL, EVEN_IN), lambda bi, i: (0, 0)),
        ],
        out_specs=pl.BlockSpec((1, TM, EVEN_IN), lambda bi, i: (bi, i, 0)),
        out_shape=jax.ShapeDtypeStruct((b, l, EVEN_IN), BF16),
        compiler_params=_params(("arbitrary", "arbitrary")),
        name="even_front",
    )(x, g, scale, shift, w)


def _low_lane_mask(shape):
    return lax.broadcasted_iota(jnp.int32, shape, len(shape) - 1) < HEAD_DIM


def _na_kernel(q_ref, kp_ref, kc_ref, kn_ref, vp_ref, vc_ref, vn_ref, bias_ref, o_ref,
               kwin, vwin, s_scr, p_scr, *, rows):
    ci = pl.program_id(1)
    halo = NA_HALO_ROWS * GRID_W
    npair = NA_HEADS // 2
    kwin[0:halo] = kp_ref[0]
    kwin[halo:halo + NA_CHUNK] = kc_ref[0]
    kwin[halo + NA_CHUNK:2 * halo + NA_CHUNK] = kn_ref[0]
    for j in range(npair):
        lanes = slice(j * LANES, (j + 1) * LANES)
        dst = slice(2 * j * LANES, (2 * j + 1) * LANES)
        vwin[0:halo, dst] = vp_ref[0, :, lanes]
        vwin[halo:halo + NA_CHUNK, dst] = vc_ref[0, :, lanes]
        vwin[halo + NA_CHUNK:2 * halo + NA_CHUNK, dst] = vn_ref[0, :, lanes]
        vwin[:, (2 * j + 1) * LANES:(2 * j + 2) * LANES] = jnp.ones((2 * halo + NA_CHUNK, LANES), BF16)

    low = _low_lane_mask((GRID_W, LANES))
    kh = NA_WIN_H
    r0 = ci * NA_ROWS_PER_STEP

    def offsets(i):
        r = r0 + i
        row_start = jnp.clip(r - kh // 2, 0, rows - kh)
        variant = r - row_start
        start = pl.multiple_of((row_start - r0 + NA_HALO_ROWS) * GRID_W, GRID_W)
        return variant, start, pl.multiple_of(i * GRID_W, GRID_W)

    def scores(i):
        variant, start, qoff = offsets(i)
        for j in range(npair):
            lanes = slice(j * LANES, (j + 1) * LANES)
            q2 = q_ref[0, pl.ds(qoff, GRID_W), lanes]
            zero = jnp.zeros_like(q2)
            lhs = jnp.concatenate([jnp.where(low, q2, zero), jnp.where(low, zero, q2)], axis=0)
            kw = kwin[pl.ds(start, kh * GRID_W), lanes]
            s_scr[j] = _dot_nt(lhs, kw) + bias_ref[variant, j]

    def probs():
        for j in range(npair):
            s = s_scr[j]
            p_scr[j] = jnp.exp(s - jnp.max(s, axis=-1, keepdims=True)).astype(BF16)

    def output(i):
        _, start, qoff = offsets(i)
        for j in range(npair):
            vw = vwin[pl.ds(start, kh * GRID_W), 2 * j * LANES:(2 * j + 2) * LANES]
            res = _dot(p_scr[j], vw)
            o = res[:, 0:LANES] / res[:, LANES:2 * LANES]
            o_ref[0, pl.ds(qoff, GRID_W), j * LANES:(j + 1) * LANES] = (
                jnp.where(low, o[:GRID_W], o[GRID_W:]).astype(BF16))

    scores(0)
    probs()
    scores(1)

    def body(i, carry):
        output(i - 2)
        probs()
        scores(i)
        return carry

    lax.fori_loop(2, NA_ROWS_PER_STEP, body, 0, unroll=2)
    output(NA_ROWS_PER_STEP - 2)
    probs()
    output(NA_ROWS_PER_STEP - 1)


def _na_attention(u, bias_tbl):
    b, l, _ = u.shape
    rows = l // GRID_W
    nchunk = l // NA_CHUNK
    halo = NA_HALO_ROWS * GRID_W
    per_chunk = NA_CHUNK // halo
    nhalo = l // halo
    qblk, kblk, vblk = U_QA // NA_W, U_KA // NA_W, U_VA // NA_W

    def window(col):
        return [
            pl.BlockSpec((1, halo, NA_W), lambda bi, i: (bi, jnp.maximum(i * per_chunk - 1, 0), col)),
            pl.BlockSpec((1, NA_CHUNK, NA_W), lambda bi, i: (bi, i, col)),
            pl.BlockSpec((1, halo, NA_W), lambda bi, i: (bi, jnp.minimum((i + 1) * per_chunk, nhalo - 1), col)),
        ]

    npair = NA_HEADS // 2
    return pl.pallas_call(
        functools.partial(_na_kernel, rows=rows),
        grid=(b, nchunk),
        in_specs=[pl.BlockSpec((1, NA_CHUNK, NA_W), lambda bi, i: (bi, i, qblk))]
        + window(kblk) + window(vblk)
        + [pl.BlockSpec(bias_tbl.shape, lambda bi, i: (0, 0, 0, 0))],
        out_specs=pl.BlockSpec((1, NA_CHUNK, NA_W), lambda bi, i: (bi, i, 0)),
        out_shape=jax.ShapeDtypeStruct((b, l, NA_W), BF16),
        scratch_shapes=[pltpu.VMEM((NA_CHUNK + 2 * halo, NA_W), BF16),
                        pltpu.VMEM((NA_CHUNK + 2 * halo, 2 * NA_W), BF16),
                        pltpu.VMEM((npair, 2 * GRID_W, NA_WIN_H * GRID_W), F32),
                        pltpu.VMEM((npair, 2 * GRID_W, NA_WIN_H * GRID_W), BF16)],
        compiler_params=_params(("arbitrary", "arbitrary")),
        name="na_attention",
    )(u, u, u, u, u, u, u, bias_tbl)


def _wb_kernel(sink_ref, q_ref, kp_ref, kc_ref, kn_ref, vp_ref, vc_ref, vn_ref, bias_ref, o_ref,
               kwin, vwin, s_scr, p_scr, ps_scr, *, nblk):
    ci = pl.program_id(1)
    npair = WB_HEADS // 2
    per_step = WB_CHUNK // WB_BLOCK
    win = WB_CHUNK + 2 * WB_BLOCK
    kwin[0:WB_BLOCK] = kp_ref[0]
    kwin[WB_BLOCK:WB_BLOCK + WB_CHUNK] = kc_ref[0]
    kwin[WB_BLOCK + WB_CHUNK:win] = kn_ref[0]
    vwin[0:WB_BLOCK, 0:LANES] = vp_ref[0]
    vwin[WB_BLOCK:WB_BLOCK + WB_CHUNK, 0:LANES] = vc_ref[0]
    vwin[WB_BLOCK + WB_CHUNK:win, 0:LANES] = vn_ref[0]
    vwin[:, LANES:2 * LANES] = jnp.ones((win, LANES), BF16)

    low = _low_lane_mask((WB_BLOCK, LANES))
    col = lax.broadcasted_iota(jnp.int32, (2 * WB_BLOCK, 3 * WB_BLOCK), 1)
    row = lax.broadcasted_iota(jnp.int32, (2 * WB_BLOCK, 1), 0)

    def scores(n):
        gblk = ci * per_step + n
        lo = jnp.where(gblk > 0, 0, WB_BLOCK)
        hi = jnp.where(gblk < nblk - 1, 3 * WB_BLOCK, 2 * WB_BLOCK)
        in_seq = jnp.logical_and(col >= lo, col < hi)
        off = pl.multiple_of(n * WB_BLOCK, WB_BLOCK)
        kw = kwin[pl.ds(off, 3 * WB_BLOCK), :]
        for j in range(npair):
            q2 = q_ref[0, pl.ds(off, WB_BLOCK), j * LANES:(j + 1) * LANES]
            zero = jnp.zeros_like(q2)
            lhs = jnp.concatenate([jnp.where(low, q2, zero), jnp.where(low, zero, q2)], axis=0)
            s_scr[j] = jnp.where(in_seq, _dot_nt(lhs, kw) + bias_ref[j], NEG)

    def probs():
        for j in range(npair):
            s = s_scr[j]
            sink = jnp.where(row < WB_BLOCK, sink_ref[WB_HEAD_ORDER[2 * j]],
                             sink_ref[WB_HEAD_ORDER[2 * j + 1]])
            m = jnp.maximum(jnp.max(s, axis=-1, keepdims=True), sink)
            p_scr[j] = jnp.exp(s - m).astype(BF16)
            ps_scr[j] = jnp.broadcast_to(jnp.exp(sink - m), (2 * WB_BLOCK, LANES))

    def output(n):
        off = pl.multiple_of(n * WB_BLOCK, WB_BLOCK)
        vw = vwin[pl.ds(off, 3 * WB_BLOCK), :]
        for j in range(npair):
            res = _dot(p_scr[j], vw)
            o = res[:, 0:LANES] / (res[:, LANES:2 * LANES] + ps_scr[j])
            o_ref[0, pl.ds(off, WB_BLOCK), j * LANES:(j + 1) * LANES] = (
                jnp.where(low, o[:WB_BLOCK], o[WB_BLOCK:]).astype(BF16))

    scores(0)
    probs()
    scores(1)

    def body(n, carry):
        output(n - 2)
        probs()
        scores(n)
        return carry

    lax.fori_loop(2, per_step, body, 0, unroll=2)
    output(per_step - 2)
    probs()
    output(per_step - 1)


def _wb_attention(u, bias_tbl, sink):
    b, l, _ = u.shape
    nblk = l // WB_BLOCK
    nchunk = l // WB_CHUNK
    per_step = WB_CHUNK // WB_BLOCK
    qblk = U_QB // WB_QW
    kcol, vcol = U_KB // WB_KVW, U_VB // WB_KVW

    def halo(colblk, d):
        if d == 0:
            return pl.BlockSpec((1, WB_CHUNK, WB_KVW), lambda bi, i: (bi, i, colblk))
        return pl.BlockSpec(
            (1, WB_BLOCK, WB_KVW),
            lambda bi, i: (bi, jnp.clip(i * per_step + (per_step if d > 0 else -1), 0, nblk - 1), colblk))

    return pl.pallas_call(
        functools.partial(_wb_kernel, nblk=nblk),
        grid=(b, nchunk),
        in_specs=[
            pl.BlockSpec(memory_space=pltpu.SMEM),
            pl.BlockSpec((1, WB_CHUNK, WB_QW), lambda bi, i: (bi, i, qblk)),
            halo(kcol, -1), halo(kcol, 0), halo(kcol, 1),
            halo(vcol, -1), halo(vcol, 0), halo(vcol, 1),
            pl.BlockSpec(bias_tbl.shape, lambda bi, i: (0, 0, 0)),
        ],
        out_specs=pl.BlockSpec((1, WB_CHUNK, WB_QW), lambda bi, i: (bi, i, 0)),
        out_shape=jax.ShapeDtypeStruct((b, l, WB_QW), BF16),
        scratch_shapes=[pltpu.VMEM((WB_CHUNK + 2 * WB_BLOCK, WB_KVW), BF16),
                        pltpu.VMEM((WB_CHUNK + 2 * WB_BLOCK, 2 * WB_KVW), BF16),
                        pltpu.VMEM((WB_HEADS // 2, 2 * WB_BLOCK, 3 * WB_BLOCK), F32),
                        pltpu.VMEM((WB_HEADS // 2, 2 * WB_BLOCK, 3 * WB_BLOCK), BF16),
                        pltpu.VMEM((WB_HEADS // 2, 2 * WB_BLOCK, LANES), F32)],
        compiler_params=_params(("arbitrary", "arbitrary")),
        name="wb_attention",
    )(sink, u, u, u, u, u, u, u, bias_tbl)


def _even_out_tile(x_ref, oa_ref, ob_ref, gate_ref, gm_ref, w_ref):
    g = _silu(gate_ref[0].astype(F32))
    o = jnp.concatenate([oa_ref[0], ob_ref[0]], axis=-1).astype(F32)
    z = (o * g).astype(BF16)
    return x_ref[0] + gm_ref[0] * _dot(z, w_ref[...])


def _odd_front_kernel(x_ref, oa_ref, ob_ref, gate_ref, gm_ref, wo_ref,
                      g_ref, sc_ref, sh_ref, w1t_ref, wnat_ref, gq_ref, gkvt_ref, gkv_ref,
                      wqbt_ref, wk_ref, wvt_ref, cost_ref, sint_ref, cosp_ref, sinp_ref,
                      x1_ref, k_ref, qt_ref, vt_ref, sgt_ref):
    x1 = _even_out_tile(x_ref, oa_ref, ob_ref, gate_ref, gm_ref, wo_ref)
    x1_ref[0] = x1
    h = _modulated_norm(x1, g_ref[...], sc_ref[0], sh_ref[0]).astype(BF16)
    reps = TM // LANES

    ut = _dot_nt(w1t_ref[...], h)
    sgt_ref[0] = _silu(ut[0:ODD_MIX]).astype(BF16)

    qlt = ut[ODD_MIX:ODD_MIX + MLA_Q_RANK]
    qn = qlt * lax.rsqrt(jnp.mean(qlt * qlt, axis=0, keepdims=True) + EPS)
    qn = (qn * jnp.concatenate([gq_ref[...]] * reps, axis=1)).astype(BF16)
    qt = _dot(wqbt_ref[...], qn)
    qscale = (MLA_QK ** -0.5) * LOG2E
    cos_t = cost_ref[...]
    sin_t = sint_ref[...]
    half = MLA_ROPE // 2
    for hd in range(MLA_HEADS):
        base = hd * MLA_QK
        qt_ref[0, hd, 0, 0:MLA_NOPE, :] = (qt[base:base + MLA_NOPE] * qscale).astype(BF16)
        x1 = qt[base + MLA_NOPE:base + MLA_NOPE + half]
        x2 = qt[base + MLA_NOPE + half:base + MLA_QK]
        qt_ref[0, hd, 0, MLA_NOPE:MLA_NOPE + half, :] = ((x1 * cos_t - x2 * sin_t) * qscale).astype(BF16)
        qt_ref[0, hd, 0, MLA_NOPE + half:MLA_QK, :] = ((x2 * cos_t + x1 * sin_t) * qscale).astype(BF16)

    kvt = ut[ODD_MIX + MLA_Q_RANK:ODD_MIX + MLA_Q_RANK + MLA_KV_RANK]
    kvnt = kvt * lax.rsqrt(jnp.mean(kvt * kvt, axis=0, keepdims=True) + EPS)
    kvnt = (kvnt * jnp.concatenate([gkvt_ref[...]] * reps, axis=1)).astype(BF16)
    vt = _dot(wvt_ref[...], kvnt)
    vt_ref[0, :, 0] = vt.reshape(MLA_HEADS, MLA_V, TM).astype(BF16)

    nat = _dot(h, wnat_ref[...])
    kvl = nat[:, 0:MLA_KV_RANK]
    kvn = kvl * lax.rsqrt(jnp.mean(kvl * kvl, axis=-1, keepdims=True) + EPS)
    kvn = (kvn * gkv_ref[...]).astype(BF16)
    kn = _dot(kvn, wk_ref[...])
    kpe = nat[:, LANES:2 * LANES] * cosp_ref[...] + nat[:, 2 * LANES:3 * LANES] * sinp_ref[...]
    for hd in range(MLA_HEADS):
        kh = kn[:, hd * LANES:(hd + 1) * LANES] + kpe
        k_ref[0, hd] = kh[:, 0:MLA_QK].astype(BF16)


def _odd_front(x, oa, ob, u, gate_mod, w_out, g, scale, shift, w1t, wnat, gq, gkvt, gkv, wqbt, wk, wvt,
               cos_t, sin_t, cos_p, sin_p):
    b, l, _ = x.shape
    nt = l // TM
    const2 = lambda bi, i: (0, 0)
    n1 = w1t.shape[0]
    return pl.pallas_call(
        _odd_front_kernel,
        grid=(b, nt),
        in_specs=[
            pl.BlockSpec((1, TM, D_MODEL), lambda bi, i: (bi, i, 0)),
            pl.BlockSpec((1, TM, NA_W), lambda bi, i: (bi, i, 0)),
            pl.BlockSpec((1, TM, WB_QW), lambda bi, i: (bi, i, 0)),
            pl.BlockSpec((1, TM, EVEN_MIX), lambda bi, i: (bi, i, U_GATE // EVEN_MIX)),
            pl.BlockSpec((1, 1, D_MODEL), lambda bi, i: (bi, 0, 0)),
            pl.BlockSpec((EVEN_MIX, D_MODEL), const2),
            pl.BlockSpec((1, D_MODEL), const2),
            pl.BlockSpec((1, 1, D_MODEL), lambda bi, i: (bi, 0, 0)),
            pl.BlockSpec((1, 1, D_MODEL), lambda bi, i: (bi, 0, 0)),
            pl.BlockSpec((n1, D_MODEL), const2),
            pl.BlockSpec((D_MODEL, 3 * LANES), const2),
            pl.BlockSpec((MLA_Q_RANK, LANES), const2),
            pl.BlockSpec((MLA_KV_RANK, LANES), const2),
            pl.BlockSpec((1, MLA_KV_RANK), const2),
            pl.BlockSpec((MLA_HEADS * MLA_QK, MLA_Q_RANK), const2),
            pl.BlockSpec((MLA_KV_RANK, MLA_HEADS * LANES), const2),
            pl.BlockSpec((ODD_MIX, MLA_KV_RANK), const2),
            pl.BlockSpec((MLA_ROPE // 2, TM), lambda bi, i: (0, i)),
            pl.BlockSpec((MLA_ROPE // 2, TM), lambda bi, i: (0, i)),
            pl.BlockSpec((TM, LANES), lambda bi, i: (i, 0)),
            pl.BlockSpec((TM, LANES), lambda bi, i: (i, 0)),
        ],
        out_specs=[
            pl.BlockSpec((1, TM, D_MODEL), lambda bi, i: (bi, i, 0)),
            pl.BlockSpec((1, MLA_HEADS, TM, MLA_QK), lambda bi, i: (bi, 0, i, 0)),
            pl.BlockSpec((1, MLA_HEADS, 1, MLA_QK, TM), lambda bi, i: (bi, 0, i, 0, 0)),
            pl.BlockSpec((1, MLA_HEADS, 1, MLA_V, TM), lambda bi, i: (bi, 0, i, 0, 0)),
            pl.BlockSpec((1, ODD_MIX, TM), lambda bi, i: (bi, 0, i)),
        ],
        out_shape=[
            jax.ShapeDtypeStruct((b, l, D_MODEL), F32),
            jax.ShapeDtypeStruct((b, MLA_HEADS, l, MLA_QK), BF16),
            jax.ShapeDtypeStruct((b, MLA_HEADS, nt, MLA_QK, TM), BF16),
            jax.ShapeDtypeStruct((b, MLA_HEADS, nt, MLA_V, TM), BF16),
            jax.ShapeDtypeStruct((b, ODD_MIX, l), BF16),
        ],
        compiler_params=_params(("arbitrary", "arbitrary")),
        name="odd_front",
    )(x, oa, ob, u, gate_mod, w_out, g, scale, shift, w1t, wnat, gq, gkvt, gkv, wqbt, wk, wvt,
      cos_t, sin_t, cos_p, sin_p)


def _mla_kernel(qt_ref, k_ref, vt_ref, o_ref, s_scr, p_scr, acc_scr, *, hb, nqb, nchunks):
    ones = jnp.ones((ONES_ROWS, MLA_TK), BF16)
    qsub = MLA_TQ // TM
    ksub = MLA_TK // TM
    chunk_bits = nchunks.bit_length() - 1
    qb_bits = nqb.bit_length() - 1
    total = hb * nqb * nchunks

    def decode(i):
        return i >> (chunk_bits + qb_bits), (i >> chunk_bits) & (nqb - 1), i & (nchunks - 1)

    def scores(i):
        h, qb, t = decode(i)
        q = jnp.concatenate([qt_ref[0, h, qsub * qb + j] for j in range(qsub)], axis=1)
        koff = pl.multiple_of(t * MLA_TK, MLA_TK)
        s = _dot(k_ref[0, h, pl.ds(koff, MLA_TK), :], q)
        s_scr[...] = s
        return jnp.max(s, axis=0, keepdims=True)

    def probs(i, m, mt):
        _, _, t = decode(i)
        m = jnp.where(t == 0, NEG, m)
        m_new = jnp.maximum(m, mt)
        p_scr[...] = jnp.exp2(s_scr[...] - m_new).astype(BF16)
        return m_new, jnp.exp2(m - m_new)

    def accumulate(i, alpha):
        h, qb, t = decode(i)
        vt = jnp.concatenate([vt_ref[0, h, ksub * t + j] for j in range(ksub)], axis=1)
        acc_scr[...] = acc_scr[...] * alpha + _dot(jnp.concatenate([vt, ones], axis=0), p_scr[...])
        acc = acc_scr[...]
        o_ref[0, h, qb] = (acc[0:MLA_V] * (1.0 / acc[MLA_V:MLA_V + 1])).astype(BF16)

    acc_scr[...] = jnp.zeros_like(acc_scr)
    m = jnp.full((1, MLA_TQ), NEG, F32)
    mt = scores(0)
    m, alpha = probs(0, m, mt)
    mt = scores(1)

    def body(i, carry):
        m, mt, alpha = carry
        accumulate(i - 2, alpha)
        m, alpha = probs(i - 1, m, mt)
        mt = scores(i)
        return m, mt, alpha

    m, mt, alpha = lax.fori_loop(2, total, body, (m, mt, alpha), unroll=2)
    accumulate(total - 2, alpha)
    m, alpha = probs(total - 1, m, mt)
    accumulate(total - 1, alpha)


def _mla_attention(qt, k, vt):
    b, _, nt, _, _ = qt.shape
    l = nt * TM
    nchunks = l // MLA_TK
    nqb = l // MLA_TQ
    assert MLA_TK % TM == 0 and MLA_TQ % TM == 0
    assert nqb & (nqb - 1) == 0 and nchunks & (nchunks - 1) == 0
    hb = max(1, min(MLA_HEADS, MLA_RESIDENT_TOKENS // l))
    assert hb * nqb * nchunks >= 2 and (hb * nqb * nchunks) % 2 == 0
    return pl.pallas_call(
        functools.partial(_mla_kernel, hb=hb, nqb=nqb, nchunks=nchunks),
        grid=(b, MLA_HEADS // hb),
        in_specs=[
            pl.BlockSpec((1, hb, nt, MLA_QK, TM), lambda bi, h: (bi, h, 0, 0, 0)),
            pl.BlockSpec((1, hb, l, MLA_QK), lambda bi, h: (bi, h, 0, 0)),
            pl.BlockSpec((1, hb, nt, MLA_V, TM), lambda bi, h: (bi, h, 0, 0, 0)),
        ],
        out_specs=pl.BlockSpec((1, hb, nqb, MLA_V, MLA_TQ), lambda bi, h: (bi, h, 0, 0, 0)),
        out_shape=jax.ShapeDtypeStruct((b, MLA_HEADS, nqb, MLA_V, MLA_TQ), BF16),
        scratch_shapes=[pltpu.VMEM((MLA_TK, MLA_TQ), F32),
                        pltpu.VMEM((MLA_TK, MLA_TQ), BF16),
                        pltpu.VMEM((MLA_V + ONES_ROWS, MLA_TQ), F32)],
        compiler_params=_params(("arbitrary", "arbitrary")),
        name="mla_attention",
    )(qt, k, vt)


def _odd_out_kernel(x_ref, ot_ref, sgt_ref, gm_ref, wt_ref, fg_ref, y_ref):
    ot = ot_ref[0, :, 0].reshape(ODD_MIX, TM)
    z = (ot.astype(F32) * sgt_ref[0].astype(F32)).astype(BF16)
    out = lax.dot_general(z, wt_ref[...], (((0,), (1,)), ((), ())), preferred_element_type=F32)
    x2 = x_ref[0] + gm_ref[0] * out
    ms = jnp.mean(x2 * x2, axis=-1, keepdims=True)
    y_ref[0] = (x2 * lax.rsqrt(ms + EPS)) * fg_ref[...]


def _odd_out(x, ot, sgt, gate_mod, w_out_t, final_g):
    b, l, _ = x.shape
    sub = MLA_TQ // TM
    return pl.pallas_call(
        _odd_out_kernel,
        grid=(b, l // TM),
        in_specs=[
            pl.BlockSpec((1, TM, D_MODEL), lambda bi, i: (bi, i, 0)),
            pl.BlockSpec((1, MLA_HEADS, 1, MLA_V, TM), lambda bi, i: (bi, 0, i // sub, 0, i % sub)),
            pl.BlockSpec((1, ODD_MIX, TM), lambda bi, i: (bi, 0, i)),
            pl.BlockSpec((1, 1, D_MODEL), lambda bi, i: (bi, 0, 0)),
            pl.BlockSpec((D_MODEL, ODD_MIX), lambda bi, i: (0, 0)),
            pl.BlockSpec((1, D_MODEL), lambda bi, i: (0, 0)),
        ],
        out_specs=pl.BlockSpec((1, TM, D_MODEL), lambda bi, i: (bi, i, 0)),
        out_shape=jax.ShapeDtypeStruct((b, l, D_MODEL), F32),
        compiler_params=_params(("arbitrary", "arbitrary")),
        name="odd_out",
    )(x, ot, sgt, gate_mod, w_out_t, final_g)


def _t5_bucket(rel):
    nb = T5_BUCKETS // 2
    ret = (rel > 0).astype(np.int32) * nb
    n = np.abs(rel)
    max_exact = nb // 2
    large = max_exact + (np.log(np.maximum(n, 1) / max_exact)
                         / np.log(T5_MAX_DIST / max_exact) * (nb - max_exact)).astype(np.int32)
    large = np.minimum(large, nb - 1)
    return ret + np.where(n < max_exact, n, large)


def _na_bias_table(rpb):
    kh = NA_WIN_H
    c = np.arange(GRID_W)
    col_start = np.clip(c - NA_WIN_W // 2, 0, GRID_W - NA_WIN_W)
    col_ok = (c[None, :] >= col_start[:, None]) & (c[None, :] < col_start[:, None] + NA_WIN_W)
    d_col = np.clip(c[None, :] - c[:, None], -(NA_WIN_W - 1), NA_WIN_W - 1) + NA_WIN_W - 1
    nrel = 2 * NA_WIN_W - 1
    rows = jnp.stack([rpb[:, NA_WIN_H - 1 - v:2 * NA_WIN_H - 1 - v] for v in range(kh)], axis=1)
    onehot = (d_col.reshape(-1)[None, :] == np.arange(nrel)[:, None]).astype(np.float32)
    bias = jnp.dot(rows.astype(F32).reshape(-1, nrel), jnp.asarray(onehot),
                   precision=lax.Precision.HIGHEST)
    bias = bias.reshape(NA_HEADS, kh, kh, GRID_W, GRID_W)
    bias = jnp.where(jnp.asarray(col_ok)[None, None, None, :, :], bias, NEG)
    bias = bias.transpose(1, 0, 3, 2, 4).reshape(kh, NA_HEADS // 2, 2 * GRID_W, kh * GRID_W)
    return bias


def _wb_bias_table(t5_bias):
    rel = (np.arange(3 * WB_BLOCK) - WB_BLOCK)[None, :] - np.arange(WB_BLOCK)[:, None]
    offs = np.arange(-(2 * WB_BLOCK - 1), 2 * WB_BLOCK + 1)
    period = 4 * WB_BLOCK
    by_off = t5_bias[_t5_bucket(offs)].astype(F32).T
    shifted = jnp.tile(by_off, (1, WB_BLOCK))[:, :WB_BLOCK * (period - 1)]
    shifted = shifted.reshape(WB_HEADS, WB_BLOCK, period - 1)
    bias = shifted[:, :, WB_BLOCK - 1:4 * WB_BLOCK - 1]
    bias = jnp.where(jnp.asarray(np.abs(rel) <= WB_WINDOW)[None], bias, NEG)
    bias = bias[np.asarray(WB_HEAD_ORDER)]
    return bias.reshape(WB_HEADS // 2, 2 * WB_BLOCK, 3 * WB_BLOCK)


def _even_weights(w_in, w_out):
    qa, ka, va, qb, kb, vb, gate = jnp.split(
        w_in, [NA_W, 2 * NA_W, 3 * NA_W, 3 * NA_W + WB_QW, 3 * NA_W + WB_QW + WB_KVW,
               3 * NA_W + WB_QW + 2 * WB_KVW], axis=-1)
    order = np.asarray(WB_HEAD_ORDER)
    perm = (order[:, None] * HEAD_DIM + np.arange(HEAD_DIM)[None, :]).reshape(-1)
    qscale = HEAD_DIM ** -0.5
    gate = jnp.concatenate([gate[:, :NA_W], gate[:, NA_W:][:, perm]], axis=-1)
    w = jnp.concatenate([gate, qa * qscale, ka, va, qb[:, perm] * qscale, kb, vb], axis=-1)
    w_out_p = jnp.concatenate([w_out[:NA_W], w_out[NA_W:][perm]], axis=0)
    return w.astype(BF16), w_out_p.astype(BF16)


def _odd_weights(w_in, q_norm, w_qb, kv_norm, w_kvb, w_out):
    q_lat, kv_lat, k_rope, gate = jnp.split(
        w_in, [MLA_Q_RANK, MLA_Q_RANK + MLA_KV_RANK, MLA_Q_RANK + MLA_KV_RANK + MLA_ROPE], axis=-1)
    w1t = jnp.concatenate([gate, q_lat, kv_lat], axis=-1).T.astype(BF16)
    half = MLA_ROPE // 2
    k_rot = jnp.concatenate([-k_rope[:, half:], k_rope[:, :half]], axis=-1)
    z64 = jnp.zeros((D_MODEL, MLA_NOPE), w_in.dtype)
    z32 = jnp.zeros((D_MODEL, LANES - MLA_QK), w_in.dtype)
    wnat = jnp.concatenate([kv_lat, z64, k_rope, z32, z64, k_rot, z32], axis=-1).astype(BF16)
    wkv = w_kvb.reshape(MLA_KV_RANK, MLA_HEADS, MLA_NOPE + MLA_V)
    wk = jnp.concatenate([wkv[:, :, :MLA_NOPE], jnp.zeros_like(wkv[:, :, :MLA_NOPE])], axis=-1)
    wk = wk.reshape(MLA_KV_RANK, MLA_HEADS * LANES).astype(BF16)
    wvt = wkv[:, :, MLA_NOPE:].reshape(MLA_KV_RANK, ODD_MIX).T.astype(BF16)
    gq = jnp.broadcast_to(q_norm.astype(F32)[:, None], (MLA_Q_RANK, LANES))
    gkvt = jnp.broadcast_to(kv_norm.astype(F32)[:, None], (MLA_KV_RANK, LANES))
    gkv = kv_norm.astype(F32)[None, :]
    return w1t, wnat, gq, gkvt, gkv, w_qb.T.astype(BF16), wk, wvt, w_out.T.astype(BF16)


def _rope_tables(l):
    inv_freq = 1.0 / (ROPE_THETA ** (jnp.arange(0, MLA_ROPE, 2, dtype=F32) / MLA_ROPE))
    ang = jnp.arange(l, dtype=F32)[:, None] * inv_freq[None, :]
    cos, sin = jnp.cos(ang), jnp.sin(ang)
    z64 = jnp.zeros((l, MLA_NOPE), F32)
    z32 = jnp.zeros((l, LANES - MLA_QK), F32)
    cos_p = jnp.concatenate([z64, cos, cos, z32], axis=-1)
    sin_p = jnp.concatenate([z64, sin, sin, z32], axis=-1)
    return cos.T, sin.T, cos_p, sin_p


def _trunk(x, mod, norm_g, ev, na_tbl, wb_tbl, sink, od, final_g):
    b, l, _ = x.shape
    assert l % TM == 0 and l % NA_CHUNK == 0 and l % WB_CHUNK == 0 and l % MLA_TK == 0
    assert l // GRID_W >= NA_ROWS_PER_STEP >= NA_HALO_ROWS >= NA_WIN_H
    shift0, scale0, gate0 = [t[:, None, :] for t in jnp.split(mod[0], 3, axis=-1)]
    shift1, scale1, gate1 = [t[:, None, :] for t in jnp.split(mod[1], 3, axis=-1)]

    w_in0, w_out0 = ev
    u = _even_front(x, norm_g[0][None, :], scale0, shift0, w_in0)
    oa = _na_attention(u, na_tbl)
    ob = _wb_attention(u, wb_tbl, sink)

    w1t, wnat, gq, gkvt, gkv, wqbt, wk, wvt, w_out1t = od
    cos_t, sin_t, cos_p, sin_p = _rope_tables(l)
    x1, k, qt, vt, sgt = _odd_front(x, oa, ob, u, gate0, w_out0, norm_g[1][None, :], scale1, shift1, w1t, wnat, gq, gkvt, gkv,
                                wqbt, wk, wvt, cos_t, sin_t, cos_p, sin_p)
    ot = _mla_attention(qt, k, vt)
    return _odd_out(x1, ot, sgt, gate1, w_out1t, final_g[None, :])


def kernel(x_prompt, x_sample, c_prompt, c_sample, ada_w, ada_b, norm_g, t5_bias, ev_w_in, na_rpb,
           wb_sink, ev_w_out, mla_w_in, mla_q_norm, mla_w_qb, mla_kv_norm, mla_w_kvb, mla_w_out, final_g):
    bp, bs = c_prompt.shape[0], c_sample.shape[0]
    rows = -(-(bp + bs) // 16) * 16
    c_pad = jnp.concatenate([c_prompt, c_sample, jnp.zeros((rows - bp - bs, D_MODEL), F32)], axis=0)
    mod = _ada_mod(c_pad, ada_w.astype(BF16), ada_b[:, None, :])

    ev = _even_weights(ev_w_in[0], ev_w_out[0])
    na_tbl = _na_bias_table(na_rpb[0])
    wb_tbl = _wb_bias_table(t5_bias)
    od = _odd_weights(mla_w_in[0], mla_q_norm[0], mla_w_qb[0], mla_kv_norm[0], mla_w_kvb[0], mla_w_out[0])
    sink = wb_sink[0].astype(F32)

    y_prompt = _trunk(x_prompt, mod[:, :bp], norm_g, ev, na_tbl, wb_tbl, sink, od, final_g)
    y_sample = _trunk(x_sample, mod[:, bp:bp + bs], norm_g, ev, na_tbl, wb_tbl, sink, od, final_g)
    return (y_prompt, y_sample)
```

```python
import functools
import math

import numpy as np
import jax
import jax.numpy as jnp
from jax import lax
from jax.experimental import pallas as pl
from jax.experimental.pallas import tpu as pltpu

D_MODEL = 1024
GRID_W = 64
HEAD_DIM = 64
EPS = 1e-6
NEG = -1e30
NA_HEADS = 8
NA_WIN_H = 8
NA_WIN_W = 16
WB_HEADS = 8
WB_KV_HEADS = 2
WB_WINDOW = 128
WB_BLOCK = 128
T5_BUCKETS = 32
T5_MAX_DIST = 128
MLA_HEADS = 16
MLA_Q_RANK = 256
MLA_KV_RANK = 128
MLA_NOPE = 64
MLA_ROPE = 32
MLA_V = 64
ROPE_THETA = 10000.0
MLA_QK = MLA_NOPE + MLA_ROPE

NA_W = NA_HEADS * HEAD_DIM
WB_QW = WB_HEADS * HEAD_DIM
WB_KVW = WB_KV_HEADS * HEAD_DIM
EVEN_MIX = NA_W + WB_QW
EVEN_IN = 3 * NA_W + WB_QW + 2 * WB_KVW + EVEN_MIX
ODD_MIX = MLA_HEADS * MLA_V

U_GATE = 0
U_QA = EVEN_MIX
U_KA = U_QA + NA_W
U_VA = U_KA + NA_W
U_QB = U_VA + NA_W
U_KB = U_QB + WB_QW
U_VB = U_KB + WB_KVW

LANES = 128
TM = 512
NA_ROWS_PER_STEP = 16
NA_HALO_ROWS = 8
NA_CHUNK = NA_ROWS_PER_STEP * GRID_W
WB_CHUNK = 1024
MLA_TQ = 1024
MLA_TK = 2048
MLA_PIECE = 512
MLA_RESIDENT_TOKENS = 16384
ONES_ROWS = 16
VMEM_LIMIT = 56 * 1024 * 1024

BF16 = jnp.bfloat16
F32 = jnp.float32
LOG2E = math.log2(math.e)

WB_HEAD_ORDER = (0, 4, 1, 5, 2, 6, 3, 7)


def _params(sem):
    return pltpu.CompilerParams(dimension_semantics=sem, vmem_limit_bytes=VMEM_LIMIT)


def _dot(a, b):
    return jnp.dot(a, b, preferred_element_type=F32)


def _dot_nt(a, b):
    return lax.dot_general(a, b, (((1,), (1,)), ((), ())), preferred_element_type=F32)


def _silu(x):
    return x * (1.0 / (1.0 + jnp.exp(-x)))


def _modulated_norm(x, g, scale, shift):
    ms = jnp.mean(x * x, axis=-1, keepdims=True)
    y = x * lax.rsqrt(ms + EPS)
    return (y * g) * (1.0 + scale) + shift


def _ada_kernel(c_ref, w_ref, b_ref, o_ref):
    c = c_ref[...]
    cs = _silu(c).astype(BF16)
    o_ref[0] = _dot(cs, w_ref[0]) + b_ref[0]


def _ada_mod(c_pad, ada_w, ada_b):
    depth = ada_w.shape[0]
    rows = c_pad.shape[0]
    tn = 768
    return pl.pallas_call(
        _ada_kernel,
        grid=(depth, 3 * D_MODEL // tn),
        in_specs=[
            pl.BlockSpec((rows, D_MODEL), lambda i, n: (0, 0)),
            pl.BlockSpec((1, D_MODEL, tn), lambda i, n: (i, 0, n)),
            pl.BlockSpec((1, 1, tn), lambda i, n: (i, 0, n)),
        ],
        out_specs=pl.BlockSpec((1, rows, tn), lambda i, n: (i, 0, n)),
        out_shape=jax.ShapeDtypeStruct((depth, rows, 3 * D_MODEL), F32),
        compiler_params=_params(("arbitrary", "arbitrary")),
        name="ada_mod",
    )(c_pad, ada_w, ada_b)


def _even_front_kernel(x_ref, g_ref, sc_ref, sh_ref, w_ref, u_ref):
    h = _modulated_norm(x_ref[0], g_ref[...], sc_ref[0], sh_ref[0]).astype(BF16)
    u_ref[0] = _dot(h, w_ref[...]).astype(BF16)


def _even_front(x, g, scale, shift, w):
    b, l, _ = x.shape
    return pl.pallas_call(
        _even_front_kernel,
        grid=(b, l // TM),
        in_specs=[
            pl.BlockSpec((1, TM, D_MODEL), lambda bi, i: (bi, i, 0)),
            pl.BlockSpec((1, D_MODEL), lambda bi, i: (0, 0)),
            pl.BlockSpec((1, 1, D_MODEL), lambda bi, i: (bi, 0, 0)),
            pl.BlockSpec((1, 1, D_MODEL), lambda bi, i: (bi, 0, 0)),
            pl.BlockSpec((D_MODEL, EVEN_IN), lambda bi, i: (0, 0)),
        ],
        out_specs=pl.BlockSpec((1, TM, EVEN_IN), lambda bi, i: (bi, i, 0)),
        out_shape=jax.ShapeDtypeStruct((b, l, EVEN_IN), BF16),
        compiler_params=_params(("arbitrary", "arbitrary")),
        name="even_front",
    )(x, g, scale, shift, w)


def _low_lane_mask(shape):
    return lax.broadcasted_iota(jnp.int32, shape, len(shape) - 1) < HEAD_DIM


def _na_kernel(q_ref, kp_ref, kc_ref, kn_ref, vp_ref, vc_ref, vn_ref, bias_ref, o_ref,
               kwin, vwin, s_scr, p_scr, *, rows):
    ci = pl.program_id(1)
    halo = NA_HALO_ROWS * GRID_W
    npair = NA_HEADS // 2
    kwin[0:halo] = kp_ref[0]
    kwin[halo:halo + NA_CHUNK] = kc_ref[0]
    kwin[halo + NA_CHUNK:2 * halo + NA_CHUNK] = kn_ref[0]
    for j in range(npair):
        lanes = slice(j * LANES, (j + 1) * LANES)
        dst = slice(2 * j * LANES, (2 * j + 1) * LANES)
        vwin[0:halo, dst] = vp_ref[0, :, lanes]
        vwin[halo:halo + NA_CHUNK, dst] = vc_ref[0, :, lanes]
        vwin[halo + NA_CHUNK:2 * halo + NA_CHUNK, dst] = vn_ref[0, :, lanes]
        vwin[:, (2 * j + 1) * LANES:(2 * j + 2) * LANES] = jnp.ones((2 * halo + NA_CHUNK, LANES), BF16)

    low = _low_lane_mask((GRID_W, LANES))
    kh = NA_WIN_H
    r0 = ci * NA_ROWS_PER_STEP

    def offsets(i):
        r = r0 + i
        row_start = jnp.clip(r - kh // 2, 0, rows - kh)
        variant = r - row_start
        start = pl.multiple_of((row_start - r0 + NA_HALO_ROWS) * GRID_W, GRID_W)
        return variant, start, pl.multiple_of(i * GRID_W, GRID_W)

    def scores(i):
        variant, start, qoff = offsets(i)
        for j in range(npair):
            lanes = slice(j * LANES, (j + 1) * LANES)
            q2 = q_ref[0, pl.ds(qoff, GRID_W), lanes]
            zero = jnp.zeros_like(q2)
            lhs = jnp.concatenate([jnp.where(low, q2, zero), jnp.where(low, zero, q2)], axis=0)
            kw = kwin[pl.ds(start, kh * GRID_W), lanes]
            s_scr[j] = _dot_nt(lhs, kw) + bias_ref[variant, j]

    def probs():
        for j in range(npair):
            s = s_scr[j]
            p_scr[j] = jnp.exp(s - jnp.max(s, axis=-1, keepdims=True)).astype(BF16)

    def output(i):
        _, start, qoff = offsets(i)
        for j in range(npair):
            vw = vwin[pl.ds(start, kh * GRID_W), 2 * j * LANES:(2 * j + 2) * LANES]
            res = _dot(p_scr[j], vw)
            o = res[:, 0:LANES] / res[:, LANES:2 * LANES]
            o_ref[0, pl.ds(qoff, GRID_W), j * LANES:(j + 1) * LANES] = (
                jnp.where(low, o[:GRID_W], o[GRID_W:]).astype(BF16))

    scores(0)
    probs()
    scores(1)

    def body(i, carry):
        output(i - 2)
        probs()
        scores(i)
        return carry

    lax.fori_loop(2, NA_ROWS_PER_STEP, body, 0, unroll=2)
    output(NA_ROWS_PER_STEP - 2)
    probs()
    output(NA_ROWS_PER_STEP - 1)


def _na_attention(u, bias_tbl):
    b, l, _ = u.shape
    rows = l // GRID_W
    nchunk = l // NA_CHUNK
    halo = NA_HALO_ROWS * GRID_W
    per_chunk = NA_CHUNK // halo
    nhalo = l // halo
    qblk, kblk, vblk = U_QA // NA_W, U_KA // NA_W, U_VA // NA_W

    def window(col):
        return [
            pl.BlockSpec((1, halo, NA_W), lambda bi, i: (bi, jnp.maximum(i * per_chunk - 1, 0), col)),
            pl.BlockSpec((1, NA_CHUNK, NA_W), lambda bi, i: (bi, i, col)),
            pl.BlockSpec((1, halo, NA_W), lambda bi, i: (bi, jnp.minimum((i + 1) * per_chunk, nhalo - 1), col)),
        ]

    npair = NA_HEADS // 2
    return pl.pallas_call(
        functools.partial(_na_kernel, rows=rows),
        grid=(b, nchunk),
        in_specs=[pl.BlockSpec((1, NA_CHUNK, NA_W), lambda bi, i: (bi, i, qblk))]
        + window(kblk) + window(vblk)
        + [pl.BlockSpec(bias_tbl.shape, lambda bi, i: (0, 0, 0, 0))],
        out_specs=pl.BlockSpec((1, NA_CHUNK, NA_W), lambda bi, i: (bi, i, 0)),
        out_shape=jax.ShapeDtypeStruct((b, l, NA_W), BF16),
        scratch_shapes=[pltpu.VMEM((NA_CHUNK + 2 * halo, NA_W), BF16),
                        pltpu.VMEM((NA_CHUNK + 2 * halo, 2 * NA_W), BF16),
                        pltpu.VMEM((npair, 2 * GRID_W, NA_WIN_H * GRID_W), F32),
                        pltpu.VMEM((npair, 2 * GRID_W, NA_WIN_H * GRID_W), BF16)],
        compiler_params=_params(("arbitrary", "arbitrary")),
        name="na_attention",
    )(u, u, u, u, u, u, u, bias_tbl)


def _wb_kernel(sink_ref, q_ref, kp_ref, kc_ref, kn_ref, vp_ref, vc_ref, vn_ref, bias_ref, o_ref,
               kwin, vwin, s_scr, p_scr, ps_scr, *, nblk):
    ci = pl.program_id(1)
    npair = WB_HEADS // 2
    per_step = WB_CHUNK // WB_BLOCK
    win = WB_CHUNK + 2 * WB_BLOCK
    kwin[0:WB_BLOCK] = kp_ref[0]
    kwin[WB_BLOCK:WB_BLOCK + WB_CHUNK] = kc_ref[0]
    kwin[WB_BLOCK + WB_CHUNK:win] = kn_ref[0]
    vwin[0:WB_BLOCK, 0:LANES] = vp_ref[0]
    vwin[WB_BLOCK:WB_BLOCK + WB_CHUNK, 0:LANES] = vc_ref[0]
    vwin[WB_BLOCK + WB_CHUNK:win, 0:LANES] = vn_ref[0]
    vwin[:, LANES:2 * LANES] = jnp.ones((win, LANES), BF16)

    low = _low_lane_mask((WB_BLOCK, LANES))
    col = lax.broadcasted_iota(jnp.int32, (2 * WB_BLOCK, 3 * WB_BLOCK), 1)
    row = lax.broadcasted_iota(jnp.int32, (2 * WB_BLOCK, 1), 0)

    def scores(n):
        gblk = ci * per_step + n
        lo = jnp.where(gblk > 0, 0, WB_BLOCK)
        hi = jnp.where(gblk < nblk - 1, 3 * WB_BLOCK, 2 * WB_BLOCK)
        in_seq = jnp.logical_and(col >= lo, col < hi)
        off = pl.multiple_of(n * WB_BLOCK, WB_BLOCK)
        kw = kwin[pl.ds(off, 3 * WB_BLOCK), :]
        for j in range(npair):
            q2 = q_ref[0, pl.ds(off, WB_BLOCK), j * LANES:(j + 1) * LANES]
            zero = jnp.zeros_like(q2)
            lhs = jnp.concatenate([jnp.where(low, q2, zero), jnp.where(low, zero, q2)], axis=0)
            s_scr[j] = jnp.where(in_seq, _dot_nt(lhs, kw) + bias_ref[j], NEG)

    def probs():
        for j in range(npair):
            s = s_scr[j]
            sink = jnp.where(row < WB_BLOCK, sink_ref[WB_HEAD_ORDER[2 * j]],
                             sink_ref[WB_HEAD_ORDER[2 * j + 1]])
            m = jnp.maximum(jnp.max(s, axis=-1, keepdims=True), sink)
            p_scr[j] = jnp.exp(s - m).astype(BF16)
            ps_scr[j] = jnp.broadcast_to(jnp.exp(sink - m), (2 * WB_BLOCK, LANES))

    def output(n):
        off = pl.multiple_of(n * WB_BLOCK, WB_BLOCK)
        vw = vwin[pl.ds(off, 3 * WB_BLOCK), :]
        for j in range(npair):
            res = _dot(p_scr[j], vw)
            o = res[:, 0:LANES] / (res[:, LANES:2 * LANES] + ps_scr[j])
            o_ref[0, pl.ds(off, WB_BLOCK), j * LANES:(j + 1) * LANES] = (
                jnp.where(low, o[:WB_BLOCK], o[WB_BLOCK:]).astype(BF16))

    scores(0)
    probs()
    scores(1)

    def body(n, carry):
        output(n - 2)
        probs()
        scores(n)
        return carry

    lax.fori_loop(2, per_step, body, 0, unroll=2)
    output(per_step - 2)
    probs()
    output(per_step - 1)


def _wb_attention(u, bias_tbl, sink):
    b, l, _ = u.shape
    nblk = l // WB_BLOCK
    nchunk = l // WB_CHUNK
    per_step = WB_CHUNK // WB_BLOCK
    qblk = U_QB // WB_QW
    kcol, vcol = U_KB // WB_KVW, U_VB // WB_KVW

    def halo(colblk, d):
        if d == 0:
            return pl.BlockSpec((1, WB_CHUNK, WB_KVW), lambda bi, i: (bi, i, colblk))
        return pl.BlockSpec(
            (1, WB_BLOCK, WB_KVW),
            lambda bi, i: (bi, jnp.clip(i * per_step + (per_step if d > 0 else -1), 0, nblk - 1), colblk))

    return pl.pallas_call(
        functools.partial(_wb_kernel, nblk=nblk),
        grid=(b, nchunk),
        in_specs=[
            pl.BlockSpec(memory_space=pltpu.SMEM),
            pl.BlockSpec((1, WB_CHUNK, WB_QW), lambda bi, i: (bi, i, qblk)),
            halo(kcol, -1), halo(kcol, 0), halo(kcol, 1),
            halo(vcol, -1), halo(vcol, 0), halo(vcol, 1),
            pl.BlockSpec(bias_tbl.shape, lambda bi, i: (0, 0, 0)),
        ],
        out_specs=pl.BlockSpec((1, WB_CHUNK, WB_QW), lambda bi, i: (bi, i, 0)),
        out_shape=jax.ShapeDtypeStruct((b, l, WB_QW), BF16),
        scratch_shapes=[pltpu.VMEM((WB_CHUNK + 2 * WB_BLOCK, WB_KVW), BF16),
                        pltpu.VMEM((WB_CHUNK + 2 * WB_BLOCK, 2 * WB_KVW), BF16),
                        pltpu.VMEM((WB_HEADS // 2, 2 * WB_BLOCK, 3 * WB_BLOCK), F32),
                        pltpu.VMEM((WB_HEADS // 2, 2 * WB_BLOCK, 3 * WB_BLOCK), BF16),
                        pltpu.VMEM((WB_HEADS // 2, 2 * WB_BLOCK, LANES), F32)],
        compiler_params=_params(("arbitrary", "arbitrary")),
        name="wb_attention",
    )(sink, u, u, u, u, u, u, u, bias_tbl)


def _even_out_tile(x_ref, oa_ref, ob_ref, gate_ref, gm_ref, w_ref):
    g = _silu(gate_ref[0].astype(F32))
    o = jnp.concatenate([oa_ref[0], ob_ref[0]], axis=-1).astype(F32)
    z = (o * g).astype(BF16)
    return x_ref[0] + gm_ref[0] * _dot(z, w_ref[...])


def _odd_front_kernel(x_ref, oa_ref, ob_ref, gate_ref, gm_ref, wo_ref,
                      g_ref, sc_ref, sh_ref, w1t_ref, wnat_ref, gq_ref, gkvt_ref, gkv_ref,
                      wqbt_ref, wk_ref, wvt_ref, cost_ref, sint_ref, cosp_ref, sinp_ref,
                      x1_ref, k_ref, qt_ref, vt_ref, sgt_ref):
    x1 = _even_out_tile(x_ref, oa_ref, ob_ref, gate_ref, gm_ref, wo_ref)
    x1_ref[0] = x1
    h = _modulated_norm(x1, g_ref[...], sc_ref[0], sh_ref[0]).astype(BF16)
    reps = TM // LANES

    ut = _dot_nt(w1t_ref[...], h)
    sgt_ref[0] = _silu(ut[0:ODD_MIX]).astype(BF16)

    qlt = ut[ODD_MIX:ODD_MIX + MLA_Q_RANK]
    qn = qlt * lax.rsqrt(jnp.mean(qlt * qlt, axis=0, keepdims=True) + EPS)
    qn = (qn * jnp.concatenate([gq_ref[...]] * reps, axis=1)).astype(BF16)
    qt = _dot(wqbt_ref[...], qn)
    qscale = (MLA_QK ** -0.5) * LOG2E
    cos_t = cost_ref[...]
    sin_t = sint_ref[...]
    half = MLA_ROPE // 2
    for hd in range(MLA_HEADS):
        base = hd * MLA_QK
        qt_ref[0, hd, 0, 0:MLA_NOPE, :] = (qt[base:base + MLA_NOPE] * qscale).astype(BF16)
        x1 = qt[base + MLA_NOPE:base + MLA_NOPE + half]
        x2 = qt[base + MLA_NOPE + half:base + MLA_QK]
        qt_ref[0, hd, 0, MLA_NOPE:MLA_NOPE + half, :] = ((x1 * cos_t - x2 * sin_t) * qscale).astype(BF16)
        qt_ref[0, hd, 0, MLA_NOPE + half:MLA_QK, :] = ((x2 * cos_t + x1 * sin_t) * qscale).astype(BF16)

    kvt = ut[ODD_MIX + MLA_Q_RANK:ODD_MIX + MLA_Q_RANK + MLA_KV_RANK]
    kvnt = kvt * lax.rsqrt(jnp.mean(kvt * kvt, axis=0, keepdims=True) + EPS)
    kvnt = (kvnt * jnp.concatenate([gkvt_ref[...]] * reps, axis=1)).astype(BF16)
    vt = _dot(wvt_ref[...], kvnt)
    vt_ref[0, :, 0] = vt.reshape(MLA_HEADS, MLA_V, TM).astype(BF16)

    nat = _dot(h, wnat_ref[...])
    kvl = nat[:, 0:MLA_KV_RANK]
    kvn = kvl * lax.rsqrt(jnp.mean(kvl * kvl, axis=-1, keepdims=True) + EPS)
    kvn = (kvn * gkv_ref[...]).astype(BF16)
    kn = _dot(kvn, wk_ref[...])
    kpe = nat[:, LANES:2 * LANES] * cosp_ref[...] + nat[:, 2 * LANES:3 * LANES] * sinp_ref[...]
    for hd in range(MLA_HEADS):
        kh = kn[:, hd * LANES:(hd + 1) * LANES] + kpe
        k_ref[0, hd] = kh[:, 0:MLA_QK].astype(BF16)


def _odd_front(x, oa, ob, u, gate_mod, w_out, g, scale, shift, w1t, wnat, gq, gkvt, gkv, wqbt, wk, wvt,
               cos_t, sin_t, cos_p, sin_p):
    b, l, _ = x.shape
    nt = l // TM
    const2 = lambda bi, i: (0, 0)
    n1 = w1t.shape[0]
    return pl.pallas_call(
        _odd_front_kernel,
        grid=(b, nt),
        in_specs=[
            pl.BlockSpec((1, TM, D_MODEL), lambda bi, i: (bi, i, 0)),
            pl.BlockSpec((1, TM, NA_W), lambda bi, i: (bi, i, 0)),
            pl.BlockSpec((1, TM, WB_QW), lambda bi, i: (bi, i, 0)),
            pl.BlockSpec((1, TM, EVEN_MIX), lambda bi, i: (bi, i, U_GATE // EVEN_MIX)),
            pl.BlockSpec((1, 1, D_MODEL), lambda bi, i: (bi, 0, 0)),
            pl.BlockSpec((EVEN_MIX, D_MODEL), const2),
            pl.BlockSpec((1, D_MODEL), const2),
            pl.BlockSpec((1, 1, D_MODEL), lambda bi, i: (bi, 0, 0)),
            pl.BlockSpec((1, 1, D_MODEL), lambda bi, i: (bi, 0, 0)),
            pl.BlockSpec((n1, D_MODEL), const2),
            pl.BlockSpec((D_MODEL, 3 * LANES), const2),
            pl.BlockSpec((MLA_Q_RANK, LANES), const2),
            pl.BlockSpec((MLA_KV_RANK, LANES), const2),
            pl.BlockSpec((1, MLA_KV_RANK), const2),
            pl.BlockSpec((MLA_HEADS * MLA_QK, MLA_Q_RANK), const2),
            pl.BlockSpec((MLA_KV_RANK, MLA_HEADS * LANES), const2),
            pl.BlockSpec((ODD_MIX, MLA_KV_RANK), const2),
            pl.BlockSpec((MLA_ROPE // 2, TM), lambda bi, i: (0, i)),
            pl.BlockSpec((MLA_ROPE // 2, TM), lambda bi, i: (0, i)),
            pl.BlockSpec((TM, LANES), lambda bi, i: (i, 0)),
            pl.BlockSpec((TM, LANES), lambda bi, i: (i, 0)),
        ],
        out_specs=[
            pl.BlockSpec((1, TM, D_MODEL), lambda bi, i: (bi, i, 0)),
            pl.BlockSpec((1, MLA_HEADS, TM, MLA_QK), lambda bi, i: (bi, 0, i, 0)),
            pl.BlockSpec((1, MLA_HEADS, 1, MLA_QK, TM), lambda bi, i: (bi, 0, i, 0, 0)),
            pl.BlockSpec((1, MLA_HEADS, 1, MLA_V, TM), lambda bi, i: (bi, 0, i, 0, 0)),
            pl.BlockSpec((1, ODD_MIX, TM), lambda bi, i: (bi, 0, i)),
        ],
        out_shape=[
            jax.ShapeDtypeStruct((b, l, D_MODEL), F32),
            jax.ShapeDtypeStruct((b, MLA_HEADS, l, MLA_QK), BF16),
            jax.ShapeDtypeStruct((b, MLA_HEADS, nt, MLA_QK, TM), BF16),
            jax.ShapeDtypeStruct((b, MLA_HEADS, nt, MLA_V, TM), BF16),
            jax.ShapeDtypeStruct((b, ODD_MIX, l), BF16),
        ],
        compiler_params=_params(("arbitrary", "arbitrary")),
        name="odd_front",
    )(x, oa, ob, u, gate_mod, w_out, g, scale, shift, w1t, wnat, gq, gkvt, gkv, wqbt, wk, wvt,
      cos_t, sin_t, cos_p, sin_p)


def _mla_kernel(qt_ref, k_ref, vt_ref, o_ref, s_scr, p_scr, acc_scr, *, hb, nqb, nchunks):
    qsub = MLA_TQ // TM
    ksub = MLA_TK // TM
    pieces = MLA_TK // MLA_PIECE
    chunk_bits = nchunks.bit_length() - 1
    qb_bits = nqb.bit_length() - 1
    total = hb * nqb * nchunks

    def decode(i):
        return i >> (chunk_bits + qb_bits), (i >> chunk_bits) & (nqb - 1), i & (nchunks - 1)

    def load_q(h, qb):
        return jnp.concatenate([qt_ref[0, h, qsub * qb + j] for j in range(qsub)], axis=1)

    def score_piece(h, koff, r, q):
        s = _dot(k_ref[0, h, pl.ds(koff + r * MLA_PIECE, MLA_PIECE), :], q)
        s_scr[r * MLA_PIECE:(r + 1) * MLA_PIECE] = s
        return jnp.max(s.reshape(MLA_PIECE // 8, 8, MLA_TQ), axis=0)

    def value_piece(h, t, r, slot):
        key = r * MLA_PIECE
        vt = vt_ref[0, h, ksub * t + key // TM][:, key % TM:key % TM + MLA_PIECE]
        v1 = jnp.concatenate([vt, jnp.ones((ONES_ROWS, MLA_PIECE), BF16)], axis=0)
        return _dot(v1, p_scr[slot, key:key + MLA_PIECE])

    def write_output(h, qb):
        acc = acc_scr[...]
        o_ref[0, h, qb] = (acc[0:MLA_V] * (1.0 / acc[MLA_V:MLA_V + 1])).astype(BF16)

    def scores(i):
        h, qb, t = decode(i)
        q = load_q(h, qb)
        koff = pl.multiple_of(t * MLA_TK, MLA_TK)
        mt8 = score_piece(h, koff, 0, q)
        for r in range(1, pieces):
            mt8 = jnp.maximum(mt8, score_piece(h, koff, r, q))
        return jnp.max(mt8, axis=0, keepdims=True)

    def probs(i, slot, m, mt):
        _, _, t = decode(i)
        m = jnp.where(t == 0, NEG, m)
        m_new = jnp.maximum(m, mt)
        p_scr[slot] = jnp.exp2(s_scr[...] - m_new).astype(BF16)
        return m_new, jnp.exp2(m - m_new)

    def accumulate(i, slot, alpha):
        h, qb, t = decode(i)
        acc = acc_scr[...] * alpha
        for r in range(pieces):
            acc = acc + value_piece(h, t, r, slot)
        acc_scr[...] = acc
        write_output(h, qb)

    def scores_accumulate(i_s, i_c, slot, alpha):
        h_s, qb_s, t_s = decode(i_s)
        h_c, qb_c, t_c = decode(i_c)
        q = load_q(h_s, qb_s)
        koff = pl.multiple_of(t_s * MLA_TK, MLA_TK)
        acc = acc_scr[...] * alpha
        mt8 = None
        for r in range(pieces):
            s8 = score_piece(h_s, koff, r, q)
            mt8 = s8 if mt8 is None else jnp.maximum(mt8, s8)
            acc = acc + value_piece(h_c, t_c, r, slot)
        acc_scr[...] = acc
        write_output(h_c, qb_c)
        return jnp.max(mt8, axis=0, keepdims=True)

    acc_scr[...] = jnp.zeros_like(acc_scr)
    m = jnp.full((1, MLA_TQ), NEG, F32)
    mt = scores(0)
    m, al_even = probs(0, 0, m, mt)
    mt = scores(1)

    def body(j, carry):
        m, mt, al_even = carry
        i = 2 * j
        m, al_odd = probs(i - 1, 1, m, mt)
        mt = scores_accumulate(i, i - 2, 0, al_even)
        m, al_even = probs(i, 0, m, mt)
        mt = scores_accumulate(i + 1, i - 1, 1, al_odd)
        return m, mt, al_even

    m, mt, al_even = lax.fori_loop(1, total // 2, body, (m, mt, al_even))
    m, al_odd = probs(total - 1, 1, m, mt)
    accumulate(total - 2, 0, al_even)
    accumulate(total - 1, 1, al_odd)


def _mla_attention(qt, k, vt):
    b, _, nt, _, _ = qt.shape
    l = nt * TM
    nchunks = l // MLA_TK
    nqb = l // MLA_TQ
    assert MLA_TK % TM == 0 and MLA_TQ % TM == 0
    assert nqb & (nqb - 1) == 0 and nchunks & (nchunks - 1) == 0
    hb = max(1, min(MLA_HEADS, MLA_RESIDENT_TOKENS // l))
    assert hb * nqb * nchunks >= 2 and (hb * nqb * nchunks) % 2 == 0
    return pl.pallas_call(
        functools.partial(_mla_kernel, hb=hb, nqb=nqb, nchunks=nchunks),
        grid=(b, MLA_HEADS // hb),
        in_specs=[
            pl.BlockSpec((1, hb, nt, MLA_QK, TM), lambda bi, h: (bi, h, 0, 0, 0)),
            pl.BlockSpec((1, hb, l, MLA_QK), lambda bi, h: (bi, h, 0, 0)),
            pl.BlockSpec((1, hb, nt, MLA_V, TM), lambda bi, h: (bi, h, 0, 0, 0)),
        ],
        out_specs=pl.BlockSpec((1, hb, nqb, MLA_V, MLA_TQ), lambda bi, h: (bi, h, 0, 0, 0)),
        out_shape=jax.ShapeDtypeStruct((b, MLA_HEADS, nqb, MLA_V, MLA_TQ), BF16),
        scratch_shapes=[pltpu.VMEM((MLA_TK, MLA_TQ), F32),
                        pltpu.VMEM((2, MLA_TK, MLA_TQ), BF16),
                        pltpu.VMEM((MLA_V + ONES_ROWS, MLA_TQ), F32)],
        compiler_params=_params(("arbitrary", "arbitrary")),
        name="mla_attention",
    )(qt, k, vt)


def _odd_out_kernel(x_ref, ot_ref, sgt_ref, gm_ref, wt_ref, fg_ref, y_ref):
    ot = ot_ref[0, :, 0].reshape(ODD_MIX, TM)
    z = (ot.astype(F32) * sgt_ref[0].astype(F32)).astype(BF16)
    out = lax.dot_general(z, wt_ref[...], (((0,), (1,)), ((), ())), preferred_element_type=F32)
    x2 = x_ref[0] + gm_ref[0] * out
    ms = jnp.mean(x2 * x2, axis=-1, keepdims=True)
    y_ref[0] = (x2 * lax.rsqrt(ms + EPS)) * fg_ref[...]


def _odd_out(x, ot, sgt, gate_mod, w_out_t, final_g):
    b, l, _ = x.shape
    sub = MLA_TQ // TM
    return pl.pallas_call(
        _odd_out_kernel,
        grid=(b, l // TM),
        in_specs=[
            pl.BlockSpec((1, TM, D_MODEL), lambda bi, i: (bi, i, 0)),
            pl.BlockSpec((1, MLA_HEADS, 1, MLA_V, TM), lambda bi, i: (bi, 0, i // sub, 0, i % sub)),
            pl.BlockSpec((1, ODD_MIX, TM), lambda bi, i: (bi, 0, i)),
            pl.BlockSpec((1, 1, D_MODEL), lambda bi, i: (bi, 0, 0)),
            pl.BlockSpec((D_MODEL, ODD_MIX), lambda bi, i: (0, 0)),
            pl.BlockSpec((1, D_MODEL), lambda bi, i: (0, 0)),
        ],
        out_specs=pl.BlockSpec((1, TM, D_MODEL), lambda bi, i: (bi, i, 0)),
        out_shape=jax.ShapeDtypeStruct((b, l, D_MODEL), F32),
        compiler_params=_params(("arbitrary", "arbitrary")),
        name="odd_out",
    )(x, ot, sgt, gate_mod, w_out_t, final_g)


def _t5_bucket(rel):
    nb = T5_BUCKETS // 2
    ret = (rel > 0).astype(np.int32) * nb
    n = np.abs(rel)
    max_exact = nb // 2
    large = max_exact + (np.log(np.maximum(n, 1) / max_exact)
                         / np.log(T5_MAX_DIST / max_exact) * (nb - max_exact)).astype(np.int32)
    large = np.minimum(large, nb - 1)
    return ret + np.where(n < max_exact, n, large)


def _na_bias_table(rpb):
    kh = NA_WIN_H
    c = np.arange(GRID_W)
    col_start = np.clip(c - NA_WIN_W // 2, 0, GRID_W - NA_WIN_W)
    col_ok = (c[None, :] >= col_start[:, None]) & (c[None, :] < col_start[:, None] + NA_WIN_W)
    d_col = np.clip(c[None, :] - c[:, None], -(NA_WIN_W - 1), NA_WIN_W - 1) + NA_WIN_W - 1
    nrel = 2 * NA_WIN_W - 1
    rows = jnp.stack([rpb[:, NA_WIN_H - 1 - v:2 * NA_WIN_H - 1 - v] for v in range(kh)], axis=1)
    onehot = (d_col.reshape(-1)[None, :] == np.arange(nrel)[:, None]).astype(np.float32)
    bias = jnp.dot(rows.astype(F32).reshape(-1, nrel), jnp.asarray(onehot),
                   precision=lax.Precision.HIGHEST)
    bias = bias.reshape(NA_HEADS, kh, kh, GRID_W, GRID_W)
    bias = jnp.where(jnp.asarray(col_ok)[None, None, None, :, :], bias, NEG)
    bias = bias.transpose(1, 0, 3, 2, 4).reshape(kh, NA_HEADS // 2, 2 * GRID_W, kh * GRID_W)
    return bias


def _wb_bias_table(t5_bias):
    rel = (np.arange(3 * WB_BLOCK) - WB_BLOCK)[None, :] - np.arange(WB_BLOCK)[:, None]
    offs = np.arange(-(2 * WB_BLOCK - 1), 2 * WB_BLOCK + 1)
    period = 4 * WB_BLOCK
    by_off = t5_bias[_t5_bucket(offs)].astype(F32).T
    shifted = jnp.tile(by_off, (1, WB_BLOCK))[:, :WB_BLOCK * (period - 1)]
    shifted = shifted.reshape(WB_HEADS, WB_BLOCK, period - 1)
    bias = shifted[:, :, WB_BLOCK - 1:4 * WB_BLOCK - 1]
    bias = jnp.where(jnp.asarray(np.abs(rel) <= WB_WINDOW)[None], bias, NEG)
    bias = bias[np.asarray(WB_HEAD_ORDER)]
    return bias.reshape(WB_HEADS // 2, 2 * WB_BLOCK, 3 * WB_BLOCK)


def _even_weights(w_in, w_out):
    qa, ka, va, qb, kb, vb, gate = jnp.split(
        w_in, [NA_W, 2 * NA_W, 3 * NA_W, 3 * NA_W + WB_QW, 3 * NA_W + WB_QW + WB_KVW,
               3 * NA_W + WB_QW + 2 * WB_KVW], axis=-1)
    order = np.asarray(WB_HEAD_ORDER)
    perm = (order[:, None] * HEAD_DIM + np.arange(HEAD_DIM)[None, :]).reshape(-1)
    qscale = HEAD_DIM ** -0.5
    gate = jnp.concatenate([gate[:, :NA_W], gate[:, NA_W:][:, perm]], axis=-1)
    w = jnp.concatenate([gate, qa * qscale, ka, va, qb[:, perm] * qscale, kb, vb], axis=-1)
    w_out_p = jnp.concatenate([w_out[:NA_W], w_out[NA_W:][perm]], axis=0)
    return w.astype(BF16), w_out_p.astype(BF16)


def _odd_weights(w_in, q_norm, w_qb, kv_norm, w_kvb, w_out):
    q_lat, kv_lat, k_rope, gate = jnp.split(
        w_in, [MLA_Q_RANK, MLA_Q_RANK + MLA_KV_RANK, MLA_Q_RANK + MLA_KV_RANK + MLA_ROPE], axis=-1)
    w1t = jnp.concatenate([gate, q_lat, kv_lat], axis=-1).T.astype(BF16)
    half = MLA_ROPE // 2
    k_rot = jnp.concatenate([-k_rope[:, half:], k_rope[:, :half]], axis=-1)
    z64 = jnp.zeros((D_MODEL, MLA_NOPE), w_in.dtype)
    z32 = jnp.zeros((D_MODEL, LANES - MLA_QK), w_in.dtype)
    wnat = jnp.concatenate([kv_lat, z64, k_rope, z32, z64, k_rot, z32], axis=-1).astype(BF16)
    wkv = w_kvb.reshape(MLA_KV_RANK, MLA_HEADS, MLA_NOPE + MLA_V)
    wk = jnp.concatenate([wkv[:, :, :MLA_NOPE], jnp.zeros_like(wkv[:, :, :MLA_NOPE])], axis=-1)
    wk = wk.reshape(MLA_KV_RANK, MLA_HEADS * LANES).astype(BF16)
    wvt = wkv[:, :, MLA_NOPE:].reshape(MLA_KV_RANK, ODD_MIX).T.astype(BF16)
    gq = jnp.broadcast_to(q_norm.astype(F32)[:, None], (MLA_Q_RANK, LANES))
    gkvt = jnp.broadcast_to(kv_norm.astype(F32)[:, None], (MLA_KV_RANK, LANES))
    gkv = kv_norm.astype(F32)[None, :]
    return w1t, wnat, gq, gkvt, gkv, w_qb.T.astype(BF16), wk, wvt, w_out.T.astype(BF16)


def _rope_tables(l):
    inv_freq = 1.0 / (ROPE_THETA ** (jnp.arange(0, MLA_ROPE, 2, dtype=F32) / MLA_ROPE))
    ang = jnp.arange(l, dtype=F32)[:, None] * inv_freq[None, :]
    cos, sin = jnp.cos(ang), jnp.sin(ang)
    z64 = jnp.zeros((l, MLA_NOPE), F32)
    z32 = jnp.zeros((l, LANES - MLA_QK), F32)
    cos_p = jnp.concatenate([z64, cos, cos, z32], axis=-1)
    sin_p = jnp.concatenate([z64, sin, sin, z32], axis=-1)
    return cos.T, sin.T, cos_p, sin_p


def _trunk(x, mod, norm_g, ev, na_tbl, wb_tbl, sink, od, final_g):
    b, l, _ = x.shape
    assert l % TM == 0 and l % NA_CHUNK == 0 and l % WB_CHUNK == 0 and l % MLA_TK == 0
    assert l // GRID_W >= NA_ROWS_PER_STEP >= NA_HALO_ROWS >= NA_WIN_H
    shift0, scale0, gate0 = [t[:, None, :] for t in jnp.split(mod[0], 3, axis=-1)]
    shift1, scale1, gate1 = [t[:, None, :] for t in jnp.split(mod[1], 3, axis=-1)]

    w_in0, w_out0 = ev
    u = _even_front(x, norm_g[0][None, :], scale0, shift0, w_in0)
    oa = _na_attention(u, na_tbl)
    ob = _wb_attention(u, wb_tbl, sink)

    w1t, wnat, gq, gkvt, gkv, wqbt, wk, wvt, w_out1t = od
    cos_t, sin_t, cos_p, sin_p = _rope_tables(l)
    x1, k, qt, vt, sgt = _odd_front(x, oa, ob, u, gate0, w_out0, norm_g[1][None, :], scale1, shift1, w1t, wnat, gq, gkvt, gkv,
                                wqbt, wk, wvt, cos_t, sin_t, cos_p, sin_p)
    ot = _mla_attention(qt, k, vt)
    return _odd_out(x1, ot, sgt, gate1, w_out1t, final_g[None, :])


def kernel(x_prompt, x_sample, c_prompt, c_sample, ada_w, ada_b, norm_g, t5_bias, ev_w_in, na_rpb,
           wb_sink, ev_w_out, mla_w_in, mla_q_norm, mla_w_qb, mla_kv_norm, mla_w_kvb, mla_w_out, final_g):
    bp, bs = c_prompt.shape[0], c_sample.shape[0]
    rows = -(-(bp + bs) // 16) * 16
    c_pad = jnp.concatenate([c_prompt, c_sample, jnp.zeros((rows - bp - bs, D_MODEL), F32)], axis=0)
    mod = _ada_mod(c_pad, ada_w.astype(BF16), ada_b[:, None, :])

    ev = _even_weights(ev_w_in[0], ev_w_out[0])
    na_tbl = _na_bias_table(na_rpb[0])
    wb_tbl = _wb_bias_table(t5_bias)
    od = _odd_weights(mla_w_in[0], mla_q_norm[0], mla_w_qb[0], mla_kv_norm[0], mla_w_kvb[0], mla_w_out[0])
    sink = wb_sink[0].astype(F32)

    y_prompt = _trunk(x_prompt, mod[:, :bp], norm_g, ev, na_tbl, wb_tbl, sink, od, final_g)
    y_sample = _trunk(x_sample, mod[:, bp:bp + bs], norm_g, ev, na_tbl, wb_tbl, sink, od, final_g)
    return (y_prompt, y_sample)
```

```python
import functools
import math

import numpy as np
import jax
import jax.numpy as jnp
from jax import lax
from jax.experimental import pallas as pl
from jax.experimental.pallas import tpu as pltpu

D_MODEL = 1024
GRID_W = 64
HEAD_DIM = 64
EPS = 1e-6
NEG = -1e30
NA_HEADS = 8
NA_WIN_H = 8
NA_WIN_W = 16
WB_HEADS = 8
WB_KV_HEADS = 2
WB_WINDOW = 128
WB_BLOCK = 128
T5_BUCKETS = 32
T5_MAX_DIST = 128
MLA_HEADS = 16
MLA_Q_RANK = 256
MLA_KV_RANK = 128
MLA_NOPE = 64
MLA_ROPE = 32
MLA_V = 64
ROPE_THETA = 10000.0
MLA_QK = MLA_NOPE + MLA_ROPE

NA_W = NA_HEADS * HEAD_DIM
WB_QW = WB_HEADS * HEAD_DIM
WB_KVW = WB_KV_HEADS * HEAD_DIM
EVEN_MIX = NA_W + WB_QW
EVEN_IN = 3 * NA_W + WB_QW + 2 * WB_KVW + EVEN_MIX
ODD_MIX = MLA_HEADS * MLA_V

U_GATE = 0
U_QA = EVEN_MIX
U_KA = U_QA + NA_W
U_VA = U_KA + NA_W
U_QB = U_VA + NA_W
U_KB = U_QB + WB_QW
U_VB = U_KB + WB_KVW

LANES = 128
TM = 512
NA_ROWS_PER_STEP = 32
NA_HALO_ROWS = 8
NA_UNROLL = 6
WB_UNROLL = 7
NA_CHUNK = NA_ROWS_PER_STEP * GRID_W
WB_CHUNK = 2048
MLA_TQ = 1024
MLA_TK = 2048
MLA_RESIDENT_TOKENS = 16384
ONES_ROWS = 16
VMEM_LIMIT = 56 * 1024 * 1024

BF16 = jnp.bfloat16
F32 = jnp.float32
LOG2E = math.log2(math.e)

WB_HEAD_ORDER = (0, 4, 1, 5, 2, 6, 3, 7)


def _params(sem):
    return pltpu.CompilerParams(dimension_semantics=sem, vmem_limit_bytes=VMEM_LIMIT)


def _dot(a, b):
    return jnp.dot(a, b, preferred_element_type=F32)


def _dot_nt(a, b):
    return lax.dot_general(a, b, (((1,), (1,)), ((), ())), preferred_element_type=F32)


def _silu(x):
    return x * (1.0 / (1.0 + jnp.exp(-x)))


def _modulated_norm(x, g, scale, shift):
    ms = jnp.mean(x * x, axis=-1, keepdims=True)
    y = x * lax.rsqrt(ms + EPS)
    return (y * g) * (1.0 + scale) + shift


def _ada_kernel(c_ref, w_ref, b_ref, o_ref):
    c = c_ref[...]
    cs = _silu(c).astype(BF16)
    o_ref[0] = _dot(cs, w_ref[0]) + b_ref[0]


def _ada_mod(c_pad, ada_w, ada_b):
    depth = ada_w.shape[0]
    rows = c_pad.shape[0]
    tn = 768
    return pl.pallas_call(
        _ada_kernel,
        grid=(depth, 3 * D_MODEL // tn),
        in_specs=[
            pl.BlockSpec((rows, D_MODEL), lambda i, n: (0, 0)),
            pl.BlockSpec((1, D_MODEL, tn), lambda i, n: (i, 0, n)),
            pl.BlockSpec((1, 1, tn), lambda i, n: (i, 0, n)),
        ],
        out_specs=pl.BlockSpec((1, rows, tn), lambda i, n: (i, 0, n)),
        out_shape=jax.ShapeDtypeStruct((depth, rows, 3 * D_MODEL), F32),
        compiler_params=_params(("arbitrary", "arbitrary")),
        name="ada_mod",
    )(c_pad, ada_w, ada_b)


def _even_front_kernel(x_ref, g_ref, sc_ref, sh_ref, w_ref, u_ref):
    h = _modulated_norm(x_ref[0], g_ref[...], sc_ref[0], sh_ref[0]).astype(BF16)
    u_ref[0] = _dot(h, w_ref[...]).astype(BF16)


def _even_front(x, g, scale, shift, w):
    b, l, _ = x.shape
    return pl.pallas_call(
        _even_front_kernel,
        grid=(b, l // TM),
        in_specs=[
            pl.BlockSpec((1, TM, D_MODEL), lambda bi, i: (bi, i, 0)),
            pl.BlockSpec((1, D_MODEL), lambda bi, i: (0, 0)),
            pl.BlockSpec((1, 1, D_MODEL), lambda bi, i: (bi, 0, 0)),
            pl.BlockSpec((1, 1, D_MODEL), lambda bi, i: (bi, 0, 0)),
            pl.BlockSpec((D_MODEL, EVEN_IN), lambda bi, i: (0, 0)),
        ],
        out_specs=pl.BlockSpec((1, TM, EVEN_IN), lambda bi, i: (bi, i, 0)),
        out_shape=jax.ShapeDtypeStruct((b, l, EVEN_IN), BF16),
        compiler_params=_params(("arbitrary", "arbitrary")),
        name="even_front",
    )(x, g, scale, shift, w)


def _low_lane_mask(shape):
    return lax.broadcasted_iota(jnp.int32, shape, len(shape) - 1) < HEAD_DIM


def _na_kernel(q_ref, kp_ref, kc_ref, kn_ref, vp_ref, vc_ref, vn_ref, bias_ref, o_ref,
               kwin, vwin, s_scr, p_scr, *, rows):
    ci = pl.program_id(1)
    halo = NA_HALO_ROWS * GRID_W
    npair = NA_HEADS // 2
    kwin[0:halo] = kp_ref[0]
    kwin[halo:halo + NA_CHUNK] = kc_ref[0]
    kwin[halo + NA_CHUNK:2 * halo + NA_CHUNK] = kn_ref[0]
    for j in range(npair):
        lanes = slice(j * LANES, (j + 1) * LANES)
        dst = slice(2 * j * LANES, (2 * j + 1) * LANES)
        vwin[0:halo, dst] = vp_ref[0, :, lanes]
        vwin[halo:halo + NA_CHUNK, dst] = vc_ref[0, :, lanes]
        vwin[halo + NA_CHUNK:2 * halo + NA_CHUNK, dst] = vn_ref[0, :, lanes]
        vwin[:, (2 * j + 1) * LANES:(2 * j + 2) * LANES] = jnp.ones((2 * halo + NA_CHUNK, LANES), BF16)

    low = _low_lane_mask((GRID_W, LANES))
    kh = NA_WIN_H
    r0 = ci * NA_ROWS_PER_STEP

    def offsets(i):
        r = r0 + i
        row_start = jnp.clip(r - kh // 2, 0, rows - kh)
        variant = r - row_start
        start = pl.multiple_of((row_start - r0 + NA_HALO_ROWS) * GRID_W, GRID_W)
        return variant, start, pl.multiple_of(i * GRID_W, GRID_W)

    def scores(i):
        variant, start, qoff = offsets(i)
        for j in range(npair):
            lanes = slice(j * LANES, (j + 1) * LANES)
            q2 = q_ref[0, pl.ds(qoff, GRID_W), lanes]
            zero = jnp.zeros_like(q2)
            lhs = jnp.concatenate([jnp.where(low, q2, zero), jnp.where(low, zero, q2)], axis=0)
            kw = kwin[pl.ds(start, kh * GRID_W), lanes]
            s_scr[j] = _dot_nt(lhs, kw) + bias_ref[variant, j]

    def probs():
        for j in range(npair):
            s = s_scr[j]
            p_scr[j] = jnp.exp(s - jnp.max(s, axis=-1, keepdims=True)).astype(BF16)

    def output(i):
        _, start, qoff = offsets(i)
        for j in range(npair):
            vw = vwin[pl.ds(start, kh * GRID_W), 2 * j * LANES:(2 * j + 2) * LANES]
            res = _dot(p_scr[j], vw)
            o = res[:, 0:LANES] / res[:, LANES:2 * LANES]
            o_ref[0, pl.ds(qoff, GRID_W), j * LANES:(j + 1) * LANES] = (
                jnp.where(low, o[:GRID_W], o[GRID_W:]).astype(BF16))

    scores(0)
    probs()
    scores(1)

    def body(i, carry):
        output(i - 2)
        probs()
        scores(i)
        return carry

    lax.fori_loop(2, NA_ROWS_PER_STEP, body, 0, unroll=NA_UNROLL)
    output(NA_ROWS_PER_STEP - 2)
    probs()
    output(NA_ROWS_PER_STEP - 1)


def _na_attention(u, bias_tbl):
    b, l, _ = u.shape
    rows = l // GRID_W
    nchunk = l // NA_CHUNK
    halo = NA_HALO_ROWS * GRID_W
    per_chunk = NA_CHUNK // halo
    nhalo = l // halo
    qblk, kblk, vblk = U_QA // NA_W, U_KA // NA_W, U_VA // NA_W

    def window(col):
        return [
            pl.BlockSpec((1, halo, NA_W), lambda bi, i: (bi, jnp.maximum(i * per_chunk - 1, 0), col)),
            pl.BlockSpec((1, NA_CHUNK, NA_W), lambda bi, i: (bi, i, col)),
            pl.BlockSpec((1, halo, NA_W), lambda bi, i: (bi, jnp.minimum((i + 1) * per_chunk, nhalo - 1), col)),
        ]

    npair = NA_HEADS // 2
    return pl.pallas_call(
        functools.partial(_na_kernel, rows=rows),
        grid=(b, nchunk),
        in_specs=[pl.BlockSpec((1, NA_CHUNK, NA_W), lambda bi, i: (bi, i, qblk))]
        + window(kblk) + window(vblk)
        + [pl.BlockSpec(bias_tbl.shape, lambda bi, i: (0, 0, 0, 0))],
        out_specs=pl.BlockSpec((1, NA_CHUNK, NA_W), lambda bi, i: (bi, i, 0)),
        out_shape=jax.ShapeDtypeStruct((b, l, NA_W), BF16),
        scratch_shapes=[pltpu.VMEM((NA_CHUNK + 2 * halo, NA_W), BF16),
                        pltpu.VMEM((NA_CHUNK + 2 * halo, 2 * NA_W), BF16),
                        pltpu.VMEM((npair, 2 * GRID_W, NA_WIN_H * GRID_W), F32),
                        pltpu.VMEM((npair, 2 * GRID_W, NA_WIN_H * GRID_W), BF16)],
        compiler_params=_params(("arbitrary", "arbitrary")),
        name="na_attention",
    )(u, u, u, u, u, u, u, bias_tbl)


def _wb_kernel(sink_ref, q_ref, kp_ref, kc_ref, kn_ref, vp_ref, vc_ref, vn_ref, bias_ref, o_ref,
               kwin, vwin, s_scr, p_scr, ps_scr, *, nblk):
    ci = pl.program_id(1)
    npair = WB_HEADS // 2
    per_step = WB_CHUNK // WB_BLOCK
    win = WB_CHUNK + 2 * WB_BLOCK
    kwin[0:WB_BLOCK] = kp_ref[0]
    kwin[WB_BLOCK:WB_BLOCK + WB_CHUNK] = kc_ref[0]
    kwin[WB_BLOCK + WB_CHUNK:win] = kn_ref[0]
    vwin[0:WB_BLOCK, 0:LANES] = vp_ref[0]
    vwin[WB_BLOCK:WB_BLOCK + WB_CHUNK, 0:LANES] = vc_ref[0]
    vwin[WB_BLOCK + WB_CHUNK:win, 0:LANES] = vn_ref[0]
    vwin[:, LANES:2 * LANES] = jnp.ones((win, LANES), BF16)

    low = _low_lane_mask((WB_BLOCK, LANES))
    col = lax.broadcasted_iota(jnp.int32, (2 * WB_BLOCK, 3 * WB_BLOCK), 1)
    row = lax.broadcasted_iota(jnp.int32, (2 * WB_BLOCK, 1), 0)

    def scores(n):
        gblk = ci * per_step + n
        lo = jnp.where(gblk > 0, 0, WB_BLOCK)
        hi = jnp.where(gblk < nblk - 1, 3 * WB_BLOCK, 2 * WB_BLOCK)
        in_seq = jnp.logical_and(col >= lo, col < hi)
        off = pl.multiple_of(n * WB_BLOCK, WB_BLOCK)
        kw = kwin[pl.ds(off, 3 * WB_BLOCK), :]
        for j in range(npair):
            q2 = q_ref[0, pl.ds(off, WB_BLOCK), j * LANES:(j + 1) * LANES]
            zero = jnp.zeros_like(q2)
            lhs = jnp.concatenate([jnp.where(low, q2, zero), jnp.where(low, zero, q2)], axis=0)
            s_scr[j] = jnp.where(in_seq, _dot_nt(lhs, kw) + bias_ref[j], NEG)

    def probs():
        for j in range(npair):
            s = s_scr[j]
            sink = jnp.where(row < WB_BLOCK, sink_ref[WB_HEAD_ORDER[2 * j]],
                             sink_ref[WB_HEAD_ORDER[2 * j + 1]])
            m = jnp.maximum(jnp.max(s, axis=-1, keepdims=True), sink)
            p_scr[j] = jnp.exp(s - m).astype(BF16)
            ps_scr[j] = jnp.broadcast_to(jnp.exp(sink - m), (2 * WB_BLOCK, LANES))

    def output(n):
        off = pl.multiple_of(n * WB_BLOCK, WB_BLOCK)
        vw = vwin[pl.ds(off, 3 * WB_BLOCK), :]
        for j in range(npair):
            res = _dot(p_scr[j], vw)
            o = res[:, 0:LANES] / (res[:, LANES:2 * LANES] + ps_scr[j])
            o_ref[0, pl.ds(off, WB_BLOCK), j * LANES:(j + 1) * LANES] = (
                jnp.where(low, o[:WB_BLOCK], o[WB_BLOCK:]).astype(BF16))

    scores(0)
    probs()
    scores(1)

    def body(n, carry):
        output(n - 2)
        probs()
        scores(n)
        return carry

    lax.fori_loop(2, per_step, body, 0, unroll=WB_UNROLL)
    output(per_step - 2)
    probs()
    output(per_step - 1)


def _wb_attention(u, bias_tbl, sink):
    b, l, _ = u.shape
    nblk = l // WB_BLOCK
    nchunk = l // WB_CHUNK
    per_step = WB_CHUNK // WB_BLOCK
    qblk = U_QB // WB_QW
    kcol, vcol = U_KB // WB_KVW, U_VB // WB_KVW

    def halo(colblk, d):
        if d == 0:
            return pl.BlockSpec((1, WB_CHUNK, WB_KVW), lambda bi, i: (bi, i, colblk))
        return pl.BlockSpec(
            (1, WB_BLOCK, WB_KVW),
            lambda bi, i: (bi, jnp.clip(i * per_step + (per_step if d > 0 else -1), 0, nblk - 1), colblk))

    return pl.pallas_call(
        functools.partial(_wb_kernel, nblk=nblk),
        grid=(b, nchunk),
        in_specs=[
            pl.BlockSpec(memory_space=pltpu.SMEM),
            pl.BlockSpec((1, WB_CHUNK, WB_QW), lambda bi, i: (bi, i, qblk)),
            halo(kcol, -1), halo(kcol, 0), halo(kcol, 1),
            halo(vcol, -1), halo(vcol, 0), halo(vcol, 1),
            pl.BlockSpec(bias_tbl.shape, lambda bi, i: (0, 0, 0)),
        ],
        out_specs=pl.BlockSpec((1, WB_CHUNK, WB_QW), lambda bi, i: (bi, i, 0)),
        out_shape=jax.ShapeDtypeStruct((b, l, WB_QW), BF16),
        scratch_shapes=[pltpu.VMEM((WB_CHUNK + 2 * WB_BLOCK, WB_KVW), BF16),
                        pltpu.VMEM((WB_CHUNK + 2 * WB_BLOCK, 2 * WB_KVW), BF16),
                        pltpu.VMEM((WB_HEADS // 2, 2 * WB_BLOCK, 3 * WB_BLOCK), F32),
                        pltpu.VMEM((WB_HEADS // 2, 2 * WB_BLOCK, 3 * WB_BLOCK), BF16),
                        pltpu.VMEM((WB_HEADS // 2, 2 * WB_BLOCK, LANES), F32)],
        compiler_params=_params(("arbitrary", "arbitrary")),
        name="wb_attention",
    )(sink, u, u, u, u, u, u, u, bias_tbl)


def _even_out_tile(x_ref, oa_ref, ob_ref, gate_ref, gm_ref, w_ref):
    g = _silu(gate_ref[0].astype(F32))
    o = jnp.concatenate([oa_ref[0], ob_ref[0]], axis=-1).astype(F32)
    z = (o * g).astype(BF16)
    return x_ref[0] + gm_ref[0] * _dot(z, w_ref[...])


def _odd_front_kernel(x_ref, oa_ref, ob_ref, gate_ref, gm_ref, wo_ref,
                      g_ref, sc_ref, sh_ref, w1t_ref, wnat_ref, gq_ref, gkvt_ref, gkv_ref,
                      wqbt_ref, wk_ref, wvt_ref, cost_ref, sint_ref, cosp_ref, sinp_ref,
                      x1_ref, k_ref, qt_ref, vt_ref, sgt_ref):
    x1 = _even_out_tile(x_ref, oa_ref, ob_ref, gate_ref, gm_ref, wo_ref)
    x1_ref[0] = x1
    h = _modulated_norm(x1, g_ref[...], sc_ref[0], sh_ref[0]).astype(BF16)
    reps = TM // LANES

    ut = _dot_nt(w1t_ref[...], h)
    sgt_ref[0] = _silu(ut[0:ODD_MIX]).astype(BF16)

    qlt = ut[ODD_MIX:ODD_MIX + MLA_Q_RANK]
    qn = qlt * lax.rsqrt(jnp.mean(qlt * qlt, axis=0, keepdims=True) + EPS)
    qn = (qn * jnp.concatenate([gq_ref[...]] * reps, axis=1)).astype(BF16)
    qt = _dot(wqbt_ref[...], qn)
    qscale = (MLA_QK ** -0.5) * LOG2E
    cos_t = cost_ref[...]
    sin_t = sint_ref[...]
    half = MLA_ROPE // 2
    for hd in range(MLA_HEADS):
        base = hd * MLA_QK
        qt_ref[0, hd, 0, 0:MLA_NOPE, :] = (qt[base:base + MLA_NOPE] * qscale).astype(BF16)
        x1 = qt[base + MLA_NOPE:base + MLA_NOPE + half]
        x2 = qt[base + MLA_NOPE + half:base + MLA_QK]
        qt_ref[0, hd, 0, MLA_NOPE:MLA_NOPE + half, :] = ((x1 * cos_t - x2 * sin_t) * qscale).astype(BF16)
        qt_ref[0, hd, 0, MLA_NOPE + half:MLA_QK, :] = ((x2 * cos_t + x1 * sin_t) * qscale).astype(BF16)

    kvt = ut[ODD_MIX + MLA_Q_RANK:ODD_MIX + MLA_Q_RANK + MLA_KV_RANK]
    kvnt = kvt * lax.rsqrt(jnp.mean(kvt * kvt, axis=0, keepdims=True) + EPS)
    kvnt = (kvnt * jnp.concatenate([gkvt_ref[...]] * reps, axis=1)).astype(BF16)
    vt = _dot(wvt_ref[...], kvnt)
    vt_ref[0, :, 0] = vt.reshape(MLA_HEADS, MLA_V, TM).astype(BF16)

    nat = _dot(h, wnat_ref[...])
    kvl = nat[:, 0:MLA_KV_RANK]
    kvn = kvl * lax.rsqrt(jnp.mean(kvl * kvl, axis=-1, keepdims=True) + EPS)
    kvn = (kvn * gkv_ref[...]).astype(BF16)
    kn = _dot(kvn, wk_ref[...])
    kpe = nat[:, LANES:2 * LANES] * cosp_ref[...] + nat[:, 2 * LANES:3 * LANES] * sinp_ref[...]
    for hd in range(MLA_HEADS):
        kh = kn[:, hd * LANES:(hd + 1) * LANES] + kpe
        k_ref[0, hd] = kh[:, 0:MLA_QK].astype(BF16)


def _odd_front(x, oa, ob, u, gate_mod, w_out, g, scale, shift, w1t, wnat, gq, gkvt, gkv, wqbt, wk, wvt,
               cos_t, sin_t, cos_p, sin_p):
    b, l, _ = x.shape
    nt = l // TM
    const2 = lambda bi, i: (0, 0)
    n1 = w1t.shape[0]
    return pl.pallas_call(
        _odd_front_kernel,
        grid=(b, nt),
        in_specs=[
            pl.BlockSpec((1, TM, D_MODEL), lambda bi, i: (bi, i, 0)),
            pl.BlockSpec((1, TM, NA_W), lambda bi, i: (bi, i, 0)),
            pl.BlockSpec((1, TM, WB_QW), lambda bi, i: (bi, i, 0)),
            pl.BlockSpec((1, TM, EVEN_MIX), lambda bi, i: (bi, i, U_GATE // EVEN_MIX)),
            pl.BlockSpec((1, 1, D_MODEL), lambda bi, i: (bi, 0, 0)),
            pl.BlockSpec((EVEN_MIX, D_MODEL), const2),
            pl.BlockSpec((1, D_MODEL), const2),
            pl.BlockSpec((1, 1, D_MODEL), lambda bi, i: (bi, 0, 0)),
            pl.BlockSpec((1, 1, D_MODEL), lambda bi, i: (bi, 0, 0)),
            pl.BlockSpec((n1, D_MODEL), const2),
            pl.BlockSpec((D_MODEL, 3 * LANES), const2),
            pl.BlockSpec((MLA_Q_RANK, LANES), const2),
            pl.BlockSpec((MLA_KV_RANK, LANES), const2),
            pl.BlockSpec((1, MLA_KV_RANK), const2),
            pl.BlockSpec((MLA_HEADS * MLA_QK, MLA_Q_RANK), const2),
            pl.BlockSpec((MLA_KV_RANK, MLA_HEADS * LANES), const2),
            pl.BlockSpec((ODD_MIX, MLA_KV_RANK), const2),
            pl.BlockSpec((MLA_ROPE // 2, TM), lambda bi, i: (0, i)),
            pl.BlockSpec((MLA_ROPE // 2, TM), lambda bi, i: (0, i)),
            pl.BlockSpec((TM, LANES), lambda bi, i: (i, 0)),
            pl.BlockSpec((TM, LANES), lambda bi, i: (i, 0)),
        ],
        out_specs=[
            pl.BlockSpec((1, TM, D_MODEL), lambda bi, i: (bi, i, 0)),
            pl.BlockSpec((1, MLA_HEADS, TM, MLA_QK), lambda bi, i: (bi, 0, i, 0)),
            pl.BlockSpec((1, MLA_HEADS, 1, MLA_QK, TM), lambda bi, i: (bi, 0, i, 0, 0)),
            pl.BlockSpec((1, MLA_HEADS, 1, MLA_V, TM), lambda bi, i: (bi, 0, i, 0, 0)),
            pl.BlockSpec((1, ODD_MIX, TM), lambda bi, i: (bi, 0, i)),
        ],
        out_shape=[
            jax.ShapeDtypeStruct((b, l, D_MODEL), F32),
            jax.ShapeDtypeStruct((b, MLA_HEADS, l, MLA_QK), BF16),
            jax.ShapeDtypeStruct((b, MLA_HEADS, nt, MLA_QK, TM), BF16),
            jax.ShapeDtypeStruct((b, MLA_HEADS, nt, MLA_V, TM), BF16),
            jax.ShapeDtypeStruct((b, ODD_MIX, l), BF16),
        ],
        compiler_params=_params(("arbitrary", "arbitrary")),
        name="odd_front",
    )(x, oa, ob, u, gate_mod, w_out, g, scale, shift, w1t, wnat, gq, gkvt, gkv, wqbt, wk, wvt,
      cos_t, sin_t, cos_p, sin_p)


def _mla_kernel(qt_ref, k_ref, vt_ref, o_ref, s_scr, p_scr, acc_scr, *, hb, nqb, nchunks):
    ones = jnp.ones((ONES_ROWS, MLA_TK), BF16)
    qsub = MLA_TQ // TM
    ksub = MLA_TK // TM
    chunk_bits = nchunks.bit_length() - 1
    qb_bits = nqb.bit_length() - 1
    total = hb * nqb * nchunks

    def decode(i):
        return i >> (chunk_bits + qb_bits), (i >> chunk_bits) & (nqb - 1), i & (nchunks - 1)

    def scores(i):
        h, qb, t = decode(i)
        q = jnp.concatenate([qt_ref[0, h, qsub * qb + j] for j in range(qsub)], axis=1)
        koff = pl.multiple_of(t * MLA_TK, MLA_TK)
        s = _dot(k_ref[0, h, pl.ds(koff, MLA_TK), :], q)
        s_scr[...] = s
        return jnp.max(s, axis=0, keepdims=True)

    def probs(i, m, mt):
        _, _, t = decode(i)
        m = jnp.where(t == 0, NEG, m)
        m_new = jnp.maximum(m, mt)
        p_scr[...] = jnp.exp2(s_scr[...] - m_new).astype(BF16)
        return m_new, jnp.exp2(m - m_new)

    def accumulate(i, alpha):
        h, qb, t = decode(i)
        vt = jnp.concatenate([vt_ref[0, h, ksub * t + j] for j in range(ksub)], axis=1)
        acc_scr[...] = acc_scr[...] * alpha + _dot(jnp.concatenate([vt, ones], axis=0), p_scr[...])
        acc = acc_scr[...]
        o_ref[0, h, qb] = (acc[0:MLA_V] * (1.0 / acc[MLA_V:MLA_V + 1])).astype(BF16)

    acc_scr[...] = jnp.zeros_like(acc_scr)
    m = jnp.full((1, MLA_TQ), NEG, F32)
    mt = scores(0)
    m, alpha = probs(0, m, mt)
    mt = scores(1)

    def body(i, carry):
        m, mt, alpha = carry
        accumulate(i - 2, alpha)
        m, alpha = probs(i - 1, m, mt)
        mt = scores(i)
        return m, mt, alpha

    m, mt, alpha = lax.fori_loop(2, total, body, (m, mt, alpha), unroll=2)
    accumulate(total - 2, alpha)
    m, alpha = probs(total - 1, m, mt)
    accumulate(total - 1, alpha)


def _mla_attention(qt, k, vt):
    b, _, nt, _, _ = qt.shape
    l = nt * TM
    nchunks = l // MLA_TK
    nqb = l // MLA_TQ
    assert MLA_TK % TM == 0 and MLA_TQ % TM == 0
    assert nqb & (nqb - 1) == 0 and nchunks & (nchunks - 1) == 0
    hb = max(1, min(MLA_HEADS, MLA_RESIDENT_TOKENS // l))
    assert hb * nqb * nchunks >= 2 and (hb * nqb * nchunks) % 2 == 0
    return pl.pallas_call(
        functools.partial(_mla_kernel, hb=hb, nqb=nqb, nchunks=nchunks),
        grid=(b, MLA_HEADS // hb),
        in_specs=[
            pl.BlockSpec((1, hb, nt, MLA_QK, TM), lambda bi, h: (bi, h, 0, 0, 0)),
            pl.BlockSpec((1, hb, l, MLA_QK), lambda bi, h: (bi, h, 0, 0)),
            pl.BlockSpec((1, hb, nt, MLA_V, TM), lambda bi, h: (bi, h, 0, 0, 0)),
        ],
        out_specs=pl.BlockSpec((1, hb, nqb, MLA_V, MLA_TQ), lambda bi, h: (bi, h, 0, 0, 0)),
        out_shape=jax.ShapeDtypeStruct((b, MLA_HEADS, nqb, MLA_V, MLA_TQ), BF16),
        scratch_shapes=[pltpu.VMEM((MLA_TK, MLA_TQ), F32),
                        pltpu.VMEM((MLA_TK, MLA_TQ), BF16),
                        pltpu.VMEM((MLA_V + ONES_ROWS, MLA_TQ), F32)],
        compiler_params=_params(("arbitrary", "arbitrary")),
        name="mla_attention",
    )(qt, k, vt)


def _odd_out_kernel(x_ref, ot_ref, sgt_ref, gm_ref, wt_ref, fg_ref, y_ref):
    ot = ot_ref[0, :, 0].reshape(ODD_MIX, TM)
    z = (ot.astype(F32) * sgt_ref[0].astype(F32)).astype(BF16)
    out = lax.dot_general(z, wt_ref[...], (((0,), (1,)), ((), ())), preferred_element_type=F32)
    x2 = x_ref[0] + gm_ref[0] * out
    ms = jnp.mean(x2 * x2, axis=-1, keepdims=True)
    y_ref[0] = (x2 * lax.rsqrt(ms + EPS)) * fg_ref[...]


def _odd_out(x, ot, sgt, gate_mod, w_out_t, final_g):
    b, l, _ = x.shape
    sub = MLA_TQ // TM
    return pl.pallas_call(
        _odd_out_kernel,
        grid=(b, l // TM),
        in_specs=[
            pl.BlockSpec((1, TM, D_MODEL), lambda bi, i: (bi, i, 0)),
            pl.BlockSpec((1, MLA_HEADS, 1, MLA_V, TM), lambda bi, i: (bi, 0, i // sub, 0, i % sub)),
            pl.BlockSpec((1, ODD_MIX, TM), lambda bi, i: (bi, 0, i)),
            pl.BlockSpec((1, 1, D_MODEL), lambda bi, i: (bi, 0, 0)),
            pl.BlockSpec((D_MODEL, ODD_MIX), lambda bi, i: (0, 0)),
            pl.BlockSpec((1, D_MODEL), lambda bi, i: (0, 0)),
        ],
        out_specs=pl.BlockSpec((1, TM, D_MODEL), lambda bi, i: (bi, i, 0)),
        out_shape=jax.ShapeDtypeStruct((b, l, D_MODEL), F32),
        compiler_params=_params(("arbitrary", "arbitrary")),
        name="odd_out",
    )(x, ot, sgt, gate_mod, w_out_t, final_g)


def _t5_bucket(rel):
    nb = T5_BUCKETS // 2
    ret = (rel > 0).astype(np.int32) * nb
    n = np.abs(rel)
    max_exact = nb // 2
    large = max_exact + (np.log(np.maximum(n, 1) / max_exact)
                         / np.log(T5_MAX_DIST / max_exact) * (nb - max_exact)).astype(np.int32)
    large = np.minimum(large, nb - 1)
    return ret + np.where(n < max_exact, n, large)


def _na_bias_table(rpb):
    kh = NA_WIN_H
    c = np.arange(GRID_W)
    col_start = np.clip(c - NA_WIN_W // 2, 0, GRID_W - NA_WIN_W)
    col_ok = (c[None, :] >= col_start[:, None]) & (c[None, :] < col_start[:, None] + NA_WIN_W)
    d_col = np.clip(c[None, :] - c[:, None], -(NA_WIN_W - 1), NA_WIN_W - 1) + NA_WIN_W - 1
    nrel = 2 * NA_WIN_W - 1
    rows = jnp.stack([rpb[:, NA_WIN_H - 1 - v:2 * NA_WIN_H - 1 - v] for v in range(kh)], axis=1)
    onehot = (d_col.reshape(-1)[None, :] == np.arange(nrel)[:, None]).astype(np.float32)
    bias = jnp.dot(rows.astype(F32).reshape(-1, nrel), jnp.asarray(onehot),
                   precision=lax.Precision.HIGHEST)
    bias = bias.reshape(NA_HEADS, kh, kh, GRID_W, GRID_W)
    bias = jnp.where(jnp.asarray(col_ok)[None, None, None, :, :], bias, NEG)
    bias = bias.transpose(1, 0, 3, 2, 4).reshape(kh, NA_HEADS // 2, 2 * GRID_W, kh * GRID_W)
    return bias


def _wb_bias_table(t5_bias):
    rel = (np.arange(3 * WB_BLOCK) - WB_BLOCK)[None, :] - np.arange(WB_BLOCK)[:, None]
    offs = np.arange(-(2 * WB_BLOCK - 1), 2 * WB_BLOCK + 1)
    period = 4 * WB_BLOCK
    by_off = t5_bias[_t5_bucket(offs)].astype(F32).T
    shifted = jnp.tile(by_off, (1, WB_BLOCK))[:, :WB_BLOCK * (period - 1)]
    shifted = shifted.reshape(WB_HEADS, WB_BLOCK, period - 1)
    bias = shifted[:, :, WB_BLOCK - 1:4 * WB_BLOCK - 1]
    bias = jnp.where(jnp.asarray(np.abs(rel) <= WB_WINDOW)[None], bias, NEG)
    bias = bias[np.asarray(WB_HEAD_ORDER)]
    return bias.reshape(WB_HEADS // 2, 2 * WB_BLOCK, 3 * WB_BLOCK)


def _even_weights(w_in, w_out):
    qa, ka, va, qb, kb, vb, gate = jnp.split(
        w_in, [NA_W, 2 * NA_W, 3 * NA_W, 3 * NA_W + WB_QW, 3 * NA_W + WB_QW + WB_KVW,
               3 * NA_W + WB_QW + 2 * WB_KVW], axis=-1)
    order = np.asarray(WB_HEAD_ORDER)
    perm = (order[:, None] * HEAD_DIM + np.arange(HEAD_DIM)[None, :]).reshape(-1)
    qscale = HEAD_DIM ** -0.5
    gate = jnp.concatenate([gate[:, :NA_W], gate[:, NA_W:][:, perm]], axis=-1)
    w = jnp.concatenate([gate, qa * qscale, ka, va, qb[:, perm] * qscale, kb, vb], axis=-1)
    w_out_p = jnp.concatenate([w_out[:NA_W], w_out[NA_W:][perm]], axis=0)
    return w.astype(BF16), w_out_p.astype(BF16)


def _odd_weights(w_in, q_norm, w_qb, kv_norm, w_kvb, w_out):
    q_lat, kv_lat, k_rope, gate = jnp.split(
        w_in, [MLA_Q_RANK, MLA_Q_RANK + MLA_KV_RANK, MLA_Q_RANK + MLA_KV_RANK + MLA_ROPE], axis=-1)
    w1t = jnp.concatenate([gate, q_lat, kv_lat], axis=-1).T.astype(BF16)
    half = MLA_ROPE // 2
    k_rot = jnp.concatenate([-k_rope[:, half:], k_rope[:, :half]], axis=-1)
    z64 = jnp.zeros((D_MODEL, MLA_NOPE), w_in.dtype)
    z32 = jnp.zeros((D_MODEL, LANES - MLA_QK), w_in.dtype)
    wnat = jnp.concatenate([kv_lat, z64, k_rope, z32, z64, k_rot, z32], axis=-1).astype(BF16)
    wkv = w_kvb.reshape(MLA_KV_RANK, MLA_HEADS, MLA_NOPE + MLA_V)
    wk = jnp.concatenate([wkv[:, :, :MLA_NOPE], jnp.zeros_like(wkv[:, :, :MLA_NOPE])], axis=-1)
    wk = wk.reshape(MLA_KV_RANK, MLA_HEADS * LANES).astype(BF16)
    wvt = wkv[:, :, MLA_NOPE:].reshape(MLA_KV_RANK, ODD_MIX).T.astype(BF16)
    gq = jnp.broadcast_to(q_norm.astype(F32)[:, None], (MLA_Q_RANK, LANES))
    gkvt = jnp.broadcast_to(kv_norm.astype(F32)[:, None], (MLA_KV_RANK, LANES))
    gkv = kv_norm.astype(F32)[None, :]
    return w1t, wnat, gq, gkvt, gkv, w_qb.T.astype(BF16), wk, wvt, w_out.T.astype(BF16)


def _rope_tables(l):
    inv_freq = 1.0 / (ROPE_THETA ** (jnp.arange(0, MLA_ROPE, 2, dtype=F32) / MLA_ROPE))
    ang = jnp.arange(l, dtype=F32)[:, None] * inv_freq[None, :]
    cos, sin = jnp.cos(ang), jnp.sin(ang)
    z64 = jnp.zeros((l, MLA_NOPE), F32)
    z32 = jnp.zeros((l, LANES - MLA_QK), F32)
    cos_p = jnp.concatenate([z64, cos, cos, z32], axis=-1)
    sin_p = jnp.concatenate([z64, sin, sin, z32], axis=-1)
    return cos.T, sin.T, cos_p, sin_p


def _trunk(x, mod, norm_g, ev, na_tbl, wb_tbl, sink, od, final_g):
    b, l, _ = x.shape
    assert l % TM == 0 and l % NA_CHUNK == 0 and l % WB_CHUNK == 0 and l % MLA_TK == 0
    assert l // GRID_W >= NA_ROWS_PER_STEP >= NA_HALO_ROWS >= NA_WIN_H
    shift0, scale0, gate0 = [t[:, None, :] for t in jnp.split(mod[0], 3, axis=-1)]
    shift1, scale1, gate1 = [t[:, None, :] for t in jnp.split(mod[1], 3, axis=-1)]

    w_in0, w_out0 = ev
    u = _even_front(x, norm_g[0][None, :], scale0, shift0, w_in0)
    oa = _na_attention(u, na_tbl)
    ob = _wb_attention(u, wb_tbl, sink)

    w1t, wnat, gq, gkvt, gkv, wqbt, wk, wvt, w_out1t = od
    cos_t, sin_t, cos_p, sin_p = _rope_tables(l)
    x1, k, qt, vt, sgt = _odd_front(x, oa, ob, u, gate0, w_out0, norm_g[1][None, :], scale1, shift1, w1t, wnat, gq, gkvt, gkv,
                                wqbt, wk, wvt, cos_t, sin_t, cos_p, sin_p)
    ot = _mla_attention(qt, k, vt)
    return _odd_out(x1, ot, sgt, gate1, w_out1t, final_g[None, :])


def kernel(x_prompt, x_sample, c_prompt, c_sample, ada_w, ada_b, norm_g, t5_bias, ev_w_in, na_rpb,
           wb_sink, ev_w_out, mla_w_in, mla_q_norm, mla_w_qb, mla_kv_norm, mla_w_kvb, mla_w_out, final_g):
    bp, bs = c_prompt.shape[0], c_sample.shape[0]
    rows = -(-(bp + bs) // 16) * 16
    c_pad = jnp.concatenate([c_prompt, c_sample, jnp.zeros((rows - bp - bs, D_MODEL), F32)], axis=0)
    mod = _ada_mod(c_pad, ada_w.astype(BF16), ada_b[:, None, :])

    ev = _even_weights(ev_w_in[0], ev_w_out[0])
    na_tbl = _na_bias_table(na_rpb[0])
    wb_tbl = _wb_bias_table(t5_bias)
    od = _odd_weights(mla_w_in[0], mla_q_norm[0], mla_w_qb[0], mla_kv_norm[0], mla_w_kvb[0], mla_w_out[0])
    sink = wb_sink[0].astype(F32)

    y_prompt = _trunk(x_prompt, mod[:, :bp], norm_g, ev, na_tbl, wb_tbl, sink, od, final_g)
    y_sample = _trunk(x_sample, mod[:, bp:bp + bs], norm_g, ev, na_tbl, wb_tbl, sink, od, final_g)
    return (y_prompt, y_sample)
```

```python
import functools
import math

import numpy as np
import jax
import jax.numpy as jnp
from jax import lax
from jax.experimental import pallas as pl
from jax.experimental.pallas import tpu as pltpu

D_MODEL = 1024
GRID_W = 64
HEAD_DIM = 64
EPS = 1e-6
NEG = -1e30
NA_HEADS = 8
NA_WIN_H = 8
NA_WIN_W = 16
WB_HEADS = 8
WB_KV_HEADS = 2
WB_WINDOW = 128
WB_BLOCK = 128
T5_BUCKETS = 32
T5_MAX_DIST = 128
MLA_HEADS = 16
MLA_Q_RANK = 256
MLA_KV_RANK = 128
MLA_NOPE = 64
MLA_ROPE = 32
MLA_V = 64
ROPE_THETA = 10000.0
MLA_QK = MLA_NOPE + MLA_ROPE

NA_W = NA_HEADS * HEAD_DIM
WB_QW = WB_HEADS * HEAD_DIM
WB_KVW = WB_KV_HEADS * HEAD_DIM
EVEN_MIX = NA_W + WB_QW
EVEN_IN = 3 * NA_W + WB_QW + 2 * WB_KVW + EVEN_MIX
ODD_MIX = MLA_HEADS * MLA_V

U_GATE = 0
U_QA = EVEN_MIX
U_KA = U_QA + NA_W
U_VA = U_KA + NA_W
U_QB = U_VA + NA_W
U_KB = U_QB + WB_QW
U_VB = U_KB + WB_KVW

LANES = 128
TM = 512
ADA_TN = 768
NA_ROWS_PER_STEP = 32
NA_HALO_ROWS = 8
NA_UNROLL = 6
WB_UNROLL = 7
NA_CHUNK = NA_ROWS_PER_STEP * GRID_W
WB_CHUNK = 2048
MLA_TQ = 1024
MLA_TK = 2048
MLA_RESIDENT_TOKENS = 16384
ONES_ROWS = 16
VMEM_LIMIT = 56 * 1024 * 1024

BF16 = jnp.bfloat16
F32 = jnp.float32
LOG2E = math.log2(math.e)

WB_HEAD_ORDER = (0, 4, 1, 5, 2, 6, 3, 7)


def _params(sem):
    return pltpu.CompilerParams(dimension_semantics=sem, vmem_limit_bytes=VMEM_LIMIT)


def _dot(a, b):
    return jnp.dot(a, b, preferred_element_type=F32)


def _dot_nt(a, b):
    return lax.dot_general(a, b, (((1,), (1,)), ((), ())), preferred_element_type=F32)


def _silu(x):
    return x * (1.0 / (1.0 + jnp.exp(-x)))


def _modulated_norm(x, g, scale, shift):
    ms = jnp.mean(x * x, axis=-1, keepdims=True)
    y = x * lax.rsqrt(ms + EPS)
    return (y * g) * (1.0 + scale) + shift


def _ada_kernel(c_ref, w_ref, b_ref, o_ref):
    c = c_ref[...]
    cs = _silu(c).astype(BF16)
    o_ref[0] = _dot(cs, w_ref[0]) + b_ref[0]


def _ada_mod(c_pad, ada_w, ada_b):
    depth = ada_w.shape[0]
    rows = c_pad.shape[0]
    tn = ADA_TN
    return pl.pallas_call(
        _ada_kernel,
        grid=(depth, 3 * D_MODEL // tn),
        in_specs=[
            pl.BlockSpec((rows, D_MODEL), lambda i, n: (0, 0)),
            pl.BlockSpec((1, D_MODEL, tn), lambda i, n: (i, 0, n)),
            pl.BlockSpec((1, 1, tn), lambda i, n: (i, 0, n)),
        ],
        out_specs=pl.BlockSpec((1, rows, tn), lambda i, n: (i, 0, n)),
        out_shape=jax.ShapeDtypeStruct((depth, rows, 3 * D_MODEL), F32),
        compiler_params=_params(("arbitrary", "arbitrary")),
        name="ada_mod",
    )(c_pad, ada_w, ada_b)


def _even_front_kernel(x_ref, g_ref, sc_ref, sh_ref, w_ref, u_ref):
    h = _modulated_norm(x_ref[0], g_ref[...], sc_ref[0], sh_ref[0]).astype(BF16)
    u_ref[0] = _dot(h, w_ref[...]).astype(BF16)


def _even_front(x, g, scale, shift, w):
    b, l, _ = x.shape
    return pl.pallas_call(
        _even_front_kernel,
        grid=(b, l // TM),
        in_specs=[
            pl.BlockSpec((1, TM, D_MODEL), lambda bi, i: (bi, i, 0)),
            pl.BlockSpec((1, D_MODEL), lambda bi, i: (0, 0)),
            pl.BlockSpec((1, 1, D_MODEL), lambda bi, i: (bi, 0, 0)),
            pl.BlockSpec((1, 1, D_MODEL), lambda bi, i: (bi, 0, 0)),
            pl.BlockSpec((D_MODEL, EVEN_IN), lambda bi, i: (0, 0)),
        ],
        out_specs=pl.BlockSpec((1, TM, EVEN_IN), lambda bi, i: (bi, i, 0)),
        out_shape=jax.ShapeDtypeStruct((b, l, EVEN_IN), BF16),
        compiler_params=_params(("arbitrary", "arbitrary")),
        name="even_front",
    )(x, g, scale, shift, w)


def _low_lane_mask(shape):
    return lax.broadcasted_iota(jnp.int32, shape, len(shape) - 1) < HEAD_DIM


def _na_kernel(q_ref, kp_ref, kc_ref, kn_ref, vp_ref, vc_ref, vn_ref, bias_ref, o_ref,
               kwin, vwin, s_scr, p_scr, *, rows):
    ci = pl.program_id(1)
    halo = NA_HALO_ROWS * GRID_W
    npair = NA_HEADS // 2
    kwin[0:halo] = kp_ref[0]
    kwin[halo:halo + NA_CHUNK] = kc_ref[0]
    kwin[halo + NA_CHUNK:2 * halo + NA_CHUNK] = kn_ref[0]
    for j in range(npair):
        lanes = slice(j * LANES, (j + 1) * LANES)
        dst = slice(2 * j * LANES, (2 * j + 1) * LANES)
        vwin[0:halo, dst] = vp_ref[0, :, lanes]
        vwin[halo:halo + NA_CHUNK, dst] = vc_ref[0, :, lanes]
        vwin[halo + NA_CHUNK:2 * halo + NA_CHUNK, dst] = vn_ref[0, :, lanes]
        vwin[:, (2 * j + 1) * LANES:(2 * j + 2) * LANES] = jnp.ones((2 * halo + NA_CHUNK, LANES), BF16)

    low = _low_lane_mask((GRID_W, LANES))
    kh = NA_WIN_H
    r0 = ci * NA_ROWS_PER_STEP

    def offsets(i):
        r = r0 + i
        row_start = jnp.clip(r - kh // 2, 0, rows - kh)
        variant = r - row_start
        start = pl.multiple_of((row_start - r0 + NA_HALO_ROWS) * GRID_W, GRID_W)
        return variant, start, pl.multiple_of(i * GRID_W, GRID_W)

    def scores(i):
        variant, start, qoff = offsets(i)
        for j in range(npair):
            lanes = slice(j * LANES, (j + 1) * LANES)
            q2 = q_ref[0, pl.ds(qoff, GRID_W), lanes]
            zero = jnp.zeros_like(q2)
            lhs = jnp.concatenate([jnp.where(low, q2, zero), jnp.where(low, zero, q2)], axis=0)
            kw = kwin[pl.ds(start, kh * GRID_W), lanes]
            s_scr[j] = _dot_nt(lhs, kw) + bias_ref[variant, j]

    def probs():
        for j in range(npair):
            s = s_scr[j]
            p_scr[j] = jnp.exp(s - jnp.max(s, axis=-1, keepdims=True)).astype(BF16)

    def output(i):
        _, start, qoff = offsets(i)
        for j in range(npair):
            vw = vwin[pl.ds(start, kh * GRID_W), 2 * j * LANES:(2 * j + 2) * LANES]
            res = _dot(p_scr[j], vw)
            o = res[:, 0:LANES] / res[:, LANES:2 * LANES]
            o_ref[0, pl.ds(qoff, GRID_W), j * LANES:(j + 1) * LANES] = (
                jnp.where(low, o[:GRID_W], o[GRID_W:]).astype(BF16))

    scores(0)
    probs()
    scores(1)

    def body(i, carry):
        output(i - 2)
        probs()
        scores(i)
        return carry

    lax.fori_loop(2, NA_ROWS_PER_STEP, body, 0, unroll=NA_UNROLL)
    output(NA_ROWS_PER_STEP - 2)
    probs()
    output(NA_ROWS_PER_STEP - 1)


def _na_attention(u, bias_tbl):
    b, l, _ = u.shape
    rows = l // GRID_W
    nchunk = l // NA_CHUNK
    halo = NA_HALO_ROWS * GRID_W
    per_chunk = NA_CHUNK // halo
    nhalo = l // halo
    qblk, kblk, vblk = U_QA // NA_W, U_KA // NA_W, U_VA // NA_W

    def window(col):
        return [
            pl.BlockSpec((1, halo, NA_W), lambda bi, i: (bi, jnp.maximum(i * per_chunk - 1, 0), col)),
            pl.BlockSpec((1, NA_CHUNK, NA_W), lambda bi, i: (bi, i, col)),
            pl.BlockSpec((1, halo, NA_W), lambda bi, i: (bi, jnp.minimum((i + 1) * per_chunk, nhalo - 1), col)),
        ]

    npair = NA_HEADS // 2
    return pl.pallas_call(
        functools.partial(_na_kernel, rows=rows),
        grid=(b, nchunk),
        in_specs=[pl.BlockSpec((1, NA_CHUNK, NA_W), lambda bi, i: (bi, i, qblk))]
        + window(kblk) + window(vblk)
        + [pl.BlockSpec(bias_tbl.shape, lambda bi, i: (0, 0, 0, 0))],
        out_specs=pl.BlockSpec((1, NA_CHUNK, NA_W), lambda bi, i: (bi, i, 0)),
        out_shape=jax.ShapeDtypeStruct((b, l, NA_W), BF16),
        scratch_shapes=[pltpu.VMEM((NA_CHUNK + 2 * halo, NA_W), BF16),
                        pltpu.VMEM((NA_CHUNK + 2 * halo, 2 * NA_W), BF16),
                        pltpu.VMEM((npair, 2 * GRID_W, NA_WIN_H * GRID_W), F32),
                        pltpu.VMEM((npair, 2 * GRID_W, NA_WIN_H * GRID_W), BF16)],
        compiler_params=_params(("arbitrary", "arbitrary")),
        name="na_attention",
    )(u, u, u, u, u, u, u, bias_tbl)


def _wb_kernel(sink_ref, q_ref, kp_ref, kc_ref, kn_ref, vp_ref, vc_ref, vn_ref, bias_ref, o_ref,
               kwin, vwin, s_scr, p_scr, ps_scr, *, nblk):
    ci = pl.program_id(1)
    npair = WB_HEADS // 2
    per_step = WB_CHUNK // WB_BLOCK
    win = WB_CHUNK + 2 * WB_BLOCK
    kwin[0:WB_BLOCK] = kp_ref[0]
    kwin[WB_BLOCK:WB_BLOCK + WB_CHUNK] = kc_ref[0]
    kwin[WB_BLOCK + WB_CHUNK:win] = kn_ref[0]
    vwin[0:WB_BLOCK, 0:LANES] = vp_ref[0]
    vwin[WB_BLOCK:WB_BLOCK + WB_CHUNK, 0:LANES] = vc_ref[0]
    vwin[WB_BLOCK + WB_CHUNK:win, 0:LANES] = vn_ref[0]
    vwin[:, LANES:2 * LANES] = jnp.ones((win, LANES), BF16)

    low = _low_lane_mask((WB_BLOCK, LANES))
    col = lax.broadcasted_iota(jnp.int32, (2 * WB_BLOCK, 3 * WB_BLOCK), 1)
    row = lax.broadcasted_iota(jnp.int32, (2 * WB_BLOCK, 1), 0)

    def scores(n):
        gblk = ci * per_step + n
        lo = jnp.where(gblk > 0, 0, WB_BLOCK)
        hi = jnp.where(gblk < nblk - 1, 3 * WB_BLOCK, 2 * WB_BLOCK)
        in_seq = jnp.logical_and(col >= lo, col < hi)
        off = pl.multiple_of(n * WB_BLOCK, WB_BLOCK)
        kw = kwin[pl.ds(off, 3 * WB_BLOCK), :]
        for j in range(npair):
            q2 = q_ref[0, pl.ds(off, WB_BLOCK), j * LANES:(j + 1) * LANES]
            zero = jnp.zeros_like(q2)
            lhs = jnp.concatenate([jnp.where(low, q2, zero), jnp.where(low, zero, q2)], axis=0)
            s_scr[j] = jnp.where(in_seq, _dot_nt(lhs, kw) + bias_ref[j], NEG)

    def probs():
        for j in range(npair):
            s = s_scr[j]
            sink = jnp.where(row < WB_BLOCK, sink_ref[WB_HEAD_ORDER[2 * j]],
                             sink_ref[WB_HEAD_ORDER[2 * j + 1]])
            m = jnp.maximum(jnp.max(s, axis=-1, keepdims=True), sink)
            p_scr[j] = jnp.exp(s - m).astype(BF16)
            ps_scr[j] = jnp.broadcast_to(jnp.exp(sink - m), (2 * WB_BLOCK, LANES))

    def output(n):
        off = pl.multiple_of(n * WB_BLOCK, WB_BLOCK)
        vw = vwin[pl.ds(off, 3 * WB_BLOCK), :]
        for j in range(npair):
            res = _dot(p_scr[j], vw)
            o = res[:, 0:LANES] / (res[:, LANES:2 * LANES] + ps_scr[j])
            o_ref[0, pl.ds(off, WB_BLOCK), j * LANES:(j + 1) * LANES] = (
                jnp.where(low, o[:WB_BLOCK], o[WB_BLOCK:]).astype(BF16))

    scores(0)
    probs()
    scores(1)

    def body(n, carry):
        output(n - 2)
        probs()
        scores(n)
        return carry

    lax.fori_loop(2, per_step, body, 0, unroll=WB_UNROLL)
    output(per_step - 2)
    probs()
    output(per_step - 1)


def _wb_attention(u, bias_tbl, sink):
    b, l, _ = u.shape
    nblk = l // WB_BLOCK
    nchunk = l // WB_CHUNK
    per_step = WB_CHUNK // WB_BLOCK
    qblk = U_QB // WB_QW
    kcol, vcol = U_KB // WB_KVW, U_VB // WB_KVW

    def halo(colblk, d):
        if d == 0:
            return pl.BlockSpec((1, WB_CHUNK, WB_KVW), lambda bi, i: (bi, i, colblk))
        return pl.BlockSpec(
            (1, WB_BLOCK, WB_KVW),
            lambda bi, i: (bi, jnp.clip(i * per_step + (per_step if d > 0 else -1), 0, nblk - 1), colblk))

    return pl.pallas_call(
        functools.partial(_wb_kernel, nblk=nblk),
        grid=(b, nchunk),
        in_specs=[
            pl.BlockSpec(memory_space=pltpu.SMEM),
            pl.BlockSpec((1, WB_CHUNK, WB_QW), lambda bi, i: (bi, i, qblk)),
            halo(kcol, -1), halo(kcol, 0), halo(kcol, 1),
            halo(vcol, -1), halo(vcol, 0), halo(vcol, 1),
            pl.BlockSpec(bias_tbl.shape, lambda bi, i: (0, 0, 0)),
        ],
        out_specs=pl.BlockSpec((1, WB_CHUNK, WB_QW), lambda bi, i: (bi, i, 0)),
        out_shape=jax.ShapeDtypeStruct((b, l, WB_QW), BF16),
        scratch_shapes=[pltpu.VMEM((WB_CHUNK + 2 * WB_BLOCK, WB_KVW), BF16),
                        pltpu.VMEM((WB_CHUNK + 2 * WB_BLOCK, 2 * WB_KVW), BF16),
                        pltpu.VMEM((WB_HEADS // 2, 2 * WB_BLOCK, 3 * WB_BLOCK), F32),
                        pltpu.VMEM((WB_HEADS // 2, 2 * WB_BLOCK, 3 * WB_BLOCK), BF16),
                        pltpu.VMEM((WB_HEADS // 2, 2 * WB_BLOCK, LANES), F32)],
        compiler_params=_params(("arbitrary", "arbitrary")),
        name="wb_attention",
    )(sink, u, u, u, u, u, u, u, bias_tbl)


def _even_out_tile(x_ref, oa_ref, ob_ref, gate_ref, gm_ref, w_ref):
    g = _silu(gate_ref[0].astype(F32))
    o = jnp.concatenate([oa_ref[0], ob_ref[0]], axis=-1).astype(F32)
    z = (o * g).astype(BF16)
    return x_ref[0] + gm_ref[0] * _dot(z, w_ref[...])


def _odd_front_kernel(x_ref, oa_ref, ob_ref, gate_ref, gm_ref, wo_ref,
                      g_ref, sc_ref, sh_ref, w1t_ref, wnat_ref, gq_ref, gkvt_ref, gkv_ref,
                      wqbt_ref, wk_ref, wvt_ref, cost_ref, sint_ref, cosp_ref, sinp_ref,
                      x1_ref, k_ref, qt_ref, vt_ref, sgt_ref):
    x1 = _even_out_tile(x_ref, oa_ref, ob_ref, gate_ref, gm_ref, wo_ref)
    x1_ref[0] = x1
    h = _modulated_norm(x1, g_ref[...], sc_ref[0], sh_ref[0]).astype(BF16)
    reps = TM // LANES

    ut = _dot_nt(w1t_ref[...], h)
    sgt_ref[0] = _silu(ut[0:ODD_MIX]).astype(BF16)

    qlt = ut[ODD_MIX:ODD_MIX + MLA_Q_RANK]
    qn = qlt * lax.rsqrt(jnp.mean(qlt * qlt, axis=0, keepdims=True) + EPS)
    qn = (qn * jnp.concatenate([gq_ref[...]] * reps, axis=1)).astype(BF16)
    qt = _dot(wqbt_ref[...], qn)
    qscale = (MLA_QK ** -0.5) * LOG2E
    cos_t = cost_ref[...]
    sin_t = sint_ref[...]
    half = MLA_ROPE // 2
    for hd in range(MLA_HEADS):
        base = hd * MLA_QK
        qt_ref[0, hd, 0, 0:MLA_NOPE, :] = (qt[base:base + MLA_NOPE] * qscale).astype(BF16)
        x1 = qt[base + MLA_NOPE:base + MLA_NOPE + half]
        x2 = qt[base + MLA_NOPE + half:base + MLA_QK]
        qt_ref[0, hd, 0, MLA_NOPE:MLA_NOPE + half, :] = ((x1 * cos_t - x2 * sin_t) * qscale).astype(BF16)
        qt_ref[0, hd, 0, MLA_NOPE + half:MLA_QK, :] = ((x2 * cos_t + x1 * sin_t) * qscale).astype(BF16)

    kvt = ut[ODD_MIX + MLA_Q_RANK:ODD_MIX + MLA_Q_RANK + MLA_KV_RANK]
    kvnt = kvt * lax.rsqrt(jnp.mean(kvt * kvt, axis=0, keepdims=True) + EPS)
    kvnt = (kvnt * jnp.concatenate([gkvt_ref[...]] * reps, axis=1)).astype(BF16)
    vt = _dot(wvt_ref[...], kvnt)
    vt_ref[0, :, 0] = vt.reshape(MLA_HEADS, MLA_V, TM).astype(BF16)

    nat = _dot(h, wnat_ref[...])
    kvl = nat[:, 0:MLA_KV_RANK]
    kvn = kvl * lax.rsqrt(jnp.mean(kvl * kvl, axis=-1, keepdims=True) + EPS)
    kvn = (kvn * gkv_ref[...]).astype(BF16)
    kn = _dot(kvn, wk_ref[...])
    kpe = nat[:, LANES:2 * LANES] * cosp_ref[...] + nat[:, 2 * LANES:3 * LANES] * sinp_ref[...]
    for hd in range(MLA_HEADS):
        kh = kn[:, hd * LANES:(hd + 1) * LANES] + kpe
        k_ref[0, hd] = kh[:, 0:MLA_QK].astype(BF16)


def _odd_front(x, oa, ob, u, gate_mod, w_out, g, scale, shift, w1t, wnat, gq, gkvt, gkv, wqbt, wk, wvt,
               cos_t, sin_t, cos_p, sin_p):
    b, l, _ = x.shape
    nt = l // TM
    const2 = lambda bi, i: (0, 0)
    n1 = w1t.shape[0]
    return pl.pallas_call(
        _odd_front_kernel,
        grid=(b, nt),
        in_specs=[
            pl.BlockSpec((1, TM, D_MODEL), lambda bi, i: (bi, i, 0)),
            pl.BlockSpec((1, TM, NA_W), lambda bi, i: (bi, i, 0)),
            pl.BlockSpec((1, TM, WB_QW), lambda bi, i: (bi, i, 0)),
            pl.BlockSpec((1, TM, EVEN_MIX), lambda bi, i: (bi, i, U_GATE // EVEN_MIX)),
            pl.BlockSpec((1, 1, D_MODEL), lambda bi, i: (bi, 0, 0)),
            pl.BlockSpec((EVEN_MIX, D_MODEL), const2),
            pl.BlockSpec((1, D_MODEL), const2),
            pl.BlockSpec((1, 1, D_MODEL), lambda bi, i: (bi, 0, 0)),
            pl.BlockSpec((1, 1, D_MODEL), lambda bi, i: (bi, 0, 0)),
            pl.BlockSpec((n1, D_MODEL), const2),
            pl.BlockSpec((D_MODEL, 3 * LANES), const2),
            pl.BlockSpec((MLA_Q_RANK, LANES), const2),
            pl.BlockSpec((MLA_KV_RANK, LANES), const2),
            pl.BlockSpec((1, MLA_KV_RANK), const2),
            pl.BlockSpec((MLA_HEADS * MLA_QK, MLA_Q_RANK), const2),
            pl.BlockSpec((MLA_KV_RANK, MLA_HEADS * LANES), const2),
            pl.BlockSpec((ODD_MIX, MLA_KV_RANK), const2),
            pl.BlockSpec((MLA_ROPE // 2, TM), lambda bi, i: (0, i)),
            pl.BlockSpec((MLA_ROPE // 2, TM), lambda bi, i: (0, i)),
            pl.BlockSpec((TM, LANES), lambda bi, i: (i, 0)),
            pl.BlockSpec((TM, LANES), lambda bi, i: (i, 0)),
        ],
        out_specs=[
            pl.BlockSpec((1, TM, D_MODEL), lambda bi, i: (bi, i, 0)),
            pl.BlockSpec((1, MLA_HEADS, TM, MLA_QK), lambda bi, i: (bi, 0, i, 0)),
            pl.BlockSpec((1, MLA_HEADS, 1, MLA_QK, TM), lambda bi, i: (bi, 0, i, 0, 0)),
            pl.BlockSpec((1, MLA_HEADS, 1, MLA_V, TM), lambda bi, i: (bi, 0, i, 0, 0)),
            pl.BlockSpec((1, ODD_MIX, TM), lambda bi, i: (bi, 0, i)),
        ],
        out_shape=[
            jax.ShapeDtypeStruct((b, l, D_MODEL), F32),
            jax.ShapeDtypeStruct((b, MLA_HEADS, l, MLA_QK), BF16),
            jax.ShapeDtypeStruct((b, MLA_HEADS, nt, MLA_QK, TM), BF16),
            jax.ShapeDtypeStruct((b, MLA_HEADS, nt, MLA_V, TM), BF16),
            jax.ShapeDtypeStruct((b, ODD_MIX, l), BF16),
        ],
        compiler_params=_params(("arbitrary", "arbitrary")),
        name="odd_front",
    )(x, oa, ob, u, gate_mod, w_out, g, scale, shift, w1t, wnat, gq, gkvt, gkv, wqbt, wk, wvt,
      cos_t, sin_t, cos_p, sin_p)


def _mla_kernel(qt_ref, k_ref, vt_ref, o_ref, s_scr, p_scr, acc_scr, *, hb, nqb, nchunks):
    ones = jnp.ones((ONES_ROWS, MLA_TK), BF16)
    qsub = MLA_TQ // TM
    ksub = MLA_TK // TM
    chunk_bits = nchunks.bit_length() - 1
    qb_bits = nqb.bit_length() - 1
    total = hb * nqb * nchunks

    def decode(i):
        return i >> (chunk_bits + qb_bits), (i >> chunk_bits) & (nqb - 1), i & (nchunks - 1)

    def scores(i):
        h, qb, t = decode(i)
        q = jnp.concatenate([qt_ref[0, h, qsub * qb + j] for j in range(qsub)], axis=1)
        koff = pl.multiple_of(t * MLA_TK, MLA_TK)
        s = _dot(k_ref[0, h, pl.ds(koff, MLA_TK), :], q)
        s_scr[...] = s
        return jnp.max(s, axis=0, keepdims=True)

    def probs(i, m, mt):
        _, _, t = decode(i)
        m = jnp.where(t == 0, NEG, m)
        m_new = jnp.maximum(m, mt)
        p_scr[...] = jnp.exp2(s_scr[...] - m_new).astype(BF16)
        return m_new, jnp.exp2(m - m_new)

    def accumulate(i, alpha):
        h, qb, t = decode(i)
        vt = jnp.concatenate([vt_ref[0, h, ksub * t + j] for j in range(ksub)], axis=1)
        acc_scr[...] = acc_scr[...] * alpha + _dot(jnp.concatenate([vt, ones], axis=0), p_scr[...])
        acc = acc_scr[...]
        o_ref[0, h, qb] = (acc[0:MLA_V] * (1.0 / acc[MLA_V:MLA_V + 1])).astype(BF16)

    acc_scr[...] = jnp.zeros_like(acc_scr)
    m = jnp.full((1, MLA_TQ), NEG, F32)
    mt = scores(0)
    m, alpha = probs(0, m, mt)
    mt = scores(1)

    def body(i, carry):
        m, mt, alpha = carry
        accumulate(i - 2, alpha)
        m, alpha = probs(i - 1, m, mt)
        mt = scores(i)
        return m, mt, alpha

    m, mt, alpha = lax.fori_loop(2, total, body, (m, mt, alpha), unroll=2)
    accumulate(total - 2, alpha)
    m, alpha = probs(total - 1, m, mt)
    accumulate(total - 1, alpha)


def _mla_attention(qt, k, vt):
    b, _, nt, _, _ = qt.shape
    l = nt * TM
    nchunks = l // MLA_TK
    nqb = l // MLA_TQ
    assert MLA_TK % TM == 0 and MLA_TQ % TM == 0
    assert nqb & (nqb - 1) == 0 and nchunks & (nchunks - 1) == 0
    hb = max(1, min(MLA_HEADS, MLA_RESIDENT_TOKENS // l))
    assert hb * nqb * nchunks >= 2 and (hb * nqb * nchunks) % 2 == 0
    return pl.pallas_call(
        functools.partial(_mla_kernel, hb=hb, nqb=nqb, nchunks=nchunks),
        grid=(b, MLA_HEADS // hb),
        in_specs=[
            pl.BlockSpec((1, hb, nt, MLA_QK, TM), lambda bi, h: (bi, h, 0, 0, 0)),
            pl.BlockSpec((1, hb, l, MLA_QK), lambda bi, h: (bi, h, 0, 0)),
            pl.BlockSpec((1, hb, nt, MLA_V, TM), lambda bi, h: (bi, h, 0, 0, 0)),
        ],
        out_specs=pl.BlockSpec((1, hb, nqb, MLA_V, MLA_TQ), lambda bi, h: (bi, h, 0, 0, 0)),
        out_shape=jax.ShapeDtypeStruct((b, MLA_HEADS, nqb, MLA_V, MLA_TQ), BF16),
        scratch_shapes=[pltpu.VMEM((MLA_TK, MLA_TQ), F32),
                        pltpu.VMEM((MLA_TK, MLA_TQ), BF16),
                        pltpu.VMEM((MLA_V + ONES_ROWS, MLA_TQ), F32)],
        compiler_params=_params(("arbitrary", "arbitrary")),
        name="mla_attention",
    )(qt, k, vt)


def _odd_out_kernel(x_ref, ot_ref, sgt_ref, gm_ref, wt_ref, fg_ref, y_ref):
    ot = ot_ref[0, :, 0].reshape(ODD_MIX, TM)
    z = (ot.astype(F32) * sgt_ref[0].astype(F32)).astype(BF16)
    out = lax.dot_general(z, wt_ref[...], (((0,), (1,)), ((), ())), preferred_element_type=F32)
    x2 = x_ref[0] + gm_ref[0] * out
    ms = jnp.mean(x2 * x2, axis=-1, keepdims=True)
    y_ref[0] = (x2 * lax.rsqrt(ms + EPS)) * fg_ref[...]


def _odd_out(x, ot, sgt, gate_mod, w_out_t, final_g):
    b, l, _ = x.shape
    sub = MLA_TQ // TM
    return pl.pallas_call(
        _odd_out_kernel,
        grid=(b, l // TM),
        in_specs=[
            pl.BlockSpec((1, TM, D_MODEL), lambda bi, i: (bi, i, 0)),
            pl.BlockSpec((1, MLA_HEADS, 1, MLA_V, TM), lambda bi, i: (bi, 0, i // sub, 0, i % sub)),
            pl.BlockSpec((1, ODD_MIX, TM), lambda bi, i: (bi, 0, i)),
            pl.BlockSpec((1, 1, D_MODEL), lambda bi, i: (bi, 0, 0)),
            pl.BlockSpec((D_MODEL, ODD_MIX), lambda bi, i: (0, 0)),
            pl.BlockSpec((1, D_MODEL), lambda bi, i: (0, 0)),
        ],
        out_specs=pl.BlockSpec((1, TM, D_MODEL), lambda bi, i: (bi, i, 0)),
        out_shape=jax.ShapeDtypeStruct((b, l, D_MODEL), F32),
        compiler_params=_params(("arbitrary", "arbitrary")),
        name="odd_out",
    )(x, ot, sgt, gate_mod, w_out_t, final_g)


def _t5_bucket(rel):
    nb = T5_BUCKETS // 2
    ret = (rel > 0).astype(np.int32) * nb
    n = np.abs(rel)
    max_exact = nb // 2
    large = max_exact + (np.log(np.maximum(n, 1) / max_exact)
                         / np.log(T5_MAX_DIST / max_exact) * (nb - max_exact)).astype(np.int32)
    large = np.minimum(large, nb - 1)
    return ret + np.where(n < max_exact, n, large)


def _na_bias_table(rpb):
    kh = NA_WIN_H
    c = np.arange(GRID_W)
    col_start = np.clip(c - NA_WIN_W // 2, 0, GRID_W - NA_WIN_W)
    col_ok = (c[None, :] >= col_start[:, None]) & (c[None, :] < col_start[:, None] + NA_WIN_W)
    d_col = np.clip(c[None, :] - c[:, None], -(NA_WIN_W - 1), NA_WIN_W - 1) + NA_WIN_W - 1
    nrel = 2 * NA_WIN_W - 1
    rows = jnp.stack([rpb[:, NA_WIN_H - 1 - v:2 * NA_WIN_H - 1 - v] for v in range(kh)], axis=1)
    onehot = (d_col.reshape(-1)[None, :] == np.arange(nrel)[:, None]).astype(np.float32)
    bias = jnp.dot(rows.astype(F32).reshape(-1, nrel), jnp.asarray(onehot),
                   precision=lax.Precision.HIGHEST)
    bias = bias.reshape(NA_HEADS, kh, kh, GRID_W, GRID_W)
    bias = jnp.where(jnp.asarray(col_ok)[None, None, None, :, :], bias, NEG)
    bias = bias.transpose(1, 0, 3, 2, 4).reshape(kh, NA_HEADS // 2, 2 * GRID_W, kh * GRID_W)
    return bias


def _wb_bias_table(t5_bias):
    rel = (np.arange(3 * WB_BLOCK) - WB_BLOCK)[None, :] - np.arange(WB_BLOCK)[:, None]
    offs = np.arange(-(2 * WB_BLOCK - 1), 2 * WB_BLOCK + 1)
    period = 4 * WB_BLOCK
    by_off = t5_bias[_t5_bucket(offs)].astype(F32).T
    shifted = jnp.tile(by_off, (1, WB_BLOCK))[:, :WB_BLOCK * (period - 1)]
    shifted = shifted.reshape(WB_HEADS, WB_BLOCK, period - 1)
    bias = shifted[:, :, WB_BLOCK - 1:4 * WB_BLOCK - 1]
    bias = jnp.where(jnp.asarray(np.abs(rel) <= WB_WINDOW)[None], bias, NEG)
    bias = bias[np.asarray(WB_HEAD_ORDER)]
    return bias.reshape(WB_HEADS // 2, 2 * WB_BLOCK, 3 * WB_BLOCK)


def _even_weights(w_in, w_out):
    qa, ka, va, qb, kb, vb, gate = jnp.split(
        w_in, [NA_W, 2 * NA_W, 3 * NA_W, 3 * NA_W + WB_QW, 3 * NA_W + WB_QW + WB_KVW,
               3 * NA_W + WB_QW + 2 * WB_KVW], axis=-1)
    order = np.asarray(WB_HEAD_ORDER)
    perm = (order[:, None] * HEAD_DIM + np.arange(HEAD_DIM)[None, :]).reshape(-1)
    qscale = HEAD_DIM ** -0.5
    gate = jnp.concatenate([gate[:, :NA_W], gate[:, NA_W:][:, perm]], axis=-1)
    w = jnp.concatenate([gate, qa * qscale, ka, va, qb[:, perm] * qscale, kb, vb], axis=-1)
    w_out_p = jnp.concatenate([w_out[:NA_W], w_out[NA_W:][perm]], axis=0)
    return w.astype(BF16), w_out_p.astype(BF16)


def _odd_weights(w_in, q_norm, w_qb, kv_norm, w_kvb, w_out):
    q_lat, kv_lat, k_rope, gate = jnp.split(
        w_in, [MLA_Q_RANK, MLA_Q_RANK + MLA_KV_RANK, MLA_Q_RANK + MLA_KV_RANK + MLA_ROPE], axis=-1)
    w1t = jnp.concatenate([gate, q_lat, kv_lat], axis=-1).T.astype(BF16)
    half = MLA_ROPE // 2
    k_rot = jnp.concatenate([-k_rope[:, half:], k_rope[:, :half]], axis=-1)
    z64 = jnp.zeros((D_MODEL, MLA_NOPE), w_in.dtype)
    z32 = jnp.zeros((D_MODEL, LANES - MLA_QK), w_in.dtype)
    wnat = jnp.concatenate([kv_lat, z64, k_rope, z32, z64, k_rot, z32], axis=-1).astype(BF16)
    wkv = w_kvb.reshape(MLA_KV_RANK, MLA_HEADS, MLA_NOPE + MLA_V)
    wk = jnp.concatenate([wkv[:, :, :MLA_NOPE], jnp.zeros_like(wkv[:, :, :MLA_NOPE])], axis=-1)
    wk = wk.reshape(MLA_KV_RANK, MLA_HEADS * LANES).astype(BF16)
    wvt = wkv[:, :, MLA_NOPE:].reshape(MLA_KV_RANK, ODD_MIX).T.astype(BF16)
    gq = jnp.broadcast_to(q_norm.astype(F32)[:, None], (MLA_Q_RANK, LANES))
    gkvt = jnp.broadcast_to(kv_norm.astype(F32)[:, None], (MLA_KV_RANK, LANES))
    gkv = kv_norm.astype(F32)[None, :]
    return w1t, wnat, gq, gkvt, gkv, w_qb.T.astype(BF16), wk, wvt, w_out.T.astype(BF16)


def _rope_tables(l):
    inv_freq = 1.0 / (ROPE_THETA ** (jnp.arange(0, MLA_ROPE, 2, dtype=F32) / MLA_ROPE))
    ang = jnp.arange(l, dtype=F32)[:, None] * inv_freq[None, :]
    cos, sin = jnp.cos(ang), jnp.sin(ang)
    z64 = jnp.zeros((l, MLA_NOPE), F32)
    z32 = jnp.zeros((l, LANES - MLA_QK), F32)
    cos_p = jnp.concatenate([z64, cos, cos, z32], axis=-1)
    sin_p = jnp.concatenate([z64, sin, sin, z32], axis=-1)
    return cos.T, sin.T, cos_p, sin_p


def _trunk(x, mod, norm_g, ev, na_tbl, wb_tbl, sink, od, final_g):
    b, l, _ = x.shape
    assert l % TM == 0 and l % NA_CHUNK == 0 and l % WB_CHUNK == 0 and l % MLA_TK == 0
    assert l // GRID_W >= NA_ROWS_PER_STEP >= NA_HALO_ROWS >= NA_WIN_H
    shift0, scale0, gate0 = [t[:, None, :] for t in jnp.split(mod[0], 3, axis=-1)]
    shift1, scale1, gate1 = [t[:, None, :] for t in jnp.split(mod[1], 3, axis=-1)]

    w_in0, w_out0 = ev
    u = _even_front(x, norm_g[0][None, :], scale0, shift0, w_in0)
    oa = _na_attention(u, na_tbl)
    ob = _wb_attention(u, wb_tbl, sink)

    w1t, wnat, gq, gkvt, gkv, wqbt, wk, wvt, w_out1t = od
    cos_t, sin_t, cos_p, sin_p = _rope_tables(l)
    x1, k, qt, vt, sgt = _odd_front(x, oa, ob, u, gate0, w_out0, norm_g[1][None, :], scale1, shift1, w1t, wnat, gq, gkvt, gkv,
                                wqbt, wk, wvt, cos_t, sin_t, cos_p, sin_p)
    ot = _mla_attention(qt, k, vt)
    return _odd_out(x1, ot, sgt, gate1, w_out1t, final_g[None, :])


def kernel(x_prompt, x_sample, c_prompt, c_sample, ada_w, ada_b, norm_g, t5_bias, ev_w_in, na_rpb,
           wb_sink, ev_w_out, mla_w_in, mla_q_norm, mla_w_qb, mla_kv_norm, mla_w_kvb, mla_w_out, final_g):
    bp, bs = c_prompt.shape[0], c_sample.shape[0]
    rows = -(-(bp + bs) // 16) * 16
    c_pad = jnp.concatenate([c_prompt, c_sample, jnp.zeros((rows - bp - bs, D_MODEL), F32)], axis=0)
    mod = _ada_mod(c_pad, ada_w.astype(BF16), ada_b[:, None, :])

    ev = _even_weights(ev_w_in[0], ev_w_out[0])
    na_tbl = _na_bias_table(na_rpb[0])
    wb_tbl = _wb_bias_table(t5_bias)
    od = _odd_weights(mla_w_in[0], mla_q_norm[0], mla_w_qb[0], mla_kv_norm[0], mla_w_kvb[0], mla_w_out[0])
    sink = wb_sink[0].astype(F32)

    y_prompt = _trunk(x_prompt, mod[:, :bp], norm_g, ev, na_tbl, wb_tbl, sink, od, final_g)
    y_sample = _trunk(x_sample, mod[:, bp:bp + bs], norm_g, ev, na_tbl, wb_tbl, sink, od, final_g)
    return (y_prompt, y_sample)
```

```python
import functools
import math

import numpy as np
import jax
import jax.numpy as jnp
from jax import lax
from jax.experimental import pallas as pl
from jax.experimental.pallas import tpu as pltpu

D_MODEL = 1024
GRID_W = 64
HEAD_DIM = 64
EPS = 1e-6
NEG = -1e30
NA_HEADS = 8
NA_WIN_H = 8
NA_WIN_W = 16
WB_HEADS = 8
WB_KV_HEADS = 2
WB_WINDOW = 128
WB_BLOCK = 128
T5_BUCKETS = 32
T5_MAX_DIST = 128
MLA_HEADS = 16
MLA_Q_RANK = 256
MLA_KV_RANK = 128
MLA_NOPE = 64
MLA_ROPE = 32
MLA_V = 64
ROPE_THETA = 10000.0
MLA_QK = MLA_NOPE + MLA_ROPE

NA_W = NA_HEADS * HEAD_DIM
WB_QW = WB_HEADS * HEAD_DIM
WB_KVW = WB_KV_HEADS * HEAD_DIM
EVEN_MIX = NA_W + WB_QW
EVEN_IN = 3 * NA_W + WB_QW + 2 * WB_KVW + EVEN_MIX
ODD_MIX = MLA_HEADS * MLA_V

U_GATE = 0
U_QA = EVEN_MIX
U_KA = U_QA + NA_W
U_VA = U_KA + NA_W
U_QB = U_VA + NA_W
U_KB = U_QB + WB_QW
U_VB = U_KB + WB_KVW

LANES = 128
TM = 512
EVEN_TM = 1024
ADA_TN = 768
NA_ROWS_PER_STEP = 32
NA_HALO_ROWS = 8
NA_UNROLL = 10
WB_UNROLL = 7
NA_CHUNK = NA_ROWS_PER_STEP * GRID_W
WB_CHUNK = 2048
MLA_TQ = 1024
MLA_TK = 2048
MLA_RESIDENT_TOKENS = 16384
ONES_ROWS = 16
VMEM_LIMIT = 56 * 1024 * 1024

BF16 = jnp.bfloat16
F32 = jnp.float32
LOG2E = math.log2(math.e)

WB_HEAD_ORDER = (0, 4, 1, 5, 2, 6, 3, 7)


def _params(sem):
    return pltpu.CompilerParams(dimension_semantics=sem, vmem_limit_bytes=VMEM_LIMIT)


def _dot(a, b):
    return jnp.dot(a, b, preferred_element_type=F32)


def _dot_nt(a, b):
    return lax.dot_general(a, b, (((1,), (1,)), ((), ())), preferred_element_type=F32)


def _silu(x):
    return x * (1.0 / (1.0 + jnp.exp(-x)))


def _modulated_norm(x, g, scale, shift):
    ms = jnp.mean(x * x, axis=-1, keepdims=True)
    y = x * lax.rsqrt(ms + EPS)
    return (y * g) * (1.0 + scale) + shift


def _ada_kernel(c_ref, w_ref, b_ref, o_ref):
    c = c_ref[...]
    cs = _silu(c).astype(BF16)
    o_ref[0] = _dot(cs, w_ref[0]) + b_ref[0]


def _ada_mod(c_pad, ada_w, ada_b):
    depth = ada_w.shape[0]
    rows = c_pad.shape[0]
    tn = ADA_TN
    return pl.pallas_call(
        _ada_kernel,
        grid=(depth, 3 * D_MODEL // tn),
        in_specs=[
            pl.BlockSpec((rows, D_MODEL), lambda i, n: (0, 0)),
            pl.BlockSpec((1, D_MODEL, tn), lambda i, n: (i, 0, n)),
            pl.BlockSpec((1, 1, tn), lambda i, n: (i, 0, n)),
        ],
        out_specs=pl.BlockSpec((1, rows, tn), lambda i, n: (i, 0, n)),
        out_shape=jax.ShapeDtypeStruct((depth, rows, 3 * D_MODEL), F32),
        compiler_params=_params(("arbitrary", "arbitrary")),
        name="ada_mod",
    )(c_pad, ada_w, ada_b)


def _even_front_kernel(x_ref, g_ref, sc_ref, sh_ref, w_ref, u_ref):
    h = _modulated_norm(x_ref[0], g_ref[...], sc_ref[0], sh_ref[0]).astype(BF16)
    u_ref[0] = _dot(h, w_ref[...]).astype(BF16)


def _even_front(x, g, scale, shift, w):
    b, l, _ = x.shape
    TM = EVEN_TM
    return pl.pallas_call(
        _even_front_kernel,
        grid=(b, l // TM),
        in_specs=[
            pl.BlockSpec((1, TM, D_MODEL), lambda bi, i: (bi, i, 0)),
            pl.BlockSpec((1, D_MODEL), lambda bi, i: (0, 0)),
            pl.BlockSpec((1, 1, D_MODEL), lambda bi, i: (bi, 0, 0)),
            pl.BlockSpec((1, 1, D_MODEL), lambda bi, i: (bi, 0, 0)),
            pl.BlockSpec((D_MODEL, EVEN_IN), lambda bi, i: (0, 0)),
        ],
        out_specs=pl.BlockSpec((1, TM, EVEN_IN), lambda bi, i: (bi, i, 0)),
        out_shape=jax.ShapeDtypeStruct((b, l, EVEN_IN), BF16),
        compiler_params=_params(("arbitrary", "arbitrary")),
        name="even_front",
    )(x, g, scale, shift, w)


def _low_lane_mask(shape):
    return lax.broadcasted_iota(jnp.int32, shape, len(shape) - 1) < HEAD_DIM


def _na_kernel(q_ref, kp_ref, kc_ref, kn_ref, vp_ref, vc_ref, vn_ref, bias_ref, o_ref,
               kwin, vwin, s_scr, p_scr, *, rows):
    ci = pl.program_id(1)
    halo = NA_HALO_ROWS * GRID_W
    npair = NA_HEADS // 2
    kwin[0:halo] = kp_ref[0]
    kwin[halo:halo + NA_CHUNK] = kc_ref[0]
    kwin[halo + NA_CHUNK:2 * halo + NA_CHUNK] = kn_ref[0]
    for j in range(npair):
        lanes = slice(j * LANES, (j + 1) * LANES)
        dst = slice(2 * j * LANES, (2 * j + 1) * LANES)
        vwin[0:halo, dst] = vp_ref[0, :, lanes]
        vwin[halo:halo + NA_CHUNK, dst] = vc_ref[0, :, lanes]
        vwin[halo + NA_CHUNK:2 * halo + NA_CHUNK, dst] = vn_ref[0, :, lanes]
        vwin[:, (2 * j + 1) * LANES:(2 * j + 2) * LANES] = jnp.ones((2 * halo + NA_CHUNK, LANES), BF16)

    low = _low_lane_mask((GRID_W, LANES))
    kh = NA_WIN_H
    r0 = ci * NA_ROWS_PER_STEP

    def offsets(i):
        r = r0 + i
        row_start = jnp.clip(r - kh // 2, 0, rows - kh)
        variant = r - row_start
        start = pl.multiple_of((row_start - r0 + NA_HALO_ROWS) * GRID_W, GRID_W)
        return variant, start, pl.multiple_of(i * GRID_W, GRID_W)

    def scores(i):
        variant, start, qoff = offsets(i)
        for j in range(npair):
            lanes = slice(j * LANES, (j + 1) * LANES)
            q2 = q_ref[0, pl.ds(qoff, GRID_W), lanes]
            zero = jnp.zeros_like(q2)
            lhs = jnp.concatenate([jnp.where(low, q2, zero), jnp.where(low, zero, q2)], axis=0)
            kw = kwin[pl.ds(start, kh * GRID_W), lanes]
            s_scr[j] = _dot_nt(lhs, kw) + bias_ref[variant, j]

    def probs():
        for j in range(npair):
            s = s_scr[j]
            p_scr[j] = jnp.exp(s - jnp.max(s, axis=-1, keepdims=True)).astype(BF16)

    def output(i):
        _, start, qoff = offsets(i)
        for j in range(npair):
            vw = vwin[pl.ds(start, kh * GRID_W), 2 * j * LANES:(2 * j + 2) * LANES]
            res = _dot(p_scr[j], vw)
            o = res[:, 0:LANES] / res[:, LANES:2 * LANES]
            o_ref[0, pl.ds(qoff, GRID_W), j * LANES:(j + 1) * LANES] = (
                jnp.where(low, o[:GRID_W], o[GRID_W:]).astype(BF16))

    scores(0)
    probs()
    scores(1)

    def body(i, carry):
        output(i - 2)
        probs()
        scores(i)
        return carry

    lax.fori_loop(2, NA_ROWS_PER_STEP, body, 0, unroll=NA_UNROLL)
    output(NA_ROWS_PER_STEP - 2)
    probs()
    output(NA_ROWS_PER_STEP - 1)


def _na_attention(u, bias_tbl):
    b, l, _ = u.shape
    rows = l // GRID_W
    nchunk = l // NA_CHUNK
    halo = NA_HALO_ROWS * GRID_W
    per_chunk = NA_CHUNK // halo
    nhalo = l // halo
    qblk, kblk, vblk = U_QA // NA_W, U_KA // NA_W, U_VA // NA_W

    def window(col):
        return [
            pl.BlockSpec((1, halo, NA_W), lambda bi, i: (bi, jnp.maximum(i * per_chunk - 1, 0), col)),
            pl.BlockSpec((1, NA_CHUNK, NA_W), lambda bi, i: (bi, i, col)),
            pl.BlockSpec((1, halo, NA_W), lambda bi, i: (bi, jnp.minimum((i + 1) * per_chunk, nhalo - 1), col)),
        ]

    npair = NA_HEADS // 2
    return pl.pallas_call(
        functools.partial(_na_kernel, rows=rows),
        grid=(b, nchunk),
        in_specs=[pl.BlockSpec((1, NA_CHUNK, NA_W), lambda bi, i: (bi, i, qblk))]
        + window(kblk) + window(vblk)
        + [pl.BlockSpec(bias_tbl.shape, lambda bi, i: (0, 0, 0, 0))],
        out_specs=pl.BlockSpec((1, NA_CHUNK, NA_W), lambda bi, i: (bi, i, 0)),
        out_shape=jax.ShapeDtypeStruct((b, l, NA_W), BF16),
        scratch_shapes=[pltpu.VMEM((NA_CHUNK + 2 * halo, NA_W), BF16),
                        pltpu.VMEM((NA_CHUNK + 2 * halo, 2 * NA_W), BF16),
                        pltpu.VMEM((npair, 2 * GRID_W, NA_WIN_H * GRID_W), F32),
                        pltpu.VMEM((npair, 2 * GRID_W, NA_WIN_H * GRID_W), BF16)],
        compiler_params=_params(("arbitrary", "arbitrary")),
        name="na_attention",
    )(u, u, u, u, u, u, u, bias_tbl)


def _wb_kernel(sink_ref, q_ref, kp_ref, kc_ref, kn_ref, vp_ref, vc_ref, vn_ref, bias_ref, o_ref,
               kwin, vwin, s_scr, p_scr, ps_scr, *, nblk):
    ci = pl.program_id(1)
    npair = WB_HEADS // 2
    per_step = WB_CHUNK // WB_BLOCK
    win = WB_CHUNK + 2 * WB_BLOCK
    kwin[0:WB_BLOCK] = kp_ref[0]
    kwin[WB_BLOCK:WB_BLOCK + WB_CHUNK] = kc_ref[0]
    kwin[WB_BLOCK + WB_CHUNK:win] = kn_ref[0]
    vwin[0:WB_BLOCK, 0:LANES] = vp_ref[0]
    vwin[WB_BLOCK:WB_BLOCK + WB_CHUNK, 0:LANES] = vc_ref[0]
    vwin[WB_BLOCK + WB_CHUNK:win, 0:LANES] = vn_ref[0]
    vwin[:, LANES:2 * LANES] = jnp.ones((win, LANES), BF16)

    low = _low_lane_mask((WB_BLOCK, LANES))
    col = lax.broadcasted_iota(jnp.int32, (2 * WB_BLOCK, 3 * WB_BLOCK), 1)
    row = lax.broadcasted_iota(jnp.int32, (2 * WB_BLOCK, 1), 0)

    def scores(n):
        gblk = ci * per_step + n
        lo = jnp.where(gblk > 0, 0, WB_BLOCK)
        hi = jnp.where(gblk < nblk - 1, 3 * WB_BLOCK, 2 * WB_BLOCK)
        in_seq = jnp.logical_and(col >= lo, col < hi)
        off = pl.multiple_of(n * WB_BLOCK, WB_BLOCK)
        kw = kwin[pl.ds(off, 3 * WB_BLOCK), :]
        for j in range(npair):
            q2 = q_ref[0, pl.ds(off, WB_BLOCK), j * LANES:(j + 1) * LANES]
            zero = jnp.zeros_like(q2)
            lhs = jnp.concatenate([jnp.where(low, q2, zero), jnp.where(low, zero, q2)], axis=0)
            s_scr[j] = jnp.where(in_seq, _dot_nt(lhs, kw) + bias_ref[j], NEG)

    def probs():
        for j in range(npair):
            s = s_scr[j]
            sink = jnp.where(row < WB_BLOCK, sink_ref[WB_HEAD_ORDER[2 * j]],
                             sink_ref[WB_HEAD_ORDER[2 * j + 1]])
            m = jnp.maximum(jnp.max(s, axis=-1, keepdims=True), sink)
            p_scr[j] = jnp.exp(s - m).astype(BF16)
            ps_scr[j] = jnp.broadcast_to(jnp.exp(sink - m), (2 * WB_BLOCK, LANES))

    def output(n):
        off = pl.multiple_of(n * WB_BLOCK, WB_BLOCK)
        vw = vwin[pl.ds(off, 3 * WB_BLOCK), :]
        for j in range(npair):
            res = _dot(p_scr[j], vw)
            o = res[:, 0:LANES] / (res[:, LANES:2 * LANES] + ps_scr[j])
            o_ref[0, pl.ds(off, WB_BLOCK), j * LANES:(j + 1) * LANES] = (
                jnp.where(low, o[:WB_BLOCK], o[WB_BLOCK:]).astype(BF16))

    scores(0)
    probs()
    scores(1)

    def body(n, carry):
        output(n - 2)
        probs()
        scores(n)
        return carry

    lax.fori_loop(2, per_step, body, 0, unroll=WB_UNROLL)
    output(per_step - 2)
    probs()
    output(per_step - 1)


def _wb_attention(u, bias_tbl, sink):
    b, l, _ = u.shape
    nblk = l // WB_BLOCK
    nchunk = l // WB_CHUNK
    per_step = WB_CHUNK // WB_BLOCK
    qblk = U_QB // WB_QW
    kcol, vcol = U_KB // WB_KVW, U_VB // WB_KVW

    def halo(colblk, d):
        if d == 0:
            return pl.BlockSpec((1, WB_CHUNK, WB_KVW), lambda bi, i: (bi, i, colblk))
        return pl.BlockSpec(
            (1, WB_BLOCK, WB_KVW),
            lambda bi, i: (bi, jnp.clip(i * per_step + (per_step if d > 0 else -1), 0, nblk - 1), colblk))

    return pl.pallas_call(
        functools.partial(_wb_kernel, nblk=nblk),
        grid=(b, nchunk),
        in_specs=[
            pl.BlockSpec(memory_space=pltpu.SMEM),
            pl.BlockSpec((1, WB_CHUNK, WB_QW), lambda bi, i: (bi, i, qblk)),
            halo(kcol, -1), halo(kcol, 0), halo(kcol, 1),
            halo(vcol, -1), halo(vcol, 0), halo(vcol, 1),
            pl.BlockSpec(bias_tbl.shape, lambda bi, i: (0, 0, 0)),
        ],
        out_specs=pl.BlockSpec((1, WB_CHUNK, WB_QW), lambda bi, i: (bi, i, 0)),
        out_shape=jax.ShapeDtypeStruct((b, l, WB_QW), BF16),
        scratch_shapes=[pltpu.VMEM((WB_CHUNK + 2 * WB_BLOCK, WB_KVW), BF16),
                        pltpu.VMEM((WB_CHUNK + 2 * WB_BLOCK, 2 * WB_KVW), BF16),
                        pltpu.VMEM((WB_HEADS // 2, 2 * WB_BLOCK, 3 * WB_BLOCK), F32),
                        pltpu.VMEM((WB_HEADS // 2, 2 * WB_BLOCK, 3 * WB_BLOCK), BF16),
                        pltpu.VMEM((WB_HEADS // 2, 2 * WB_BLOCK, LANES), F32)],
        compiler_params=_params(("arbitrary", "arbitrary")),
        name="wb_attention",
    )(sink, u, u, u, u, u, u, u, bias_tbl)


def _even_out_tile(x_ref, oa_ref, ob_ref, gate_ref, gm_ref, w_ref):
    g = _silu(gate_ref[0].astype(F32))
    o = jnp.concatenate([oa_ref[0], ob_ref[0]], axis=-1).astype(F32)
    z = (o * g).astype(BF16)
    return x_ref[0] + gm_ref[0] * _dot(z, w_ref[...])


def _odd_front_kernel(x_ref, oa_ref, ob_ref, gate_ref, gm_ref, wo_ref,
                      g_ref, sc_ref, sh_ref, w1t_ref, wnat_ref, gq_ref, gkvt_ref, gkv_ref,
                      wqbt_ref, wk_ref, wvt_ref, cost_ref, sint_ref, cosp_ref, sinp_ref,
                      x1_ref, k_ref, qt_ref, vt_ref, sgt_ref):
    x1 = _even_out_tile(x_ref, oa_ref, ob_ref, gate_ref, gm_ref, wo_ref)
    x1_ref[0] = x1
    h = _modulated_norm(x1, g_ref[...], sc_ref[0], sh_ref[0]).astype(BF16)
    reps = TM // LANES

    ut = _dot_nt(w1t_ref[...], h)
    sgt_ref[0] = _silu(ut[0:ODD_MIX]).astype(BF16)

    qlt = ut[ODD_MIX:ODD_MIX + MLA_Q_RANK]
    qn = qlt * lax.rsqrt(jnp.mean(qlt * qlt, axis=0, keepdims=True) + EPS)
    qn = (qn * jnp.concatenate([gq_ref[...]] * reps, axis=1)).astype(BF16)
    qt = _dot(wqbt_ref[...], qn)
    qscale = (MLA_QK ** -0.5) * LOG2E
    cos_t = cost_ref[...]
    sin_t = sint_ref[...]
    half = MLA_ROPE // 2
    for hd in range(MLA_HEADS):
        base = hd * MLA_QK
        qt_ref[0, hd, 0, 0:MLA_NOPE, :] = (qt[base:base + MLA_NOPE] * qscale).astype(BF16)
        x1 = qt[base + MLA_NOPE:base + MLA_NOPE + half]
        x2 = qt[base + MLA_NOPE + half:base + MLA_QK]
        qt_ref[0, hd, 0, MLA_NOPE:MLA_NOPE + half, :] = ((x1 * cos_t - x2 * sin_t) * qscale).astype(BF16)
        qt_ref[0, hd, 0, MLA_NOPE + half:MLA_QK, :] = ((x2 * cos_t + x1 * sin_t) * qscale).astype(BF16)

    kvt = ut[ODD_MIX + MLA_Q_RANK:ODD_MIX + MLA_Q_RANK + MLA_KV_RANK]
    kvnt = kvt * lax.rsqrt(jnp.mean(kvt * kvt, axis=0, keepdims=True) + EPS)
    kvnt = (kvnt * jnp.concatenate([gkvt_ref[...]] * reps, axis=1)).astype(BF16)
    vt = _dot(wvt_ref[...], kvnt)
    vt_ref[0, :, 0] = vt.reshape(MLA_HEADS, MLA_V, TM).astype(BF16)

    nat = _dot(h, wnat_ref[...])
    kvl = nat[:, 0:MLA_KV_RANK]
    kvn = kvl * lax.rsqrt(jnp.mean(kvl * kvl, axis=-1, keepdims=True) + EPS)
    kvn = (kvn * gkv_ref[...]).astype(BF16)
    kn = _dot(kvn, wk_ref[...])
    kpe = nat[:, LANES:2 * LANES] * cosp_ref[...] + nat[:, 2 * LANES:3 * LANES] * sinp_ref[...]
    for hd in range(MLA_HEADS):
        kh = kn[:, hd * LANES:(hd + 1) * LANES] + kpe
        k_ref[0, hd] = kh[:, 0:MLA_QK].astype(BF16)


def _odd_front(x, oa, ob, u, gate_mod, w_out, g, scale, shift, w1t, wnat, gq, gkvt, gkv, wqbt, wk, wvt,
               cos_t, sin_t, cos_p, sin_p):
    b, l, _ = x.shape
    nt = l // TM
    const2 = lambda bi, i: (0, 0)
    n1 = w1t.shape[0]
    return pl.pallas_call(
        _odd_front_kernel,
        grid=(b, nt),
        in_specs=[
            pl.BlockSpec((1, TM, D_MODEL), lambda bi, i: (bi, i, 0)),
            pl.BlockSpec((1, TM, NA_W), lambda bi, i: (bi, i, 0)),
            pl.BlockSpec((1, TM, WB_QW), lambda bi, i: (bi, i, 0)),
            pl.BlockSpec((1, TM, EVEN_MIX), lambda bi, i: (bi, i, U_GATE // EVEN_MIX)),
            pl.BlockSpec((1, 1, D_MODEL), lambda bi, i: (bi, 0, 0)),
            pl.BlockSpec((EVEN_MIX, D_MODEL), const2),
            pl.BlockSpec((1, D_MODEL), const2),
            pl.BlockSpec((1, 1, D_MODEL), lambda bi, i: (bi, 0, 0)),
            pl.BlockSpec((1, 1, D_MODEL), lambda bi, i: (bi, 0, 0)),
            pl.BlockSpec((n1, D_MODEL), const2),
            pl.BlockSpec((D_MODEL, 3 * LANES), const2),
            pl.BlockSpec((MLA_Q_RANK, LANES), const2),
            pl.BlockSpec((MLA_KV_RANK, LANES), const2),
            pl.BlockSpec((1, MLA_KV_RANK), const2),
            pl.BlockSpec((MLA_HEADS * MLA_QK, MLA_Q_RANK), const2),
            pl.BlockSpec((MLA_KV_RANK, MLA_HEADS * LANES), const2),
            pl.BlockSpec((ODD_MIX, MLA_KV_RANK), const2),
            pl.BlockSpec((MLA_ROPE // 2, TM), lambda bi, i: (0, i)),
            pl.BlockSpec((MLA_ROPE // 2, TM), lambda bi, i: (0, i)),
            pl.BlockSpec((TM, LANES), lambda bi, i: (i, 0)),
            pl.BlockSpec((TM, LANES), lambda bi, i: (i, 0)),
        ],
        out_specs=[
            pl.BlockSpec((1, TM, D_MODEL), lambda bi, i: (bi, i, 0)),
            pl.BlockSpec((1, MLA_HEADS, TM, MLA_QK), lambda bi, i: (bi, 0, i, 0)),
            pl.BlockSpec((1, MLA_HEADS, 1, MLA_QK, TM), lambda bi, i: (bi, 0, i, 0, 0)),
            pl.BlockSpec((1, MLA_HEADS, 1, MLA_V, TM), lambda bi, i: (bi, 0, i, 0, 0)),
            pl.BlockSpec((1, ODD_MIX, TM), lambda bi, i: (bi, 0, i)),
        ],
        out_shape=[
            jax.ShapeDtypeStruct((b, l, D_MODEL), F32),
            jax.ShapeDtypeStruct((b, MLA_HEADS, l, MLA_QK), BF16),
            jax.ShapeDtypeStruct((b, MLA_HEADS, nt, MLA_QK, TM), BF16),
            jax.ShapeDtypeStruct((b, MLA_HEADS, nt, MLA_V, TM), BF16),
            jax.ShapeDtypeStruct((b, ODD_MIX, l), BF16),
        ],
        compiler_params=_params(("arbitrary", "arbitrary")),
        name="odd_front",
    )(x, oa, ob, u, gate_mod, w_out, g, scale, shift, w1t, wnat, gq, gkvt, gkv, wqbt, wk, wvt,
      cos_t, sin_t, cos_p, sin_p)


def _mla_kernel(qt_ref, k_ref, vt_ref, o_ref, s_scr, p_scr, acc_scr, *, hb, nqb, nchunks):
    ones = jnp.ones((ONES_ROWS, MLA_TK), BF16)
    qsub = MLA_TQ // TM
    ksub = MLA_TK // TM
    chunk_bits = nchunks.bit_length() - 1
    qb_bits = nqb.bit_length() - 1
    total = hb * nqb * nchunks

    def decode(i):
        return i >> (chunk_bits + qb_bits), (i >> chunk_bits) & (nqb - 1), i & (nchunks - 1)

    def scores(i):
        h, qb, t = decode(i)
        q = jnp.concatenate([qt_ref[0, h, qsub * qb + j] for j in range(qsub)], axis=1)
        koff = pl.multiple_of(t * MLA_TK, MLA_TK)
        s = _dot(k_ref[0, h, pl.ds(koff, MLA_TK), :], q)
        s_scr[...] = s
        return jnp.max(s, axis=0, keepdims=True)

    def probs(i, m, mt):
        _, _, t = decode(i)
        m = jnp.where(t == 0, NEG, m)
        m_new = jnp.maximum(m, mt)
        p_scr[...] = jnp.exp2(s_scr[...] - m_new).astype(BF16)
        return m_new, jnp.exp2(m - m_new)

    def accumulate(i, alpha):
        h, qb, t = decode(i)
        vt = jnp.concatenate([vt_ref[0, h, ksub * t + j] for j in range(ksub)], axis=1)
        acc_scr[...] = acc_scr[...] * alpha + _dot(jnp.concatenate([vt, ones], axis=0), p_scr[...])
        acc = acc_scr[...]
        o_ref[0, h, qb] = (acc[0:MLA_V] * (1.0 / acc[MLA_V:MLA_V + 1])).astype(BF16)

    acc_scr[...] = jnp.zeros_like(acc_scr)
    m = jnp.full((1, MLA_TQ), NEG, F32)
    mt = scores(0)
    m, alpha = probs(0, m, mt)
    mt = scores(1)

    def body(i, carry):
        m, mt, alpha = carry
        accumulate(i - 2, alpha)
        m, alpha = probs(i - 1, m, mt)
        mt = scores(i)
        return m, mt, alpha

    m, mt, alpha = lax.fori_loop(2, total, body, (m, mt, alpha), unroll=2)
    accumulate(total - 2, alpha)
    m, alpha = probs(total - 1, m, mt)
    accumulate(total - 1, alpha)


def _mla_attention(qt, k, vt):
    b, _, nt, _, _ = qt.shape
    l = nt * TM
    nchunks = l // MLA_TK
    nqb = l // MLA_TQ
    assert MLA_TK % TM == 0 and MLA_TQ % TM == 0
    assert nqb & (nqb - 1) == 0 and nchunks & (nchunks - 1) == 0
    hb = max(1, min(MLA_HEADS, MLA_RESIDENT_TOKENS // l))
    assert hb * nqb * nchunks >= 2 and (hb * nqb * nchunks) % 2 == 0
    return pl.pallas_call(
        functools.partial(_mla_kernel, hb=hb, nqb=nqb, nchunks=nchunks),
        grid=(b, MLA_HEADS // hb),
        in_specs=[
            pl.BlockSpec((1, hb, nt, MLA_QK, TM), lambda bi, h: (bi, h, 0, 0, 0)),
            pl.BlockSpec((1, hb, l, MLA_QK), lambda bi, h: (bi, h, 0, 0)),
            pl.BlockSpec((1, hb, nt, MLA_V, TM), lambda bi, h: (bi, h, 0, 0, 0)),
        ],
        out_specs=pl.BlockSpec((1, hb, nqb, MLA_V, MLA_TQ), lambda bi, h: (bi, h, 0, 0, 0)),
        out_shape=jax.ShapeDtypeStruct((b, MLA_HEADS, nqb, MLA_V, MLA_TQ), BF16),
        scratch_shapes=[pltpu.VMEM((MLA_TK, MLA_TQ), F32),
                        pltpu.VMEM((MLA_TK, MLA_TQ), BF16),
                        pltpu.VMEM((MLA_V + ONES_ROWS, MLA_TQ), F32)],
        compiler_params=_params(("arbitrary", "arbitrary")),
        name="mla_attention",
    )(qt, k, vt)


def _odd_out_kernel(x_ref, ot_ref, sgt_ref, gm_ref, wt_ref, fg_ref, y_ref):
    ot = ot_ref[0, :, 0].reshape(ODD_MIX, TM)
    z = (ot.astype(F32) * sgt_ref[0].astype(F32)).astype(BF16)
    out = lax.dot_general(z, wt_ref[...], (((0,), (1,)), ((), ())), preferred_element_type=F32)
    x2 = x_ref[0] + gm_ref[0] * out
    ms = jnp.mean(x2 * x2, axis=-1, keepdims=True)
    y_ref[0] = (x2 * lax.rsqrt(ms + EPS)) * fg_ref[...]


def _odd_out(x, ot, sgt, gate_mod, w_out_t, final_g):
    b, l, _ = x.shape
    sub = MLA_TQ // TM
    return pl.pallas_call(
        _odd_out_kernel,
        grid=(b, l // TM),
        in_specs=[
            pl.BlockSpec((1, TM, D_MODEL), lambda bi, i: (bi, i, 0)),
            pl.BlockSpec((1, MLA_HEADS, 1, MLA_V, TM), lambda bi, i: (bi, 0, i // sub, 0, i % sub)),
            pl.BlockSpec((1, ODD_MIX, TM), lambda bi, i: (bi, 0, i)),
            pl.BlockSpec((1, 1, D_MODEL), lambda bi, i: (bi, 0, 0)),
            pl.BlockSpec((D_MODEL, ODD_MIX), lambda bi, i: (0, 0)),
            pl.BlockSpec((1, D_MODEL), lambda bi, i: (0, 0)),
        ],
        out_specs=pl.BlockSpec((1, TM, D_MODEL), lambda bi, i: (bi, i, 0)),
        out_shape=jax.ShapeDtypeStruct((b, l, D_MODEL), F32),
        compiler_params=_params(("arbitrary", "arbitrary")),
        name="odd_out",
    )(x, ot, sgt, gate_mod, w_out_t, final_g)


def _t5_bucket(rel):
    nb = T5_BUCKETS // 2
    ret = (rel > 0).astype(np.int32) * nb
    n = np.abs(rel)
    max_exact = nb // 2
    large = max_exact + (np.log(np.maximum(n, 1) / max_exact)
                         / np.log(T5_MAX_DIST / max_exact) * (nb - max_exact)).astype(np.int32)
    large = np.minimum(large, nb - 1)
    return ret + np.where(n < max_exact, n, large)


def _na_bias_table(rpb):
    kh = NA_WIN_H
    c = np.arange(GRID_W)
    col_start = np.clip(c - NA_WIN_W // 2, 0, GRID_W - NA_WIN_W)
    col_ok = (c[None, :] >= col_start[:, None]) & (c[None, :] < col_start[:, None] + NA_WIN_W)
    d_col = np.clip(c[None, :] - c[:, None], -(NA_WIN_W - 1), NA_WIN_W - 1) + NA_WIN_W - 1
    nrel = 2 * NA_WIN_W - 1
    rows = jnp.stack([rpb[:, NA_WIN_H - 1 - v:2 * NA_WIN_H - 1 - v] for v in range(kh)], axis=1)
    onehot = (d_col.reshape(-1)[None, :] == np.arange(nrel)[:, None]).astype(np.float32)
    bias = jnp.dot(rows.astype(F32).reshape(-1, nrel), jnp.asarray(onehot),
                   precision=lax.Precision.HIGHEST)
    bias = bias.reshape(NA_HEADS, kh, kh, GRID_W, GRID_W)
    bias = jnp.where(jnp.asarray(col_ok)[None, None, None, :, :], bias, NEG)
    bias = bias.transpose(1, 0, 3, 2, 4).reshape(kh, NA_HEADS // 2, 2 * GRID_W, kh * GRID_W)
    return bias


def _wb_bias_table(t5_bias):
    rel = (np.arange(3 * WB_BLOCK) - WB_BLOCK)[None, :] - np.arange(WB_BLOCK)[:, None]
    offs = np.arange(-(2 * WB_BLOCK - 1), 2 * WB_BLOCK + 1)
    period = 4 * WB_BLOCK
    by_off = t5_bias[_t5_bucket(offs)].astype(F32).T
    shifted = jnp.tile(by_off, (1, WB_BLOCK))[:, :WB_BLOCK * (period - 1)]
    shifted = shifted.reshape(WB_HEADS, WB_BLOCK, period - 1)
    bias = shifted[:, :, WB_BLOCK - 1:4 * WB_BLOCK - 1]
    bias = jnp.where(jnp.asarray(np.abs(rel) <= WB_WINDOW)[None], bias, NEG)
    bias = bias[np.asarray(WB_HEAD_ORDER)]
    return bias.reshape(WB_HEADS // 2, 2 * WB_BLOCK, 3 * WB_BLOCK)


def _even_weights(w_in, w_out):
    qa, ka, va, qb, kb, vb, gate = jnp.split(
        w_in, [NA_W, 2 * NA_W, 3 * NA_W, 3 * NA_W + WB_QW, 3 * NA_W + WB_QW + WB_KVW,
               3 * NA_W + WB_QW + 2 * WB_KVW], axis=-1)
    order = np.asarray(WB_HEAD_ORDER)
    perm = (order[:, None] * HEAD_DIM + np.arange(HEAD_DIM)[None, :]).reshape(-1)
    qscale = HEAD_DIM ** -0.5
    gate = jnp.concatenate([gate[:, :NA_W], gate[:, NA_W:][:, perm]], axis=-1)
    w = jnp.concatenate([gate, qa * qscale, ka, va, qb[:, perm] * qscale, kb, vb], axis=-1)
    w_out_p = jnp.concatenate([w_out[:NA_W], w_out[NA_W:][perm]], axis=0)
    return w.astype(BF16), w_out_p.astype(BF16)


def _odd_weights(w_in, q_norm, w_qb, kv_norm, w_kvb, w_out):
    q_lat, kv_lat, k_rope, gate = jnp.split(
        w_in, [MLA_Q_RANK, MLA_Q_RANK + MLA_KV_RANK, MLA_Q_RANK + MLA_KV_RANK + MLA_ROPE], axis=-1)
    w1t = jnp.concatenate([gate, q_lat, kv_lat], axis=-1).T.astype(BF16)
    half = MLA_ROPE // 2
    k_rot = jnp.concatenate([-k_rope[:, half:], k_rope[:, :half]], axis=-1)
    z64 = jnp.zeros((D_MODEL, MLA_NOPE), w_in.dtype)
    z32 = jnp.zeros((D_MODEL, LANES - MLA_QK), w_in.dtype)
    wnat = jnp.concatenate([kv_lat, z64, k_rope, z32, z64, k_rot, z32], axis=-1).astype(BF16)
    wkv = w_kvb.reshape(MLA_KV_RANK, MLA_HEADS, MLA_NOPE + MLA_V)
    wk = jnp.concatenate([wkv[:, :, :MLA_NOPE], jnp.zeros_like(wkv[:, :, :MLA_NOPE])], axis=-1)
    wk = wk.reshape(MLA_KV_RANK, MLA_HEADS * LANES).astype(BF16)
    wvt = wkv[:, :, MLA_NOPE:].reshape(MLA_KV_RANK, ODD_MIX).T.astype(BF16)
    gq = jnp.broadcast_to(q_norm.astype(F32)[:, None], (MLA_Q_RANK, LANES))
    gkvt = jnp.broadcast_to(kv_norm.astype(F32)[:, None], (MLA_KV_RANK, LANES))
    gkv = kv_norm.astype(F32)[None, :]
    return w1t, wnat, gq, gkvt, gkv, w_qb.T.astype(BF16), wk, wvt, w_out.T.astype(BF16)


def _rope_tables(l):
    inv_freq = 1.0 / (ROPE_THETA ** (jnp.arange(0, MLA_ROPE, 2, dtype=F32) / MLA_ROPE))
    ang = jnp.arange(l, dtype=F32)[:, None] * inv_freq[None, :]
    cos, sin = jnp.cos(ang), jnp.sin(ang)
    z64 = jnp.zeros((l, MLA_NOPE), F32)
    z32 = jnp.zeros((l, LANES - MLA_QK), F32)
    cos_p = jnp.concatenate([z64, cos, cos, z32], axis=-1)
    sin_p = jnp.concatenate([z64, sin, sin, z32], axis=-1)
    return cos.T, sin.T, cos_p, sin_p


def _trunk(x, mod, norm_g, ev, na_tbl, wb_tbl, sink, od, final_g):
    b, l, _ = x.shape
    assert l % TM == 0 and l % NA_CHUNK == 0 and l % WB_CHUNK == 0 and l % MLA_TK == 0
    assert l // GRID_W >= NA_ROWS_PER_STEP >= NA_HALO_ROWS >= NA_WIN_H
    shift0, scale0, gate0 = [t[:, None, :] for t in jnp.split(mod[0], 3, axis=-1)]
    shift1, scale1, gate1 = [t[:, None, :] for t in jnp.split(mod[1], 3, axis=-1)]

    w_in0, w_out0 = ev
    u = _even_front(x, norm_g[0][None, :], scale0, shift0, w_in0)
    oa = _na_attention(u, na_tbl)
    ob = _wb_attention(u, wb_tbl, sink)

    w1t, wnat, gq, gkvt, gkv, wqbt, wk, wvt, w_out1t = od
    cos_t, sin_t, cos_p, sin_p = _rope_tables(l)
    x1, k, qt, vt, sgt = _odd_front(x, oa, ob, u, gate0, w_out0, norm_g[1][None, :], scale1, shift1, w1t, wnat, gq, gkvt, gkv,
                                wqbt, wk, wvt, cos_t, sin_t, cos_p, sin_p)
    ot = _mla_attention(qt, k, vt)
    return _odd_out(x1, ot, sgt, gate1, w_out1t, final_g[None, :])


def kernel(x_prompt, x_sample, c_prompt, c_sample, ada_w, ada_b, norm_g, t5_bias, ev_w_in, na_rpb,
           wb_sink, ev_w_out, mla_w_in, mla_q_norm, mla_w_qb, mla_kv_norm, mla_w_kvb, mla_w_out, final_g):
    bp, bs = c_prompt.shape[0], c_sample.shape[0]
    rows = -(-(bp + bs) // 16) * 16
    c_pad = jnp.concatenate([c_prompt, c_sample, jnp.zeros((rows - bp - bs, D_MODEL), F32)], axis=0)
    mod = _ada_mod(c_pad, ada_w.astype(BF16), ada_b[:, None, :])

    ev = _even_weights(ev_w_in[0], ev_w_out[0])
    na_tbl = _na_bias_table(na_rpb[0])
    wb_tbl = _wb_bias_table(t5_bias)
    od = _odd_weights(mla_w_in[0], mla_q_norm[0], mla_w_qb[0], mla_kv_norm[0], mla_w_kvb[0], mla_w_out[0])
    sink = wb_sink[0].astype(F32)

    y_prompt = _trunk(x_prompt, mod[:, :bp], norm_g, ev, na_tbl, wb_tbl, sink, od, final_g)
    y_sample = _trunk(x_sample, mod[:, bp:bp + bs], norm_g, ev, na_tbl, wb_tbl, sink, od, final_g)
    return (y_prompt, y_sample)
```

```python
import functools
import math

import numpy as np
import jax
import jax.numpy as jnp
from jax import lax
from jax.experimental import pallas as pl
from jax.experimental.pallas import tpu as pltpu

D_MODEL = 1024
GRID_W = 64
HEAD_DIM = 64
EPS = 1e-6
NEG = -1e30
NA_HEADS = 8
NA_WIN_H = 8
NA_WIN_W = 16
WB_HEADS = 8
WB_KV_HEADS = 2
WB_WINDOW = 128
WB_BLOCK = 128
T5_BUCKETS = 32
T5_MAX_DIST = 128
MLA_HEADS = 16
MLA_Q_RANK = 256
MLA_KV_RANK = 128
MLA_NOPE = 64
MLA_ROPE = 32
MLA_V = 64
ROPE_THETA = 10000.0
MLA_QK = MLA_NOPE + MLA_ROPE

NA_W = NA_HEADS * HEAD_DIM
WB_QW = WB_HEADS * HEAD_DIM
WB_KVW = WB_KV_HEADS * HEAD_DIM
EVEN_MIX = NA_W + WB_QW
EVEN_IN = 3 * NA_W + WB_QW + 2 * WB_KVW + EVEN_MIX
ODD_MIX = MLA_HEADS * MLA_V

U_GATE = 0
U_QA = EVEN_MIX
U_KA = U_QA + NA_W
U_VA = U_KA + NA_W
U_QB = U_VA + NA_W
U_KB = U_QB + WB_QW
U_VB = U_KB + WB_KVW

LANES = 128
TM = 512
EVEN_TM = 1024
ADA_TN = 768
NA_ROWS_PER_STEP = 32
NA_HALO_ROWS = 8
NA_UNROLL = 10
WB_UNROLL = 7
NA_CHUNK = NA_ROWS_PER_STEP * GRID_W
WB_CHUNK = 2048
MLA_TQ = 1024
MLA_TK = 4096
MLA_RESIDENT_TOKENS = 16384
ONES_ROWS = 16
VMEM_LIMIT = 56 * 1024 * 1024

BF16 = jnp.bfloat16
F32 = jnp.float32
LOG2E = math.log2(math.e)

WB_HEAD_ORDER = (0, 4, 1, 5, 2, 6, 3, 7)


def _params(sem):
    return pltpu.CompilerParams(dimension_semantics=sem, vmem_limit_bytes=VMEM_LIMIT)


def _dot(a, b):
    return jnp.dot(a, b, preferred_element_type=F32)


def _dot_nt(a, b):
    return lax.dot_general(a, b, (((1,), (1,)), ((), ())), preferred_element_type=F32)


def _silu(x):
    return x * (1.0 / (1.0 + jnp.exp(-x)))


def _modulated_norm(x, g, scale, shift):
    ms = jnp.mean(x * x, axis=-1, keepdims=True)
    y = x * lax.rsqrt(ms + EPS)
    return (y * g) * (1.0 + scale) + shift


def _ada_kernel(c_ref, w_ref, b_ref, o_ref):
    c = c_ref[...]
    cs = _silu(c).astype(BF16)
    o_ref[0] = _dot(cs, w_ref[0]) + b_ref[0]


def _ada_mod(c_pad, ada_w, ada_b):
    depth = ada_w.shape[0]
    rows = c_pad.shape[0]
    tn = ADA_TN
    return pl.pallas_call(
        _ada_kernel,
        grid=(depth, 3 * D_MODEL // tn),
        in_specs=[
            pl.BlockSpec((rows, D_MODEL), lambda i, n: (0, 0)),
            pl.BlockSpec((1, D_MODEL, tn), lambda i, n: (i, 0, n)),
            pl.BlockSpec((1, 1, tn), lambda i, n: (i, 0, n)),
        ],
        out_specs=pl.BlockSpec((1, rows, tn), lambda i, n: (i, 0, n)),
        out_shape=jax.ShapeDtypeStruct((depth, rows, 3 * D_MODEL), F32),
        compiler_params=_params(("arbitrary", "arbitrary")),
        name="ada_mod",
    )(c_pad, ada_w, ada_b)


def _even_front_kernel(x_ref, g_ref, sc_ref, sh_ref, w_ref, u_ref):
    h = _modulated_norm(x_ref[0], g_ref[...], sc_ref[0], sh_ref[0]).astype(BF16)
    u_ref[0] = _dot(h, w_ref[...]).astype(BF16)


def _even_front(x, g, scale, shift, w):
    b, l, _ = x.shape
    TM = EVEN_TM
    return pl.pallas_call(
        _even_front_kernel,
        grid=(b, l // TM),
        in_specs=[
            pl.BlockSpec((1, TM, D_MODEL), lambda bi, i: (bi, i, 0)),
            pl.BlockSpec((1, D_MODEL), lambda bi, i: (0, 0)),
            pl.BlockSpec((1, 1, D_MODEL), lambda bi, i: (bi, 0, 0)),
            pl.BlockSpec((1, 1, D_MODEL), lambda bi, i: (bi, 0, 0)),
            pl.BlockSpec((D_MODEL, EVEN_IN), lambda bi, i: (0, 0)),
        ],
        out_specs=pl.BlockSpec((1, TM, EVEN_IN), lambda bi, i: (bi, i, 0)),
        out_shape=jax.ShapeDtypeStruct((b, l, EVEN_IN), BF16),
        compiler_params=_params(("arbitrary", "arbitrary")),
        name="even_front",
    )(x, g, scale, shift, w)


def _low_lane_mask(shape):
    return lax.broadcasted_iota(jnp.int32, shape, len(shape) - 1) < HEAD_DIM


def _na_kernel(q_ref, kp_ref, kc_ref, kn_ref, vp_ref, vc_ref, vn_ref, bias_ref, o_ref,
               kwin, vwin, s_scr, p_scr, *, rows):
    ci = pl.program_id(1)
    halo = NA_HALO_ROWS * GRID_W
    npair = NA_HEADS // 2
    kwin[0:halo] = kp_ref[0]
    kwin[halo:halo + NA_CHUNK] = kc_ref[0]
    kwin[halo + NA_CHUNK:2 * halo + NA_CHUNK] = kn_ref[0]
    for j in range(npair):
        lanes = slice(j * LANES, (j + 1) * LANES)
        dst = slice(2 * j * LANES, (2 * j + 1) * LANES)
        vwin[0:halo, dst] = vp_ref[0, :, lanes]
        vwin[halo:halo + NA_CHUNK, dst] = vc_ref[0, :, lanes]
        vwin[halo + NA_CHUNK:2 * halo + NA_CHUNK, dst] = vn_ref[0, :, lanes]
        vwin[:, (2 * j + 1) * LANES:(2 * j + 2) * LANES] = jnp.ones((2 * halo + NA_CHUNK, LANES), BF16)

    low = _low_lane_mask((GRID_W, LANES))
    kh = NA_WIN_H
    r0 = ci * NA_ROWS_PER_STEP

    def offsets(i):
        r = r0 + i
        row_start = jnp.clip(r - kh // 2, 0, rows - kh)
        variant = r - row_start
        start = pl.multiple_of((row_start - r0 + NA_HALO_ROWS) * GRID_W, GRID_W)
        return variant, start, pl.multiple_of(i * GRID_W, GRID_W)

    def scores(i):
        variant, start, qoff = offsets(i)
        for j in range(npair):
            lanes = slice(j * LANES, (j + 1) * LANES)
            q2 = q_ref[0, pl.ds(qoff, GRID_W), lanes]
            zero = jnp.zeros_like(q2)
            lhs = jnp.concatenate([jnp.where(low, q2, zero), jnp.where(low, zero, q2)], axis=0)
            kw = kwin[pl.ds(start, kh * GRID_W), lanes]
            s_scr[j] = _dot_nt(lhs, kw) + bias_ref[variant, j]

    def probs():
        for j in range(npair):
            s = s_scr[j]
            p_scr[j] = jnp.exp(s - jnp.max(s, axis=-1, keepdims=True)).astype(BF16)

    def output(i):
        _, start, qoff = offsets(i)
        for j in range(npair):
            vw = vwin[pl.ds(start, kh * GRID_W), 2 * j * LANES:(2 * j + 2) * LANES]
            res = _dot(p_scr[j], vw)
            o = res[:, 0:LANES] / res[:, LANES:2 * LANES]
            o_ref[0, pl.ds(qoff, GRID_W), j * LANES:(j + 1) * LANES] = (
                jnp.where(low, o[:GRID_W], o[GRID_W:]).astype(BF16))

    scores(0)
    probs()
    scores(1)

    def body(i, carry):
        output(i - 2)
        probs()
        scores(i)
        return carry

    lax.fori_loop(2, NA_ROWS_PER_STEP, body, 0, unroll=NA_UNROLL)
    output(NA_ROWS_PER_STEP - 2)
    probs()
    output(NA_ROWS_PER_STEP - 1)


def _na_attention(u, bias_tbl):
    b, l, _ = u.shape
    rows = l // GRID_W
    nchunk = l // NA_CHUNK
    halo = NA_HALO_ROWS * GRID_W
    per_chunk = NA_CHUNK // halo
    nhalo = l // halo
    qblk, kblk, vblk = U_QA // NA_W, U_KA // NA_W, U_VA // NA_W

    def window(col):
        return [
            pl.BlockSpec((1, halo, NA_W), lambda bi, i: (bi, jnp.maximum(i * per_chunk - 1, 0), col)),
            pl.BlockSpec((1, NA_CHUNK, NA_W), lambda bi, i: (bi, i, col)),
            pl.BlockSpec((1, halo, NA_W), lambda bi, i: (bi, jnp.minimum((i + 1) * per_chunk, nhalo - 1), col)),
        ]

    npair = NA_HEADS // 2
    return pl.pallas_call(
        functools.partial(_na_kernel, rows=rows),
        grid=(b, nchunk),
        in_specs=[pl.BlockSpec((1, NA_CHUNK, NA_W), lambda bi, i: (bi, i, qblk))]
        + window(kblk) + window(vblk)
        + [pl.BlockSpec(bias_tbl.shape, lambda bi, i: (0, 0, 0, 0))],
        out_specs=pl.BlockSpec((1, NA_CHUNK, NA_W), lambda bi, i: (bi, i, 0)),
        out_shape=jax.ShapeDtypeStruct((b, l, NA_W), BF16),
        scratch_shapes=[pltpu.VMEM((NA_CHUNK + 2 * halo, NA_W), BF16),
                        pltpu.VMEM((NA_CHUNK + 2 * halo, 2 * NA_W), BF16),
                        pltpu.VMEM((npair, 2 * GRID_W, NA_WIN_H * GRID_W), F32),
                        pltpu.VMEM((npair, 2 * GRID_W, NA_WIN_H * GRID_W), BF16)],
        compiler_params=_params(("arbitrary", "arbitrary")),
        name="na_attention",
    )(u, u, u, u, u, u, u, bias_tbl)


def _wb_kernel(sink_ref, q_ref, kp_ref, kc_ref, kn_ref, vp_ref, vc_ref, vn_ref, bias_ref, o_ref,
               kwin, vwin, s_scr, p_scr, ps_scr, *, nblk):
    ci = pl.program_id(1)
    npair = WB_HEADS // 2
    per_step = WB_CHUNK // WB_BLOCK
    win = WB_CHUNK + 2 * WB_BLOCK
    kwin[0:WB_BLOCK] = kp_ref[0]
    kwin[WB_BLOCK:WB_BLOCK + WB_CHUNK] = kc_ref[0]
    kwin[WB_BLOCK + WB_CHUNK:win] = kn_ref[0]
    vwin[0:WB_BLOCK, 0:LANES] = vp_ref[0]
    vwin[WB_BLOCK:WB_BLOCK + WB_CHUNK, 0:LANES] = vc_ref[0]
    vwin[WB_BLOCK + WB_CHUNK:win, 0:LANES] = vn_ref[0]
    vwin[:, LANES:2 * LANES] = jnp.ones((win, LANES), BF16)

    low = _low_lane_mask((WB_BLOCK, LANES))
    col = lax.broadcasted_iota(jnp.int32, (2 * WB_BLOCK, 3 * WB_BLOCK), 1)
    row = lax.broadcasted_iota(jnp.int32, (2 * WB_BLOCK, 1), 0)

    def scores(n):
        gblk = ci * per_step + n
        lo = jnp.where(gblk > 0, 0, WB_BLOCK)
        hi = jnp.where(gblk < nblk - 1, 3 * WB_BLOCK, 2 * WB_BLOCK)
        in_seq = jnp.logical_and(col >= lo, col < hi)
        off = pl.multiple_of(n * WB_BLOCK, WB_BLOCK)
        kw = kwin[pl.ds(off, 3 * WB_BLOCK), :]
        for j in range(npair):
            q2 = q_ref[0, pl.ds(off, WB_BLOCK), j * LANES:(j + 1) * LANES]
            zero = jnp.zeros_like(q2)
            lhs = jnp.concatenate([jnp.where(low, q2, zero), jnp.where(low, zero, q2)], axis=0)
            s_scr[j] = jnp.where(in_seq, _dot_nt(lhs, kw) + bias_ref[j], NEG)

    def probs():
        for j in range(npair):
            s = s_scr[j]
            sink = jnp.where(row < WB_BLOCK, sink_ref[WB_HEAD_ORDER[2 * j]],
                             sink_ref[WB_HEAD_ORDER[2 * j + 1]])
            m = jnp.maximum(jnp.max(s, axis=-1, keepdims=True), sink)
            p_scr[j] = jnp.exp(s - m).astype(BF16)
            ps_scr[j] = jnp.broadcast_to(jnp.exp(sink - m), (2 * WB_BLOCK, LANES))

    def output(n):
        off = pl.multiple_of(n * WB_BLOCK, WB_BLOCK)
        vw = vwin[pl.ds(off, 3 * WB_BLOCK), :]
        for j in range(npair):
            res = _dot(p_scr[j], vw)
            o = res[:, 0:LANES] / (res[:, LANES:2 * LANES] + ps_scr[j])
            o_ref[0, pl.ds(off, WB_BLOCK), j * LANES:(j + 1) * LANES] = (
                jnp.where(low, o[:WB_BLOCK], o[WB_BLOCK:]).astype(BF16))

    scores(0)
    probs()
    scores(1)

    def body(n, carry):
        output(n - 2)
        probs()
        scores(n)
        return carry

    lax.fori_loop(2, per_step, body, 0, unroll=WB_UNROLL)
    output(per_step - 2)
    probs()
    output(per_step - 1)


def _wb_attention(u, bias_tbl, sink):
    b, l, _ = u.shape
    nblk = l // WB_BLOCK
    nchunk = l // WB_CHUNK
    per_step = WB_CHUNK // WB_BLOCK
    qblk = U_QB // WB_QW
    kcol, vcol = U_KB // WB_KVW, U_VB // WB_KVW

    def halo(colblk, d):
        if d == 0:
            return pl.BlockSpec((1, WB_CHUNK, WB_KVW), lambda bi, i: (bi, i, colblk))
        return pl.BlockSpec(
            (1, WB_BLOCK, WB_KVW),
            lambda bi, i: (bi, jnp.clip(i * per_step + (per_step if d > 0 else -1), 0, nblk - 1), colblk))

    return pl.pallas_call(
        functools.partial(_wb_kernel, nblk=nblk),
        grid=(b, nchunk),
        in_specs=[
            pl.BlockSpec(memory_space=pltpu.SMEM),
            pl.BlockSpec((1, WB_CHUNK, WB_QW), lambda bi, i: (bi, i, qblk)),
            halo(kcol, -1), halo(kcol, 0), halo(kcol, 1),
            halo(vcol, -1), halo(vcol, 0), halo(vcol, 1),
            pl.BlockSpec(bias_tbl.shape, lambda bi, i: (0, 0, 0)),
        ],
        out_specs=pl.BlockSpec((1, WB_CHUNK, WB_QW), lambda bi, i: (bi, i, 0)),
        out_shape=jax.ShapeDtypeStruct((b, l, WB_QW), BF16),
        scratch_shapes=[pltpu.VMEM((WB_CHUNK + 2 * WB_BLOCK, WB_KVW), BF16),
                        pltpu.VMEM((WB_CHUNK + 2 * WB_BLOCK, 2 * WB_KVW), BF16),
                        pltpu.VMEM((WB_HEADS // 2, 2 * WB_BLOCK, 3 * WB_BLOCK), F32),
                        pltpu.VMEM((WB_HEADS // 2, 2 * WB_BLOCK, 3 * WB_BLOCK), BF16),
                        pltpu.VMEM((WB_HEADS // 2, 2 * WB_BLOCK, LANES), F32)],
        compiler_params=_params(("arbitrary", "arbitrary")),
        name="wb_attention",
    )(sink, u, u, u, u, u, u, u, bias_tbl)


def _even_out_tile(x_ref, oa_ref, ob_ref, gate_ref, gm_ref, w_ref):
    g = _silu(gate_ref[0].astype(F32))
    o = jnp.concatenate([oa_ref[0], ob_ref[0]], axis=-1).astype(F32)
    z = (o * g).astype(BF16)
    return x_ref[0] + gm_ref[0] * _dot(z, w_ref[...])


def _odd_front_kernel(x_ref, oa_ref, ob_ref, gate_ref, gm_ref, wo_ref,
                      g_ref, sc_ref, sh_ref, w1t_ref, wnat_ref, gq_ref, gkvt_ref, gkv_ref,
                      wqbt_ref, wk_ref, wvt_ref, cost_ref, sint_ref, cosp_ref, sinp_ref,
                      x1_ref, k_ref, qt_ref, vt_ref, sgt_ref):
    x1 = _even_out_tile(x_ref, oa_ref, ob_ref, gate_ref, gm_ref, wo_ref)
    x1_ref[0] = x1
    h = _modulated_norm(x1, g_ref[...], sc_ref[0], sh_ref[0]).astype(BF16)
    reps = TM // LANES

    ut = _dot_nt(w1t_ref[...], h)
    sgt_ref[0] = _silu(ut[0:ODD_MIX]).astype(BF16)

    qlt = ut[ODD_MIX:ODD_MIX + MLA_Q_RANK]
    qn = qlt * lax.rsqrt(jnp.mean(qlt * qlt, axis=0, keepdims=True) + EPS)
    qn = (qn * jnp.concatenate([gq_ref[...]] * reps, axis=1)).astype(BF16)
    qt = _dot(wqbt_ref[...], qn)
    qscale = (MLA_QK ** -0.5) * LOG2E
    cos_t = cost_ref[...]
    sin_t = sint_ref[...]
    half = MLA_ROPE // 2
    for hd in range(MLA_HEADS):
        base = hd * MLA_QK
        qt_ref[0, hd, 0, 0:MLA_NOPE, :] = (qt[base:base + MLA_NOPE] * qscale).astype(BF16)
        x1 = qt[base + MLA_NOPE:base + MLA_NOPE + half]
        x2 = qt[base + MLA_NOPE + half:base + MLA_QK]
        qt_ref[0, hd, 0, MLA_NOPE:MLA_NOPE + half, :] = ((x1 * cos_t - x2 * sin_t) * qscale).astype(BF16)
        qt_ref[0, hd, 0, MLA_NOPE + half:MLA_QK, :] = ((x2 * cos_t + x1 * sin_t) * qscale).astype(BF16)

    kvt = ut[ODD_MIX + MLA_Q_RANK:ODD_MIX + MLA_Q_RANK + MLA_KV_RANK]
    kvnt = kvt * lax.rsqrt(jnp.mean(kvt * kvt, axis=0, keepdims=True) + EPS)
    kvnt = (kvnt * jnp.concatenate([gkvt_ref[...]] * reps, axis=1)).astype(BF16)
    vt = _dot(wvt_ref[...], kvnt)
    vt_ref[0, :, 0] = vt.reshape(MLA_HEADS, MLA_V, TM).astype(BF16)

    nat = _dot(h, wnat_ref[...])
    kvl = nat[:, 0:MLA_KV_RANK]
    kvn = kvl * lax.rsqrt(jnp.mean(kvl * kvl, axis=-1, keepdims=True) + EPS)
    kvn = (kvn * gkv_ref[...]).astype(BF16)
    kn = _dot(kvn, wk_ref[...])
    kpe = nat[:, LANES:2 * LANES] * cosp_ref[...] + nat[:, 2 * LANES:3 * LANES] * sinp_ref[...]
    for hd in range(MLA_HEADS):
        kh = kn[:, hd * LANES:(hd + 1) * LANES] + kpe
        k_ref[0, hd] = kh[:, 0:MLA_QK].astype(BF16)


def _odd_front(x, oa, ob, u, gate_mod, w_out, g, scale, shift, w1t, wnat, gq, gkvt, gkv, wqbt, wk, wvt,
               cos_t, sin_t, cos_p, sin_p):
    b, l, _ = x.shape
    nt = l // TM
    const2 = lambda bi, i: (0, 0)
    n1 = w1t.shape[0]
    return pl.pallas_call(
        _odd_front_kernel,
        grid=(b, nt),
        in_specs=[
            pl.BlockSpec((1, TM, D_MODEL), lambda bi, i: (bi, i, 0)),
            pl.BlockSpec((1, TM, NA_W), lambda bi, i: (bi, i, 0)),
            pl.BlockSpec((1, TM, WB_QW), lambda bi, i: (bi, i, 0)),
            pl.BlockSpec((1, TM, EVEN_MIX), lambda bi, i: (bi, i, U_GATE // EVEN_MIX)),
            pl.BlockSpec((1, 1, D_MODEL), lambda bi, i: (bi, 0, 0)),
            pl.BlockSpec((EVEN_MIX, D_MODEL), const2),
            pl.BlockSpec((1, D_MODEL), const2),
            pl.BlockSpec((1, 1, D_MODEL), lambda bi, i: (bi, 0, 0)),
            pl.BlockSpec((1, 1, D_MODEL), lambda bi, i: (bi, 0, 0)),
            pl.BlockSpec((n1, D_MODEL), const2),
            pl.BlockSpec((D_MODEL, 3 * LANES), const2),
            pl.BlockSpec((MLA_Q_RANK, LANES), const2),
            pl.BlockSpec((MLA_KV_RANK, LANES), const2),
            pl.BlockSpec((1, MLA_KV_RANK), const2),
            pl.BlockSpec((MLA_HEADS * MLA_QK, MLA_Q_RANK), const2),
            pl.BlockSpec((MLA_KV_RANK, MLA_HEADS * LANES), const2),
            pl.BlockSpec((ODD_MIX, MLA_KV_RANK), const2),
            pl.BlockSpec((MLA_ROPE // 2, TM), lambda bi, i: (0, i)),
            pl.BlockSpec((MLA_ROPE // 2, TM), lambda bi, i: (0, i)),
            pl.BlockSpec((TM, LANES), lambda bi, i: (i, 0)),
            pl.BlockSpec((TM, LANES), lambda bi, i: (i, 0)),
        ],
        out_specs=[
            pl.BlockSpec((1, TM, D_MODEL), lambda bi, i: (bi, i, 0)),
            pl.BlockSpec((1, MLA_HEADS, TM, MLA_QK), lambda bi, i: (bi, 0, i, 0)),
            pl.BlockSpec((1, MLA_HEADS, 1, MLA_QK, TM), lambda bi, i: (bi, 0, i, 0, 0)),
            pl.BlockSpec((1, MLA_HEADS, 1, MLA_V, TM), lambda bi, i: (bi, 0, i, 0, 0)),
            pl.BlockSpec((1, ODD_MIX, TM), lambda bi, i: (bi, 0, i)),
        ],
        out_shape=[
            jax.ShapeDtypeStruct((b, l, D_MODEL), F32),
            jax.ShapeDtypeStruct((b, MLA_HEADS, l, MLA_QK), BF16),
            jax.ShapeDtypeStruct((b, MLA_HEADS, nt, MLA_QK, TM), BF16),
            jax.ShapeDtypeStruct((b, MLA_HEADS, nt, MLA_V, TM), BF16),
            jax.ShapeDtypeStruct((b, ODD_MIX, l), BF16),
        ],
        compiler_params=_params(("arbitrary", "arbitrary")),
        name="odd_front",
    )(x, oa, ob, u, gate_mod, w_out, g, scale, shift, w1t, wnat, gq, gkvt, gkv, wqbt, wk, wvt,
      cos_t, sin_t, cos_p, sin_p)


def _mla_kernel(qt_ref, k_ref, vt_ref, o_ref, s_scr, p_scr, acc_scr, *, hb, nqb, nchunks, tk):
    ones = jnp.ones((ONES_ROWS, tk), BF16)
    qsub = MLA_TQ // TM
    ksub = tk // TM
    chunk_bits = nchunks.bit_length() - 1
    qb_bits = nqb.bit_length() - 1
    total = hb * nqb * nchunks

    def decode(i):
        return i >> (chunk_bits + qb_bits), (i >> chunk_bits) & (nqb - 1), i & (nchunks - 1)

    def scores(i):
        h, qb, t = decode(i)
        q = jnp.concatenate([qt_ref[0, h, qsub * qb + j] for j in range(qsub)], axis=1)
        koff = pl.multiple_of(t * tk, tk)
        s = _dot(k_ref[0, h, pl.ds(koff, tk), :], q)
        s_scr[...] = s
        return jnp.max(s, axis=0, keepdims=True)

    def probs(i, m, mt):
        _, _, t = decode(i)
        m = jnp.where(t == 0, NEG, m)
        m_new = jnp.maximum(m, mt)
        p_scr[...] = jnp.exp2(s_scr[...] - m_new).astype(BF16)
        return m_new, jnp.exp2(m - m_new)

    def accumulate(i, alpha):
        h, qb, t = decode(i)
        vt = jnp.concatenate([vt_ref[0, h, ksub * t + j] for j in range(ksub)], axis=1)
        acc_scr[...] = acc_scr[...] * alpha + _dot(jnp.concatenate([vt, ones], axis=0), p_scr[...])
        acc = acc_scr[...]
        o_ref[0, h, qb] = (acc[0:MLA_V] * (1.0 / acc[MLA_V:MLA_V + 1])).astype(BF16)

    acc_scr[...] = jnp.zeros_like(acc_scr)
    m = jnp.full((1, MLA_TQ), NEG, F32)
    mt = scores(0)
    m, alpha = probs(0, m, mt)
    mt = scores(1)

    def body(i, carry):
        m, mt, alpha = carry
        accumulate(i - 2, alpha)
        m, alpha = probs(i - 1, m, mt)
        mt = scores(i)
        return m, mt, alpha

    m, mt, alpha = lax.fori_loop(2, total, body, (m, mt, alpha), unroll=2)
    accumulate(total - 2, alpha)
    m, alpha = probs(total - 1, m, mt)
    accumulate(total - 1, alpha)


def _mla_attention(qt, k, vt):
    b, _, nt, _, _ = qt.shape
    l = nt * TM
    tk = min(l, MLA_TK)
    nchunks = l // tk
    nqb = l // MLA_TQ
    assert tk % TM == 0 and MLA_TQ % TM == 0 and l % tk == 0
    assert nqb & (nqb - 1) == 0 and nchunks & (nchunks - 1) == 0
    hb = max(1, min(MLA_HEADS, MLA_RESIDENT_TOKENS // l))
    assert hb * nqb * nchunks >= 2 and (hb * nqb * nchunks) % 2 == 0
    return pl.pallas_call(
        functools.partial(_mla_kernel, hb=hb, nqb=nqb, nchunks=nchunks, tk=tk),
        grid=(b, MLA_HEADS // hb),
        in_specs=[
            pl.BlockSpec((1, hb, nt, MLA_QK, TM), lambda bi, h: (bi, h, 0, 0, 0)),
            pl.BlockSpec((1, hb, l, MLA_QK), lambda bi, h: (bi, h, 0, 0)),
            pl.BlockSpec((1, hb, nt, MLA_V, TM), lambda bi, h: (bi, h, 0, 0, 0)),
        ],
        out_specs=pl.BlockSpec((1, hb, nqb, MLA_V, MLA_TQ), lambda bi, h: (bi, h, 0, 0, 0)),
        out_shape=jax.ShapeDtypeStruct((b, MLA_HEADS, nqb, MLA_V, MLA_TQ), BF16),
        scratch_shapes=[pltpu.VMEM((tk, MLA_TQ), F32),
                        pltpu.VMEM((tk, MLA_TQ), BF16),
                        pltpu.VMEM((MLA_V + ONES_ROWS, MLA_TQ), F32)],
        compiler_params=_params(("arbitrary", "arbitrary")),
        name="mla_attention",
    )(qt, k, vt)


def _odd_out_kernel(x_ref, ot_ref, sgt_ref, gm_ref, wt_ref, fg_ref, y_ref):
    ot = ot_ref[0, :, 0].reshape(ODD_MIX, TM)
    z = (ot.astype(F32) * sgt_ref[0].astype(F32)).astype(BF16)
    out = lax.dot_general(z, wt_ref[...], (((0,), (1,)), ((), ())), preferred_element_type=F32)
    x2 = x_ref[0] + gm_ref[0] * out
    ms = jnp.mean(x2 * x2, axis=-1, keepdims=True)
    y_ref[0] = (x2 * lax.rsqrt(ms + EPS)) * fg_ref[...]


def _odd_out(x, ot, sgt, gate_mod, w_out_t, final_g):
    b, l, _ = x.shape
    sub = MLA_TQ // TM
    return pl.pallas_call(
        _odd_out_kernel,
        grid=(b, l // TM),
        in_specs=[
            pl.BlockSpec((1, TM, D_MODEL), lambda bi, i: (bi, i, 0)),
            pl.BlockSpec((1, MLA_HEADS, 1, MLA_V, TM), lambda bi, i: (bi, 0, i // sub, 0, i % sub)),
            pl.BlockSpec((1, ODD_MIX, TM), lambda bi, i: (bi, 0, i)),
            pl.BlockSpec((1, 1, D_MODEL), lambda bi, i: (bi, 0, 0)),
            pl.BlockSpec((D_MODEL, ODD_MIX), lambda bi, i: (0, 0)),
            pl.BlockSpec((1, D_MODEL), lambda bi, i: (0, 0)),
        ],
        out_specs=pl.BlockSpec((1, TM, D_MODEL), lambda bi, i: (bi, i, 0)),
        out_shape=jax.ShapeDtypeStruct((b, l, D_MODEL), F32),
        compiler_params=_params(("arbitrary", "arbitrary")),
        name="odd_out",
    )(x, ot, sgt, gate_mod, w_out_t, final_g)


def _t5_bucket(rel):
    nb = T5_BUCKETS // 2
    ret = (rel > 0).astype(np.int32) * nb
    n = np.abs(rel)
    max_exact = nb // 2
    large = max_exact + (np.log(np.maximum(n, 1) / max_exact)
                         / np.log(T5_MAX_DIST / max_exact) * (nb - max_exact)).astype(np.int32)
    large = np.minimum(large, nb - 1)
    return ret + np.where(n < max_exact, n, large)


def _na_bias_table(rpb):
    kh = NA_WIN_H
    c = np.arange(GRID_W)
    col_start = np.clip(c - NA_WIN_W // 2, 0, GRID_W - NA_WIN_W)
    col_ok = (c[None, :] >= col_start[:, None]) & (c[None, :] < col_start[:, None] + NA_WIN_W)
    d_col = np.clip(c[None, :] - c[:, None], -(NA_WIN_W - 1), NA_WIN_W - 1) + NA_WIN_W - 1
    nrel = 2 * NA_WIN_W - 1
    rows = jnp.stack([rpb[:, NA_WIN_H - 1 - v:2 * NA_WIN_H - 1 - v] for v in range(kh)], axis=1)
    onehot = (d_col.reshape(-1)[None, :] == np.arange(nrel)[:, None]).astype(np.float32)
    bias = jnp.dot(rows.astype(F32).reshape(-1, nrel), jnp.asarray(onehot),
                   precision=lax.Precision.HIGHEST)
    bias = bias.reshape(NA_HEADS, kh, kh, GRID_W, GRID_W)
    bias = jnp.where(jnp.asarray(col_ok)[None, None, None, :, :], bias, NEG)
    bias = bias.transpose(1, 0, 3, 2, 4).reshape(kh, NA_HEADS // 2, 2 * GRID_W, kh * GRID_W)
    return bias


def _wb_bias_table(t5_bias):
    rel = (np.arange(3 * WB_BLOCK) - WB_BLOCK)[None, :] - np.arange(WB_BLOCK)[:, None]
    offs = np.arange(-(2 * WB_BLOCK - 1), 2 * WB_BLOCK + 1)
    period = 4 * WB_BLOCK
    by_off = t5_bias[_t5_bucket(offs)].astype(F32).T
    shifted = jnp.tile(by_off, (1, WB_BLOCK))[:, :WB_BLOCK * (period - 1)]
    shifted = shifted.reshape(WB_HEADS, WB_BLOCK, period - 1)
    bias = shifted[:, :, WB_BLOCK - 1:4 * WB_BLOCK - 1]
    bias = jnp.where(jnp.asarray(np.abs(rel) <= WB_WINDOW)[None], bias, NEG)
    bias = bias[np.asarray(WB_HEAD_ORDER)]
    return bias.reshape(WB_HEADS // 2, 2 * WB_BLOCK, 3 * WB_BLOCK)


def _even_weights(w_in, w_out):
    qa, ka, va, qb, kb, vb, gate = jnp.split(
        w_in, [NA_W, 2 * NA_W, 3 * NA_W, 3 * NA_W + WB_QW, 3 * NA_W + WB_QW + WB_KVW,
               3 * NA_W + WB_QW + 2 * WB_KVW], axis=-1)
    order = np.asarray(WB_HEAD_ORDER)
    perm = (order[:, None] * HEAD_DIM + np.arange(HEAD_DIM)[None, :]).reshape(-1)
    qscale = HEAD_DIM ** -0.5
    gate = jnp.concatenate([gate[:, :NA_W], gate[:, NA_W:][:, perm]], axis=-1)
    w = jnp.concatenate([gate, qa * qscale, ka, va, qb[:, perm] * qscale, kb, vb], axis=-1)
    w_out_p = jnp.concatenate([w_out[:NA_W], w_out[NA_W:][perm]], axis=0)
    return w.astype(BF16), w_out_p.astype(BF16)


def _odd_weights(w_in, q_norm, w_qb, kv_norm, w_kvb, w_out):
    q_lat, kv_lat, k_rope, gate = jnp.split(
        w_in, [MLA_Q_RANK, MLA_Q_RANK + MLA_KV_RANK, MLA_Q_RANK + MLA_KV_RANK + MLA_ROPE], axis=-1)
    w1t = jnp.concatenate([gate, q_lat, kv_lat], axis=-1).T.astype(BF16)
    half = MLA_ROPE // 2
    k_rot = jnp.concatenate([-k_rope[:, half:], k_rope[:, :half]], axis=-1)
    z64 = jnp.zeros((D_MODEL, MLA_NOPE), w_in.dtype)
    z32 = jnp.zeros((D_MODEL, LANES - MLA_QK), w_in.dtype)
    wnat = jnp.concatenate([kv_lat, z64, k_rope, z32, z64, k_rot, z32], axis=-1).astype(BF16)
    wkv = w_kvb.reshape(MLA_KV_RANK, MLA_HEADS, MLA_NOPE + MLA_V)
    wk = jnp.concatenate([wkv[:, :, :MLA_NOPE], jnp.zeros_like(wkv[:, :, :MLA_NOPE])], axis=-1)
    wk = wk.reshape(MLA_KV_RANK, MLA_HEADS * LANES).astype(BF16)
    wvt = wkv[:, :, MLA_NOPE:].reshape(MLA_KV_RANK, ODD_MIX).T.astype(BF16)
    gq = jnp.broadcast_to(q_norm.astype(F32)[:, None], (MLA_Q_RANK, LANES))
    gkvt = jnp.broadcast_to(kv_norm.astype(F32)[:, None], (MLA_KV_RANK, LANES))
    gkv = kv_norm.astype(F32)[None, :]
    return w1t, wnat, gq, gkvt, gkv, w_qb.T.astype(BF16), wk, wvt, w_out.T.astype(BF16)


def _rope_tables(l):
    inv_freq = 1.0 / (ROPE_THETA ** (jnp.arange(0, MLA_ROPE, 2, dtype=F32) / MLA_ROPE))
    ang = jnp.arange(l, dtype=F32)[:, None] * inv_freq[None, :]
    cos, sin = jnp.cos(ang), jnp.sin(ang)
    z64 = jnp.zeros((l, MLA_NOPE), F32)
    z32 = jnp.zeros((l, LANES - MLA_QK), F32)
    cos_p = jnp.concatenate([z64, cos, cos, z32], axis=-1)
    sin_p = jnp.concatenate([z64, sin, sin, z32], axis=-1)
    return cos.T, sin.T, cos_p, sin_p


def _trunk(x, mod, norm_g, ev, na_tbl, wb_tbl, sink, od, final_g):
    b, l, _ = x.shape
    assert l % TM == 0 and l % NA_CHUNK == 0 and l % WB_CHUNK == 0
    assert l // GRID_W >= NA_ROWS_PER_STEP >= NA_HALO_ROWS >= NA_WIN_H
    shift0, scale0, gate0 = [t[:, None, :] for t in jnp.split(mod[0], 3, axis=-1)]
    shift1, scale1, gate1 = [t[:, None, :] for t in jnp.split(mod[1], 3, axis=-1)]

    w_in0, w_out0 = ev
    u = _even_front(x, norm_g[0][None, :], scale0, shift0, w_in0)
    oa = _na_attention(u, na_tbl)
    ob = _wb_attention(u, wb_tbl, sink)

    w1t, wnat, gq, gkvt, gkv, wqbt, wk, wvt, w_out1t = od
    cos_t, sin_t, cos_p, sin_p = _rope_tables(l)
    x1, k, qt, vt, sgt = _odd_front(x, oa, ob, u, gate0, w_out0, norm_g[1][None, :], scale1, shift1, w1t, wnat, gq, gkvt, gkv,
                                wqbt, wk, wvt, cos_t, sin_t, cos_p, sin_p)
    ot = _mla_attention(qt, k, vt)
    return _odd_out(x1, ot, sgt, gate1, w_out1t, final_g[None, :])


def kernel(x_prompt, x_sample, c_prompt, c_sample, ada_w, ada_b, norm_g, t5_bias, ev_w_in, na_rpb,
           wb_sink, ev_w_out, mla_w_in, mla_q_norm, mla_w_qb, mla_kv_norm, mla_w_kvb, mla_w_out, final_g):
    bp, bs = c_prompt.shape[0], c_sample.shape[0]
    rows = -(-(bp + bs) // 16) * 16
    c_pad = jnp.concatenate([c_prompt, c_sample, jnp.zeros((rows - bp - bs, D_MODEL), F32)], axis=0)
    mod = _ada_mod(c_pad, ada_w.astype(BF16), ada_b[:, None, :])

    ev = _even_weights(ev_w_in[0], ev_w_out[0])
    na_tbl = _na_bias_table(na_rpb[0])
    wb_tbl = _wb_bias_table(t5_bias)
    od = _odd_weights(mla_w_in[0], mla_q_norm[0], mla_w_qb[0], mla_kv_norm[0], mla_w_kvb[0], mla_w_out[0])
    sink = wb_sink[0].astype(F32)

    y_prompt = _trunk(x_prompt, mod[:, :bp], norm_g, ev, na_tbl, wb_tbl, sink, od, final_g)
    y_sample = _trunk(x_sample, mod[:, bp:bp + bs], norm_g, ev, na_tbl, wb_tbl, sink, od, final_g)
    return (y_prompt, y_sample)
```

```python
import functools
import math

import numpy as np
import jax
import jax.numpy as jnp
from jax import lax
from jax.experimental import pallas as pl
from jax.experimental.pallas import tpu as pltpu

D_MODEL = 1024
GRID_W = 64
HEAD_DIM = 64
EPS = 1e-6
NEG = -1e30
NA_HEADS = 8
NA_WIN_H = 8
NA_WIN_W = 16
WB_HEADS = 8
WB_KV_HEADS = 2
WB_WINDOW = 128
WB_BLOCK = 128
T5_BUCKETS = 32
T5_MAX_DIST = 128
MLA_HEADS = 16
MLA_Q_RANK = 256
MLA_KV_RANK = 128
MLA_NOPE = 64
MLA_ROPE = 32
MLA_V = 64
ROPE_THETA = 10000.0
MLA_QK = MLA_NOPE + MLA_ROPE

NA_W = NA_HEADS * HEAD_DIM
WB_QW = WB_HEADS * HEAD_DIM
WB_KVW = WB_KV_HEADS * HEAD_DIM
EVEN_MIX = NA_W + WB_QW
EVEN_IN = 3 * NA_W + WB_QW + 2 * WB_KVW + EVEN_MIX
ODD_MIX = MLA_HEADS * MLA_V

U_GATE = 0
U_QA = EVEN_MIX
U_KA = U_QA + NA_W
U_VA = U_KA + NA_W
U_QB = U_VA + NA_W
U_KB = U_QB + WB_QW
U_VB = U_KB + WB_KVW

LANES = 128
TM = 512
EVEN_TM = 1024
ADA_TN = 768
NA_ROWS_PER_STEP = 32
NA_HALO_ROWS = 8
NA_UNROLL = 10
WB_UNROLL = 7
NA_CHUNK = NA_ROWS_PER_STEP * GRID_W
WB_CHUNK = 2048
MLA_TQ = 1024
MLA_TK = 4096
MLA_RESIDENT_TOKENS = 16384
ONES_ROWS = 16
VMEM_LIMIT = 56 * 1024 * 1024

BF16 = jnp.bfloat16
F32 = jnp.float32
LOG2E = math.log2(math.e)

WB_HEAD_ORDER = (0, 4, 1, 5, 2, 6, 3, 7)


def _params(sem):
    return pltpu.CompilerParams(dimension_semantics=sem, vmem_limit_bytes=VMEM_LIMIT)


def _dot(a, b):
    return jnp.dot(a, b, preferred_element_type=F32)


def _dot_nt(a, b):
    return lax.dot_general(a, b, (((1,), (1,)), ((), ())), preferred_element_type=F32)


def _silu(x):
    return x * (1.0 / (1.0 + jnp.exp(-x)))


def _modulated_norm(x, g, scale, shift):
    ms = jnp.mean(x * x, axis=-1, keepdims=True)
    y = x * lax.rsqrt(ms + EPS)
    return (y * g) * (1.0 + scale) + shift


def _ada_kernel(c_ref, w_ref, b_ref, o_ref):
    c = c_ref[...]
    cs = _silu(c).astype(BF16)
    o_ref[0] = _dot(cs, w_ref[0]) + b_ref[0]


def _ada_mod(c_pad, ada_w, ada_b):
    depth = ada_w.shape[0]
    rows = c_pad.shape[0]
    tn = ADA_TN
    return pl.pallas_call(
        _ada_kernel,
        grid=(depth, 3 * D_MODEL // tn),
        in_specs=[
            pl.BlockSpec((rows, D_MODEL), lambda i, n: (0, 0)),
            pl.BlockSpec((1, D_MODEL, tn), lambda i, n: (i, 0, n)),
            pl.BlockSpec((1, 1, tn), lambda i, n: (i, 0, n)),
        ],
        out_specs=pl.BlockSpec((1, rows, tn), lambda i, n: (i, 0, n)),
        out_shape=jax.ShapeDtypeStruct((depth, rows, 3 * D_MODEL), F32),
        compiler_params=_params(("arbitrary", "arbitrary")),
        name="ada_mod",
    )(c_pad, ada_w, ada_b)


def _even_front_kernel(x_ref, g_ref, sc_ref, sh_ref, w_ref, u_ref):
    h = _modulated_norm(x_ref[0], g_ref[...], sc_ref[0], sh_ref[0]).astype(BF16)
    u_ref[0] = _dot(h, w_ref[...]).astype(BF16)


def _even_front(x, g, scale, shift, w):
    b, l, _ = x.shape
    TM = EVEN_TM
    return pl.pallas_call(
        _even_front_kernel,
        grid=(b, l // TM),
        in_specs=[
            pl.BlockSpec((1, TM, D_MODEL), lambda bi, i: (bi, i, 0)),
            pl.BlockSpec((1, D_MODEL), lambda bi, i: (0, 0)),
            pl.BlockSpec((1, 1, D_MODEL), lambda bi, i: (bi, 0, 0)),
            pl.BlockSpec((1, 1, D_MODEL), lambda bi, i: (bi, 0, 0)),
            pl.BlockSpec((D_MODEL, EVEN_IN), lambda bi, i: (0, 0)),
        ],
        out_specs=pl.BlockSpec((1, TM, EVEN_IN), lambda bi, i: (bi, i, 0)),
        out_shape=jax.ShapeDtypeStruct((b, l, EVEN_IN), BF16),
        compiler_params=_params(("arbitrary", "arbitrary")),
        name="even_front",
    )(x, g, scale, shift, w)


def _low_lane_mask(shape):
    return lax.broadcasted_iota(jnp.int32, shape, len(shape) - 1) < HEAD_DIM


def _na_kernel(q_ref, kp_ref, kc_ref, kn_ref, vp_ref, vc_ref, vn_ref, bias_ref, o_ref,
               kwin, vwin, s_scr, p_scr, *, rows):
    ci = pl.program_id(1)
    halo = NA_HALO_ROWS * GRID_W
    npair = NA_HEADS // 2
    kwin[0:halo] = kp_ref[0]
    kwin[halo:halo + NA_CHUNK] = kc_ref[0]
    kwin[halo + NA_CHUNK:2 * halo + NA_CHUNK] = kn_ref[0]
    for j in range(npair):
        lanes = slice(j * LANES, (j + 1) * LANES)
        dst = slice(2 * j * LANES, (2 * j + 1) * LANES)
        vwin[0:halo, dst] = vp_ref[0, :, lanes]
        vwin[halo:halo + NA_CHUNK, dst] = vc_ref[0, :, lanes]
        vwin[halo + NA_CHUNK:2 * halo + NA_CHUNK, dst] = vn_ref[0, :, lanes]
        vwin[:, (2 * j + 1) * LANES:(2 * j + 2) * LANES] = jnp.ones((2 * halo + NA_CHUNK, LANES), BF16)

    low = _low_lane_mask((GRID_W, LANES))
    kh = NA_WIN_H
    r0 = ci * NA_ROWS_PER_STEP

    def offsets(i):
        r = r0 + i
        row_start = jnp.clip(r - kh // 2, 0, rows - kh)
        variant = r - row_start
        start = pl.multiple_of((row_start - r0 + NA_HALO_ROWS) * GRID_W, GRID_W)
        return variant, start, pl.multiple_of(i * GRID_W, GRID_W)

    def scores(i):
        variant, start, qoff = offsets(i)
        for j in range(npair):
            lanes = slice(j * LANES, (j + 1) * LANES)
            q2 = q_ref[0, pl.ds(qoff, GRID_W), lanes]
            zero = jnp.zeros_like(q2)
            lhs = jnp.concatenate([jnp.where(low, q2, zero), jnp.where(low, zero, q2)], axis=0)
            kw = kwin[pl.ds(start, kh * GRID_W), lanes]
            s_scr[j] = _dot_nt(lhs, kw) + bias_ref[variant, j]

    def probs():
        for j in range(npair):
            s = s_scr[j]
            p_scr[j] = jnp.exp(s - jnp.max(s, axis=-1, keepdims=True)).astype(BF16)

    def output(i):
        _, start, qoff = offsets(i)
        for j in range(npair):
            vw = vwin[pl.ds(start, kh * GRID_W), 2 * j * LANES:(2 * j + 2) * LANES]
            res = _dot(p_scr[j], vw)
            o = res[:, 0:LANES] / res[:, LANES:2 * LANES]
            o_ref[0, pl.ds(qoff, GRID_W), j * LANES:(j + 1) * LANES] = (
                jnp.where(low, o[:GRID_W], o[GRID_W:]).astype(BF16))

    scores(0)
    probs()
    scores(1)

    def body(i, carry):
        output(i - 2)
        probs()
        scores(i)
        return carry

    lax.fori_loop(2, NA_ROWS_PER_STEP, body, 0, unroll=NA_UNROLL)
    output(NA_ROWS_PER_STEP - 2)
    probs()
    output(NA_ROWS_PER_STEP - 1)


def _na_attention(u, bias_tbl):
    b, l, _ = u.shape
    rows = l // GRID_W
    nchunk = l // NA_CHUNK
    halo = NA_HALO_ROWS * GRID_W
    per_chunk = NA_CHUNK // halo
    nhalo = l // halo
    qblk, kblk, vblk = U_QA // NA_W, U_KA // NA_W, U_VA // NA_W

    def window(col):
        return [
            pl.BlockSpec((1, halo, NA_W), lambda bi, i: (bi, jnp.maximum(i * per_chunk - 1, 0), col)),
            pl.BlockSpec((1, NA_CHUNK, NA_W), lambda bi, i: (bi, i, col)),
            pl.BlockSpec((1, halo, NA_W), lambda bi, i: (bi, jnp.minimum((i + 1) * per_chunk, nhalo - 1), col)),
        ]

    npair = NA_HEADS // 2
    return pl.pallas_call(
        functools.partial(_na_kernel, rows=rows),
        grid=(b, nchunk),
        in_specs=[pl.BlockSpec((1, NA_CHUNK, NA_W), lambda bi, i: (bi, i, qblk))]
        + window(kblk) + window(vblk)
        + [pl.BlockSpec(bias_tbl.shape, lambda bi, i: (0, 0, 0, 0))],
        out_specs=pl.BlockSpec((1, NA_CHUNK, NA_W), lambda bi, i: (bi, i, 0)),
        out_shape=jax.ShapeDtypeStruct((b, l, NA_W), BF16),
        scratch_shapes=[pltpu.VMEM((NA_CHUNK + 2 * halo, NA_W), BF16),
                        pltpu.VMEM((NA_CHUNK + 2 * halo, 2 * NA_W), BF16),
                        pltpu.VMEM((npair, 2 * GRID_W, NA_WIN_H * GRID_W), F32),
                        pltpu.VMEM((npair, 2 * GRID_W, NA_WIN_H * GRID_W), BF16)],
        compiler_params=_params(("arbitrary", "arbitrary")),
        name="na_attention",
    )(u, u, u, u, u, u, u, bias_tbl)


def _wb_kernel(sink_ref, q_ref, kp_ref, kc_ref, kn_ref, vp_ref, vc_ref, vn_ref, bias_ref, o_ref,
               kwin, vwin, s_scr, p_scr, ps_scr, *, nblk):
    ci = pl.program_id(1)
    npair = WB_HEADS // 2
    per_step = WB_CHUNK // WB_BLOCK
    win = WB_CHUNK + 2 * WB_BLOCK
    kwin[0:WB_BLOCK] = kp_ref[0]
    kwin[WB_BLOCK:WB_BLOCK + WB_CHUNK] = kc_ref[0]
    kwin[WB_BLOCK + WB_CHUNK:win] = kn_ref[0]
    vwin[0:WB_BLOCK, 0:LANES] = vp_ref[0]
    vwin[WB_BLOCK:WB_BLOCK + WB_CHUNK, 0:LANES] = vc_ref[0]
    vwin[WB_BLOCK + WB_CHUNK:win, 0:LANES] = vn_ref[0]
    vwin[:, LANES:2 * LANES] = jnp.ones((win, LANES), BF16)

    low = _low_lane_mask((WB_BLOCK, LANES))
    col = lax.broadcasted_iota(jnp.int32, (2 * WB_BLOCK, 3 * WB_BLOCK), 1)
    row = lax.broadcasted_iota(jnp.int32, (2 * WB_BLOCK, 1), 0)

    def scores(n):
        gblk = ci * per_step + n
        lo = jnp.where(gblk > 0, 0, WB_BLOCK)
        hi = jnp.where(gblk < nblk - 1, 3 * WB_BLOCK, 2 * WB_BLOCK)
        in_seq = jnp.logical_and(col >= lo, col < hi)
        off = pl.multiple_of(n * WB_BLOCK, WB_BLOCK)
        kw = kwin[pl.ds(off, 3 * WB_BLOCK), :]
        lhs = []
        for j in range(npair):
            q2 = q_ref[0, pl.ds(off, WB_BLOCK), j * LANES:(j + 1) * LANES]
            zero = jnp.zeros_like(q2)
            lhs += [jnp.where(low, q2, zero), jnp.where(low, zero, q2)]
        s_all = _dot_nt(jnp.concatenate(lhs, axis=0), kw)
        for j in range(npair):
            rows = slice(2 * j * WB_BLOCK, 2 * (j + 1) * WB_BLOCK)
            s_scr[j] = jnp.where(in_seq, s_all[rows] + bias_ref[j], NEG)

    def probs():
        for j in range(npair):
            s = s_scr[j]
            sink = jnp.where(row < WB_BLOCK, sink_ref[WB_HEAD_ORDER[2 * j]],
                             sink_ref[WB_HEAD_ORDER[2 * j + 1]])
            m = jnp.maximum(jnp.max(s, axis=-1, keepdims=True), sink)
            p_scr[j] = jnp.exp(s - m).astype(BF16)
            ps_scr[j] = jnp.broadcast_to(jnp.exp(sink - m), (2 * WB_BLOCK, LANES))

    def output(n):
        off = pl.multiple_of(n * WB_BLOCK, WB_BLOCK)
        vw = vwin[pl.ds(off, 3 * WB_BLOCK), :]
        res_all = _dot(p_scr[...].reshape(npair * 2 * WB_BLOCK, 3 * WB_BLOCK), vw)
        for j in range(npair):
            res = res_all[2 * j * WB_BLOCK:2 * (j + 1) * WB_BLOCK]
            o = res[:, 0:LANES] / (res[:, LANES:2 * LANES] + ps_scr[j])
            o_ref[0, pl.ds(off, WB_BLOCK), j * LANES:(j + 1) * LANES] = (
                jnp.where(low, o[:WB_BLOCK], o[WB_BLOCK:]).astype(BF16))

    scores(0)
    probs()
    scores(1)

    def body(n, carry):
        output(n - 2)
        probs()
        scores(n)
        return carry

    lax.fori_loop(2, per_step, body, 0, unroll=WB_UNROLL)
    output(per_step - 2)
    probs()
    output(per_step - 1)


def _wb_attention(u, bias_tbl, sink):
    b, l, _ = u.shape
    nblk = l // WB_BLOCK
    nchunk = l // WB_CHUNK
    per_step = WB_CHUNK // WB_BLOCK
    qblk = U_QB // WB_QW
    kcol, vcol = U_KB // WB_KVW, U_VB // WB_KVW

    def halo(colblk, d):
        if d == 0:
            return pl.BlockSpec((1, WB_CHUNK, WB_KVW), lambda bi, i: (bi, i, colblk))
        return pl.BlockSpec(
            (1, WB_BLOCK, WB_KVW),
            lambda bi, i: (bi, jnp.clip(i * per_step + (per_step if d > 0 else -1), 0, nblk - 1), colblk))

    return pl.pallas_call(
        functools.partial(_wb_kernel, nblk=nblk),
        grid=(b, nchunk),
        in_specs=[
            pl.BlockSpec(memory_space=pltpu.SMEM),
            pl.BlockSpec((1, WB_CHUNK, WB_QW), lambda bi, i: (bi, i, qblk)),
            halo(kcol, -1), halo(kcol, 0), halo(kcol, 1),
            halo(vcol, -1), halo(vcol, 0), halo(vcol, 1),
            pl.BlockSpec(bias_tbl.shape, lambda bi, i: (0, 0, 0)),
        ],
        out_specs=pl.BlockSpec((1, WB_CHUNK, WB_QW), lambda bi, i: (bi, i, 0)),
        out_shape=jax.ShapeDtypeStruct((b, l, WB_QW), BF16),
        scratch_shapes=[pltpu.VMEM((WB_CHUNK + 2 * WB_BLOCK, WB_KVW), BF16),
                        pltpu.VMEM((WB_CHUNK + 2 * WB_BLOCK, 2 * WB_KVW), BF16),
                        pltpu.VMEM((WB_HEADS // 2, 2 * WB_BLOCK, 3 * WB_BLOCK), F32),
                        pltpu.VMEM((WB_HEADS // 2, 2 * WB_BLOCK, 3 * WB_BLOCK), BF16),
                        pltpu.VMEM((WB_HEADS // 2, 2 * WB_BLOCK, LANES), F32)],
        compiler_params=_params(("arbitrary", "arbitrary")),
        name="wb_attention",
    )(sink, u, u, u, u, u, u, u, bias_tbl)


def _even_out_tile(x_ref, oa_ref, ob_ref, gate_ref, gm_ref, w_ref):
    g = _silu(gate_ref[0].astype(F32))
    o = jnp.concatenate([oa_ref[0], ob_ref[0]], axis=-1).astype(F32)
    z = (o * g).astype(BF16)
    return x_ref[0] + gm_ref[0] * _dot(z, w_ref[...])


def _odd_front_kernel(x_ref, oa_ref, ob_ref, gate_ref, gm_ref, wo_ref,
                      g_ref, sc_ref, sh_ref, w1t_ref, wnat_ref, gq_ref, gkvt_ref, gkv_ref,
                      wqbt_ref, wk_ref, wvt_ref, cost_ref, sint_ref, cosp_ref, sinp_ref,
                      x1_ref, k_ref, qt_ref, vt_ref, sgt_ref):
    x1 = _even_out_tile(x_ref, oa_ref, ob_ref, gate_ref, gm_ref, wo_ref)
    x1_ref[0] = x1
    h = _modulated_norm(x1, g_ref[...], sc_ref[0], sh_ref[0]).astype(BF16)
    reps = TM // LANES

    ut = _dot_nt(w1t_ref[...], h)
    sgt_ref[0] = _silu(ut[0:ODD_MIX]).astype(BF16)

    qlt = ut[ODD_MIX:ODD_MIX + MLA_Q_RANK]
    qn = qlt * lax.rsqrt(jnp.mean(qlt * qlt, axis=0, keepdims=True) + EPS)
    qn = (qn * jnp.concatenate([gq_ref[...]] * reps, axis=1)).astype(BF16)
    qt = _dot(wqbt_ref[...], qn)
    qscale = (MLA_QK ** -0.5) * LOG2E
    cos_t = cost_ref[...]
    sin_t = sint_ref[...]
    half = MLA_ROPE // 2
    for hd in range(MLA_HEADS):
        base = hd * MLA_QK
        qt_ref[0, hd, 0, 0:MLA_NOPE, :] = (qt[base:base + MLA_NOPE] * qscale).astype(BF16)
        x1 = qt[base + MLA_NOPE:base + MLA_NOPE + half]
        x2 = qt[base + MLA_NOPE + half:base + MLA_QK]
        qt_ref[0, hd, 0, MLA_NOPE:MLA_NOPE + half, :] = ((x1 * cos_t - x2 * sin_t) * qscale).astype(BF16)
        qt_ref[0, hd, 0, MLA_NOPE + half:MLA_QK, :] = ((x2 * cos_t + x1 * sin_t) * qscale).astype(BF16)

    kvt = ut[ODD_MIX + MLA_Q_RANK:ODD_MIX + MLA_Q_RANK + MLA_KV_RANK]
    kvnt = kvt * lax.rsqrt(jnp.mean(kvt * kvt, axis=0, keepdims=True) + EPS)
    kvnt = (kvnt * jnp.concatenate([gkvt_ref[...]] * reps, axis=1)).astype(BF16)
    vt = _dot(wvt_ref[...], kvnt)
    vt_ref[0, :, 0] = vt.reshape(MLA_HEADS, MLA_V, TM).astype(BF16)

    nat = _dot(h, wnat_ref[...])
    kvl = nat[:, 0:MLA_KV_RANK]
    kvn = kvl * lax.rsqrt(jnp.mean(kvl * kvl, axis=-1, keepdims=True) + EPS)
    kvn = (kvn * gkv_ref[...]).astype(BF16)
    kn = _dot(kvn, wk_ref[...])
    kpe = nat[:, LANES:2 * LANES] * cosp_ref[...] + nat[:, 2 * LANES:3 * LANES] * sinp_ref[...]
    for hd in range(MLA_HEADS):
        kh = kn[:, hd * LANES:(hd + 1) * LANES] + kpe
        k_ref[0, hd] = kh[:, 0:MLA_QK].astype(BF16)


def _odd_front(x, oa, ob, u, gate_mod, w_out, g, scale, shift, w1t, wnat, gq, gkvt, gkv, wqbt, wk, wvt,
               cos_t, sin_t, cos_p, sin_p):
    b, l, _ = x.shape
    nt = l // TM
    const2 = lambda bi, i: (0, 0)
    n1 = w1t.shape[0]
    return pl.pallas_call(
        _odd_front_kernel,
        grid=(b, nt),
        in_specs=[
            pl.BlockSpec((1, TM, D_MODEL), lambda bi, i: (bi, i, 0)),
            pl.BlockSpec((1, TM, NA_W), lambda bi, i: (bi, i, 0)),
            pl.BlockSpec((1, TM, WB_QW), lambda bi, i: (bi, i, 0)),
            pl.BlockSpec((1, TM, EVEN_MIX), lambda bi, i: (bi, i, U_GATE // EVEN_MIX)),
            pl.BlockSpec((1, 1, D_MODEL), lambda bi, i: (bi, 0, 0)),
            pl.BlockSpec((EVEN_MIX, D_MODEL), const2),
            pl.BlockSpec((1, D_MODEL), const2),
            pl.BlockSpec((1, 1, D_MODEL), lambda bi, i: (bi, 0, 0)),
            pl.BlockSpec((1, 1, D_MODEL), lambda bi, i: (bi, 0, 0)),
            pl.BlockSpec((n1, D_MODEL), const2),
            pl.BlockSpec((D_MODEL, 3 * LANES), const2),
            pl.BlockSpec((MLA_Q_RANK, LANES), const2),
            pl.BlockSpec((MLA_KV_RANK, LANES), const2),
            pl.BlockSpec((1, MLA_KV_RANK), const2),
            pl.BlockSpec((MLA_HEADS * MLA_QK, MLA_Q_RANK), const2),
            pl.BlockSpec((MLA_KV_RANK, MLA_HEADS * LANES), const2),
            pl.BlockSpec((ODD_MIX, MLA_KV_RANK), const2),
            pl.BlockSpec((MLA_ROPE // 2, TM), lambda bi, i: (0, i)),
            pl.BlockSpec((MLA_ROPE // 2, TM), lambda bi, i: (0, i)),
            pl.BlockSpec((TM, LANES), lambda bi, i: (i, 0)),
            pl.BlockSpec((TM, LANES), lambda bi, i: (i, 0)),
        ],
        out_specs=[
            pl.BlockSpec((1, TM, D_MODEL), lambda bi, i: (bi, i, 0)),
            pl.BlockSpec((1, MLA_HEADS, TM, MLA_QK), lambda bi, i: (bi, 0, i, 0)),
            pl.BlockSpec((1, MLA_HEADS, 1, MLA_QK, TM), lambda bi, i: (bi, 0, i, 0, 0)),
            pl.BlockSpec((1, MLA_HEADS, 1, MLA_V, TM), lambda bi, i: (bi, 0, i, 0, 0)),
            pl.BlockSpec((1, ODD_MIX, TM), lambda bi, i: (bi, 0, i)),
        ],
        out_shape=[
            jax.ShapeDtypeStruct((b, l, D_MODEL), F32),
            jax.ShapeDtypeStruct((b, MLA_HEADS, l, MLA_QK), BF16),
            jax.ShapeDtypeStruct((b, MLA_HEADS, nt, MLA_QK, TM), BF16),
            jax.ShapeDtypeStruct((b, MLA_HEADS, nt, MLA_V, TM), BF16),
            jax.ShapeDtypeStruct((b, ODD_MIX, l), BF16),
        ],
        compiler_params=_params(("arbitrary", "arbitrary")),
        name="odd_front",
    )(x, oa, ob, u, gate_mod, w_out, g, scale, shift, w1t, wnat, gq, gkvt, gkv, wqbt, wk, wvt,
      cos_t, sin_t, cos_p, sin_p)


def _mla_kernel(qt_ref, k_ref, vt_ref, o_ref, s_scr, p_scr, acc_scr, *, hb, nqb, nchunks, tk):
    ones = jnp.ones((ONES_ROWS, tk), BF16)
    qsub = MLA_TQ // TM
    ksub = tk // TM
    chunk_bits = nchunks.bit_length() - 1
    qb_bits = nqb.bit_length() - 1
    total = hb * nqb * nchunks

    def decode(i):
        return i >> (chunk_bits + qb_bits), (i >> chunk_bits) & (nqb - 1), i & (nchunks - 1)

    def scores(i):
        h, qb, t = decode(i)
        q = jnp.concatenate([qt_ref[0, h, qsub * qb + j] for j in range(qsub)], axis=1)
        koff = pl.multiple_of(t * tk, tk)
        s = _dot(k_ref[0, h, pl.ds(koff, tk), :], q)
        s_scr[...] = s
        return jnp.max(s, axis=0, keepdims=True)

    def probs(i, m, mt):
        _, _, t = decode(i)
        m = jnp.where(t == 0, NEG, m)
        m_new = jnp.maximum(m, mt)
        p_scr[...] = jnp.exp2(s_scr[...] - m_new).astype(BF16)
        return m_new, jnp.exp2(m - m_new)

    def accumulate(i, alpha):
        h, qb, t = decode(i)
        vt = jnp.concatenate([vt_ref[0, h, ksub * t + j] for j in range(ksub)], axis=1)
        acc_scr[...] = acc_scr[...] * alpha + _dot(jnp.concatenate([vt, ones], axis=0), p_scr[...])
        acc = acc_scr[...]
        o_ref[0, h, qb] = (acc[0:MLA_V] * (1.0 / acc[MLA_V:MLA_V + 1])).astype(BF16)

    acc_scr[...] = jnp.zeros_like(acc_scr)
    m = jnp.full((1, MLA_TQ), NEG, F32)
    mt = scores(0)
    m, alpha = probs(0, m, mt)
    mt = scores(1)

    def body(i, carry):
        m, mt, alpha = carry
        accumulate(i - 2, alpha)
        m, alpha = probs(i - 1, m, mt)
        mt = scores(i)
        return m, mt, alpha

    m, mt, alpha = lax.fori_loop(2, total, body, (m, mt, alpha), unroll=2)
    accumulate(total - 2, alpha)
    m, alpha = probs(total - 1, m, mt)
    accumulate(total - 1, alpha)


def _mla_attention(qt, k, vt):
    b, _, nt, _, _ = qt.shape
    l = nt * TM
    tk = min(l, MLA_TK)
    nchunks = l // tk
    nqb = l // MLA_TQ
    assert tk % TM == 0 and MLA_TQ % TM == 0 and l % tk == 0
    assert nqb & (nqb - 1) == 0 and nchunks & (nchunks - 1) == 0
    hb = max(1, min(MLA_HEADS, MLA_RESIDENT_TOKENS // l))
    assert hb * nqb * nchunks >= 2 and (hb * nqb * nchunks) % 2 == 0
    return pl.pallas_call(
        functools.partial(_mla_kernel, hb=hb, nqb=nqb, nchunks=nchunks, tk=tk),
        grid=(b, MLA_HEADS // hb),
        in_specs=[
            pl.BlockSpec((1, hb, nt, MLA_QK, TM), lambda bi, h: (bi, h, 0, 0, 0)),
            pl.BlockSpec((1, hb, l, MLA_QK), lambda bi, h: (bi, h, 0, 0)),
            pl.BlockSpec((1, hb, nt, MLA_V, TM), lambda bi, h: (bi, h, 0, 0, 0)),
        ],
        out_specs=pl.BlockSpec((1, hb, nqb, MLA_V, MLA_TQ), lambda bi, h: (bi, h, 0, 0, 0)),
        out_shape=jax.ShapeDtypeStruct((b, MLA_HEADS, nqb, MLA_V, MLA_TQ), BF16),
        scratch_shapes=[pltpu.VMEM((tk, MLA_TQ), F32),
                        pltpu.VMEM((tk, MLA_TQ), BF16),
                        pltpu.VMEM((MLA_V + ONES_ROWS, MLA_TQ), F32)],
        compiler_params=_params(("arbitrary", "arbitrary")),
        name="mla_attention",
    )(qt, k, vt)


def _odd_out_kernel(x_ref, ot_ref, sgt_ref, gm_ref, wt_ref, fg_ref, y_ref):
    ot = ot_ref[0, :, 0].reshape(ODD_MIX, TM)
    z = (ot.astype(F32) * sgt_ref[0].astype(F32)).astype(BF16)
    out = lax.dot_general(z, wt_ref[...], (((0,), (1,)), ((), ())), preferred_element_type=F32)
    x2 = x_ref[0] + gm_ref[0] * out
    ms = jnp.mean(x2 * x2, axis=-1, keepdims=True)
    y_ref[0] = (x2 * lax.rsqrt(ms + EPS)) * fg_ref[...]


def _odd_out(x, ot, sgt, gate_mod, w_out_t, final_g):
    b, l, _ = x.shape
    sub = MLA_TQ // TM
    return pl.pallas_call(
        _odd_out_kernel,
        grid=(b, l // TM),
        in_specs=[
            pl.BlockSpec((1, TM, D_MODEL), lambda bi, i: (bi, i, 0)),
            pl.BlockSpec((1, MLA_HEADS, 1, MLA_V, TM), lambda bi, i: (bi, 0, i // sub, 0, i % sub)),
            pl.BlockSpec((1, ODD_MIX, TM), lambda bi, i: (bi, 0, i)),
            pl.BlockSpec((1, 1, D_MODEL), lambda bi, i: (bi, 0, 0)),
            pl.BlockSpec((D_MODEL, ODD_MIX), lambda bi, i: (0, 0)),
            pl.BlockSpec((1, D_MODEL), lambda bi, i: (0, 0)),
        ],
        out_specs=pl.BlockSpec((1, TM, D_MODEL), lambda bi, i: (bi, i, 0)),
        out_shape=jax.ShapeDtypeStruct((b, l, D_MODEL), F32),
        compiler_params=_params(("arbitrary", "arbitrary")),
        name="odd_out",
    )(x, ot, sgt, gate_mod, w_out_t, final_g)


def _t5_bucket(rel):
    nb = T5_BUCKETS // 2
    ret = (rel > 0).astype(np.int32) * nb
    n = np.abs(rel)
    max_exact = nb // 2
    large = max_exact + (np.log(np.maximum(n, 1) / max_exact)
                         / np.log(T5_MAX_DIST / max_exact) * (nb - max_exact)).astype(np.int32)
    large = np.minimum(large, nb - 1)
    return ret + np.where(n < max_exact, n, large)


def _na_bias_table(rpb):
    kh = NA_WIN_H
    c = np.arange(GRID_W)
    col_start = np.clip(c - NA_WIN_W // 2, 0, GRID_W - NA_WIN_W)
    col_ok = (c[None, :] >= col_start[:, None]) & (c[None, :] < col_start[:, None] + NA_WIN_W)
    d_col = np.clip(c[None, :] - c[:, None], -(NA_WIN_W - 1), NA_WIN_W - 1) + NA_WIN_W - 1
    nrel = 2 * NA_WIN_W - 1
    rows = jnp.stack([rpb[:, NA_WIN_H - 1 - v:2 * NA_WIN_H - 1 - v] for v in range(kh)], axis=1)
    onehot = (d_col.reshape(-1)[None, :] == np.arange(nrel)[:, None]).astype(np.float32)
    bias = jnp.dot(rows.astype(F32).reshape(-1, nrel), jnp.asarray(onehot),
                   precision=lax.Precision.HIGHEST)
    bias = bias.reshape(NA_HEADS, kh, kh, GRID_W, GRID_W)
    bias = jnp.where(jnp.asarray(col_ok)[None, None, None, :, :], bias, NEG)
    bias = bias.transpose(1, 0, 3, 2, 4).reshape(kh, NA_HEADS // 2, 2 * GRID_W, kh * GRID_W)
    return bias


def _wb_bias_table(t5_bias):
    rel = (np.arange(3 * WB_BLOCK) - WB_BLOCK)[None, :] - np.arange(WB_BLOCK)[:, None]
    offs = np.arange(-(2 * WB_BLOCK - 1), 2 * WB_BLOCK + 1)
    period = 4 * WB_BLOCK
    by_off = t5_bias[_t5_bucket(offs)].astype(F32).T
    shifted = jnp.tile(by_off, (1, WB_BLOCK))[:, :WB_BLOCK * (period - 1)]
    shifted = shifted.reshape(WB_HEADS, WB_BLOCK, period - 1)
    bias = shifted[:, :, WB_BLOCK - 1:4 * WB_BLOCK - 1]
    bias = jnp.where(jnp.asarray(np.abs(rel) <= WB_WINDOW)[None], bias, NEG)
    bias = bias[np.asarray(WB_HEAD_ORDER)]
    return bias.reshape(WB_HEADS // 2, 2 * WB_BLOCK, 3 * WB_BLOCK)


def _even_weights(w_in, w_out):
    qa, ka, va, qb, kb, vb, gate = jnp.split(
        w_in, [NA_W, 2 * NA_W, 3 * NA_W, 3 * NA_W + WB_QW, 3 * NA_W + WB_QW + WB_KVW,
               3 * NA_W + WB_QW + 2 * WB_KVW], axis=-1)
    order = np.asarray(WB_HEAD_ORDER)
    perm = (order[:, None] * HEAD_DIM + np.arange(HEAD_DIM)[None, :]).reshape(-1)
    qscale = HEAD_DIM ** -0.5
    gate = jnp.concatenate([gate[:, :NA_W], gate[:, NA_W:][:, perm]], axis=-1)
    w = jnp.concatenate([gate, qa * qscale, ka, va, qb[:, perm] * qscale, kb, vb], axis=-1)
    w_out_p = jnp.concatenate([w_out[:NA_W], w_out[NA_W:][perm]], axis=0)
    return w.astype(BF16), w_out_p.astype(BF16)


def _odd_weights(w_in, q_norm, w_qb, kv_norm, w_kvb, w_out):
    q_lat, kv_lat, k_rope, gate = jnp.split(
        w_in, [MLA_Q_RANK, MLA_Q_RANK + MLA_KV_RANK, MLA_Q_RANK + MLA_KV_RANK + MLA_ROPE], axis=-1)
    w1t = jnp.concatenate([gate, q_lat, kv_lat], axis=-1).T.astype(BF16)
    half = MLA_ROPE // 2
    k_rot = jnp.concatenate([-k_rope[:, half:], k_rope[:, :half]], axis=-1)
    z64 = jnp.zeros((D_MODEL, MLA_NOPE), w_in.dtype)
    z32 = jnp.zeros((D_MODEL, LANES - MLA_QK), w_in.dtype)
    wnat = jnp.concatenate([kv_lat, z64, k_rope, z32, z64, k_rot, z32], axis=-1).astype(BF16)
    wkv = w_kvb.reshape(MLA_KV_RANK, MLA_HEADS, MLA_NOPE + MLA_V)
    wk = jnp.concatenate([wkv[:, :, :MLA_NOPE], jnp.zeros_like(wkv[:, :, :MLA_NOPE])], axis=-1)
    wk = wk.reshape(MLA_KV_RANK, MLA_HEADS * LANES).astype(BF16)
    wvt = wkv[:, :, MLA_NOPE:].reshape(MLA_KV_RANK, ODD_MIX).T.astype(BF16)
    gq = jnp.broadcast_to(q_norm.astype(F32)[:, None], (MLA_Q_RANK, LANES))
    gkvt = jnp.broadcast_to(kv_norm.astype(F32)[:, None], (MLA_KV_RANK, LANES))
    gkv = kv_norm.astype(F32)[None, :]
    return w1t, wnat, gq, gkvt, gkv, w_qb.T.astype(BF16), wk, wvt, w_out.T.astype(BF16)


def _rope_tables(l):
    inv_freq = 1.0 / (ROPE_THETA ** (jnp.arange(0, MLA_ROPE, 2, dtype=F32) / MLA_ROPE))
    ang = jnp.arange(l, dtype=F32)[:, None] * inv_freq[None, :]
    cos, sin = jnp.cos(ang), jnp.sin(ang)
    z64 = jnp.zeros((l, MLA_NOPE), F32)
    z32 = jnp.zeros((l, LANES - MLA_QK), F32)
    cos_p = jnp.concatenate([z64, cos, cos, z32], axis=-1)
    sin_p = jnp.concatenate([z64, sin, sin, z32], axis=-1)
    return cos.T, sin.T, cos_p, sin_p


def _trunk(x, mod, norm_g, ev, na_tbl, wb_tbl, sink, od, final_g):
    b, l, _ = x.shape
    assert l % TM == 0 and l % NA_CHUNK == 0 and l % WB_CHUNK == 0
    assert l // GRID_W >= NA_ROWS_PER_STEP >= NA_HALO_ROWS >= NA_WIN_H
    shift0, scale0, gate0 = [t[:, None, :] for t in jnp.split(mod[0], 3, axis=-1)]
    shift1, scale1, gate1 = [t[:, None, :] for t in jnp.split(mod[1], 3, axis=-1)]

    w_in0, w_out0 = ev
    u = _even_front(x, norm_g[0][None, :], scale0, shift0, w_in0)
    oa = _na_attention(u, na_tbl)
    ob = _wb_attention(u, wb_tbl, sink)

    w1t, wnat, gq, gkvt, gkv, wqbt, wk, wvt, w_out1t = od
    cos_t, sin_t, cos_p, sin_p = _rope_tables(l)
    x1, k, qt, vt, sgt = _odd_front(x, oa, ob, u, gate0, w_out0, norm_g[1][None, :], scale1, shift1, w1t, wnat, gq, gkvt, gkv,
                                wqbt, wk, wvt, cos_t, sin_t, cos_p, sin_p)
    ot = _mla_attention(qt, k, vt)
    return _odd_out(x1, ot, sgt, gate1, w_out1t, final_g[None, :])


def kernel(x_prompt, x_sample, c_prompt, c_sample, ada_w, ada_b, norm_g, t5_bias, ev_w_in, na_rpb,
           wb_sink, ev_w_out, mla_w_in, mla_q_norm, mla_w_qb, mla_kv_norm, mla_w_kvb, mla_w_out, final_g):
    bp, bs = c_prompt.shape[0], c_sample.shape[0]
    rows = -(-(bp + bs) // 16) * 16
    c_pad = jnp.concatenate([c_prompt, c_sample, jnp.zeros((rows - bp - bs, D_MODEL), F32)], axis=0)
    mod = _ada_mod(c_pad, ada_w.astype(BF16), ada_b[:, None, :])

    ev = _even_weights(ev_w_in[0], ev_w_out[0])
    na_tbl = _na_bias_table(na_rpb[0])
    wb_tbl = _wb_bias_table(t5_bias)
    od = _odd_weights(mla_w_in[0], mla_q_norm[0], mla_w_qb[0], mla_kv_norm[0], mla_w_kvb[0], mla_w_out[0])
    sink = wb_sink[0].astype(F32)

    y_prompt = _trunk(x_prompt, mod[:, :bp], norm_g, ev, na_tbl, wb_tbl, sink, od, final_g)
    y_sample = _trunk(x_sample, mod[:, bp:bp + bs], norm_g, ev, na_tbl, wb_tbl, sink, od, final_g)
    return (y_prompt, y_sample)
```

```python
import functools
import math

import numpy as np
import jax
import jax.numpy as jnp
from jax import lax
from jax.experimental import pallas as pl
from jax.experimental.pallas import tpu as pltpu

D_MODEL = 1024
GRID_W = 64
HEAD_DIM = 64
EPS = 1e-6
NEG = -1e30
NA_HEADS = 8
NA_WIN_H = 8
NA_WIN_W = 16
WB_HEADS = 8
WB_KV_HEADS = 2
WB_WINDOW = 128
WB_BLOCK = 128
T5_BUCKETS = 32
T5_MAX_DIST = 128
MLA_HEADS = 16
MLA_Q_RANK = 256
MLA_KV_RANK = 128
MLA_NOPE = 64
MLA_ROPE = 32
MLA_V = 64
ROPE_THETA = 10000.0
MLA_QK = MLA_NOPE + MLA_ROPE

NA_W = NA_HEADS * HEAD_DIM
WB_QW = WB_HEADS * HEAD_DIM
WB_KVW = WB_KV_HEADS * HEAD_DIM
EVEN_MIX = NA_W + WB_QW
EVEN_IN = 3 * NA_W + WB_QW + 2 * WB_KVW + EVEN_MIX
ODD_MIX = MLA_HEADS * MLA_V

U_GATE = 0
U_QA = EVEN_MIX
U_KA = U_QA + NA_W
U_VA = U_KA + NA_W
U_QB = U_VA + NA_W
U_KB = U_QB + WB_QW
U_VB = U_KB + WB_KVW

LANES = 128
TM = 512
EVEN_TM = 1024
ADA_TN = 768
NA_ROWS_PER_STEP = 32
NA_HALO_ROWS = 8
NA_UNROLL = 10
WB_UNROLL = 7
NA_CHUNK = NA_ROWS_PER_STEP * GRID_W
WB_CHUNK = 2048
MLA_TQ = 1024
MLA_TK = 4096
MLA_RESIDENT_TOKENS = 16384
ONES_ROWS = 16
VMEM_LIMIT = 56 * 1024 * 1024

BF16 = jnp.bfloat16
F32 = jnp.float32
LOG2E = math.log2(math.e)

WB_HEAD_ORDER = (0, 4, 1, 5, 2, 6, 3, 7)


def _params(sem):
    return pltpu.CompilerParams(dimension_semantics=sem, vmem_limit_bytes=VMEM_LIMIT)


def _dot(a, b):
    return jnp.dot(a, b, preferred_element_type=F32)


def _dot_nt(a, b):
    return lax.dot_general(a, b, (((1,), (1,)), ((), ())), preferred_element_type=F32)


def _silu(x):
    return x * (1.0 / (1.0 + jnp.exp(-x)))


def _modulated_norm(x, g, scale, shift):
    ms = jnp.mean(x * x, axis=-1, keepdims=True)
    y = x * lax.rsqrt(ms + EPS)
    return (y * g) * (1.0 + scale) + shift


def _ada_kernel(c_ref, w_ref, b_ref, o_ref):
    c = c_ref[...]
    cs = _silu(c).astype(BF16)
    o_ref[0] = _dot(cs, w_ref[0]) + b_ref[0]


def _ada_mod(c_pad, ada_w, ada_b):
    depth = ada_w.shape[0]
    rows = c_pad.shape[0]
    tn = ADA_TN
    return pl.pallas_call(
        _ada_kernel,
        grid=(depth, 3 * D_MODEL // tn),
        in_specs=[
            pl.BlockSpec((rows, D_MODEL), lambda i, n: (0, 0)),
            pl.BlockSpec((1, D_MODEL, tn), lambda i, n: (i, 0, n)),
            pl.BlockSpec((1, 1, tn), lambda i, n: (i, 0, n)),
        ],
        out_specs=pl.BlockSpec((1, rows, tn), lambda i, n: (i, 0, n)),
        out_shape=jax.ShapeDtypeStruct((depth, rows, 3 * D_MODEL), F32),
        compiler_params=_params(("arbitrary", "arbitrary")),
        name="ada_mod",
    )(c_pad, ada_w, ada_b)


def _even_front_kernel(x_ref, g_ref, sc_ref, sh_ref, w_ref, u_ref):
    h = _modulated_norm(x_ref[0], g_ref[...], sc_ref[0], sh_ref[0]).astype(BF16)
    u_ref[0] = _dot(h, w_ref[...]).astype(BF16)


def _even_front(x, g, scale, shift, w):
    b, l, _ = x.shape
    TM = EVEN_TM
    return pl.pallas_call(
        _even_front_kernel,
        grid=(b, l // TM),
        in_specs=[
            pl.BlockSpec((1, TM, D_MODEL), lambda bi, i: (bi, i, 0)),
            pl.BlockSpec((1, D_MODEL), lambda bi, i: (0, 0)),
            pl.BlockSpec((1, 1, D_MODEL), lambda bi, i: (bi, 0, 0)),
            pl.BlockSpec((1, 1, D_MODEL), lambda bi, i: (bi, 0, 0)),
            pl.BlockSpec((D_MODEL, EVEN_IN), lambda bi, i: (0, 0)),
        ],
        out_specs=pl.BlockSpec((1, TM, EVEN_IN), lambda bi, i: (bi, i, 0)),
        out_shape=jax.ShapeDtypeStruct((b, l, EVEN_IN), BF16),
        compiler_params=_params(("arbitrary", "arbitrary")),
        name="even_front",
    )(x, g, scale, shift, w)


def _low_lane_mask(shape):
    return lax.broadcasted_iota(jnp.int32, shape, len(shape) - 1) < HEAD_DIM


def _na_kernel(q_ref, kp_ref, kc_ref, kn_ref, vp_ref, vc_ref, vn_ref, bias_ref, o_ref,
               kwin, vwin, s_scr, p_scr, *, rows):
    ci = pl.program_id(1)
    halo = NA_HALO_ROWS * GRID_W
    npair = NA_HEADS // 2
    kwin[0:halo] = kp_ref[0]
    kwin[halo:halo + NA_CHUNK] = kc_ref[0]
    kwin[halo + NA_CHUNK:2 * halo + NA_CHUNK] = kn_ref[0]
    for j in range(npair):
        lanes = slice(j * LANES, (j + 1) * LANES)
        dst = slice(2 * j * LANES, (2 * j + 1) * LANES)
        vwin[0:halo, dst] = vp_ref[0, :, lanes]
        vwin[halo:halo + NA_CHUNK, dst] = vc_ref[0, :, lanes]
        vwin[halo + NA_CHUNK:2 * halo + NA_CHUNK, dst] = vn_ref[0, :, lanes]
        vwin[:, (2 * j + 1) * LANES:(2 * j + 2) * LANES] = jnp.ones((2 * halo + NA_CHUNK, LANES), BF16)

    low = _low_lane_mask((GRID_W, LANES))
    kh = NA_WIN_H
    r0 = ci * NA_ROWS_PER_STEP

    def offsets(i):
        r = r0 + i
        row_start = jnp.clip(r - kh // 2, 0, rows - kh)
        variant = r - row_start
        start = pl.multiple_of((row_start - r0 + NA_HALO_ROWS) * GRID_W, GRID_W)
        return variant, start, pl.multiple_of(i * GRID_W, GRID_W)

    def scores(i):
        variant, start, qoff = offsets(i)
        for j in range(npair):
            lanes = slice(j * LANES, (j + 1) * LANES)
            q2 = q_ref[0, pl.ds(qoff, GRID_W), lanes]
            zero = jnp.zeros_like(q2)
            lhs = jnp.concatenate([jnp.where(low, q2, zero), jnp.where(low, zero, q2)], axis=0)
            kw = kwin[pl.ds(start, kh * GRID_W), lanes]
            s_scr[j] = _dot_nt(lhs, kw) + bias_ref[variant, j]

    def probs():
        for j in range(npair):
            s = s_scr[j]
            p_scr[j] = jnp.exp(s - jnp.max(s, axis=-1, keepdims=True)).astype(BF16)

    def output(i):
        _, start, qoff = offsets(i)
        for j in range(npair):
            vw = vwin[pl.ds(start, kh * GRID_W), 2 * j * LANES:(2 * j + 2) * LANES]
            res = _dot(p_scr[j], vw)
            o = res[:, 0:LANES] / res[:, LANES:2 * LANES]
            o_ref[0, pl.ds(qoff, GRID_W), j * LANES:(j + 1) * LANES] = (
                jnp.where(low, o[:GRID_W], o[GRID_W:]).astype(BF16))

    scores(0)
    probs()
    scores(1)

    def body(i, carry):
        output(i - 2)
        probs()
        scores(i)
        return carry

    lax.fori_loop(2, NA_ROWS_PER_STEP, body, 0, unroll=NA_UNROLL)
    output(NA_ROWS_PER_STEP - 2)
    probs()
    output(NA_ROWS_PER_STEP - 1)


def _na_attention(u, bias_tbl):
    b, l, _ = u.shape
    rows = l // GRID_W
    nchunk = l // NA_CHUNK
    halo = NA_HALO_ROWS * GRID_W
    per_chunk = NA_CHUNK // halo
    nhalo = l // halo
    qblk, kblk, vblk = U_QA // NA_W, U_KA // NA_W, U_VA // NA_W

    def window(col):
        return [
            pl.BlockSpec((1, halo, NA_W), lambda bi, i: (bi, jnp.maximum(i * per_chunk - 1, 0), col)),
            pl.BlockSpec((1, NA_CHUNK, NA_W), lambda bi, i: (bi, i, col)),
            pl.BlockSpec((1, halo, NA_W), lambda bi, i: (bi, jnp.minimum((i + 1) * per_chunk, nhalo - 1), col)),
        ]

    npair = NA_HEADS // 2
    return pl.pallas_call(
        functools.partial(_na_kernel, rows=rows),
        grid=(b, nchunk),
        in_specs=[pl.BlockSpec((1, NA_CHUNK, NA_W), lambda bi, i: (bi, i, qblk))]
        + window(kblk) + window(vblk)
        + [pl.BlockSpec(bias_tbl.shape, lambda bi, i: (0, 0, 0, 0))],
        out_specs=pl.BlockSpec((1, NA_CHUNK, NA_W), lambda bi, i: (bi, i, 0)),
        out_shape=jax.ShapeDtypeStruct((b, l, NA_W), BF16),
        scratch_shapes=[pltpu.VMEM((NA_CHUNK + 2 * halo, NA_W), BF16),
                        pltpu.VMEM((NA_CHUNK + 2 * halo, 2 * NA_W), BF16),
                        pltpu.VMEM((npair, 2 * GRID_W, NA_WIN_H * GRID_W), F32),
                        pltpu.VMEM((npair, 2 * GRID_W, NA_WIN_H * GRID_W), BF16)],
        compiler_params=_params(("arbitrary", "arbitrary")),
        name="na_attention",
    )(u, u, u, u, u, u, u, bias_tbl)


def _wb_kernel(sink_ref, q_ref, kp_ref, kc_ref, kn_ref, vp_ref, vc_ref, vn_ref, bias_ref, o_ref,
               kwin, vwin, s_scr, p_scr, ps_scr, *, nblk):
    ci = pl.program_id(1)
    npair = WB_HEADS // 2
    per_step = WB_CHUNK // WB_BLOCK
    win = WB_CHUNK + 2 * WB_BLOCK
    kwin[0:WB_BLOCK] = kp_ref[0]
    kwin[WB_BLOCK:WB_BLOCK + WB_CHUNK] = kc_ref[0]
    kwin[WB_BLOCK + WB_CHUNK:win] = kn_ref[0]
    vwin[0:WB_BLOCK, 0:LANES] = vp_ref[0]
    vwin[WB_BLOCK:WB_BLOCK + WB_CHUNK, 0:LANES] = vc_ref[0]
    vwin[WB_BLOCK + WB_CHUNK:win, 0:LANES] = vn_ref[0]
    vwin[:, LANES:2 * LANES] = jnp.ones((win, LANES), BF16)

    low = _low_lane_mask((WB_BLOCK, LANES))
    col = lax.broadcasted_iota(jnp.int32, (2 * WB_BLOCK, 3 * WB_BLOCK), 1)
    row = lax.broadcasted_iota(jnp.int32, (2 * WB_BLOCK, 1), 0)

    def scores(n):
        gblk = ci * per_step + n
        lo = jnp.where(gblk > 0, 0, WB_BLOCK)
        hi = jnp.where(gblk < nblk - 1, 3 * WB_BLOCK, 2 * WB_BLOCK)
        in_seq = jnp.logical_and(col >= lo, col < hi)
        off = pl.multiple_of(n * WB_BLOCK, WB_BLOCK)
        kw = kwin[pl.ds(off, 3 * WB_BLOCK), :]
        for j in range(npair):
            q2 = q_ref[0, pl.ds(off, WB_BLOCK), j * LANES:(j + 1) * LANES]
            zero = jnp.zeros_like(q2)
            lhs = jnp.concatenate([jnp.where(low, q2, zero), jnp.where(low, zero, q2)], axis=0)
            s_scr[j] = jnp.where(in_seq, _dot_nt(lhs, kw) + bias_ref[j], NEG)

    def probs():
        for j in range(npair):
            s = s_scr[j]
            sink = jnp.where(row < WB_BLOCK, sink_ref[WB_HEAD_ORDER[2 * j]],
                             sink_ref[WB_HEAD_ORDER[2 * j + 1]])
            m = jnp.maximum(jnp.max(s, axis=-1, keepdims=True), sink)
            p_scr[j] = jnp.exp(s - m).astype(BF16)
            ps_scr[j] = jnp.broadcast_to(jnp.exp(sink - m), (2 * WB_BLOCK, LANES))

    def output(n):
        off = pl.multiple_of(n * WB_BLOCK, WB_BLOCK)
        vw = vwin[pl.ds(off, 3 * WB_BLOCK), :]
        for j in range(npair):
            res = _dot(p_scr[j], vw)
            o = res[:, 0:LANES] / (res[:, LANES:2 * LANES] + ps_scr[j])
            o_ref[0, pl.ds(off, WB_BLOCK), j * LANES:(j + 1) * LANES] = (
                jnp.where(low, o[:WB_BLOCK], o[WB_BLOCK:]).astype(BF16))

    scores(0)
    probs()
    scores(1)

    def body(n, carry):
        output(n - 2)
        probs()
        scores(n)
        return carry

    lax.fori_loop(2, per_step, body, 0, unroll=WB_UNROLL)
    output(per_step - 2)
    probs()
    output(per_step - 1)


def _wb_attention(u, bias_tbl, sink):
    b, l, _ = u.shape
    nblk = l // WB_BLOCK
    nchunk = l // WB_CHUNK
    per_step = WB_CHUNK // WB_BLOCK
    qblk = U_QB // WB_QW
    kcol, vcol = U_KB // WB_KVW, U_VB // WB_KVW

    def halo(colblk, d):
        if d == 0:
            return pl.BlockSpec((1, WB_CHUNK, WB_KVW), lambda bi, i: (bi, i, colblk))
        return pl.BlockSpec(
            (1, WB_BLOCK, WB_KVW),
            lambda bi, i: (bi, jnp.clip(i * per_step + (per_step if d > 0 else -1), 0, nblk - 1), colblk))

    return pl.pallas_call(
        functools.partial(_wb_kernel, nblk=nblk),
        grid=(b, nchunk),
        in_specs=[
            pl.BlockSpec(memory_space=pltpu.SMEM),
            pl.BlockSpec((1, WB_CHUNK, WB_QW), lambda bi, i: (bi, i, qblk)),
            halo(kcol, -1), halo(kcol, 0), halo(kcol, 1),
            halo(vcol, -1), halo(vcol, 0), halo(vcol, 1),
            pl.BlockSpec(bias_tbl.shape, lambda bi, i: (0, 0, 0)),
        ],
        out_specs=pl.BlockSpec((1, WB_CHUNK, WB_QW), lambda bi, i: (bi, i, 0)),
        out_shape=jax.ShapeDtypeStruct((b, l, WB_QW), BF16),
        scratch_shapes=[pltpu.VMEM((WB_CHUNK + 2 * WB_BLOCK, WB_KVW), BF16),
                        pltpu.VMEM((WB_CHUNK + 2 * WB_BLOCK, 2 * WB_KVW), BF16),
                        pltpu.VMEM((WB_HEADS // 2, 2 * WB_BLOCK, 3 * WB_BLOCK), F32),
                        pltpu.VMEM((WB_HEADS // 2, 2 * WB_BLOCK, 3 * WB_BLOCK), BF16),
                        pltpu.VMEM((WB_HEADS // 2, 2 * WB_BLOCK, LANES), F32)],
        compiler_params=_params(("arbitrary", "arbitrary")),
        name="wb_attention",
    )(sink, u, u, u, u, u, u, u, bias_tbl)


def _even_out_tile(x_ref, oa_ref, ob_ref, gate_ref, gm_ref, w_ref):
    g = _silu(gate_ref[0].astype(F32))
    o = jnp.concatenate([oa_ref[0], ob_ref[0]], axis=-1).astype(F32)
    z = (o * g).astype(BF16)
    return x_ref[0] + gm_ref[0] * _dot(z, w_ref[...])


def _odd_front_kernel(x_ref, oa_ref, ob_ref, gate_ref, gm_ref, wo_ref,
                      g_ref, sc_ref, sh_ref, w1t_ref, wnat_ref, gq_ref, gkvt_ref, gkv_ref,
                      wqbt_ref, wk_ref, wvt_ref, cost_ref, sint_ref, cosp_ref, sinp_ref,
                      x1_ref, k_ref, qt_ref, vt_ref, sgt_ref):
    x1 = _even_out_tile(x_ref, oa_ref, ob_ref, gate_ref, gm_ref, wo_ref)
    x1_ref[0] = x1
    h = _modulated_norm(x1, g_ref[...], sc_ref[0], sh_ref[0]).astype(BF16)
    reps = TM // LANES

    ut = _dot_nt(w1t_ref[...], h)
    sgt_ref[0] = _silu(ut[0:ODD_MIX]).astype(BF16)

    qlt = ut[ODD_MIX:ODD_MIX + MLA_Q_RANK]
    qn = qlt * lax.rsqrt(jnp.mean(qlt * qlt, axis=0, keepdims=True) + EPS)
    qn = (qn * jnp.concatenate([gq_ref[...]] * reps, axis=1)).astype(BF16)
    qt = _dot(wqbt_ref[...], qn)
    qscale = (MLA_QK ** -0.5) * LOG2E
    cos_t = cost_ref[...]
    sin_t = sint_ref[...]
    half = MLA_ROPE // 2
    for hd in range(MLA_HEADS):
        base = hd * MLA_QK
        qt_ref[0, hd, 0, 0:MLA_NOPE, :] = (qt[base:base + MLA_NOPE] * qscale).astype(BF16)
        x1 = qt[base + MLA_NOPE:base + MLA_NOPE + half]
        x2 = qt[base + MLA_NOPE + half:base + MLA_QK]
        qt_ref[0, hd, 0, MLA_NOPE:MLA_NOPE + half, :] = ((x1 * cos_t - x2 * sin_t) * qscale).astype(BF16)
        qt_ref[0, hd, 0, MLA_NOPE + half:MLA_QK, :] = ((x2 * cos_t + x1 * sin_t) * qscale).astype(BF16)

    kvt = ut[ODD_MIX + MLA_Q_RANK:ODD_MIX + MLA_Q_RANK + MLA_KV_RANK]
    kvnt = kvt * lax.rsqrt(jnp.mean(kvt * kvt, axis=0, keepdims=True) + EPS)
    kvnt = (kvnt * jnp.concatenate([gkvt_ref[...]] * reps, axis=1)).astype(BF16)
    vt = _dot(wvt_ref[...], kvnt)
    vt_ref[0, :, 0] = vt.reshape(MLA_HEADS, MLA_V, TM).astype(BF16)

    nat = _dot(h, wnat_ref[...])
    kvl = nat[:, 0:MLA_KV_RANK]
    kvn = kvl * lax.rsqrt(jnp.mean(kvl * kvl, axis=-1, keepdims=True) + EPS)
    kvn = (kvn * gkv_ref[...]).astype(BF16)
    kn = _dot(kvn, wk_ref[...])
    kpe = nat[:, LANES:2 * LANES] * cosp_ref[...] + nat[:, 2 * LANES:3 * LANES] * sinp_ref[...]
    for hd in range(MLA_HEADS):
        kh = kn[:, hd * LANES:(hd + 1) * LANES] + kpe
        k_ref[0, hd] = kh[:, 0:MLA_QK].astype(BF16)


def _odd_front(x, oa, ob, u, gate_mod, w_out, g, scale, shift, w1t, wnat, gq, gkvt, gkv, wqbt, wk, wvt,
               cos_t, sin_t, cos_p, sin_p):
    b, l, _ = x.shape
    nt = l // TM
    const2 = lambda bi, i: (0, 0)
    n1 = w1t.shape[0]
    return pl.pallas_call(
        _odd_front_kernel,
        grid=(b, nt),
        in_specs=[
            pl.BlockSpec((1, TM, D_MODEL), lambda bi, i: (bi, i, 0)),
            pl.BlockSpec((1, TM, NA_W), lambda bi, i: (bi, i, 0)),
            pl.BlockSpec((1, TM, WB_QW), lambda bi, i: (bi, i, 0)),
            pl.BlockSpec((1, TM, EVEN_MIX), lambda bi, i: (bi, i, U_GATE // EVEN_MIX)),
            pl.BlockSpec((1, 1, D_MODEL), lambda bi, i: (bi, 0, 0)),
            pl.BlockSpec((EVEN_MIX, D_MODEL), const2),
            pl.BlockSpec((1, D_MODEL), const2),
            pl.BlockSpec((1, 1, D_MODEL), lambda bi, i: (bi, 0, 0)),
            pl.BlockSpec((1, 1, D_MODEL), lambda bi, i: (bi, 0, 0)),
            pl.BlockSpec((n1, D_MODEL), const2),
            pl.BlockSpec((D_MODEL, 3 * LANES), const2),
            pl.BlockSpec((MLA_Q_RANK, LANES), const2),
            pl.BlockSpec((MLA_KV_RANK, LANES), const2),
            pl.BlockSpec((1, MLA_KV_RANK), const2),
            pl.BlockSpec((MLA_HEADS * MLA_QK, MLA_Q_RANK), const2),
            pl.BlockSpec((MLA_KV_RANK, MLA_HEADS * LANES), const2),
            pl.BlockSpec((ODD_MIX, MLA_KV_RANK), const2),
            pl.BlockSpec((MLA_ROPE // 2, TM), lambda bi, i: (0, i)),
            pl.BlockSpec((MLA_ROPE // 2, TM), lambda bi, i: (0, i)),
            pl.BlockSpec((TM, LANES), lambda bi, i: (i, 0)),
            pl.BlockSpec((TM, LANES), lambda bi, i: (i, 0)),
        ],
        out_specs=[
            pl.BlockSpec((1, TM, D_MODEL), lambda bi, i: (bi, i, 0)),
            pl.BlockSpec((1, MLA_HEADS, TM, MLA_QK), lambda bi, i: (bi, 0, i, 0)),
            pl.BlockSpec((1, MLA_HEADS, 1, MLA_QK, TM), lambda bi, i: (bi, 0, i, 0, 0)),
            pl.BlockSpec((1, MLA_HEADS, 1, MLA_V, TM), lambda bi, i: (bi, 0, i, 0, 0)),
            pl.BlockSpec((1, ODD_MIX, TM), lambda bi, i: (bi, 0, i)),
        ],
        out_shape=[
            jax.ShapeDtypeStruct((b, l, D_MODEL), F32),
            jax.ShapeDtypeStruct((b, MLA_HEADS, l, MLA_QK), BF16),
            jax.ShapeDtypeStruct((b, MLA_HEADS, nt, MLA_QK, TM), BF16),
            jax.ShapeDtypeStruct((b, MLA_HEADS, nt, MLA_V, TM), BF16),
            jax.ShapeDtypeStruct((b, ODD_MIX, l), BF16),
        ],
        compiler_params=_params(("arbitrary", "arbitrary")),
        name="odd_front",
    )(x, oa, ob, u, gate_mod, w_out, g, scale, shift, w1t, wnat, gq, gkvt, gkv, wqbt, wk, wvt,
      cos_t, sin_t, cos_p, sin_p)


def _mla_kernel(qt_ref, k_ref, vt_ref, o_ref, s_scr, p_scr, acc_scr, *, hb, nqb, nchunks, tk):
    ones = jnp.ones((ONES_ROWS, tk), BF16)
    qsub = MLA_TQ // TM
    ksub = tk // TM
    chunk_bits = nchunks.bit_length() - 1
    qb_bits = nqb.bit_length() - 1
    total = hb * nqb * nchunks

    def decode(i):
        return i >> (chunk_bits + qb_bits), (i >> chunk_bits) & (nqb - 1), i & (nchunks - 1)

    def scores(i):
        h, qb, t = decode(i)
        q = jnp.concatenate([qt_ref[0, h, qsub * qb + j] for j in range(qsub)], axis=1)
        koff = pl.multiple_of(t * tk, tk)
        s = _dot(k_ref[0, h, pl.ds(koff, tk), :], q)
        s_scr[...] = s
        return jnp.max(s, axis=0, keepdims=True)

    def probs(i, m, mt):
        _, _, t = decode(i)
        m = jnp.where(t == 0, NEG, m)
        m_new = jnp.maximum(m, mt)
        p_scr[...] = jnp.exp2(s_scr[...] - m_new).astype(BF16)
        return m_new, jnp.exp2(m - m_new)

    def accumulate(i, alpha):
        h, qb, t = decode(i)
        vt = jnp.concatenate([vt_ref[0, h, ksub * t + j] for j in range(ksub)], axis=1)
        acc_scr[...] = acc_scr[...] * alpha + _dot(jnp.concatenate([vt, ones], axis=0), p_scr[...])
        acc = acc_scr[...]
        o_ref[0, h, qb] = (acc[0:MLA_V] * (1.0 / acc[MLA_V:MLA_V + 1])).astype(BF16)

    acc_scr[...] = jnp.zeros_like(acc_scr)
    m = jnp.full((1, MLA_TQ), NEG, F32)
    mt = scores(0)
    m, alpha = probs(0, m, mt)
    mt = scores(1)

    def body(i, carry):
        m, mt, alpha = carry
        accumulate(i - 2, alpha)
        m, alpha = probs(i - 1, m, mt)
        mt = scores(i)
        return m, mt, alpha

    m, mt, alpha = lax.fori_loop(2, total, body, (m, mt, alpha), unroll=2)
    accumulate(total - 2, alpha)
    m, alpha = probs(total - 1, m, mt)
    accumulate(total - 1, alpha)


def _mla_attention(qt, k, vt):
    b, _, nt, _, _ = qt.shape
    l = nt * TM
    tk = min(l, MLA_TK)
    nchunks = l // tk
    nqb = l // MLA_TQ
    assert tk % TM == 0 and MLA_TQ % TM == 0 and l % tk == 0
    assert nqb & (nqb - 1) == 0 and nchunks & (nchunks - 1) == 0
    hb = max(1, min(MLA_HEADS, MLA_RESIDENT_TOKENS // l))
    assert hb * nqb * nchunks >= 2 and (hb * nqb * nchunks) % 2 == 0
    return pl.pallas_call(
        functools.partial(_mla_kernel, hb=hb, nqb=nqb, nchunks=nchunks, tk=tk),
        grid=(b, MLA_HEADS // hb),
        in_specs=[
            pl.BlockSpec((1, hb, nt, MLA_QK, TM), lambda bi, h: (bi, h, 0, 0, 0)),
            pl.BlockSpec((1, hb, l, MLA_QK), lambda bi, h: (bi, h, 0, 0)),
            pl.BlockSpec((1, hb, nt, MLA_V, TM), lambda bi, h: (bi, h, 0, 0, 0)),
        ],
        out_specs=pl.BlockSpec((1, hb, nqb, MLA_V, MLA_TQ), lambda bi, h: (bi, h, 0, 0, 0)),
        out_shape=jax.ShapeDtypeStruct((b, MLA_HEADS, nqb, MLA_V, MLA_TQ), BF16),
        scratch_shapes=[pltpu.VMEM((tk, MLA_TQ), F32),
                        pltpu.VMEM((tk, MLA_TQ), BF16),
                        pltpu.VMEM((MLA_V + ONES_ROWS, MLA_TQ), F32)],
        compiler_params=_params(("arbitrary", "arbitrary")),
        name="mla_attention",
    )(qt, k, vt)


def _odd_out_kernel(x_ref, ot_ref, sgt_ref, gm_ref, wt_ref, fg_ref, y_ref):
    ot = ot_ref[0, :, 0]
    ot = ot.reshape(ODD_MIX, ot.shape[-1])
    z = (ot.astype(F32) * sgt_ref[0].astype(F32)).astype(BF16)
    out = lax.dot_general(z, wt_ref[...], (((0,), (1,)), ((), ())), preferred_element_type=F32)
    x2 = x_ref[0] + gm_ref[0] * out
    ms = jnp.mean(x2 * x2, axis=-1, keepdims=True)
    y_ref[0] = (x2 * lax.rsqrt(ms + EPS)) * fg_ref[...]


def _odd_out(x, ot, sgt, gate_mod, w_out_t, final_g):
    b, l, _ = x.shape
    TM = MLA_TQ
    sub = MLA_TQ // TM
    return pl.pallas_call(
        _odd_out_kernel,
        grid=(b, l // TM),
        in_specs=[
            pl.BlockSpec((1, TM, D_MODEL), lambda bi, i: (bi, i, 0)),
            pl.BlockSpec((1, MLA_HEADS, 1, MLA_V, TM), lambda bi, i: (bi, 0, i // sub, 0, i % sub)),
            pl.BlockSpec((1, ODD_MIX, TM), lambda bi, i: (bi, 0, i)),
            pl.BlockSpec((1, 1, D_MODEL), lambda bi, i: (bi, 0, 0)),
            pl.BlockSpec((D_MODEL, ODD_MIX), lambda bi, i: (0, 0)),
            pl.BlockSpec((1, D_MODEL), lambda bi, i: (0, 0)),
        ],
        out_specs=pl.BlockSpec((1, TM, D_MODEL), lambda bi, i: (bi, i, 0)),
        out_shape=jax.ShapeDtypeStruct((b, l, D_MODEL), F32),
        compiler_params=_params(("arbitrary", "arbitrary")),
        name="odd_out",
    )(x, ot, sgt, gate_mod, w_out_t, final_g)


def _t5_bucket(rel):
    nb = T5_BUCKETS // 2
    ret = (rel > 0).astype(np.int32) * nb
    n = np.abs(rel)
    max_exact = nb // 2
    large = max_exact + (np.log(np.maximum(n, 1) / max_exact)
                         / np.log(T5_MAX_DIST / max_exact) * (nb - max_exact)).astype(np.int32)
    large = np.minimum(large, nb - 1)
    return ret + np.where(n < max_exact, n, large)


def _na_bias_table(rpb):
    kh = NA_WIN_H
    c = np.arange(GRID_W)
    col_start = np.clip(c - NA_WIN_W // 2, 0, GRID_W - NA_WIN_W)
    col_ok = (c[None, :] >= col_start[:, None]) & (c[None, :] < col_start[:, None] + NA_WIN_W)
    d_col = np.clip(c[None, :] - c[:, None], -(NA_WIN_W - 1), NA_WIN_W - 1) + NA_WIN_W - 1
    nrel = 2 * NA_WIN_W - 1
    rows = jnp.stack([rpb[:, NA_WIN_H - 1 - v:2 * NA_WIN_H - 1 - v] for v in range(kh)], axis=1)
    onehot = (d_col.reshape(-1)[None, :] == np.arange(nrel)[:, None]).astype(np.float32)
    bias = jnp.dot(rows.astype(F32).reshape(-1, nrel), jnp.asarray(onehot),
                   precision=lax.Precision.HIGHEST)
    bias = bias.reshape(NA_HEADS, kh, kh, GRID_W, GRID_W)
    bias = jnp.where(jnp.asarray(col_ok)[None, None, None, :, :], bias, NEG)
    bias = bias.transpose(1, 0, 3, 2, 4).reshape(kh, NA_HEADS // 2, 2 * GRID_W, kh * GRID_W)
    return bias


def _wb_bias_table(t5_bias):
    rel = (np.arange(3 * WB_BLOCK) - WB_BLOCK)[None, :] - np.arange(WB_BLOCK)[:, None]
    offs = np.arange(-(2 * WB_BLOCK - 1), 2 * WB_BLOCK + 1)
    period = 4 * WB_BLOCK
    by_off = t5_bias[_t5_bucket(offs)].astype(F32).T
    shifted = jnp.tile(by_off, (1, WB_BLOCK))[:, :WB_BLOCK * (period - 1)]
    shifted = shifted.reshape(WB_HEADS, WB_BLOCK, period - 1)
    bias = shifted[:, :, WB_BLOCK - 1:4 * WB_BLOCK - 1]
    bias = jnp.where(jnp.asarray(np.abs(rel) <= WB_WINDOW)[None], bias, NEG)
    bias = bias[np.asarray(WB_HEAD_ORDER)]
    return bias.reshape(WB_HEADS // 2, 2 * WB_BLOCK, 3 * WB_BLOCK)


def _even_weights(w_in, w_out):
    qa, ka, va, qb, kb, vb, gate = jnp.split(
        w_in, [NA_W, 2 * NA_W, 3 * NA_W, 3 * NA_W + WB_QW, 3 * NA_W + WB_QW + WB_KVW,
               3 * NA_W + WB_QW + 2 * WB_KVW], axis=-1)
    order = np.asarray(WB_HEAD_ORDER)
    perm = (order[:, None] * HEAD_DIM + np.arange(HEAD_DIM)[None, :]).reshape(-1)
    qscale = HEAD_DIM ** -0.5
    gate = jnp.concatenate([gate[:, :NA_W], gate[:, NA_W:][:, perm]], axis=-1)
    w = jnp.concatenate([gate, qa * qscale, ka, va, qb[:, perm] * qscale, kb, vb], axis=-1)
    w_out_p = jnp.concatenate([w_out[:NA_W], w_out[NA_W:][perm]], axis=0)
    return w.astype(BF16), w_out_p.astype(BF16)


def _odd_weights(w_in, q_norm, w_qb, kv_norm, w_kvb, w_out):
    q_lat, kv_lat, k_rope, gate = jnp.split(
        w_in, [MLA_Q_RANK, MLA_Q_RANK + MLA_KV_RANK, MLA_Q_RANK + MLA_KV_RANK + MLA_ROPE], axis=-1)
    w1t = jnp.concatenate([gate, q_lat, kv_lat], axis=-1).T.astype(BF16)
    half = MLA_ROPE // 2
    k_rot = jnp.concatenate([-k_rope[:, half:], k_rope[:, :half]], axis=-1)
    z64 = jnp.zeros((D_MODEL, MLA_NOPE), w_in.dtype)
    z32 = jnp.zeros((D_MODEL, LANES - MLA_QK), w_in.dtype)
    wnat = jnp.concatenate([kv_lat, z64, k_rope, z32, z64, k_rot, z32], axis=-1).astype(BF16)
    wkv = w_kvb.reshape(MLA_KV_RANK, MLA_HEADS, MLA_NOPE + MLA_V)
    wk = jnp.concatenate([wkv[:, :, :MLA_NOPE], jnp.zeros_like(wkv[:, :, :MLA_NOPE])], axis=-1)
    wk = wk.reshape(MLA_KV_RANK, MLA_HEADS * LANES).astype(BF16)
    wvt = wkv[:, :, MLA_NOPE:].reshape(MLA_KV_RANK, ODD_MIX).T.astype(BF16)
    gq = jnp.broadcast_to(q_norm.astype(F32)[:, None], (MLA_Q_RANK, LANES))
    gkvt = jnp.broadcast_to(kv_norm.astype(F32)[:, None], (MLA_KV_RANK, LANES))
    gkv = kv_norm.astype(F32)[None, :]
    return w1t, wnat, gq, gkvt, gkv, w_qb.T.astype(BF16), wk, wvt, w_out.T.astype(BF16)


def _rope_tables(l):
    inv_freq = 1.0 / (ROPE_THETA ** (jnp.arange(0, MLA_ROPE, 2, dtype=F32) / MLA_ROPE))
    ang = jnp.arange(l, dtype=F32)[:, None] * inv_freq[None, :]
    cos, sin = jnp.cos(ang), jnp.sin(ang)
    z64 = jnp.zeros((l, MLA_NOPE), F32)
    z32 = jnp.zeros((l, LANES - MLA_QK), F32)
    cos_p = jnp.concatenate([z64, cos, cos, z32], axis=-1)
    sin_p = jnp.concatenate([z64, sin, sin, z32], axis=-1)
    return cos.T, sin.T, cos_p, sin_p


def _trunk(x, mod, norm_g, ev, na_tbl, wb_tbl, sink, od, final_g):
    b, l, _ = x.shape
    assert l % TM == 0 and l % NA_CHUNK == 0 and l % WB_CHUNK == 0
    assert l // GRID_W >= NA_ROWS_PER_STEP >= NA_HALO_ROWS >= NA_WIN_H
    shift0, scale0, gate0 = [t[:, None, :] for t in jnp.split(mod[0], 3, axis=-1)]
    shift1, scale1, gate1 = [t[:, None, :] for t in jnp.split(mod[1], 3, axis=-1)]

    w_in0, w_out0 = ev
    u = _even_front(x, norm_g[0][None, :], scale0, shift0, w_in0)
    oa = _na_attention(u, na_tbl)
    ob = _wb_attention(u, wb_tbl, sink)

    w1t, wnat, gq, gkvt, gkv, wqbt, wk, wvt, w_out1t = od
    cos_t, sin_t, cos_p, sin_p = _rope_tables(l)
    x1, k, qt, vt, sgt = _odd_front(x, oa, ob, u, gate0, w_out0, norm_g[1][None, :], scale1, shift1, w1t, wnat, gq, gkvt, gkv,
                                wqbt, wk, wvt, cos_t, sin_t, cos_p, sin_p)
    ot = _mla_attention(qt, k, vt)
    return _odd_out(x1, ot, sgt, gate1, w_out1t, final_g[None, :])


def kernel(x_prompt, x_sample, c_prompt, c_sample, ada_w, ada_b, norm_g, t5_bias, ev_w_in, na_rpb,
           wb_sink, ev_w_out, mla_w_in, mla_q_norm, mla_w_qb, mla_kv_norm, mla_w_kvb, mla_w_out, final_g):
    bp, bs = c_prompt.shape[0], c_sample.shape[0]
    rows = -(-(bp + bs) // 16) * 16
    c_pad = jnp.concatenate([c_prompt, c_sample, jnp.zeros((rows - bp - bs, D_MODEL), F32)], axis=0)
    mod = _ada_mod(c_pad, ada_w.astype(BF16), ada_b[:, None, :])

    ev = _even_weights(ev_w_in[0], ev_w_out[0])
    na_tbl = _na_bias_table(na_rpb[0])
    wb_tbl = _wb_bias_table(t5_bias)
    od = _odd_weights(mla_w_in[0], mla_q_norm[0], mla_w_qb[0], mla_kv_norm[0], mla_w_kvb[0], mla_w_out[0])
    sink = wb_sink[0].astype(F32)

    y_prompt = _trunk(x_prompt, mod[:, :bp], norm_g, ev, na_tbl, wb_tbl, sink, od, final_g)
    y_sample = _trunk(x_sample, mod[:, bp:bp + bs], norm_g, ev, na_tbl, wb_tbl, sink, od, final_g)
    return (y_prompt, y_sample)
```

```python
import functools
import math

import numpy as np
import jax
import jax.numpy as jnp
from jax import lax
from jax.experimental import pallas as pl
from jax.experimental.pallas import tpu as pltpu

D_MODEL = 1024
GRID_W = 64
HEAD_DIM = 64
EPS = 1e-6
NEG = -1e30
NA_HEADS = 8
NA_WIN_H = 8
NA_WIN_W = 16
WB_HEADS = 8
WB_KV_HEADS = 2
WB_WINDOW = 128
WB_BLOCK = 128
T5_BUCKETS = 32
T5_MAX_DIST = 128
MLA_HEADS = 16
MLA_Q_RANK = 256
MLA_KV_RANK = 128
MLA_NOPE = 64
MLA_ROPE = 32
MLA_V = 64
ROPE_THETA = 10000.0
MLA_QK = MLA_NOPE + MLA_ROPE

NA_W = NA_HEADS * HEAD_DIM
WB_QW = WB_HEADS * HEAD_DIM
WB_KVW = WB_KV_HEADS * HEAD_DIM
EVEN_MIX = NA_W + WB_QW
EVEN_IN = 3 * NA_W + WB_QW + 2 * WB_KVW + EVEN_MIX
ODD_MIX = MLA_HEADS * MLA_V

U_GATE = 0
U_QA = EVEN_MIX
U_KA = U_QA + NA_W
U_VA = U_KA + NA_W
U_QB = U_VA + NA_W
U_KB = U_QB + WB_QW
U_VB = U_KB + WB_KVW

LANES = 128
TM = 512
EVEN_TM = 1024
ADA_TN = 768
NA_ROWS_PER_STEP = 32
NA_HALO_ROWS = 8
NA_UNROLL = 10
WB_UNROLL = 7
NA_CHUNK = NA_ROWS_PER_STEP * GRID_W
WB_CHUNK = 2048
MLA_TQ = 1024
MLA_TK = 4096
MLA_RESIDENT_TOKENS = 16384
ONES_ROWS = 16
VMEM_LIMIT = 56 * 1024 * 1024

BF16 = jnp.bfloat16
F32 = jnp.float32
LOG2E = math.log2(math.e)

WB_HEAD_ORDER = (0, 4, 1, 5, 2, 6, 3, 7)


def _params(sem):
    return pltpu.CompilerParams(dimension_semantics=sem, vmem_limit_bytes=VMEM_LIMIT)


def _dot(a, b):
    return jnp.dot(a, b, preferred_element_type=F32)


def _dot_nt(a, b):
    return lax.dot_general(a, b, (((1,), (1,)), ((), ())), preferred_element_type=F32)


def _silu(x):
    return x * (1.0 / (1.0 + jnp.exp(-x)))


def _modulated_norm(x, g, scale, shift):
    ms = jnp.mean(x * x, axis=-1, keepdims=True)
    y = x * lax.rsqrt(ms + EPS)
    return (y * g) * (1.0 + scale) + shift


def _ada_kernel(c_ref, w_ref, b_ref, o_ref):
    c = c_ref[...]
    cs = _silu(c).astype(BF16)
    o_ref[0] = _dot(cs, w_ref[0]) + b_ref[0]


def _ada_mod(c_pad, ada_w, ada_b):
    depth = ada_w.shape[0]
    rows = c_pad.shape[0]
    tn = ADA_TN
    return pl.pallas_call(
        _ada_kernel,
        grid=(depth, 3 * D_MODEL // tn),
        in_specs=[
            pl.BlockSpec((rows, D_MODEL), lambda i, n: (0, 0)),
            pl.BlockSpec((1, D_MODEL, tn), lambda i, n: (i, 0, n)),
            pl.BlockSpec((1, 1, tn), lambda i, n: (i, 0, n)),
        ],
        out_specs=pl.BlockSpec((1, rows, tn), lambda i, n: (i, 0, n)),
        out_shape=jax.ShapeDtypeStruct((depth, rows, 3 * D_MODEL), F32),
        compiler_params=_params(("arbitrary", "arbitrary")),
        name="ada_mod",
    )(c_pad, ada_w, ada_b)


def _even_front_kernel(x_ref, g_ref, sc_ref, sh_ref, w_ref, u_ref):
    h = _modulated_norm(x_ref[0], g_ref[...], sc_ref[0], sh_ref[0]).astype(BF16)
    u_ref[0] = _dot(h, w_ref[...]).astype(BF16)


def _even_front(x, g, scale, shift, w):
    b, l, _ = x.shape
    TM = EVEN_TM
    return pl.pallas_call(
        _even_front_kernel,
        grid=(b, l // TM),
        in_specs=[
            pl.BlockSpec((1, TM, D_MODEL), lambda bi, i: (bi, i, 0)),
            pl.BlockSpec((1, D_MODEL), lambda bi, i: (0, 0)),
            pl.BlockSpec((1, 1, D_MODEL), lambda bi, i: (bi, 0, 0)),
            pl.BlockSpec((1, 1, D_MODEL), lambda bi, i: (bi, 0, 0)),
            pl.BlockSpec((D_MODEL, EVEN_IN), lambda bi, i: (0, 0)),
        ],
        out_specs=pl.BlockSpec((1, TM, EVEN_IN), lambda bi, i: (bi, i, 0)),
        out_shape=jax.ShapeDtypeStruct((b, l, EVEN_IN), BF16),
        compiler_params=_params(("arbitrary", "arbitrary")),
        name="even_front",
    )(x, g, scale, shift, w)


def _low_lane_mask(shape):
    return lax.broadcasted_iota(jnp.int32, shape, len(shape) - 1) < HEAD_DIM


def _na_kernel(q_ref, kp_ref, kc_ref, kn_ref, vp_ref, vc_ref, vn_ref, bias_ref, o_ref,
               kwin, vwin, s_scr, p_scr, *, rows):
    ci = pl.program_id(1)
    halo = NA_HALO_ROWS * GRID_W
    npair = NA_HEADS // 2
    kwin[0:halo] = kp_ref[0]
    kwin[halo:halo + NA_CHUNK] = kc_ref[0]
    kwin[halo + NA_CHUNK:2 * halo + NA_CHUNK] = kn_ref[0]
    for j in range(npair):
        lanes = slice(j * LANES, (j + 1) * LANES)
        dst = slice(2 * j * LANES, (2 * j + 1) * LANES)
        vwin[0:halo, dst] = vp_ref[0, :, lanes]
        vwin[halo:halo + NA_CHUNK, dst] = vc_ref[0, :, lanes]
        vwin[halo + NA_CHUNK:2 * halo + NA_CHUNK, dst] = vn_ref[0, :, lanes]
        vwin[:, (2 * j + 1) * LANES:(2 * j + 2) * LANES] = jnp.ones((2 * halo + NA_CHUNK, LANES), BF16)

    low = _low_lane_mask((GRID_W, LANES))
    kh = NA_WIN_H
    r0 = ci * NA_ROWS_PER_STEP

    def offsets(i):
        r = r0 + i
        row_start = jnp.clip(r - kh // 2, 0, rows - kh)
        variant = r - row_start
        start = pl.multiple_of((row_start - r0 + NA_HALO_ROWS) * GRID_W, GRID_W)
        return variant, start, pl.multiple_of(i * GRID_W, GRID_W)

    def scores(i):
        variant, start, qoff = offsets(i)
        for j in range(npair):
            lanes = slice(j * LANES, (j + 1) * LANES)
            q2 = q_ref[0, pl.ds(qoff, GRID_W), lanes]
            zero = jnp.zeros_like(q2)
            lhs = jnp.concatenate([jnp.where(low, q2, zero), jnp.where(low, zero, q2)], axis=0)
            kw = kwin[pl.ds(start, kh * GRID_W), lanes]
            s_scr[j] = _dot_nt(lhs, kw) + bias_ref[variant, j]

    def probs():
        for j in range(npair):
            s = s_scr[j]
            p_scr[j] = jnp.exp(s - jnp.max(s, axis=-1, keepdims=True)).astype(BF16)

    def output(i):
        _, start, qoff = offsets(i)
        for j in range(npair):
            vw = vwin[pl.ds(start, kh * GRID_W), 2 * j * LANES:(2 * j + 2) * LANES]
            res = _dot(p_scr[j], vw)
            o = res[:, 0:LANES] / res[:, LANES:2 * LANES]
            o_ref[0, pl.ds(qoff, GRID_W), j * LANES:(j + 1) * LANES] = (
                jnp.where(low, o[:GRID_W], o[GRID_W:]).astype(BF16))

    scores(0)
    probs()
    scores(1)

    def body(i, carry):
        output(i - 2)
        probs()
        scores(i)
        return carry

    lax.fori_loop(2, NA_ROWS_PER_STEP, body, 0, unroll=NA_UNROLL)
    output(NA_ROWS_PER_STEP - 2)
    probs()
    output(NA_ROWS_PER_STEP - 1)


def _na_attention(u, bias_tbl):
    b, l, _ = u.shape
    rows = l // GRID_W
    nchunk = l // NA_CHUNK
    halo = NA_HALO_ROWS * GRID_W
    per_chunk = NA_CHUNK // halo
    nhalo = l // halo
    qblk, kblk, vblk = U_QA // NA_W, U_KA // NA_W, U_VA // NA_W

    def window(col):
        return [
            pl.BlockSpec((1, halo, NA_W), lambda bi, i: (bi, jnp.maximum(i * per_chunk - 1, 0), col)),
            pl.BlockSpec((1, NA_CHUNK, NA_W), lambda bi, i: (bi, i, col)),
            pl.BlockSpec((1, halo, NA_W), lambda bi, i: (bi, jnp.minimum((i + 1) * per_chunk, nhalo - 1), col)),
        ]

    npair = NA_HEADS // 2
    return pl.pallas_call(
        functools.partial(_na_kernel, rows=rows),
        grid=(b, nchunk),
        in_specs=[pl.BlockSpec((1, NA_CHUNK, NA_W), lambda bi, i: (bi, i, qblk))]
        + window(kblk) + window(vblk)
        + [pl.BlockSpec(bias_tbl.shape, lambda bi, i: (0, 0, 0, 0), pipeline_mode=pl.Buffered(1))],
        out_specs=pl.BlockSpec((1, NA_CHUNK, NA_W), lambda bi, i: (bi, i, 0)),
        out_shape=jax.ShapeDtypeStruct((b, l, NA_W), BF16),
        scratch_shapes=[pltpu.VMEM((NA_CHUNK + 2 * halo, NA_W), BF16),
                        pltpu.VMEM((NA_CHUNK + 2 * halo, 2 * NA_W), BF16),
                        pltpu.VMEM((npair, 2 * GRID_W, NA_WIN_H * GRID_W), F32),
                        pltpu.VMEM((npair, 2 * GRID_W, NA_WIN_H * GRID_W), BF16)],
        compiler_params=_params(("arbitrary", "arbitrary")),
        name="na_attention",
    )(u, u, u, u, u, u, u, bias_tbl)


def _wb_kernel(sink_ref, q_ref, kp_ref, kc_ref, kn_ref, vp_ref, vc_ref, vn_ref, bias_ref, o_ref,
               kwin, vwin, s_scr, p_scr, ps_scr, *, nblk):
    ci = pl.program_id(1)
    npair = WB_HEADS // 2
    per_step = WB_CHUNK // WB_BLOCK
    win = WB_CHUNK + 2 * WB_BLOCK
    kwin[0:WB_BLOCK] = kp_ref[0]
    kwin[WB_BLOCK:WB_BLOCK + WB_CHUNK] = kc_ref[0]
    kwin[WB_BLOCK + WB_CHUNK:win] = kn_ref[0]
    vwin[0:WB_BLOCK, 0:LANES] = vp_ref[0]
    vwin[WB_BLOCK:WB_BLOCK + WB_CHUNK, 0:LANES] = vc_ref[0]
    vwin[WB_BLOCK + WB_CHUNK:win, 0:LANES] = vn_ref[0]
    vwin[:, LANES:2 * LANES] = jnp.ones((win, LANES), BF16)

    low = _low_lane_mask((WB_BLOCK, LANES))
    col = lax.broadcasted_iota(jnp.int32, (2 * WB_BLOCK, 3 * WB_BLOCK), 1)
    row = lax.broadcasted_iota(jnp.int32, (2 * WB_BLOCK, 1), 0)

    def scores(n):
        gblk = ci * per_step + n
        lo = jnp.where(gblk > 0, 0, WB_BLOCK)
        hi = jnp.where(gblk < nblk - 1, 3 * WB_BLOCK, 2 * WB_BLOCK)
        in_seq = jnp.logical_and(col >= lo, col < hi)
        off = pl.multiple_of(n * WB_BLOCK, WB_BLOCK)
        kw = kwin[pl.ds(off, 3 * WB_BLOCK), :]
        for j in range(npair):
            q2 = q_ref[0, pl.ds(off, WB_BLOCK), j * LANES:(j + 1) * LANES]
            zero = jnp.zeros_like(q2)
            lhs = jnp.concatenate([jnp.where(low, q2, zero), jnp.where(low, zero, q2)], axis=0)
            s_scr[j] = jnp.where(in_seq, _dot_nt(lhs, kw) + bias_ref[j], NEG)

    def probs():
        for j in range(npair):
            s = s_scr[j]
            sink = jnp.where(row < WB_BLOCK, sink_ref[WB_HEAD_ORDER[2 * j]],
                             sink_ref[WB_HEAD_ORDER[2 * j + 1]])
            m = jnp.maximum(jnp.max(s, axis=-1, keepdims=True), sink)
            p_scr[j] = jnp.exp(s - m).astype(BF16)
            ps_scr[j] = jnp.broadcast_to(jnp.exp(sink - m), (2 * WB_BLOCK, LANES))

    def output(n):
        off = pl.multiple_of(n * WB_BLOCK, WB_BLOCK)
        vw = vwin[pl.ds(off, 3 * WB_BLOCK), :]
        for j in range(npair):
            res = _dot(p_scr[j], vw)
            o = res[:, 0:LANES] / (res[:, LANES:2 * LANES] + ps_scr[j])
            o_ref[0, pl.ds(off, WB_BLOCK), j * LANES:(j + 1) * LANES] = (
                jnp.where(low, o[:WB_BLOCK], o[WB_BLOCK:]).astype(BF16))

    scores(0)
    probs()
    scores(1)

    def body(n, carry):
        output(n - 2)
        probs()
        scores(n)
        return carry

    lax.fori_loop(2, per_step, body, 0, unroll=WB_UNROLL)
    output(per_step - 2)
    probs()
    output(per_step - 1)


def _wb_attention(u, bias_tbl, sink):
    b, l, _ = u.shape
    nblk = l // WB_BLOCK
    nchunk = l // WB_CHUNK
    per_step = WB_CHUNK // WB_BLOCK
    qblk = U_QB // WB_QW
    kcol, vcol = U_KB // WB_KVW, U_VB // WB_KVW

    def halo(colblk, d):
        if d == 0:
            return pl.BlockSpec((1, WB_CHUNK, WB_KVW), lambda bi, i: (bi, i, colblk))
        return pl.BlockSpec(
            (1, WB_BLOCK, WB_KVW),
            lambda bi, i: (bi, jnp.clip(i * per_step + (per_step if d > 0 else -1), 0, nblk - 1), colblk))

    return pl.pallas_call(
        functools.partial(_wb_kernel, nblk=nblk),
        grid=(b, nchunk),
        in_specs=[
            pl.BlockSpec(memory_space=pltpu.SMEM),
            pl.BlockSpec((1, WB_CHUNK, WB_QW), lambda bi, i: (bi, i, qblk)),
            halo(kcol, -1), halo(kcol, 0), halo(kcol, 1),
            halo(vcol, -1), halo(vcol, 0), halo(vcol, 1),
            pl.BlockSpec(bias_tbl.shape, lambda bi, i: (0, 0, 0), pipeline_mode=pl.Buffered(1)),
        ],
        out_specs=pl.BlockSpec((1, WB_CHUNK, WB_QW), lambda bi, i: (bi, i, 0)),
        out_shape=jax.ShapeDtypeStruct((b, l, WB_QW), BF16),
        scratch_shapes=[pltpu.VMEM((WB_CHUNK + 2 * WB_BLOCK, WB_KVW), BF16),
                        pltpu.VMEM((WB_CHUNK + 2 * WB_BLOCK, 2 * WB_KVW), BF16),
                        pltpu.VMEM((WB_HEADS // 2, 2 * WB_BLOCK, 3 * WB_BLOCK), F32),
                        pltpu.VMEM((WB_HEADS // 2, 2 * WB_BLOCK, 3 * WB_BLOCK), BF16),
                        pltpu.VMEM((WB_HEADS // 2, 2 * WB_BLOCK, LANES), F32)],
        compiler_params=_params(("arbitrary", "arbitrary")),
        name="wb_attention",
    )(sink, u, u, u, u, u, u, u, bias_tbl)


def _even_out_tile(x_ref, oa_ref, ob_ref, gate_ref, gm_ref, w_ref):
    g = _silu(gate_ref[0].astype(F32))
    o = jnp.concatenate([oa_ref[0], ob_ref[0]], axis=-1).astype(F32)
    z = (o * g).astype(BF16)
    return x_ref[0] + gm_ref[0] * _dot(z, w_ref[...])


def _odd_front_kernel(x_ref, oa_ref, ob_ref, gate_ref, gm_ref, wo_ref,
                      g_ref, sc_ref, sh_ref, w1t_ref, wnat_ref, gq_ref, gkvt_ref, gkv_ref,
                      wqbt_ref, wk_ref, wvt_ref, cost_ref, sint_ref, cosp_ref, sinp_ref,
                      x1_ref, k_ref, qt_ref, vt_ref, sgt_ref):
    x1 = _even_out_tile(x_ref, oa_ref, ob_ref, gate_ref, gm_ref, wo_ref)
    x1_ref[0] = x1
    h = _modulated_norm(x1, g_ref[...], sc_ref[0], sh_ref[0]).astype(BF16)
    reps = TM // LANES

    ut = _dot_nt(w1t_ref[...], h)
    sgt_ref[0] = _silu(ut[0:ODD_MIX]).astype(BF16)

    qlt = ut[ODD_MIX:ODD_MIX + MLA_Q_RANK]
    qn = qlt * lax.rsqrt(jnp.mean(qlt * qlt, axis=0, keepdims=True) + EPS)
    qn = (qn * jnp.concatenate([gq_ref[...]] * reps, axis=1)).astype(BF16)
    qt = _dot(wqbt_ref[...], qn)
    qscale = (MLA_QK ** -0.5) * LOG2E
    cos_t = cost_ref[...]
    sin_t = sint_ref[...]
    half = MLA_ROPE // 2
    for hd in range(MLA_HEADS):
        base = hd * MLA_QK
        qt_ref[0, hd, 0, 0:MLA_NOPE, :] = (qt[base:base + MLA_NOPE] * qscale).astype(BF16)
        x1 = qt[base + MLA_NOPE:base + MLA_NOPE + half]
        x2 = qt[base + MLA_NOPE + half:base + MLA_QK]
        qt_ref[0, hd, 0, MLA_NOPE:MLA_NOPE + half, :] = ((x1 * cos_t - x2 * sin_t) * qscale).astype(BF16)
        qt_ref[0, hd, 0, MLA_NOPE + half:MLA_QK, :] = ((x2 * cos_t + x1 * sin_t) * qscale).astype(BF16)

    kvt = ut[ODD_MIX + MLA_Q_RANK:ODD_MIX + MLA_Q_RANK + MLA_KV_RANK]
    kvnt = kvt * lax.rsqrt(jnp.mean(kvt * kvt, axis=0, keepdims=True) + EPS)
    kvnt = (kvnt * jnp.concatenate([gkvt_ref[...]] * reps, axis=1)).astype(BF16)
    vt = _dot(wvt_ref[...], kvnt)
    vt_ref[0, :, 0] = vt.reshape(MLA_HEADS, MLA_V, TM).astype(BF16)

    nat = _dot(h, wnat_ref[...])
    kvl = nat[:, 0:MLA_KV_RANK]
    kvn = kvl * lax.rsqrt(jnp.mean(kvl * kvl, axis=-1, keepdims=True) + EPS)
    kvn = (kvn * gkv_ref[...]).astype(BF16)
    kn = _dot(kvn, wk_ref[...])
    kpe = nat[:, LANES:2 * LANES] * cosp_ref[...] + nat[:, 2 * LANES:3 * LANES] * sinp_ref[...]
    for hd in range(MLA_HEADS):
        kh = kn[:, hd * LANES:(hd + 1) * LANES] + kpe
        k_ref[0, hd] = kh[:, 0:MLA_QK].astype(BF16)


def _odd_front(x, oa, ob, u, gate_mod, w_out, g, scale, shift, w1t, wnat, gq, gkvt, gkv, wqbt, wk, wvt,
               cos_t, sin_t, cos_p, sin_p):
    b, l, _ = x.shape
    nt = l // TM
    const2 = lambda bi, i: (0, 0)
    n1 = w1t.shape[0]
    return pl.pallas_call(
        _odd_front_kernel,
        grid=(b, nt),
        in_specs=[
            pl.BlockSpec((1, TM, D_MODEL), lambda bi, i: (bi, i, 0)),
            pl.BlockSpec((1, TM, NA_W), lambda bi, i: (bi, i, 0)),
            pl.BlockSpec((1, TM, WB_QW), lambda bi, i: (bi, i, 0)),
            pl.BlockSpec((1, TM, EVEN_MIX), lambda bi, i: (bi, i, U_GATE // EVEN_MIX)),
            pl.BlockSpec((1, 1, D_MODEL), lambda bi, i: (bi, 0, 0)),
            pl.BlockSpec((EVEN_MIX, D_MODEL), const2),
            pl.BlockSpec((1, D_MODEL), const2),
            pl.BlockSpec((1, 1, D_MODEL), lambda bi, i: (bi, 0, 0)),
            pl.BlockSpec((1, 1, D_MODEL), lambda bi, i: (bi, 0, 0)),
            pl.BlockSpec((n1, D_MODEL), const2),
            pl.BlockSpec((D_MODEL, 3 * LANES), const2),
            pl.BlockSpec((MLA_Q_RANK, LANES), const2),
            pl.BlockSpec((MLA_KV_RANK, LANES), const2),
            pl.BlockSpec((1, MLA_KV_RANK), const2),
            pl.BlockSpec((MLA_HEADS * MLA_QK, MLA_Q_RANK), const2),
            pl.BlockSpec((MLA_KV_RANK, MLA_HEADS * LANES), const2),
            pl.BlockSpec((ODD_MIX, MLA_KV_RANK), const2),
            pl.BlockSpec((MLA_ROPE // 2, TM), lambda bi, i: (0, i)),
            pl.BlockSpec((MLA_ROPE // 2, TM), lambda bi, i: (0, i)),
            pl.BlockSpec((TM, LANES), lambda bi, i: (i, 0)),
            pl.BlockSpec((TM, LANES), lambda bi, i: (i, 0)),
        ],
        out_specs=[
            pl.BlockSpec((1, TM, D_MODEL), lambda bi, i: (bi, i, 0)),
            pl.BlockSpec((1, MLA_HEADS, TM, MLA_QK), lambda bi, i: (bi, 0, i, 0)),
            pl.BlockSpec((1, MLA_HEADS, 1, MLA_QK, TM), lambda bi, i: (bi, 0, i, 0, 0)),
            pl.BlockSpec((1, MLA_HEADS, 1, MLA_V, TM), lambda bi, i: (bi, 0, i, 0, 0)),
            pl.BlockSpec((1, ODD_MIX, TM), lambda bi, i: (bi, 0, i)),
        ],
        out_shape=[
            jax.ShapeDtypeStruct((b, l, D_MODEL), F32),
            jax.ShapeDtypeStruct((b, MLA_HEADS, l, MLA_QK), BF16),
            jax.ShapeDtypeStruct((b, MLA_HEADS, nt, MLA_QK, TM), BF16),
            jax.ShapeDtypeStruct((b, MLA_HEADS, nt, MLA_V, TM), BF16),
            jax.ShapeDtypeStruct((b, ODD_MIX, l), BF16),
        ],
        compiler_params=_params(("arbitrary", "arbitrary")),
        name="odd_front",
    )(x, oa, ob, u, gate_mod, w_out, g, scale, shift, w1t, wnat, gq, gkvt, gkv, wqbt, wk, wvt,
      cos_t, sin_t, cos_p, sin_p)


def _mla_kernel(qt_ref, k_ref, vt_ref, o_ref, s_scr, p_scr, acc_scr, *, hb, nqb, nchunks, tk):
    ones = jnp.ones((ONES_ROWS, tk), BF16)
    qsub = MLA_TQ // TM
    ksub = tk // TM
    chunk_bits = nchunks.bit_length() - 1
    qb_bits = nqb.bit_length() - 1
    total = hb * nqb * nchunks

    def decode(i):
        return i >> (chunk_bits + qb_bits), (i >> chunk_bits) & (nqb - 1), i & (nchunks - 1)

    def scores(i):
        h, qb, t = decode(i)
        q = jnp.concatenate([qt_ref[0, h, qsub * qb + j] for j in range(qsub)], axis=1)
        koff = pl.multiple_of(t * tk, tk)
        s = _dot(k_ref[0, h, pl.ds(koff, tk), :], q)
        s_scr[...] = s
        return jnp.max(s, axis=0, keepdims=True)

    def probs(i, m, mt):
        _, _, t = decode(i)
        m = jnp.where(t == 0, NEG, m)
        m_new = jnp.maximum(m, mt)
        p_scr[...] = jnp.exp2(s_scr[...] - m_new).astype(BF16)
        return m_new, jnp.exp2(m - m_new)

    def accumulate(i, alpha):
        h, qb, t = decode(i)
        vt = jnp.concatenate([vt_ref[0, h, ksub * t + j] for j in range(ksub)], axis=1)
        acc_scr[...] = acc_scr[...] * alpha + _dot(jnp.concatenate([vt, ones], axis=0), p_scr[...])
        acc = acc_scr[...]
        o_ref[0, h, qb] = (acc[0:MLA_V] * (1.0 / acc[MLA_V:MLA_V + 1])).astype(BF16)

    acc_scr[...] = jnp.zeros_like(acc_scr)
    m = jnp.full((1, MLA_TQ), NEG, F32)
    mt = scores(0)
    m, alpha = probs(0, m, mt)
    mt = scores(1)

    def body(i, carry):
        m, mt, alpha = carry
        accumulate(i - 2, alpha)
        m, alpha = probs(i - 1, m, mt)
        mt = scores(i)
        return m, mt, alpha

    m, mt, alpha = lax.fori_loop(2, total, body, (m, mt, alpha), unroll=2)
    accumulate(total - 2, alpha)
    m, alpha = probs(total - 1, m, mt)
    accumulate(total - 1, alpha)


def _mla_attention(qt, k, vt):
    b, _, nt, _, _ = qt.shape
    l = nt * TM
    tk = min(l, MLA_TK)
    nchunks = l // tk
    nqb = l // MLA_TQ
    assert tk % TM == 0 and MLA_TQ % TM == 0 and l % tk == 0
    assert nqb & (nqb - 1) == 0 and nchunks & (nchunks - 1) == 0
    hb = max(1, min(MLA_HEADS, MLA_RESIDENT_TOKENS // l))
    assert hb * nqb * nchunks >= 2 and (hb * nqb * nchunks) % 2 == 0
    return pl.pallas_call(
        functools.partial(_mla_kernel, hb=hb, nqb=nqb, nchunks=nchunks, tk=tk),
        grid=(b, MLA_HEADS // hb),
        in_specs=[
            pl.BlockSpec((1, hb, nt, MLA_QK, TM), lambda bi, h: (bi, h, 0, 0, 0)),
            pl.BlockSpec((1, hb, l, MLA_QK), lambda bi, h: (bi, h, 0, 0)),
            pl.BlockSpec((1, hb, nt, MLA_V, TM), lambda bi, h: (bi, h, 0, 0, 0)),
        ],
        out_specs=pl.BlockSpec((1, hb, nqb, MLA_V, MLA_TQ), lambda bi, h: (bi, h, 0, 0, 0)),
        out_shape=jax.ShapeDtypeStruct((b, MLA_HEADS, nqb, MLA_V, MLA_TQ), BF16),
        scratch_shapes=[pltpu.VMEM((tk, MLA_TQ), F32),
                        pltpu.VMEM((tk, MLA_TQ), BF16),
                        pltpu.VMEM((MLA_V + ONES_ROWS, MLA_TQ), F32)],
        compiler_params=_params(("arbitrary", "arbitrary")),
        name="mla_attention",
    )(qt, k, vt)


def _odd_out_kernel(x_ref, ot_ref, sgt_ref, gm_ref, wt_ref, fg_ref, y_ref):
    ot = ot_ref[0, :, 0]
    ot = ot.reshape(ODD_MIX, ot.shape[-1])
    z = (ot.astype(F32) * sgt_ref[0].astype(F32)).astype(BF16)
    out = lax.dot_general(z, wt_ref[...], (((0,), (1,)), ((), ())), preferred_element_type=F32)
    x2 = x_ref[0] + gm_ref[0] * out
    ms = jnp.mean(x2 * x2, axis=-1, keepdims=True)
    y_ref[0] = (x2 * lax.rsqrt(ms + EPS)) * fg_ref[...]


def _odd_out(x, ot, sgt, gate_mod, w_out_t, final_g):
    b, l, _ = x.shape
    TM = MLA_TQ
    sub = MLA_TQ // TM
    return pl.pallas_call(
        _odd_out_kernel,
        grid=(b, l // TM),
        in_specs=[
            pl.BlockSpec((1, TM, D_MODEL), lambda bi, i: (bi, i, 0)),
            pl.BlockSpec((1, MLA_HEADS, 1, MLA_V, TM), lambda bi, i: (bi, 0, i // sub, 0, i % sub)),
            pl.BlockSpec((1, ODD_MIX, TM), lambda bi, i: (bi, 0, i)),
            pl.BlockSpec((1, 1, D_MODEL), lambda bi, i: (bi, 0, 0)),
            pl.BlockSpec((D_MODEL, ODD_MIX), lambda bi, i: (0, 0)),
            pl.BlockSpec((1, D_MODEL), lambda bi, i: (0, 0)),
        ],
        out_specs=pl.BlockSpec((1, TM, D_MODEL), lambda bi, i: (bi, i, 0)),
        out_shape=jax.ShapeDtypeStruct((b, l, D_MODEL), F32),
        compiler_params=_params(("arbitrary", "arbitrary")),
        name="odd_out",
    )(x, ot, sgt, gate_mod, w_out_t, final_g)


def _t5_bucket(rel):
    nb = T5_BUCKETS // 2
    ret = (rel > 0).astype(np.int32) * nb
    n = np.abs(rel)
    max_exact = nb // 2
    large = max_exact + (np.log(np.maximum(n, 1) / max_exact)
                         / np.log(T5_MAX_DIST / max_exact) * (nb - max_exact)).astype(np.int32)
    large = np.minimum(large, nb - 1)
    return ret + np.where(n < max_exact, n, large)


def _na_bias_table(rpb):
    kh = NA_WIN_H
    c = np.arange(GRID_W)
    col_start = np.clip(c - NA_WIN_W // 2, 0, GRID_W - NA_WIN_W)
    col_ok = (c[None, :] >= col_start[:, None]) & (c[None, :] < col_start[:, None] + NA_WIN_W)
    d_col = np.clip(c[None, :] - c[:, None], -(NA_WIN_W - 1), NA_WIN_W - 1) + NA_WIN_W - 1
    nrel = 2 * NA_WIN_W - 1
    rows = jnp.stack([rpb[:, NA_WIN_H - 1 - v:2 * NA_WIN_H - 1 - v] for v in range(kh)], axis=1)
    onehot = (d_col.reshape(-1)[None, :] == np.arange(nrel)[:, None]).astype(np.float32)
    bias = jnp.dot(rows.astype(F32).reshape(-1, nrel), jnp.asarray(onehot),
                   precision=lax.Precision.HIGHEST)
    bias = bias.reshape(NA_HEADS, kh, kh, GRID_W, GRID_W)
    bias = jnp.where(jnp.asarray(col_ok)[None, None, None, :, :], bias, NEG)
    bias = bias.transpose(1, 0, 3, 2, 4).reshape(kh, NA_HEADS // 2, 2 * GRID_W, kh * GRID_W)
    return bias


def _wb_bias_table(t5_bias):
    rel = (np.arange(3 * WB_BLOCK) - WB_BLOCK)[None, :] - np.arange(WB_BLOCK)[:, None]
    offs = np.arange(-(2 * WB_BLOCK - 1), 2 * WB_BLOCK + 1)
    period = 4 * WB_BLOCK
    by_off = t5_bias[_t5_bucket(offs)].astype(F32).T
    shifted = jnp.tile(by_off, (1, WB_BLOCK))[:, :WB_BLOCK * (period - 1)]
    shifted = shifted.reshape(WB_HEADS, WB_BLOCK, period - 1)
    bias = shifted[:, :, WB_BLOCK - 1:4 * WB_BLOCK - 1]
    bias = jnp.where(jnp.asarray(np.abs(rel) <= WB_WINDOW)[None], bias, NEG)
    bias = bias[np.asarray(WB_HEAD_ORDER)]
    return bias.reshape(WB_HEADS // 2, 2 * WB_BLOCK, 3 * WB_BLOCK)


def _even_weights(w_in, w_out):
    qa, ka, va, qb, kb, vb, gate = jnp.split(
        w_in, [NA_W, 2 * NA_W, 3 * NA_W, 3 * NA_W + WB_QW, 3 * NA_W + WB_QW + WB_KVW,
               3 * NA_W + WB_QW + 2 * WB_KVW], axis=-1)
    order = np.asarray(WB_HEAD_ORDER)
    perm = (order[:, None] * HEAD_DIM + np.arange(HEAD_DIM)[None, :]).reshape(-1)
    qscale = HEAD_DIM ** -0.5
    gate = jnp.concatenate([gate[:, :NA_W], gate[:, NA_W:][:, perm]], axis=-1)
    w = jnp.concatenate([gate, qa * qscale, ka, va, qb[:, perm] * qscale, kb, vb], axis=-1)
    w_out_p = jnp.concatenate([w_out[:NA_W], w_out[NA_W:][perm]], axis=0)
    return w.astype(BF16), w_out_p.astype(BF16)


def _odd_weights(w_in, q_norm, w_qb, kv_norm, w_kvb, w_out):
    q_lat, kv_lat, k_rope, gate = jnp.split(
        w_in, [MLA_Q_RANK, MLA_Q_RANK + MLA_KV_RANK, MLA_Q_RANK + MLA_KV_RANK + MLA_ROPE], axis=-1)
    w1t = jnp.concatenate([gate, q_lat, kv_lat], axis=-1).T.astype(BF16)
    half = MLA_ROPE // 2
    k_rot = jnp.concatenate([-k_rope[:, half:], k_rope[:, :half]], axis=-1)
    z64 = jnp.zeros((D_MODEL, MLA_NOPE), w_in.dtype)
    z32 = jnp.zeros((D_MODEL, LANES - MLA_QK), w_in.dtype)
    wnat = jnp.concatenate([kv_lat, z64, k_rope, z32, z64, k_rot, z32], axis=-1).astype(BF16)
    wkv = w_kvb.reshape(MLA_KV_RANK, MLA_HEADS, MLA_NOPE + MLA_V)
    wk = jnp.concatenate([wkv[:, :, :MLA_NOPE], jnp.zeros_like(wkv[:, :, :MLA_NOPE])], axis=-1)
    wk = wk.reshape(MLA_KV_RANK, MLA_HEADS * LANES).astype(BF16)
    wvt = wkv[:, :, MLA_NOPE:].reshape(MLA_KV_RANK, ODD_MIX).T.astype(BF16)
    gq = jnp.broadcast_to(q_norm.astype(F32)[:, None], (MLA_Q_RANK, LANES))
    gkvt = jnp.broadcast_to(kv_norm.astype(F32)[:, None], (MLA_KV_RANK, LANES))
    gkv = kv_norm.astype(F32)[None, :]
    return w1t, wnat, gq, gkvt, gkv, w_qb.T.astype(BF16), wk, wvt, w_out.T.astype(BF16)


def _rope_tables(l):
    inv_freq = 1.0 / (ROPE_THETA ** (jnp.arange(0, MLA_ROPE, 2, dtype=F32) / MLA_ROPE))
    ang = jnp.arange(l, dtype=F32)[:, None] * inv_freq[None, :]
    cos, sin = jnp.cos(ang), jnp.sin(ang)
    z64 = jnp.zeros((l, MLA_NOPE), F32)
    z32 = jnp.zeros((l, LANES - MLA_QK), F32)
    cos_p = jnp.concatenate([z64, cos, cos, z32], axis=-1)
    sin_p = jnp.concatenate([z64, sin, sin, z32], axis=-1)
    return cos.T, sin.T, cos_p, sin_p


def _trunk(x, mod, norm_g, ev, na_tbl, wb_tbl, sink, od, final_g):
    b, l, _ = x.shape
    assert l % TM == 0 and l % NA_CHUNK == 0 and l % WB_CHUNK == 0
    assert l // GRID_W >= NA_ROWS_PER_STEP >= NA_HALO_ROWS >= NA_WIN_H
    shift0, scale0, gate0 = [t[:, None, :] for t in jnp.split(mod[0], 3, axis=-1)]
    shift1, scale1, gate1 = [t[:, None, :] for t in jnp.split(mod[1], 3, axis=-1)]

    w_in0, w_out0 = ev
    u = _even_front(x, norm_g[0][None, :], scale0, shift0, w_in0)
    oa = _na_attention(u, na_tbl)
    ob = _wb_attention(u, wb_tbl, sink)

    w1t, wnat, gq, gkvt, gkv, wqbt, wk, wvt, w_out1t = od
    cos_t, sin_t, cos_p, sin_p = _rope_tables(l)
    x1, k, qt, vt, sgt = _odd_front(x, oa, ob, u, gate0, w_out0, norm_g[1][None, :], scale1, shift1, w1t, wnat, gq, gkvt, gkv,
                                wqbt, wk, wvt, cos_t, sin_t, cos_p, sin_p)
    ot = _mla_attention(qt, k, vt)
    return _odd_out(x1, ot, sgt, gate1, w_out1t, final_g[None, :])


def kernel(x_prompt, x_sample, c_prompt, c_sample, ada_w, ada_b, norm_g, t5_bias, ev_w_in, na_rpb,
           wb_sink, ev_w_out, mla_w_in, mla_q_norm, mla_w_qb, mla_kv_norm, mla_w_kvb, mla_w_out, final_g):
    bp, bs = c_prompt.shape[0], c_sample.shape[0]
    rows = -(-(bp + bs) // 16) * 16
    c_pad = jnp.concatenate([c_prompt, c_sample, jnp.zeros((rows - bp - bs, D_MODEL), F32)], axis=0)
    mod = _ada_mod(c_pad, ada_w.astype(BF16), ada_b[:, None, :])

    ev = _even_weights(ev_w_in[0], ev_w_out[0])
    na_tbl = _na_bias_table(na_rpb[0])
    wb_tbl = _wb_bias_table(t5_bias)
    od = _odd_weights(mla_w_in[0], mla_q_norm[0], mla_w_qb[0], mla_kv_norm[0], mla_w_kvb[0], mla_w_out[0])
    sink = wb_sink[0].astype(F32)

    y_prompt = _trunk(x_prompt, mod[:, :bp], norm_g, ev, na_tbl, wb_tbl, sink, od, final_g)
    y_sample = _trunk(x_sample, mod[:, bp:bp + bs], norm_g, ev, na_tbl, wb_tbl, sink, od, final_g)
    return (y_prompt, y_sample)
```
